```python
import jax, jax.numpy as jnp
from jax import lax
import numpy as np

D_MODEL = 1024
BATCH = 2
SEQ = 8192
DEPTH = 1

LRU_WIDTH = D_MODEL
LRU_HEADS = 16
LRU_BLOCK = LRU_WIDTH // LRU_HEADS
CONV_WIDTH = 4
CONV_LEFT = 2
RGLRU_C = 8.0
N_DIR = 2
N_HEADS = 16
N_KV_HEADS = 4
HEAD_DIM = 64
GROUP = N_HEADS // N_KV_HEADS
WINDOW = 128
BLOCK = 128
D_FF = ((8 * D_MODEL // 3 + 255) // 256) * 256
N_BRANCH = 2
Q_W = N_HEADS * HEAD_DIM
KV_W = N_KV_HEADS * HEAD_DIM
IN_W = 2 * LRU_WIDTH + Q_W + 2 * KV_W + N_BRANCH * D_MODEL
EPS = 1e-6
NEG_INF = -1e30

kernel_name = "hybrid_rglru_swa_gated_encoder"


def rmsnorm(x, g):
    xf = x.astype(jnp.float32)
    y = xf * lax.rsqrt(jnp.mean(xf * xf, axis=-1, keepdims=True) + EPS)
    return (y * g.astype(jnp.float32)).astype(x.dtype)


def centred_depthwise_conv(u, w, b):
    s = u.shape[1]
    up = jnp.pad(u, ((0, 0), (CONV_LEFT, CONV_WIDTH - 1 - CONV_LEFT), (0, 0)))
    out = up[:, 0:s] * w[0]
    for k in range(1, CONV_WIDTH):
        out = out + up[:, k:k + s] * w[k]
    return out + b


def _linear_combine(left, right):
    a1, b1 = left
    a2, b2 = right
    return a1 * a2, a2 * b1 + b2


def rg_lru(u, lam, wa, ba, wx, bx, reverse):
    bsz, s, c = u.shape
    ub = u.reshape(bsz, s, LRU_HEADS, LRU_BLOCK)
    r = jax.nn.sigmoid(jnp.einsum("bshi,hij->bshj", ub, wa.astype(jnp.float32)).reshape(bsz, s, c) + ba.astype(jnp.float32))
    i = jax.nn.sigmoid(jnp.einsum("bshi,hij->bshj", ub, wx.astype(jnp.float32)).reshape(bsz, s, c) + bx.astype(jnp.float32))
    log_a = -RGLRU_C * r * jax.nn.softplus(-lam.astype(jnp.float32))
    a = jnp.exp(log_a)
    beta = jnp.sqrt(jnp.maximum(-jnp.expm1(2.0 * log_a), 0.0))
    _, h = lax.associative_scan(_linear_combine, (a, beta * (i * u)), axis=1, reverse=reverse)
    return h


def banded_alibi_sink_attention(q, k, v, sink):
    bsz, s = q.shape[0], q.shape[1]
    nb = s // BLOCK
    qb = (q.astype(jnp.float32) * (HEAD_DIM ** -0.5)).reshape(bsz, nb, BLOCK, N_KV_HEADS, GROUP, HEAD_DIM)

    def key_blocks(t):
        tp = jnp.pad(t.astype(jnp.float32), ((0, 0), (BLOCK, BLOCK), (0, 0), (0, 0)))
        tp = tp.reshape(bsz, nb + 2, BLOCK, N_KV_HEADS, HEAD_DIM)
        return jnp.concatenate([tp[:, j:j + nb] for j in range(3)], axis=2)

    kb = key_blocks(k)
    vb = key_blocks(v)
    scores = jnp.einsum("bnqkgd,bnskd->bnkgqs", qb, kb)

    q_loc = jnp.arange(BLOCK)
    k_loc = jnp.arange(3 * BLOCK)
    dist = q_loc[:, None] + BLOCK - k_loc[None, :]
    kpos = jnp.arange(nb)[:, None] * BLOCK - BLOCK + k_loc[None, :]
    valid = (jnp.abs(dist) <= WINDOW)[None] & ((kpos >= 0) & (kpos < s))[:, None, :]

    slopes = jnp.exp2(-8.0 * (jnp.arange(N_HEADS, dtype=jnp.float32) + 1.0) / N_HEADS)
    alibi = -slopes.reshape(N_KV_HEADS, GROUP, 1, 1) * jnp.abs(dist).astype(jnp.float32)
    scores = jnp.where(valid[None, :, None, None], scores + alibi, NEG_INF)

    sink_l = sink.astype(jnp.float32).reshape(1, 1, N_KV_HEADS, GROUP, 1, 1)
    m = jnp.maximum(jnp.max(scores, axis=-1, keepdims=True), sink_l)
    p = jnp.exp(scores - m)
    denom = jnp.sum(p, axis=-1, keepdims=True) + jnp.exp(sink_l - m)
    o = jnp.einsum("bnkgqs,bnskd->bnqkgd", p / denom, vb)
    return o.reshape(bsz, s, Q_W)


def setup_inputs(seed: int = 0) -> dict:
    key = jax.random.key(seed)
    ks = jax.random.split(key, 20)
    f32 = jnp.float32
    x = jax.random.normal(ks[0], (BATCH, SEQ, D_MODEL), f32)
    norm_mix_g = 1.0 + 0.05 * jax.random.normal(ks[1], (DEPTH, D_MODEL), f32)
    w_in = jax.random.normal(ks[2], (DEPTH, D_MODEL, IN_W), f32) * D_MODEL ** -0.5
    b_gate = 0.01 * jax.random.normal(ks[3], (DEPTH, N_BRANCH * D_MODEL), f32)
    conv_w = jax.random.normal(ks[4], (DEPTH, CONV_WIDTH, LRU_WIDTH), f32) * CONV_WIDTH ** -0.5
    conv_b = 0.01 * jax.random.normal(ks[5], (DEPTH, LRU_WIDTH), f32)
    u = jax.random.uniform(ks[6], (DEPTH, N_DIR, LRU_WIDTH), f32, minval=0.9, maxval=0.999)
    p = u ** (1.0 / RGLRU_C)
    lru_lambda = jnp.log(p) - jnp.log1p(-p)
    lru_wa = jax.random.normal(ks[7], (DEPTH, N_DIR, LRU_HEADS, LRU_BLOCK, LRU_BLOCK), f32) * LRU_BLOCK ** -0.5
    lru_ba = 0.01 * jax.random.normal(ks[8], (DEPTH, N_DIR, LRU_WIDTH), f32)
    lru_wx = jax.random.normal(ks[9], (DEPTH, N_DIR, LRU_HEADS, LRU_BLOCK, LRU_BLOCK), f32) * LRU_BLOCK ** -0.5
    lru_bx = 0.01 * jax.random.normal(ks[10], (DEPTH, N_DIR, LRU_WIDTH), f32)
    attn_sink = 0.5 * jax.random.normal(ks[11], (DEPTH, N_HEADS), f32)
    w_out = jax.random.normal(ks[12], (DEPTH, D_MODEL, D_MODEL), f32) * D_MODEL ** -0.5
    norm_ffn_g = 1.0 + 0.05 * jax.random.normal(ks[13], (DEPTH, D_MODEL), f32)
    w_ffn_in = jax.random.normal(ks[14], (DEPTH, D_MODEL, 2 * D_FF), f32) * D_MODEL ** -0.5
    w_ffn_out = jax.random.normal(ks[15], (DEPTH, D_FF, D_MODEL), f32) * D_FF ** -0.5
    norm_final_g = 1.0 + 0.05 * jax.random.normal(ks[16], (D_MODEL,), f32)
    return {"x": x, "norm_mix_g": norm_mix_g, "w_in": w_in, "b_gate": b_gate,
            "conv_w": conv_w, "conv_b": conv_b, "lru_lambda": lru_lambda,
            "lru_wa": lru_wa, "lru_ba": lru_ba, "lru_wx": lru_wx, "lru_bx": lru_bx,
            "attn_sink": attn_sink, "w_out": w_out, "norm_ffn_g": norm_ffn_g,
            "w_ffn_in": w_ffn_in, "w_ffn_out": w_ffn_out, "norm_final_g": norm_final_g}


def reference(x, norm_mix_g, w_in, b_gate, conv_w, conv_b, lru_lambda, lru_wa, lru_ba,
              lru_wx, lru_bx, attn_sink, w_out, norm_ffn_g, w_ffn_in, w_ffn_out, norm_final_g):
    bsz, s, _ = x.shape
    splits = [LRU_WIDTH, 2 * LRU_WIDTH, 2 * LRU_WIDTH + Q_W, 2 * LRU_WIDTH + Q_W + KV_W,
              2 * LRU_WIDTH + Q_W + 2 * KV_W]
    for l in range(DEPTH):
        xn = rmsnorm(x, norm_mix_g[l])
        proj = xn @ w_in[l]
        u, g_lru, q, k, v, z = jnp.split(proj, splits, axis=-1)

        uc = centred_depthwise_conv(u, conv_w[l], conv_b[l]).astype(jnp.float32)
        h_fwd = rg_lru(uc, lru_lambda[l, 0], lru_wa[l, 0], lru_ba[l, 0], lru_wx[l, 0], lru_bx[l, 0], False)
        h_bwd = rg_lru(uc, lru_lambda[l, 1], lru_wa[l, 1], lru_ba[l, 1], lru_wx[l, 1], lru_bx[l, 1], True)
        y_a = ((h_fwd + h_bwd) * jax.nn.gelu(g_lru.astype(jnp.float32))).astype(x.dtype)

        y_b = banded_alibi_sink_attention(
            q.reshape(bsz, s, N_HEADS, HEAD_DIM),
            k.reshape(bsz, s, N_KV_HEADS, HEAD_DIM),
            v.reshape(bsz, s, N_KV_HEADS, HEAD_DIM),
            attn_sink[l]).astype(x.dtype)

        gates = jax.nn.sigmoid(z + b_gate[l]).reshape(bsz, s, N_BRANCH, D_MODEL)
        merged = gates[:, :, 0] * y_a + gates[:, :, 1] * y_b
        x = x + merged @ w_out[l]

        xn2 = rmsnorm(x, norm_ffn_g[l])
        gu = xn2 @ w_ffn_in[l]
        ff_gate, ff_up = jnp.split(gu, [D_FF], axis=-1)
        x = x + (jax.nn.silu(ff_gate) * ff_up) @ w_ffn_out[l]
    return rmsnorm(x, norm_final_g)
```

```python
import functools
import math

import numpy as np
import jax
import jax.numpy as jnp
from jax import lax
from jax.experimental import pallas as pl
from jax.experimental.pallas import tpu as pltpu

F32 = jnp.float32
BF16 = jnp.bfloat16

D_MODEL = 1024
LRU_HEADS = 16
LRU_BLOCK = D_MODEL // LRU_HEADS
CONV_WIDTH = 4
CONV_LEFT = 2
RGLRU_C = 8.0
N_HEADS = 16
N_KV_HEADS = 4
HEAD_DIM = 64
GROUP = N_HEADS // N_KV_HEADS
WINDOW = 128
BLOCK = 128
D_FF = 2816
Q_W = N_HEADS * HEAD_DIM
KV_W = N_KV_HEADS * HEAD_DIM
IN_W = 2 * D_MODEL + Q_W + 2 * KV_W + 2 * D_MODEL
EPS = 1e-6
NEG_INF = -1e30

LANES = 128
SUBLANES = 8
VMEM_LIMIT_BYTES = 56 * 1024 * 1024

TM_PROJ = 512
TS_LRU = 512
TQ_ATTN = 512
TM_FFN = 256
LRU_GROUP = 256

_ALIBI_SLOPES = [float(v) for v in np.exp2(
    -8.0 * (np.arange(N_HEADS, dtype=np.float32) + 1.0) / N_HEADS).astype(np.float32)]


def _sigmoid(x):
    return 0.5 * jnp.tanh(0.5 * x) + 0.5


def _rmsnorm(x, g):
    return x * lax.rsqrt(jnp.mean(x * x, axis=-1, keepdims=True) + EPS) * g


def _resident(shape):
    nd = len(shape)
    return pl.BlockSpec(shape, lambda *_: (0,) * nd, pipeline_mode=pl.Buffered(1))


def _in_proj_kernel(x_ref, g_ref, w_ref, u_ref, gl_ref, q_ref, k_ref, v_ref, z_ref):
    xn = _rmsnorm(x_ref[...], g_ref[...]).astype(BF16)

    def proj(lo, hi):
        return jnp.dot(xn, w_ref[:, lo:hi], preferred_element_type=F32)

    o = 0
    u_ref[...] = proj(o, o + D_MODEL); o += D_MODEL
    gl_ref[...] = proj(o, o + D_MODEL); o += D_MODEL
    q_ref[...] = (proj(o, o + Q_W) * (HEAD_DIM ** -0.5)).astype(BF16); o += Q_W
    k_ref[...] = proj(o, o + KV_W).astype(BF16); o += KV_W
    v_ref[...] = proj(o, o + KV_W).astype(BF16); o += KV_W
    z_ref[...] = proj(o, o + 2 * D_MODEL)


def _in_proj(x2, g, w):
    t = x2.shape[0]
    tm = TM_PROJ
    row = lambda w_: pl.BlockSpec((tm, w_), lambda i: (i, 0))
    return pl.pallas_call(
        _in_proj_kernel,
        grid=(t // tm,),
        in_specs=[row(D_MODEL), _resident((1, D_MODEL)), _resident((D_MODEL, IN_W))],
        out_specs=[row(D_MODEL), row(D_MODEL), row(Q_W), row(KV_W), row(KV_W), row(2 * D_MODEL)],
        out_shape=[
            jax.ShapeDtypeStruct((t, D_MODEL), F32),
            jax.ShapeDtypeStruct((t, D_MODEL), F32),
            jax.ShapeDtypeStruct((t, Q_W), BF16),
            jax.ShapeDtypeStruct((t, KV_W), BF16),
            jax.ShapeDtypeStruct((t, KV_W), BF16),
            jax.ShapeDtypeStruct((t, 2 * D_MODEL), F32),
        ],
        compiler_params=pltpu.CompilerParams(
            dimension_semantics=("arbitrary",), vmem_limit_bytes=VMEM_LIMIT_BYTES),
        name="in_proj",
    )(x2, g, w)


def _lru_kernel(reverse, fuse, nblk, *refs):
    if fuse:
        (um_ref, up_ref, un_ref, cw_ref, cb_ref, lam_ref, wg_ref, ba_ref, bx_ref,
         ho_ref, g_ref, out_ref, uext_ref, uc_ref, ra_ref, ix_ref, carry_ref) = refs
    else:
        (um_ref, up_ref, un_ref, cw_ref, cb_ref, lam_ref, wg_ref, ba_ref, bx_ref,
         out_ref, uext_ref, uc_ref, ra_ref, ix_ref, carry_ref) = refs
    ts = um_ref.shape[0]
    j = pl.program_id(1)
    jb = (nblk - 1 - j) if reverse else j

    @pl.when(j == 0)
    def _():
        carry_ref[...] = jnp.zeros_like(carry_ref)

    uext_ref[0:SUBLANES, :] = jnp.where(jb > 0, up_ref[...], 0.0)
    uext_ref[SUBLANES:SUBLANES + ts, :] = um_ref[...]
    uext_ref[SUBLANES + ts:, :] = jnp.where(jb < nblk - 1, un_ref[...], 0.0)

    rc = 64
    for c in range(ts // rc):
        base = c * rc + SUBLANES - CONV_LEFT
        acc = uext_ref[base:base + rc, :] * cw_ref[0:1, :]
        for k in range(1, CONV_WIDTH):
            acc = acc + uext_ref[base + k:base + k + rc, :] * cw_ref[k:k + 1, :]
        uc_ref[c * rc:(c + 1) * rc, :] = acc + cb_ref[...]

    for gi in range(D_MODEL // LRU_GROUP):
        cols = slice(gi * LRU_GROUP, (gi + 1) * LRU_GROUP)
        pre = jnp.dot(uc_ref[:, cols].astype(BF16), wg_ref[gi], preferred_element_type=F32)
        ra_ref[:, cols] = pre[:, :LRU_GROUP]
        ix_ref[:, cols] = pre[:, LRU_GROUP:]

    lam = lam_ref[...]
    decay = -RGLRU_C * (jnp.maximum(-lam, 0.0) + jnp.log1p(jnp.exp(-jnp.abs(lam))))
    ba = ba_ref[...]
    bx = bx_ref[...]
    sub = lax.broadcasted_iota(jnp.int32, (SUBLANES, D_MODEL), 0)
    nt = ts // SUBLANES

    def tile(k, carry):
        kk = (nt - 1 - k) if reverse else k
        rows = pl.ds(pl.multiple_of(kk * SUBLANES, SUBLANES), SUBLANES)
        uc = uc_ref[rows, :]
        r = _sigmoid(ra_ref[rows, :] + ba)
        i = _sigmoid(ix_ref[rows, :] + bx)
        log_a = decay * r
        a = jnp.exp(log_a)
        beta = jnp.sqrt(jnp.maximum(-jnp.tanh(log_a) * (a * a + 1.0), 0.0))
        b = beta * (i * uc)
        for d in (1, 2, 4):
            if reverse:
                keep = sub < SUBLANES - d
                shift = SUBLANES - d
            else:
                keep = sub >= d
                shift = d
            a_n = jnp.where(keep, pltpu.roll(a, shift, axis=0), 1.0)
            b_n = jnp.where(keep, pltpu.roll(b, shift, axis=0), 0.0)
            b = a * b_n + b
            a = a * a_n
        h = a * carry + b
        if fuse:
            g = g_ref[rows, :]
            gelu = 0.5 * g * (1.0 + jnp.tanh(math.sqrt(2.0 / math.pi) * (g + 0.044715 * (g * g * g))))
            out_ref[rows, :] = (ho_ref[rows, :] + h) * gelu
        else:
            out_ref[rows, :] = h
        edge = h[0:1, :] if reverse else h[SUBLANES - 1:SUBLANES, :]
        return jnp.broadcast_to(edge, (SUBLANES, D_MODEL))

    carry_ref[...] = lax.fori_loop(0, nt, tile, carry_ref[...], unroll=2)


def _lru(reverse, u3, cw, cb, lam, wg, ba, bx, h_other=None, g3=None):
    bsz, s, _ = u3.shape
    ts = TS_LRU
    nblk = s // ts
    hb = ts // SUBLANES
    nh = s // SUBLANES
    fuse = h_other is not None
    tidx = (lambda j: nblk - 1 - j) if reverse else (lambda j: j)
    main = pl.BlockSpec((None, ts, D_MODEL), lambda b, j: (b, tidx(j), 0))
    prev = pl.BlockSpec((None, SUBLANES, D_MODEL),
                        lambda b, j: (b, jnp.maximum(tidx(j) * hb - 1, 0), 0))
    nxt = pl.BlockSpec((None, SUBLANES, D_MODEL),
                       lambda b, j: (b, jnp.minimum((tidx(j) + 1) * hb, nh - 1), 0))
    in_specs = [main, prev, nxt,
                _resident((CONV_WIDTH, D_MODEL)), _resident((1, D_MODEL)), _resident((1, D_MODEL)),
                _resident(wg.shape), _resident((1, D_MODEL)), _resident((1, D_MODEL))]
    args = [u3, u3, u3, cw, cb, lam, wg, ba, bx]
    if fuse:
        in_specs += [main, main]
        args += [h_other, g3]
    return pl.pallas_call(
        functools.partial(_lru_kernel, reverse, fuse, nblk),
        grid=(bsz, nblk),
        in_specs=in_specs,
        out_specs=main,
        out_shape=jax.ShapeDtypeStruct((bsz, s, D_MODEL), F32),
        scratch_shapes=[
            pltpu.VMEM((ts + 2 * SUBLANES, D_MODEL), F32),
            pltpu.VMEM((ts, D_MODEL), F32),
            pltpu.VMEM((ts, D_MODEL), F32),
            pltpu.VMEM((ts, D_MODEL), F32),
            pltpu.VMEM((SUBLANES, D_MODEL), F32),
        ],
        compiler_params=pltpu.CompilerParams(
            dimension_semantics=("arbitrary", "arbitrary"), vmem_limit_bytes=VMEM_LIMIT_BYTES),
        name="lru_bwd" if reverse else "lru_fwd",
    )(*args)


def _attn_kernel(seq, sink_ref, q_ref, kp_ref, km_ref, kn_ref, vp_ref, vm_ref, vn_ref,
                 o_ref, kz_ref, vz_ref):
    tq = q_ref.shape[0]
    tk = tq + 2 * BLOCK
    nsub = tq // BLOCK
    j = pl.program_id(1)

    low = lax.broadcasted_iota(jnp.int32, (tk, LANES), 1) < HEAD_DIM
    for src, dst in (((kp_ref, km_ref, kn_ref), kz_ref), ((vp_ref, vm_ref, vn_ref), vz_ref)):
        for t in range(KV_W // LANES):
            cols = slice(t * LANES, (t + 1) * LANES)
            full = jnp.concatenate([r[:, cols] for r in src], axis=0).astype(F32)
            swapped = pltpu.roll(full, HEAD_DIM, axis=1)
            dst[4 * t + 0] = jnp.where(low, full, 0.0).astype(BF16)
            dst[4 * t + 1] = jnp.where(low, 0.0, swapped).astype(BF16)
            dst[4 * t + 2] = jnp.where(low, swapped, 0.0).astype(BF16)
            dst[4 * t + 3] = jnp.where(low, 0.0, full).astype(BF16)

    row = lax.broadcasted_iota(jnp.int32, (BLOCK, 3 * BLOCK), 0)
    col = lax.broadcasted_iota(jnp.int32, (BLOCK, 3 * BLOCK), 1)
    dist = jnp.abs(row + BLOCK - col)
    absd = dist.astype(F32)

    def sub_block(qi, _):
        q0 = pl.multiple_of(qi * BLOCK, BLOCK)
        kpos = (j * nsub + qi - 1) * BLOCK + col
        valid = (dist <= WINDOW) & (kpos >= 0) & (kpos < seq)
        for g in range(N_KV_HEADS):
            for pp in range(GROUP // 2):
                tile_cols = slice((2 * g + pp) * LANES, (2 * g + pp + 1) * LANES)
                qt = q_ref[pl.ds(q0, BLOCK), tile_cols]
                acc = None
                for half in range(2):
                    h = GROUP * g + 2 * pp + half
                    kz = kz_ref[2 * g + half, pl.ds(q0, 3 * BLOCK), :]
                    sc = lax.dot_general(qt, kz, (((1,), (1,)), ((), ())),
                                         preferred_element_type=F32)
                    sc = jnp.where(valid, sc - _ALIBI_SLOPES[h] * absd, NEG_INF)
                    sink = sink_ref[h]
                    m = jnp.maximum(jnp.max(sc, axis=-1, keepdims=True), sink)
                    p = jnp.exp(sc - m)
                    denom = jnp.sum(p, axis=-1, keepdims=True) + jnp.exp(sink - m)
                    vz = vz_ref[2 * g + half, pl.ds(q0, 3 * BLOCK), :]
                    o = jnp.dot(p.astype(BF16), vz, preferred_element_type=F32) * (1.0 / denom)
                    acc = o if acc is None else acc + o
                o_ref[pl.ds(q0, BLOCK), tile_cols] = acc
        return 0

    lax.fori_loop(0, nsub, sub_block, 0)


def _attention(q3, k3, v3, sink):
    bsz, s, _ = q3.shape
    tq = TQ_ATTN
    nq = s // tq
    hb = tq // BLOCK
    nb = s // BLOCK
    kv_main = pl.BlockSpec((None, tq, KV_W), lambda b, j: (b, j, 0))
    kv_prev = pl.BlockSpec((None, BLOCK, KV_W), lambda b, j: (b, jnp.maximum(j * hb - 1, 0), 0))
    kv_next = pl.BlockSpec((None, BLOCK, KV_W), lambda b, j: (b, jnp.minimum((j + 1) * hb, nb - 1), 0))
    qo = pl.BlockSpec((None, tq, Q_W), lambda b, j: (b, j, 0))
    return pl.pallas_call(
        functools.partial(_attn_kernel, s),
        grid=(bsz, nq),
        in_specs=[pl.BlockSpec(memory_space=pltpu.SMEM), qo,
                  kv_prev, kv_main, kv_next, kv_prev, kv_main, kv_next],
        out_specs=qo,
        out_shape=jax.ShapeDtypeStruct((bsz, s, Q_W), F32),
        scratch_shapes=[
            pltpu.VMEM((2 * N_KV_HEADS, tq + 2 * BLOCK, LANES), BF16),
            pltpu.VMEM((2 * N_KV_HEADS, tq + 2 * BLOCK, LANES), BF16),
        ],
        compiler_params=pltpu.CompilerParams(
            dimension_semantics=("arbitrary", "arbitrary"), vmem_limit_bytes=VMEM_LIMIT_BYTES),
        name="attention",
    )(sink, q3, k3, k3, k3, v3, v3, v3)


def _ffn_kernel(x_ref, ya_ref, yb_ref, z_ref, bg_ref, wo_ref, g2_ref, wi_ref, wd_ref, g3_ref, o_ref):
    gates = _sigmoid(z_ref[...] + bg_ref[...])
    merged = gates[:, :D_MODEL] * ya_ref[...] + gates[:, D_MODEL:] * yb_ref[...]
    x1 = x_ref[...] + jnp.dot(merged.astype(BF16), wo_ref[...], preferred_element_type=F32)
    xn = _rmsnorm(x1, g2_ref[...]).astype(BF16)
    gate = jnp.dot(xn, wi_ref[:, :D_FF], preferred_element_type=F32)
    up = jnp.dot(xn, wi_ref[:, D_FF:], preferred_element_type=F32)
    act = (gate * _sigmoid(gate) * up).astype(BF16)
    x2 = x1 + jnp.dot(act, wd_ref[...], preferred_element_type=F32)
    o_ref[...] = _rmsnorm(x2, g3_ref[...])


def _ffn(x2, ya, yb, z, bg, wo, g2, wi, wd, g3):
    t = x2.shape[0]
    tm = TM_FFN
    row = lambda w_: pl.BlockSpec((tm, w_), lambda i: (i, 0))
    return pl.pallas_call(
        _ffn_kernel,
        grid=(t // tm,),
        in_specs=[row(D_MODEL), row(D_MODEL), row(D_MODEL), row(2 * D_MODEL),
                  _resident((1, 2 * D_MODEL)), _resident((D_MODEL, D_MODEL)), _resident((1, D_MODEL)),
                  _resident((D_MODEL, 2 * D_FF)), _resident((D_FF, D_MODEL)), _resident((1, D_MODEL))],
        out_specs=row(D_MODEL),
        out_shape=jax.ShapeDtypeStruct((t, D_MODEL), F32),
        compiler_params=pltpu.CompilerParams(
            dimension_semantics=("arbitrary",), vmem_limit_bytes=VMEM_LIMIT_BYTES),
        name="merge_ffn",
    )(x2, ya, yb, z, bg, wo, g2, wi, wd, g3)


def _gate_weights(wa, wx):
    def blockdiag(w):
        w = w.reshape(D_MODEL // LRU_GROUP, LRU_GROUP // LRU_BLOCK, LRU_BLOCK, LRU_BLOCK)
        eye = jnp.eye(LRU_GROUP // LRU_BLOCK, dtype=w.dtype)
        return jnp.einsum("ghij,hk->ghikj", w, eye).reshape(-1, LRU_GROUP, LRU_GROUP)
    return jnp.concatenate([blockdiag(wa), blockdiag(wx)], axis=-1).astype(BF16)


def kernel(x, norm_mix_g, w_in, b_gate, conv_w, conv_b, lru_lambda, lru_wa, lru_ba, lru_wx, lru_bx,
           attn_sink, w_out, norm_ffn_g, w_ffn_in, w_ffn_out, norm_final_g):
    bsz, s, d = x.shape
    depth = w_in.shape[0]
    assert depth == 1, "the merge/ffn kernel applies the final RMSNorm, so it must be the last layer"
    t = bsz * s
    row = lambda a: a.reshape(1, -1)
    x2 = x.reshape(t, d)
    for l in range(depth):
        u, g_lru, q, k, v, z = _in_proj(x2, row(norm_mix_g[l]), w_in[l].astype(BF16))
        seq = lambda a: a.reshape(bsz, s, a.shape[-1])
        u3 = seq(u)
        lru_args = lambda dr: (conv_w[l], row(conv_b[l]), row(lru_lambda[l, dr]),
                               _gate_weights(lru_wa[l, dr], lru_wx[l, dr]),
                               row(lru_ba[l, dr]), row(lru_bx[l, dr]))
        h_fwd = _lru(False, u3, *lru_args(0))
        y_a = _lru(True, u3, *lru_args(1), h_other=h_fwd, g3=seq(g_lru))
        y_b = _attention(seq(q), seq(k), seq(v), attn_sink[l])
        x2 = _ffn(x2, y_a.reshape(t, d), y_b.reshape(t, d), z, row(b_gate[l]),
                  w_out[l].astype(BF16), row(norm_ffn_g[l]), w_ffn_in[l].astype(BF16),
                  w_ffn_out[l].astype(BF16), row(norm_final_g))
    return x2.reshape(bsz, s, d)
```

```python
import functools
import math

import numpy as np
import jax
import jax.numpy as jnp
from jax import lax
from jax.experimental import pallas as pl
from jax.experimental.pallas import tpu as pltpu

F32 = jnp.float32
BF16 = jnp.bfloat16

D_MODEL = 1024
LRU_HEADS = 16
LRU_BLOCK = D_MODEL // LRU_HEADS
CONV_WIDTH = 4
CONV_LEFT = 2
RGLRU_C = 8.0
N_HEADS = 16
N_KV_HEADS = 4
HEAD_DIM = 64
GROUP = N_HEADS // N_KV_HEADS
WINDOW = 128
BLOCK = 128
D_FF = 2816
Q_W = N_HEADS * HEAD_DIM
KV_W = N_KV_HEADS * HEAD_DIM
IN_W = 2 * D_MODEL + Q_W + 2 * KV_W + 2 * D_MODEL
EPS = 1e-6
NEG_INF = -1e30

LANES = 128
SUBLANES = 8
VMEM_LIMIT_BYTES = 56 * 1024 * 1024

TM_PROJ = 512
TS_LRU = 512
TQ_ATTN = 512
TM_FFN = 256
LRU_GROUP = 256

_ALIBI_SLOPES = [float(v) for v in np.exp2(
    -8.0 * (np.arange(N_HEADS, dtype=np.float32) + 1.0) / N_HEADS).astype(np.float32)]


def _sigmoid(x):
    return 0.5 * jnp.tanh(0.5 * x) + 0.5


def _rmsnorm(x, g):
    return x * lax.rsqrt(jnp.mean(x * x, axis=-1, keepdims=True) + EPS) * g


def _resident(shape):
    nd = len(shape)
    return pl.BlockSpec(shape, lambda *_: (0,) * nd, pipeline_mode=pl.Buffered(1))


def _in_proj_kernel(x_ref, g_ref, w_ref, u_ref, gl_ref, q_ref, k_ref, v_ref, z_ref):
    xn = _rmsnorm(x_ref[...], g_ref[...]).astype(BF16)

    def proj(lo, hi):
        return jnp.dot(xn, w_ref[:, lo:hi], preferred_element_type=F32)

    o = 0
    u_ref[...] = proj(o, o + D_MODEL); o += D_MODEL
    gl_ref[...] = proj(o, o + D_MODEL); o += D_MODEL
    q_ref[...] = (proj(o, o + Q_W) * (HEAD_DIM ** -0.5)).astype(BF16); o += Q_W
    k_ref[...] = proj(o, o + KV_W).astype(BF16); o += KV_W
    v_ref[...] = proj(o, o + KV_W).astype(BF16); o += KV_W
    z_ref[...] = proj(o, o + 2 * D_MODEL)


def _in_proj(x2, g, w):
    t = x2.shape[0]
    tm = TM_PROJ
    row = lambda w_: pl.BlockSpec((tm, w_), lambda i: (i, 0))
    return pl.pallas_call(
        _in_proj_kernel,
        grid=(t // tm,),
        in_specs=[row(D_MODEL), _resident((1, D_MODEL)), _resident((D_MODEL, IN_W))],
        out_specs=[row(D_MODEL), row(D_MODEL), row(Q_W), row(KV_W), row(KV_W), row(2 * D_MODEL)],
        out_shape=[
            jax.ShapeDtypeStruct((t, D_MODEL), F32),
            jax.ShapeDtypeStruct((t, D_MODEL), F32),
            jax.ShapeDtypeStruct((t, Q_W), BF16),
            jax.ShapeDtypeStruct((t, KV_W), BF16),
            jax.ShapeDtypeStruct((t, KV_W), BF16),
            jax.ShapeDtypeStruct((t, 2 * D_MODEL), F32),
        ],
        compiler_params=pltpu.CompilerParams(
            dimension_semantics=("arbitrary",), vmem_limit_bytes=VMEM_LIMIT_BYTES),
        name="in_proj",
    )(x2, g, w)


def _lru_kernel(reverse, fuse, nblk, *refs):
    if fuse:
        (um_ref, up_ref, un_ref, cw_ref, cb_ref, lam_ref, wg_ref, ba_ref, bx_ref,
         ho_ref, g_ref, out_ref, uext_ref, uc_ref, ra_ref, ix_ref, carry_ref) = refs
    else:
        (um_ref, up_ref, un_ref, cw_ref, cb_ref, lam_ref, wg_ref, ba_ref, bx_ref,
         out_ref, uext_ref, uc_ref, ra_ref, ix_ref, carry_ref) = refs
    ts = um_ref.shape[0]
    j = pl.program_id(1)
    jb = (nblk - 1 - j) if reverse else j

    @pl.when(j == 0)
    def _():
        carry_ref[...] = jnp.zeros_like(carry_ref)

    uext_ref[0:SUBLANES, :] = jnp.where(jb > 0, up_ref[...], 0.0)
    uext_ref[SUBLANES:SUBLANES + ts, :] = um_ref[...]
    uext_ref[SUBLANES + ts:, :] = jnp.where(jb < nblk - 1, un_ref[...], 0.0)

    rc = 64
    for c in range(ts // rc):
        base = c * rc + SUBLANES - CONV_LEFT
        acc = uext_ref[base:base + rc, :] * cw_ref[0:1, :]
        for k in range(1, CONV_WIDTH):
            acc = acc + uext_ref[base + k:base + k + rc, :] * cw_ref[k:k + 1, :]
        uc_ref[c * rc:(c + 1) * rc, :] = acc + cb_ref[...]

    for gi in range(D_MODEL // LRU_GROUP):
        cols = slice(gi * LRU_GROUP, (gi + 1) * LRU_GROUP)
        pre = jnp.dot(uc_ref[:, cols].astype(BF16), wg_ref[gi], preferred_element_type=F32)
        ra_ref[:, cols] = pre[:, :LRU_GROUP]
        ix_ref[:, cols] = pre[:, LRU_GROUP:]

    lam = lam_ref[...]
    decay = -RGLRU_C * (jnp.maximum(-lam, 0.0) + jnp.log1p(jnp.exp(-jnp.abs(lam))))
    ba = ba_ref[...]
    bx = bx_ref[...]
    sub = lax.broadcasted_iota(jnp.int32, (SUBLANES, D_MODEL), 0)
    nt = ts // SUBLANES

    def tile(k, carry):
        kk = (nt - 1 - k) if reverse else k
        rows = pl.ds(pl.multiple_of(kk * SUBLANES, SUBLANES), SUBLANES)
        uc = uc_ref[rows, :]
        r = _sigmoid(ra_ref[rows, :] + ba)
        i = _sigmoid(ix_ref[rows, :] + bx)
        log_a = decay * r
        a = jnp.exp(log_a)
        beta = jnp.sqrt(jnp.maximum(-jnp.tanh(log_a) * (a * a + 1.0), 0.0))
        b = beta * (i * uc)
        for d in (1, 2, 4):
            if reverse:
                keep = sub < SUBLANES - d
                shift = SUBLANES - d
            else:
                keep = sub >= d
                shift = d
            a_n = jnp.where(keep, pltpu.roll(a, shift, axis=0), 1.0)
            b_n = jnp.where(keep, pltpu.roll(b, shift, axis=0), 0.0)
            b = a * b_n + b
            a = a * a_n
        h = a * carry + b
        if fuse:
            g = g_ref[rows, :]
            gelu = 0.5 * g * (1.0 + jnp.tanh(math.sqrt(2.0 / math.pi) * (g + 0.044715 * (g * g * g))))
            out_ref[rows, :] = (ho_ref[rows, :] + h) * gelu
        else:
            out_ref[rows, :] = h
        edge = h[0:1, :] if reverse else h[SUBLANES - 1:SUBLANES, :]
        return jnp.broadcast_to(edge, (SUBLANES, D_MODEL))

    carry_ref[...] = lax.fori_loop(0, nt, tile, carry_ref[...], unroll=2)


def _lru(reverse, u3, cw, cb, lam, wg, ba, bx, h_other=None, g3=None):
    bsz, s, _ = u3.shape
    ts = TS_LRU
    nblk = s // ts
    hb = ts // SUBLANES
    nh = s // SUBLANES
    fuse = h_other is not None
    tidx = (lambda j: nblk - 1 - j) if reverse else (lambda j: j)
    main = pl.BlockSpec((None, ts, D_MODEL), lambda b, j: (b, tidx(j), 0))
    prev = pl.BlockSpec((None, SUBLANES, D_MODEL),
                        lambda b, j: (b, jnp.maximum(tidx(j) * hb - 1, 0), 0))
    nxt = pl.BlockSpec((None, SUBLANES, D_MODEL),
                       lambda b, j: (b, jnp.minimum((tidx(j) + 1) * hb, nh - 1), 0))
    in_specs = [main, prev, nxt,
                _resident((CONV_WIDTH, D_MODEL)), _resident((1, D_MODEL)), _resident((1, D_MODEL)),
                _resident(wg.shape), _resident((1, D_MODEL)), _resident((1, D_MODEL))]
    args = [u3, u3, u3, cw, cb, lam, wg, ba, bx]
    if fuse:
        in_specs += [main, main]
        args += [h_other, g3]
    return pl.pallas_call(
        functools.partial(_lru_kernel, reverse, fuse, nblk),
        grid=(bsz, nblk),
        in_specs=in_specs,
        out_specs=main,
        out_shape=jax.ShapeDtypeStruct((bsz, s, D_MODEL), F32),
        scratch_shapes=[
            pltpu.VMEM((ts + 2 * SUBLANES, D_MODEL), F32),
            pltpu.VMEM((ts, D_MODEL), F32),
            pltpu.VMEM((ts, D_MODEL), F32),
            pltpu.VMEM((ts, D_MODEL), F32),
            pltpu.VMEM((SUBLANES, D_MODEL), F32),
        ],
        compiler_params=pltpu.CompilerParams(
            dimension_semantics=("arbitrary", "arbitrary"), vmem_limit_bytes=VMEM_LIMIT_BYTES),
        name="lru_bwd" if reverse else "lru_fwd",
    )(*args)


_N_BIAS = 3 * N_HEADS + 1
_MASKED_TILE = 3 * N_HEADS


def _attn_kernel(nb, sink_ref, q_ref, kp_ref, km_ref, kn_ref, vp_ref, vm_ref, vn_ref,
                 o_ref, kz_ref, vab_ref, bias_ref, s_ref, m_ref, p_ref):
    tq = q_ref.shape[0]
    tk = tq + 2 * BLOCK
    nsub = tq // BLOCK
    j = pl.program_id(1)

    @pl.when((pl.program_id(0) == 0) & (j == 0))
    def _():
        row = lax.broadcasted_iota(jnp.int32, (BLOCK, BLOCK), 0)
        col = lax.broadcasted_iota(jnp.int32, (BLOCK, BLOCK), 1)
        bias_ref[_MASKED_TILE] = jnp.ones((BLOCK, BLOCK), F32)
        for c in range(3):
            dist = jnp.abs(row + (1 - c) * BLOCK - col)
            absd = dist.astype(F32)
            for h in range(N_HEADS):
                bias_ref[3 * h + c] = jnp.where(dist <= WINDOW, -_ALIBI_SLOPES[h] * absd, 1.0)

    low = lax.broadcasted_iota(jnp.int32, (tk, LANES), 1) < HEAD_DIM
    ones_lo = jnp.where(low, 1.0, 0.0).astype(BF16)
    ones_hi = jnp.where(low, 0.0, 1.0).astype(BF16)
    for t in range(KV_W // LANES):
        cols = slice(t * LANES, (t + 1) * LANES)
        full = jnp.concatenate([r[:, cols] for r in (kp_ref, km_ref, kn_ref)], axis=0).astype(F32)
        swapped = pltpu.roll(full, HEAD_DIM, axis=1)
        kz_ref[4 * t + 0] = jnp.where(low, full, 0.0).astype(BF16)
        kz_ref[4 * t + 1] = jnp.where(low, 0.0, swapped).astype(BF16)
        kz_ref[4 * t + 2] = jnp.where(low, swapped, 0.0).astype(BF16)
        kz_ref[4 * t + 3] = jnp.where(low, 0.0, full).astype(BF16)
        full = jnp.concatenate([r[:, cols] for r in (vp_ref, vm_ref, vn_ref)], axis=0).astype(F32)
        swapped = pltpu.roll(full, HEAD_DIM, axis=1)
        vab_ref[4 * t + 0, :, :LANES] = jnp.where(low, full, 0.0).astype(BF16)
        vab_ref[4 * t + 1, :, :LANES] = jnp.where(low, 0.0, swapped).astype(BF16)
        vab_ref[4 * t + 2, :, :LANES] = jnp.where(low, swapped, 0.0).astype(BF16)
        vab_ref[4 * t + 3, :, :LANES] = jnp.where(low, 0.0, full).astype(BF16)
        for i in range(4):
            vab_ref[4 * t + i, :, LANES:] = ones_lo if i % 2 == 0 else ones_hi

    low_q = lax.broadcasted_iota(jnp.int32, (BLOCK, LANES), 1) < HEAD_DIM

    def sub_block(qi, _):
        q0 = pl.multiple_of(qi * BLOCK, BLOCK)
        blk = j * nsub + qi
        win = pl.ds(q0, 3 * BLOCK)
        for h in range(N_HEADS):
            g, pp, half = h // GROUP, (h % GROUP) // 2, h % 2
            qt = q_ref[pl.ds(q0, BLOCK), (2 * g + pp) * LANES:(2 * g + pp + 1) * LANES]
            sc = lax.dot_general(qt, kz_ref[2 * g + half, win, :], (((1,), (1,)), ((), ())),
                                 preferred_element_type=F32)
            tiles = []
            for c in range(3):
                t = sc[:, c * BLOCK:(c + 1) * BLOCK]
                if c == 1:
                    t = t + bias_ref[3 * h + 1]
                else:
                    edge = (blk == 0) if c == 0 else (blk == nb - 1)
                    b = bias_ref[jnp.where(edge, _MASKED_TILE, 3 * h + c)]
                    t = jnp.where(b > 0.0, NEG_INF, t + b)
                s_ref[h, :, c * BLOCK:(c + 1) * BLOCK] = t
                tiles.append(t)
            mx = jnp.max(jnp.maximum(jnp.maximum(tiles[0], tiles[1]), tiles[2]), axis=-1, keepdims=True)
            m_ref[h] = jnp.broadcast_to(jnp.maximum(mx, sink_ref[h]), (BLOCK, LANES))
        for h in range(N_HEADS):
            m = m_ref[h]
            for c in range(3):
                cs = slice(c * BLOCK, (c + 1) * BLOCK)
                p_ref[h, :, cs] = jnp.exp(s_ref[h, :, cs] - m).astype(BF16)
        for pr in range(N_HEADS // 2):
            g = pr // 2
            h0, h1 = 2 * pr, 2 * pr + 1
            o = (jnp.dot(p_ref[h0], vab_ref[2 * g, win, :], preferred_element_type=F32)
                 + jnp.dot(p_ref[h1], vab_ref[2 * g + 1, win, :], preferred_element_type=F32))
            e = jnp.where(low_q, jnp.exp(sink_ref[h0] - m_ref[h0]), jnp.exp(sink_ref[h1] - m_ref[h1]))
            o_ref[pl.ds(q0, BLOCK), pr * LANES:(pr + 1) * LANES] = o[:, :LANES] * (1.0 / (o[:, LANES:] + e))
        return 0

    lax.fori_loop(0, nsub, sub_block, 0)


def _attention(q3, k3, v3, sink):
    bsz, s, _ = q3.shape
    tq = TQ_ATTN
    nq = s // tq
    hb = tq // BLOCK
    nb = s // BLOCK
    tk = tq + 2 * BLOCK
    kv_main = pl.BlockSpec((None, tq, KV_W), lambda b, j: (b, j, 0))
    kv_prev = pl.BlockSpec((None, BLOCK, KV_W), lambda b, j: (b, jnp.maximum(j * hb - 1, 0), 0))
    kv_next = pl.BlockSpec((None, BLOCK, KV_W), lambda b, j: (b, jnp.minimum((j + 1) * hb, nb - 1), 0))
    qo = pl.BlockSpec((None, tq, Q_W), lambda b, j: (b, j, 0))
    return pl.pallas_call(
        functools.partial(_attn_kernel, nb),
        grid=(bsz, nq),
        in_specs=[pl.BlockSpec(memory_space=pltpu.SMEM), qo,
                  kv_prev, kv_main, kv_next, kv_prev, kv_main, kv_next],
        out_specs=qo,
        out_shape=jax.ShapeDtypeStruct((bsz, s, Q_W), F32),
        scratch_shapes=[
            pltpu.VMEM((2 * N_KV_HEADS, tk, LANES), BF16),
            pltpu.VMEM((2 * N_KV_HEADS, tk, 2 * LANES), BF16),
            pltpu.VMEM((_N_BIAS, BLOCK, BLOCK), F32),
            pltpu.VMEM((N_HEADS, BLOCK, 3 * BLOCK), F32),
            pltpu.VMEM((N_HEADS, BLOCK, LANES), F32),
            pltpu.VMEM((N_HEADS, BLOCK, 3 * BLOCK), BF16),
        ],
        compiler_params=pltpu.CompilerParams(
            dimension_semantics=("arbitrary", "arbitrary"), vmem_limit_bytes=VMEM_LIMIT_BYTES),
        name="attention",
    )(sink, q3, k3, k3, k3, v3, v3, v3)


def _ffn_kernel(x_ref, ya_ref, yb_ref, z_ref, bg_ref, wo_ref, g2_ref, wi_ref, wd_ref, g3_ref, o_ref):
    gates = _sigmoid(z_ref[...] + bg_ref[...])
    merged = gates[:, :D_MODEL] * ya_ref[...] + gates[:, D_MODEL:] * yb_ref[...]
    x1 = x_ref[...] + jnp.dot(merged.astype(BF16), wo_ref[...], preferred_element_type=F32)
    xn = _rmsnorm(x1, g2_ref[...]).astype(BF16)
    gate = jnp.dot(xn, wi_ref[:, :D_FF], preferred_element_type=F32)
    up = jnp.dot(xn, wi_ref[:, D_FF:], preferred_element_type=F32)
    act = (gate * _sigmoid(gate) * up).astype(BF16)
    x2 = x1 + jnp.dot(act, wd_ref[...], preferred_element_type=F32)
    o_ref[...] = _rmsnorm(x2, g3_ref[...])


def _ffn(x2, ya, yb, z, bg, wo, g2, wi, wd, g3):
    t = x2.shape[0]
    tm = TM_FFN
    row = lambda w_: pl.BlockSpec((tm, w_), lambda i: (i, 0))
    return pl.pallas_call(
        _ffn_kernel,
        grid=(t // tm,),
        in_specs=[row(D_MODEL), row(D_MODEL), row(D_MODEL), row(2 * D_MODEL),
                  _resident((1, 2 * D_MODEL)), _resident((D_MODEL, D_MODEL)), _resident((1, D_MODEL)),
                  _resident((D_MODEL, 2 * D_FF)), _resident((D_FF, D_MODEL)), _resident((1, D_MODEL))],
        out_specs=row(D_MODEL),
        out_shape=jax.ShapeDtypeStruct((t, D_MODEL), F32),
        compiler_params=pltpu.CompilerParams(
            dimension_semantics=("arbitrary",), vmem_limit_bytes=VMEM_LIMIT_BYTES),
        name="merge_ffn",
    )(x2, ya, yb, z, bg, wo, g2, wi, wd, g3)


def _gate_weights(wa, wx):
    def blockdiag(w):
        w = w.reshape(D_MODEL // LRU_GROUP, LRU_GROUP // LRU_BLOCK, LRU_BLOCK, LRU_BLOCK)
        eye = jnp.eye(LRU_GROUP // LRU_BLOCK, dtype=w.dtype)
        return jnp.einsum("ghij,hk->ghikj", w, eye).reshape(-1, LRU_GROUP, LRU_GROUP)
    return jnp.concatenate([blockdiag(wa), blockdiag(wx)], axis=-1).astype(BF16)


def kernel(x, norm_mix_g, w_in, b_gate, conv_w, conv_b, lru_lambda, lru_wa, lru_ba, lru_wx, lru_bx,
           attn_sink, w_out, norm_ffn_g, w_ffn_in, w_ffn_out, norm_final_g):
    bsz, s, d = x.shape
    depth = w_in.shape[0]
    assert depth == 1, "the merge/ffn kernel applies the final RMSNorm, so it must be the last layer"
    t = bsz * s
    row = lambda a: a.reshape(1, -1)
    x2 = x.reshape(t, d)
    for l in range(depth):
        u, g_lru, q, k, v, z = _in_proj(x2, row(norm_mix_g[l]), w_in[l].astype(BF16))
        seq = lambda a: a.reshape(bsz, s, a.shape[-1])
        u3 = seq(u)
        lru_args = lambda dr: (conv_w[l], row(conv_b[l]), row(lru_lambda[l, dr]),
                               _gate_weights(lru_wa[l, dr], lru_wx[l, dr]),
                               row(lru_ba[l, dr]), row(lru_bx[l, dr]))
        h_fwd = _lru(False, u3, *lru_args(0))
        y_a = _lru(True, u3, *lru_args(1), h_other=h_fwd, g3=seq(g_lru))
        y_b = _attention(seq(q), seq(k), seq(v), attn_sink[l])
        x2 = _ffn(x2, y_a.reshape(t, d), y_b.reshape(t, d), z, row(b_gate[l]),
                  w_out[l].astype(BF16), row(norm_ffn_g[l]), w_ffn_in[l].astype(BF16),
                  w_ffn_out[l].astype(BF16), row(norm_final_g))
    return x2.reshape(bsz, s, d)
```

```python
import functools
import math

import numpy as np
import jax
import jax.numpy as jnp
from jax import lax
from jax.experimental import pallas as pl
from jax.experimental.pallas import tpu as pltpu

F32 = jnp.float32
BF16 = jnp.bfloat16

D_MODEL = 1024
LRU_HEADS = 16
LRU_BLOCK = D_MODEL // LRU_HEADS
CONV_WIDTH = 4
CONV_LEFT = 2
RGLRU_C = 8.0
N_HEADS = 16
N_KV_HEADS = 4
HEAD_DIM = 64
GROUP = N_HEADS // N_KV_HEADS
WINDOW = 128
BLOCK = 128
D_FF = 2816
Q_W = N_HEADS * HEAD_DIM
KV_W = N_KV_HEADS * HEAD_DIM
IN_W = 2 * D_MODEL + Q_W + 2 * KV_W + 2 * D_MODEL
EPS = 1e-6
NEG_INF = -1e30

LANES = 128
SUBLANES = 8
BF16_ROWS = 16
VMEM_LIMIT_BYTES = 58 * 1024 * 1024

TM_PROJ = 512
TS_LRU = 512
TQ_ATTN = 512
TM_FFN = 256
LRU_GROUP = 256
N_GROUPS = D_MODEL // LRU_GROUP

_ALIBI_SLOPES = [float(v) for v in np.exp2(
    -8.0 * (np.arange(N_HEADS, dtype=np.float32) + 1.0) / N_HEADS).astype(np.float32)]


def _sigmoid(x):
    return 0.5 * jnp.tanh(0.5 * x) + 0.5


def _rmsnorm(x, g):
    return x * lax.rsqrt(jnp.mean(x * x, axis=-1, keepdims=True) + EPS) * g


def _resident(shape):
    nd = len(shape)
    return pl.BlockSpec(shape, lambda *_: (0,) * nd, pipeline_mode=pl.Buffered(1))


def _half_decay(lam):
    return (-0.5 * RGLRU_C) * (jnp.maximum(-lam, 0.0) + jnp.log1p(jnp.exp(-jnp.abs(lam))))


def _gate_terms(pre_r, pre_i, uc, hdecay, hba, hbx):
    log_a = hdecay * jnp.tanh(pre_r + hba) + hdecay
    a = jnp.exp(log_a)
    y = jnp.maximum(-jnp.tanh(log_a) * (a * a + 1.0), 0.0)
    beta = jnp.where(y > 0.0, y * lax.rsqrt(y), 0.0)
    i = 0.5 * jnp.tanh(pre_i + hbx) + 0.5
    return a, beta * (i * uc)


def _tile_scan(a, b, sub, reverse):
    for d in (1, 2, 4):
        if reverse:
            keep = sub < SUBLANES - d
            shift = SUBLANES - d
        else:
            keep = sub >= d
            shift = d
        a_n = jnp.where(keep, pltpu.roll(a, shift, axis=0), 1.0)
        b_n = jnp.where(keep, pltpu.roll(b, shift, axis=0), 0.0)
        b = a * b_n + b
        a = a * a_n
    return a, b


def _in_proj_kernel(nblk, xm_ref, xp_ref, xnx_ref, gn_ref, w_ref, cw_ref, cb_ref, lam_ref, wg_ref,
                    ba_ref, bx_ref,
                    hf_ref, ab_ref, bb_ref, gl_ref, q_ref, k_ref, v_ref, z_ref,
                    xn_ref, ue_ref, uc_ref, pre_ref, carry_ref):
    tm = xm_ref.shape[0]
    jb = pl.program_id(0) % nblk

    @pl.when(jb == 0)
    def _():
        carry_ref[...] = jnp.zeros_like(carry_ref)

    gn = gn_ref[...]
    xn = _rmsnorm(xm_ref[...], gn).astype(BF16)
    xn_ref[...] = xn
    halo = jnp.concatenate([xp_ref[...], xnx_ref[...]], axis=0)
    xh = _rmsnorm(halo, gn).astype(BF16)

    ue = jnp.dot(jnp.concatenate([xn, xh], axis=0), w_ref[:, :D_MODEL], preferred_element_type=F32)
    ue_ref[SUBLANES:SUBLANES + tm, :] = ue[:tm]
    ue_ref[0:SUBLANES, :] = jnp.where(jb > 0, ue[tm:tm + SUBLANES], 0.0)
    ue_ref[SUBLANES + tm:, :] = jnp.where(jb < nblk - 1, ue[tm + SUBLANES:], 0.0)

    rc = 64
    for c in range(tm // rc):
        base = c * rc + SUBLANES - CONV_LEFT
        acc = ue_ref[base:base + rc, :] * cw_ref[0:1, :]
        for k in range(1, CONV_WIDTH):
            acc = acc + ue_ref[base + k:base + k + rc, :] * cw_ref[k:k + 1, :]
        uc_ref[c * rc:(c + 1) * rc, :] = acc + cb_ref[...]

    hdecay = _half_decay(lam_ref[...])
    hba = 0.5 * ba_ref[...]
    hbx = 0.5 * bx_ref[...]
    sub = lax.broadcasted_iota(jnp.int32, (SUBLANES, LRU_GROUP), 0)
    nt = tm // SUBLANES
    G = LRU_GROUP

    pw = 2 * LANES
    todo = []
    o = D_MODEL
    for dst, width, post in ((gl_ref, D_MODEL, None),
                             (q_ref, Q_W, lambda v: (v * (HEAD_DIM ** -0.5)).astype(BF16)),
                             (k_ref, KV_W, lambda v: v.astype(BF16)),
                             (v_ref, KV_W, lambda v: v.astype(BF16)),
                             (z_ref, 2 * D_MODEL, None)):
        for c in range(0, width, pw):
            todo.append((dst, c, o + c, post))
        o += width
    n_tiles = N_GROUPS * nt
    every = -(-n_tiles // len(todo))

    def project_block():
        dst, c, wo, post = todo.pop(0)
        val = jnp.dot(xn_ref[...], w_ref[:, wo:wo + pw], preferred_element_type=F32)
        dst[:, c:c + pw] = val if post is None else post(val)

    done = 0
    for gi in range(N_GROUPS):
        cols = slice(gi * G, (gi + 1) * G)
        buf = gi % 2
        pre_ref[buf] = jnp.dot(uc_ref[:, cols].astype(BF16), wg_ref[gi], preferred_element_type=F32)
        consts = [(hdecay[d:d + 1, cols], hba[d:d + 1, cols], hbx[d:d + 1, cols]) for d in range(2)]
        carry = carry_ref[:, cols]
        for t in range(nt):
            if done % every == 0 and todo:
                project_block()
            done += 1
            rows = slice(t * SUBLANES, (t + 1) * SUBLANES)
            uc = uc_ref[rows, cols]
            a, b = _gate_terms(pre_ref[buf, rows, 0:G], pre_ref[buf, rows, G:2 * G], uc, *consts[0])
            a, b = _tile_scan(a, b, sub, reverse=False)
            h = a * carry + b
            hf_ref[rows, cols] = h
            carry = jnp.broadcast_to(h[SUBLANES - 1:SUBLANES, :], (SUBLANES, G))
            a, b = _gate_terms(pre_ref[buf, rows, 2 * G:3 * G], pre_ref[buf, rows, 3 * G:4 * G], uc,
                               *consts[1])
            ab_ref[rows, cols] = a
            bb_ref[rows, cols] = b
        carry_ref[:, cols] = carry
    while todo:
        project_block()


def _in_proj(x2, seq, gn, w, cw, cb, lam, wg, ba, bx):
    t = x2.shape[0]
    tm = TM_PROJ
    nblk = seq // tm
    hb = tm // SUBLANES
    nh = t // SUBLANES
    row = lambda w_: pl.BlockSpec((tm, w_), lambda i: (i, 0))
    prev = pl.BlockSpec((SUBLANES, D_MODEL), lambda i: (jnp.maximum(i * hb - 1, 0), 0))
    nxt = pl.BlockSpec((SUBLANES, D_MODEL), lambda i: (jnp.minimum((i + 1) * hb, nh - 1), 0))
    f32 = lambda w_: jax.ShapeDtypeStruct((t, w_), F32)
    bf16 = lambda w_: jax.ShapeDtypeStruct((t, w_), BF16)
    return pl.pallas_call(
        functools.partial(_in_proj_kernel, nblk),
        grid=(t // tm,),
        in_specs=[row(D_MODEL), prev, nxt, _resident((1, D_MODEL)), _resident((D_MODEL, IN_W)),
                  _resident((CONV_WIDTH, D_MODEL)), _resident((1, D_MODEL)), _resident((2, D_MODEL)),
                  _resident(wg.shape), _resident((2, D_MODEL)), _resident((2, D_MODEL))],
        out_specs=[row(D_MODEL), row(D_MODEL), row(D_MODEL), row(D_MODEL),
                   row(Q_W), row(KV_W), row(KV_W), row(2 * D_MODEL)],
        out_shape=[f32(D_MODEL), f32(D_MODEL), f32(D_MODEL), f32(D_MODEL),
                   bf16(Q_W), bf16(KV_W), bf16(KV_W), f32(2 * D_MODEL)],
        scratch_shapes=[
            pltpu.VMEM((tm, D_MODEL), BF16),
            pltpu.VMEM((tm + 2 * SUBLANES, D_MODEL), F32),
            pltpu.VMEM((tm, D_MODEL), F32),
            pltpu.VMEM((2, tm, 4 * LRU_GROUP), F32),
            pltpu.VMEM((SUBLANES, D_MODEL), F32),
        ],
        compiler_params=pltpu.CompilerParams(
            dimension_semantics=("arbitrary",), vmem_limit_bytes=VMEM_LIMIT_BYTES),
        name="in_proj",
    )(x2, x2, x2, gn, w, cw, cb, lam, wg, ba, bx)


def _lru_bwd_kernel(a_ref, b_ref, hf_ref, g_ref, out_ref, carry_ref):
    ts = a_ref.shape[0]
    nt = ts // SUBLANES

    @pl.when(pl.program_id(1) == 0)
    def _():
        carry_ref[...] = jnp.zeros_like(carry_ref)

    sub = lax.broadcasted_iota(jnp.int32, (SUBLANES, D_MODEL), 0)

    def tile(k, carry):
        rows = pl.ds(pl.multiple_of((nt - 1 - k) * SUBLANES, SUBLANES), SUBLANES)
        a, b = _tile_scan(a_ref[rows, :], b_ref[rows, :], sub, reverse=True)
        h = a * carry + b
        g = g_ref[rows, :]
        gelu = 0.5 * g * (1.0 + jnp.tanh(math.sqrt(2.0 / math.pi) * (g + 0.044715 * (g * g * g))))
        out_ref[rows, :] = (hf_ref[rows, :] + h) * gelu
        return jnp.broadcast_to(h[0:1, :], (SUBLANES, D_MODEL))

    carry_ref[...] = lax.fori_loop(0, nt, tile, carry_ref[...], unroll=2)


def _lru_bwd(a3, b3, hf3, g3):
    bsz, s, _ = a3.shape
    ts = TS_LRU
    nblk = s // ts
    blk = pl.BlockSpec((None, ts, D_MODEL), lambda b, j: (b, nblk - 1 - j, 0))
    return pl.pallas_call(
        _lru_bwd_kernel,
        grid=(bsz, nblk),
        in_specs=[blk, blk, blk, blk],
        out_specs=blk,
        out_shape=jax.ShapeDtypeStruct((bsz, s, D_MODEL), F32),
        scratch_shapes=[pltpu.VMEM((SUBLANES, D_MODEL), F32)],
        compiler_params=pltpu.CompilerParams(
            dimension_semantics=("arbitrary", "arbitrary"), vmem_limit_bytes=VMEM_LIMIT_BYTES),
        name="lru_bwd",
    )(a3, b3, hf3, g3)


_N_BIAS = 3 * N_HEADS + 1
_MASKED_TILE = 3 * N_HEADS


def _attn_kernel(nb, sink_ref, q_ref, kp_ref, km_ref, kn_ref, vp_ref, vm_ref, vn_ref,
                 o_ref, kz_ref, vab_ref, bias_ref, s_ref, m_ref, p_ref):
    tq = q_ref.shape[0]
    tk = tq + 2 * BLOCK
    nsub = tq // BLOCK
    j = pl.program_id(1)

    @pl.when((pl.program_id(0) == 0) & (j == 0))
    def _():
        row = lax.broadcasted_iota(jnp.int32, (BLOCK, BLOCK), 0)
        col = lax.broadcasted_iota(jnp.int32, (BLOCK, BLOCK), 1)
        bias_ref[_MASKED_TILE] = jnp.ones((BLOCK, BLOCK), F32)
        for c in range(3):
            dist = jnp.abs(row + (1 - c) * BLOCK - col)
            absd = dist.astype(F32)
            for h in range(N_HEADS):
                bias_ref[3 * h + c] = jnp.where(dist <= WINDOW, -_ALIBI_SLOPES[h] * absd, 1.0)

    low = lax.broadcasted_iota(jnp.int32, (tk, LANES), 1) < HEAD_DIM
    ones_lo = jnp.where(low, 1.0, 0.0).astype(BF16)
    ones_hi = jnp.where(low, 0.0, 1.0).astype(BF16)
    for t in range(KV_W // LANES):
        cols = slice(t * LANES, (t + 1) * LANES)
        full = jnp.concatenate([r[:, cols] for r in (kp_ref, km_ref, kn_ref)], axis=0).astype(F32)
        swapped = pltpu.roll(full, HEAD_DIM, axis=1)
        kz_ref[4 * t + 0] = jnp.where(low, full, 0.0).astype(BF16)
        kz_ref[4 * t + 1] = jnp.where(low, 0.0, swapped).astype(BF16)
        kz_ref[4 * t + 2] = jnp.where(low, swapped, 0.0).astype(BF16)
        kz_ref[4 * t + 3] = jnp.where(low, 0.0, full).astype(BF16)
        full = jnp.concatenate([r[:, cols] for r in (vp_ref, vm_ref, vn_ref)], axis=0).astype(F32)
        swapped = pltpu.roll(full, HEAD_DIM, axis=1)
        vab_ref[4 * t + 0, :, :LANES] = jnp.where(low, full, 0.0).astype(BF16)
        vab_ref[4 * t + 1, :, :LANES] = jnp.where(low, 0.0, swapped).astype(BF16)
        vab_ref[4 * t + 2, :, :LANES] = jnp.where(low, swapped, 0.0).astype(BF16)
        vab_ref[4 * t + 3, :, :LANES] = jnp.where(low, 0.0, full).astype(BF16)
        for i in range(4):
            vab_ref[4 * t + i, :, LANES:] = ones_lo if i % 2 == 0 else ones_hi

    low_q = lax.broadcasted_iota(jnp.int32, (BLOCK, LANES), 1) < HEAD_DIM

    def sub_block(qi, _):
        q0 = pl.multiple_of(qi * BLOCK, BLOCK)
        blk = j * nsub + qi
        win = pl.ds(q0, 3 * BLOCK)
        for h in range(N_HEADS):
            g, pp, half = h // GROUP, (h % GROUP) // 2, h % 2
            qt = q_ref[pl.ds(q0, BLOCK), (2 * g + pp) * LANES:(2 * g + pp + 1) * LANES]
            sc = lax.dot_general(qt, kz_ref[2 * g + half, win, :], (((1,), (1,)), ((), ())),
                                 preferred_element_type=F32)
            tiles = []
            for c in range(3):
                t = sc[:, c * BLOCK:(c + 1) * BLOCK]
                if c == 1:
                    t = t + bias_ref[3 * h + 1]
                else:
                    edge = (blk == 0) if c == 0 else (blk == nb - 1)
                    b = bias_ref[jnp.where(edge, _MASKED_TILE, 3 * h + c)]
                    t = jnp.where(b > 0.0, NEG_INF, t + b)
                s_ref[h, :, c * BLOCK:(c + 1) * BLOCK] = t
                tiles.append(t)
            mx = jnp.max(jnp.maximum(jnp.maximum(tiles[0], tiles[1]), tiles[2]), axis=-1, keepdims=True)
            m_ref[h] = jnp.broadcast_to(jnp.maximum(mx, sink_ref[h]), (BLOCK, LANES))
        for h in range(N_HEADS):
            m = m_ref[h]
            for c in range(3):
                cs = slice(c * BLOCK, (c + 1) * BLOCK)
                p_ref[h, :, cs] = jnp.exp(s_ref[h, :, cs] - m).astype(BF16)
        for pr in range(N_HEADS // 2):
            g = pr // 2
            h0, h1 = 2 * pr, 2 * pr + 1
            o = (jnp.dot(p_ref[h0], vab_ref[2 * g, win, :], preferred_element_type=F32)
                 + jnp.dot(p_ref[h1], vab_ref[2 * g + 1, win, :], preferred_element_type=F32))
            e = jnp.where(low_q, jnp.exp(sink_ref[h0] - m_ref[h0]), jnp.exp(sink_ref[h1] - m_ref[h1]))
            o_ref[pl.ds(q0, BLOCK), pr * LANES:(pr + 1) * LANES] = o[:, :LANES] * (1.0 / (o[:, LANES:] + e))
        return 0

    lax.fori_loop(0, nsub, sub_block, 0)


def _attention(q3, k3, v3, sink):
    bsz, s, _ = q3.shape
    tq = TQ_ATTN
    nq = s // tq
    hb = tq // BLOCK
    nb = s // BLOCK
    tk = tq + 2 * BLOCK
    kv_main = pl.BlockSpec((None, tq, KV_W), lambda b, j: (b, j, 0))
    kv_prev = pl.BlockSpec((None, BLOCK, KV_W), lambda b, j: (b, jnp.maximum(j * hb - 1, 0), 0))
    kv_next = pl.BlockSpec((None, BLOCK, KV_W), lambda b, j: (b, jnp.minimum((j + 1) * hb, nb - 1), 0))
    qo = pl.BlockSpec((None, tq, Q_W), lambda b, j: (b, j, 0))
    return pl.pallas_call(
        functools.partial(_attn_kernel, nb),
        grid=(bsz, nq),
        in_specs=[pl.BlockSpec(memory_space=pltpu.SMEM), qo,
                  kv_prev, kv_main, kv_next, kv_prev, kv_main, kv_next],
        out_specs=qo,
        out_shape=jax.ShapeDtypeStruct((bsz, s, Q_W), F32),
        scratch_shapes=[
            pltpu.VMEM((2 * N_KV_HEADS, tk, LANES), BF16),
            pltpu.VMEM((2 * N_KV_HEADS, tk, 2 * LANES), BF16),
            pltpu.VMEM((_N_BIAS, BLOCK, BLOCK), F32),
            pltpu.VMEM((N_HEADS, BLOCK, 3 * BLOCK), F32),
            pltpu.VMEM((N_HEADS, BLOCK, LANES), F32),
            pltpu.VMEM((N_HEADS, BLOCK, 3 * BLOCK), BF16),
        ],
        compiler_params=pltpu.CompilerParams(
            dimension_semantics=("arbitrary", "arbitrary"), vmem_limit_bytes=VMEM_LIMIT_BYTES),
        name="attention",
    )(sink, q3, k3, k3, k3, v3, v3, v3)


def _ffn_kernel(x_ref, ya_ref, yb_ref, z_ref, bg_ref, wo_ref, g2_ref, wi_ref, wd_ref, g3_ref, o_ref):
    gates = _sigmoid(z_ref[...] + bg_ref[...])
    merged = gates[:, :D_MODEL] * ya_ref[...] + gates[:, D_MODEL:] * yb_ref[...]
    x1 = x_ref[...] + jnp.dot(merged.astype(BF16), wo_ref[...], preferred_element_type=F32)
    xn = _rmsnorm(x1, g2_ref[...]).astype(BF16)
    gate = jnp.dot(xn, wi_ref[:, :D_FF], preferred_element_type=F32)
    up = jnp.dot(xn, wi_ref[:, D_FF:], preferred_element_type=F32)
    act = (gate * _sigmoid(gate) * up).astype(BF16)
    x2 = x1 + jnp.dot(act, wd_ref[...], preferred_element_type=F32)
    o_ref[...] = _rmsnorm(x2, g3_ref[...])


def _ffn(x2, ya, yb, z, bg, wo, g2, wi, wd, g3):
    t = x2.shape[0]
    tm = TM_FFN
    row = lambda w_: pl.BlockSpec((tm, w_), lambda i: (i, 0))
    return pl.pallas_call(
        _ffn_kernel,
        grid=(t // tm,),
        in_specs=[row(D_MODEL), row(D_MODEL), row(D_MODEL), row(2 * D_MODEL),
                  _resident((1, 2 * D_MODEL)), _resident((D_MODEL, D_MODEL)), _resident((1, D_MODEL)),
                  _resident((D_MODEL, 2 * D_FF)), _resident((D_FF, D_MODEL)), _resident((1, D_MODEL))],
        out_specs=row(D_MODEL),
        out_shape=jax.ShapeDtypeStruct((t, D_MODEL), F32),
        compiler_params=pltpu.CompilerParams(
            dimension_semantics=("arbitrary",), vmem_limit_bytes=VMEM_LIMIT_BYTES),
        name="merge_ffn",
    )(x2, ya, yb, z, bg, wo, g2, wi, wd, g3)


def _gate_weights(wa, wx):
    def blockdiag(w):
        w = 0.5 * w.reshape(N_GROUPS, LRU_GROUP // LRU_BLOCK, LRU_BLOCK, LRU_BLOCK)
        eye = jnp.eye(LRU_GROUP // LRU_BLOCK, dtype=w.dtype)
        return jnp.einsum("ghij,hk->ghikj", w, eye).reshape(N_GROUPS, LRU_GROUP, LRU_GROUP)
    return jnp.concatenate(
        [blockdiag(wa[0]), blockdiag(wx[0]), blockdiag(wa[1]), blockdiag(wx[1])], axis=-1).astype(BF16)


def kernel(x, norm_mix_g, w_in, b_gate, conv_w, conv_b, lru_lambda, lru_wa, lru_ba, lru_wx, lru_bx,
           attn_sink, w_out, norm_ffn_g, w_ffn_in, w_ffn_out, norm_final_g):
    bsz, s, d = x.shape
    depth = w_in.shape[0]
    assert depth == 1, "the merge/ffn kernel applies the final RMSNorm, so it must be the last layer"
    t = bsz * s
    row = lambda a: a.reshape(1, -1)
    x2 = x.reshape(t, d)
    for l in range(depth):
        h_fwd, a_b, b_b, g_lru, q, k, v, z = _in_proj(
            x2, s, row(norm_mix_g[l]), w_in[l].astype(BF16), conv_w[l], row(conv_b[l]),
            lru_lambda[l], _gate_weights(lru_wa[l], lru_wx[l]), lru_ba[l], lru_bx[l])
        seq = lambda a: a.reshape(bsz, s, a.shape[-1])
        y_a = _lru_bwd(seq(a_b), seq(b_b), seq(h_fwd), seq(g_lru))
        y_b = _attention(seq(q), seq(k), seq(v), attn_sink[l])
        x2 = _ffn(x2, y_a.reshape(t, d), y_b.reshape(t, d), z, row(b_gate[l]),
                  w_out[l].astype(BF16), row(norm_ffn_g[l]), w_ffn_in[l].astype(BF16),
                  w_ffn_out[l].astype(BF16), row(norm_final_g))
    return x2.reshape(bsz, s, d)
```

```python
import functools
import math

import numpy as np
import jax
import jax.numpy as jnp
from jax import lax
from jax.experimental import pallas as pl
from jax.experimental.pallas import tpu as pltpu

F32 = jnp.float32
BF16 = jnp.bfloat16

D_MODEL = 1024
LRU_HEADS = 16
LRU_BLOCK = D_MODEL // LRU_HEADS
CONV_WIDTH = 4
CONV_LEFT = 2
RGLRU_C = 8.0
N_HEADS = 16
N_KV_HEADS = 4
HEAD_DIM = 64
GROUP = N_HEADS // N_KV_HEADS
WINDOW = 128
BLOCK = 128
D_FF = 2816
Q_W = N_HEADS * HEAD_DIM
KV_W = N_KV_HEADS * HEAD_DIM
IN_W = 2 * D_MODEL + Q_W + 2 * KV_W + 2 * D_MODEL
EPS = 1e-6
NEG_INF = -1e30

LANES = 128
SUBLANES = 8
VMEM_LIMIT_BYTES = 56 * 1024 * 1024

TM_PROJ = 512
TS_LRU = 512
TQ_ATTN = 512
TM_FFN = 256
LRU_GROUP = 256
N_GROUPS = D_MODEL // LRU_GROUP
LANE_TILES = D_MODEL // LANES
assert LANE_TILES == SUBLANES
SLAB = SUBLANES * LANE_TILES
TILES_PER_STEP = 4
N_SLABS = 3 * TILES_PER_STEP

_ALIBI_SLOPES = [float(v) for v in np.exp2(
    -8.0 * (np.arange(N_HEADS, dtype=np.float32) + 1.0) / N_HEADS).astype(np.float32)]


def _sigmoid(x):
    return 0.5 * jnp.tanh(0.5 * x) + 0.5


def _rmsnorm(x, g):
    return x * lax.rsqrt(jnp.mean(x * x, axis=-1, keepdims=True) + EPS) * g


def _resident(shape):
    nd = len(shape)
    return pl.BlockSpec(shape, lambda *_: (0,) * nd, pipeline_mode=pl.Buffered(1))


def _in_proj_kernel(x_ref, g_ref, w_ref, u_ref, gl_ref, q_ref, k_ref, v_ref, z_ref):
    xn = _rmsnorm(x_ref[...], g_ref[...]).astype(BF16)

    def proj(lo, hi):
        return jnp.dot(xn, w_ref[:, lo:hi], preferred_element_type=F32)

    o = 0
    u_ref[...] = proj(o, o + D_MODEL); o += D_MODEL
    gl_ref[...] = proj(o, o + D_MODEL); o += D_MODEL
    q_ref[...] = (proj(o, o + Q_W) * (HEAD_DIM ** -0.5)).astype(BF16); o += Q_W
    k_ref[...] = proj(o, o + KV_W).astype(BF16); o += KV_W
    v_ref[...] = proj(o, o + KV_W).astype(BF16); o += KV_W
    z_ref[...] = proj(o, o + 2 * D_MODEL)


def _in_proj(x2, g, w):
    t = x2.shape[0]
    tm = TM_PROJ
    row = lambda w_: pl.BlockSpec((tm, w_), lambda i: (i, 0))
    return pl.pallas_call(
        _in_proj_kernel,
        grid=(t // tm,),
        in_specs=[row(D_MODEL), _resident((1, D_MODEL)), _resident((D_MODEL, IN_W))],
        out_specs=[row(D_MODEL), row(D_MODEL), row(Q_W), row(KV_W), row(KV_W), row(2 * D_MODEL)],
        out_shape=[
            jax.ShapeDtypeStruct((t, D_MODEL), F32),
            jax.ShapeDtypeStruct((t, D_MODEL), F32),
            jax.ShapeDtypeStruct((t, Q_W), BF16),
            jax.ShapeDtypeStruct((t, KV_W), BF16),
            jax.ShapeDtypeStruct((t, KV_W), BF16),
            jax.ShapeDtypeStruct((t, 2 * D_MODEL), F32),
        ],
        compiler_params=pltpu.CompilerParams(
            dimension_semantics=("arbitrary",), vmem_limit_bytes=VMEM_LIMIT_BYTES),
        name="in_proj",
    )(x2, g, w)


def _split_time(tile, slab_ref, s):
    for j in range(LANE_TILES):
        slab_ref[s, j * SUBLANES:(j + 1) * SUBLANES, :] = tile[:, j * LANES:(j + 1) * LANES]
    return [slab_ref[s, pl.ds(r, LANE_TILES, stride=SUBLANES), :] for r in range(SUBLANES)]


def _join_time(steps, slab_ref, s):
    for r in range(SUBLANES):
        slab_ref[s, pl.ds(r, LANE_TILES, stride=SUBLANES), :] = steps[r]
    return jnp.concatenate(
        [slab_ref[s, j * SUBLANES:(j + 1) * SUBLANES, :] for j in range(LANE_TILES)], axis=1)


def _half_decay(lam):
    return (-0.5 * RGLRU_C) * (jnp.maximum(-lam, 0.0) + jnp.log1p(jnp.exp(-jnp.abs(lam))))


def _gate_terms(pre_r, pre_i, uc, hdecay, hba, hbx):
    log_a = hdecay * jnp.tanh(pre_r + hba) + hdecay
    a = jnp.exp(log_a)
    y = jnp.maximum(1.0 - a * a, 0.0)
    beta = jnp.where(y > 0.0, y * lax.rsqrt(y), 0.0)
    i = 0.5 * jnp.tanh(pre_i + hbx) + 0.5
    return a, beta * (i * uc)


def _gelu_tanh(g):
    c = math.sqrt(2.0 / math.pi)
    half = 0.5 * g
    return half * jnp.tanh(g * ((c * 0.044715) * (g * g) + c)) + half


def _gate_maps(uc_ref, wg_ref, ra_ref, ix_ref):
    for gi in range(N_GROUPS):
        cols = slice(gi * LRU_GROUP, (gi + 1) * LRU_GROUP)
        pre = jnp.dot(uc_ref[:, cols].astype(BF16), wg_ref[gi], preferred_element_type=F32)
        ra_ref[:, cols] = pre[:, :LRU_GROUP]
        ix_ref[:, cols] = pre[:, LRU_GROUP:]


def _recurrence(reverse, uc_ref, ra_ref, ix_ref, lam_ref, ba_ref, bx_ref, carry_ref, slab_ref, emit):
    nt = uc_ref.shape[0] // SUBLANES
    hdecay = _half_decay(lam_ref[...])
    hba = 0.5 * ba_ref[...]
    hbx = 0.5 * bx_ref[...]

    def group(i, h):
        for s in range(TILES_PER_STEP):
            k = TILES_PER_STEP * i + s
            kk = (nt - 1 - k) if reverse else k
            rows = pl.ds(pl.multiple_of(kk * SUBLANES, SUBLANES), SUBLANES)
            a, b = _gate_terms(ra_ref[rows, :], ix_ref[rows, :], uc_ref[rows, :], hdecay, hba, hbx)
            a_t = _split_time(a, slab_ref, 3 * s)
            b_t = _split_time(b, slab_ref, 3 * s + 1)
            hs = [None] * SUBLANES
            for r in (range(SUBLANES - 1, -1, -1) if reverse else range(SUBLANES)):
                h = a_t[r] * h + b_t[r]
                hs[r] = h
            emit(rows, _join_time(hs, slab_ref, 3 * s + 2))
        return h

    carry_ref[...] = lax.fori_loop(0, nt // TILES_PER_STEP, group, carry_ref[...])


def _lru_fwd_kernel(nblk, um_ref, up_ref, un_ref, cw_ref, cb_ref, lam_ref, wg_ref, ba_ref, bx_ref,
                    hf_ref, uc_ref, uext_ref, ra_ref, ix_ref, carry_ref, slab_ref):
    ts = um_ref.shape[0]
    nt = ts // SUBLANES
    j = pl.program_id(1)

    @pl.when(j == 0)
    def _():
        carry_ref[...] = jnp.zeros_like(carry_ref)

    uext_ref[0:SUBLANES, :] = jnp.where(j > 0, up_ref[...], 0.0)
    uext_ref[SUBLANES:SUBLANES + ts, :] = um_ref[...]
    uext_ref[SUBLANES + ts:, :] = jnp.where(j < nblk - 1, un_ref[...], 0.0)

    w = [cw_ref[k] for k in range(CONV_WIDTH)]
    cb = cb_ref[...]

    def ext_tile(e):
        return uext_ref[pl.ds(pl.multiple_of(e * SUBLANES, SUBLANES), SUBLANES), :]

    def conv_tiles(i, state):
        prev2, prev1, cur = state[0], state[1], list(state[2:])
        for s in range(2):
            t = 2 * i + s
            nxt = _split_time(ext_tile(t + 2), slab_ref, 2 * s)
            x = [prev2, prev1] + cur + [nxt[0]]
            out = [(x[r] * w[0] + x[r + 1] * w[1]) + (x[r + 2] * w[2] + x[r + 3] * w[3]) + cb
                   for r in range(SUBLANES)]
            uc_ref[pl.ds(pl.multiple_of(t * SUBLANES, SUBLANES), SUBLANES), :] = _join_time(
                out, slab_ref, 2 * s + 1)
            prev2, prev1, cur = cur[SUBLANES - 2], cur[SUBLANES - 1], nxt
        return (prev2, prev1, *cur)

    halo = _split_time(ext_tile(0), slab_ref, 4)
    first = _split_time(ext_tile(1), slab_ref, 5)
    lax.fori_loop(0, nt // 2, conv_tiles, (halo[SUBLANES - 2], halo[SUBLANES - 1], *first))

    _gate_maps(uc_ref, wg_ref, ra_ref, ix_ref)

    def emit(rows, h):
        hf_ref[rows, :] = h

    _recurrence(False, uc_ref, ra_ref, ix_ref, lam_ref, ba_ref, bx_ref, carry_ref, slab_ref, emit)


def _lru_bwd_kernel(uc_ref, hf_ref, g_ref, lam_ref, wg_ref, ba_ref, bx_ref,
                    ya_ref, ra_ref, ix_ref, carry_ref, slab_ref):
    @pl.when(pl.program_id(1) == 0)
    def _():
        carry_ref[...] = jnp.zeros_like(carry_ref)

    _gate_maps(uc_ref, wg_ref, ra_ref, ix_ref)

    def emit(rows, h):
        ya_ref[rows, :] = (hf_ref[rows, :] + h) * _gelu_tanh(g_ref[rows, :])

    _recurrence(True, uc_ref, ra_ref, ix_ref, lam_ref, ba_ref, bx_ref, carry_ref, slab_ref, emit)


_LRU_SCRATCH_TAIL = [
    pltpu.VMEM((TS_LRU, D_MODEL), F32),
    pltpu.VMEM((TS_LRU, D_MODEL), F32),
    pltpu.VMEM((LANE_TILES, LANES), F32),
    pltpu.VMEM((N_SLABS, SLAB, LANES), F32),
]


def _lru_fwd(u3, cw, cb, lam, wg, ba, bx):
    bsz, s, _ = u3.shape
    ts = TS_LRU
    nblk = s // ts
    hb = ts // SUBLANES
    nh = s // SUBLANES
    main = pl.BlockSpec((None, ts, D_MODEL), lambda b, j: (b, j, 0))
    prev = pl.BlockSpec((None, SUBLANES, D_MODEL), lambda b, j: (b, jnp.maximum(j * hb - 1, 0), 0))
    nxt = pl.BlockSpec((None, SUBLANES, D_MODEL), lambda b, j: (b, jnp.minimum((j + 1) * hb, nh - 1), 0))
    out = jax.ShapeDtypeStruct((bsz, s, D_MODEL), F32)
    return pl.pallas_call(
        functools.partial(_lru_fwd_kernel, nblk),
        grid=(bsz, nblk),
        in_specs=[main, prev, nxt, _resident(cw.shape), _resident(cb.shape), _resident((1, D_MODEL)),
                  _resident(wg.shape), _resident((1, D_MODEL)), _resident((1, D_MODEL))],
        out_specs=[main, main],
        out_shape=[out, out],
        scratch_shapes=[pltpu.VMEM((ts + 2 * SUBLANES, D_MODEL), F32)] + _LRU_SCRATCH_TAIL,
        compiler_params=pltpu.CompilerParams(
            dimension_semantics=("arbitrary", "arbitrary"), vmem_limit_bytes=VMEM_LIMIT_BYTES),
        name="lru_fwd",
    )(u3, u3, u3, cw, cb, lam, wg, ba, bx)


def _lru_bwd(uc3, hf3, g3, lam, wg, ba, bx):
    bsz, s, _ = uc3.shape
    ts = TS_LRU
    nblk = s // ts
    blk = pl.BlockSpec((None, ts, D_MODEL), lambda b, j: (b, nblk - 1 - j, 0))
    return pl.pallas_call(
        _lru_bwd_kernel,
        grid=(bsz, nblk),
        in_specs=[blk, blk, blk, _resident((1, D_MODEL)), _resident(wg.shape),
                  _resident((1, D_MODEL)), _resident((1, D_MODEL))],
        out_specs=blk,
        out_shape=jax.ShapeDtypeStruct((bsz, s, D_MODEL), F32),
        scratch_shapes=_LRU_SCRATCH_TAIL,
        compiler_params=pltpu.CompilerParams(
            dimension_semantics=("arbitrary", "arbitrary"), vmem_limit_bytes=VMEM_LIMIT_BYTES),
        name="lru_bwd",
    )(uc3, hf3, g3, lam, wg, ba, bx)


_N_BIAS = 3 * N_HEADS + 1
_MASKED_TILE = 3 * N_HEADS


def _attn_kernel(nb, sink_ref, q_ref, kp_ref, km_ref, kn_ref, vp_ref, vm_ref, vn_ref,
                 o_ref, kz_ref, vab_ref, bias_ref, s_ref, m_ref, p_ref):
    tq = q_ref.shape[0]
    tk = tq + 2 * BLOCK
    nsub = tq // BLOCK
    j = pl.program_id(1)

    @pl.when((pl.program_id(0) == 0) & (j == 0))
    def _():
        row = lax.broadcasted_iota(jnp.int32, (BLOCK, BLOCK), 0)
        col = lax.broadcasted_iota(jnp.int32, (BLOCK, BLOCK), 1)
        bias_ref[_MASKED_TILE] = jnp.ones((BLOCK, BLOCK), F32)
        for c in range(3):
            dist = jnp.abs(row + (1 - c) * BLOCK - col)
            absd = dist.astype(F32)
            for h in range(N_HEADS):
                bias_ref[3 * h + c] = jnp.where(dist <= WINDOW, -_ALIBI_SLOPES[h] * absd, 1.0)

    low = lax.broadcasted_iota(jnp.int32, (tk, LANES), 1) < HEAD_DIM
    ones_lo = jnp.where(low, 1.0, 0.0).astype(BF16)
    ones_hi = jnp.where(low, 0.0, 1.0).astype(BF16)
    for t in range(KV_W // LANES):
        cols = slice(t * LANES, (t + 1) * LANES)
        full = jnp.concatenate([r[:, cols] for r in (kp_ref, km_ref, kn_ref)], axis=0).astype(F32)
        swapped = pltpu.roll(full, HEAD_DIM, axis=1)
        kz_ref[4 * t + 0] = jnp.where(low, full, 0.0).astype(BF16)
        kz_ref[4 * t + 1] = jnp.where(low, 0.0, swapped).astype(BF16)
        kz_ref[4 * t + 2] = jnp.where(low, swapped, 0.0).astype(BF16)
        kz_ref[4 * t + 3] = jnp.where(low, 0.0, full).astype(BF16)
        full = jnp.concatenate([r[:, cols] for r in (vp_ref, vm_ref, vn_ref)], axis=0).astype(F32)
        swapped = pltpu.roll(full, HEAD_DIM, axis=1)
        vab_ref[4 * t + 0, :, :LANES] = jnp.where(low, full, 0.0).astype(BF16)
        vab_ref[4 * t + 1, :, :LANES] = jnp.where(low, 0.0, swapped).astype(BF16)
        vab_ref[4 * t + 2, :, :LANES] = jnp.where(low, swapped, 0.0).astype(BF16)
        vab_ref[4 * t + 3, :, :LANES] = jnp.where(low, 0.0, full).astype(BF16)
        for i in range(4):
            vab_ref[4 * t + i, :, LANES:] = ones_lo if i % 2 == 0 else ones_hi

    low_q = lax.broadcasted_iota(jnp.int32, (BLOCK, LANES), 1) < HEAD_DIM

    def sub_block(qi, _):
        q0 = pl.multiple_of(qi * BLOCK, BLOCK)
        blk = j * nsub + qi
        win = pl.ds(q0, 3 * BLOCK)
        for h in range(N_HEADS):
            g, pp, half = h // GROUP, (h % GROUP) // 2, h % 2
            qt = q_ref[pl.ds(q0, BLOCK), (2 * g + pp) * LANES:(2 * g + pp + 1) * LANES]
            sc = lax.dot_general(qt, kz_ref[2 * g + half, win, :], (((1,), (1,)), ((), ())),
                                 preferred_element_type=F32)
            tiles = []
            for c in range(3):
                t = sc[:, c * BLOCK:(c + 1) * BLOCK]
                if c == 1:
                    t = t + bias_ref[3 * h + 1]
                else:
                    edge = (blk == 0) if c == 0 else (blk == nb - 1)
                    b = bias_ref[jnp.where(edge, _MASKED_TILE, 3 * h + c)]
                    t = jnp.where(b > 0.0, NEG_INF, t + b)
                s_ref[h, :, c * BLOCK:(c + 1) * BLOCK] = t
                tiles.append(t)
            mx = jnp.max(jnp.maximum(jnp.maximum(tiles[0], tiles[1]), tiles[2]), axis=-1, keepdims=True)
            m_ref[h] = jnp.broadcast_to(jnp.maximum(mx, sink_ref[h]), (BLOCK, LANES))
        for h in range(N_HEADS):
            m = m_ref[h]
            for c in range(3):
                cs = slice(c * BLOCK, (c + 1) * BLOCK)
                p_ref[h, :, cs] = jnp.exp(s_ref[h, :, cs] - m).astype(BF16)
        for pr in range(N_HEADS // 2):
            g = pr // 2
            h0, h1 = 2 * pr, 2 * pr + 1
            o = (jnp.dot(p_ref[h0], vab_ref[2 * g, win, :], preferred_element_type=F32)
                 + jnp.dot(p_ref[h1], vab_ref[2 * g + 1, win, :], preferred_element_type=F32))
            e = jnp.where(low_q, jnp.exp(sink_ref[h0] - m_ref[h0]), jnp.exp(sink_ref[h1] - m_ref[h1]))
            o_ref[pl.ds(q0, BLOCK), pr * LANES:(pr + 1) * LANES] = o[:, :LANES] * (1.0 / (o[:, LANES:] + e))
        return 0

    lax.fori_loop(0, nsub, sub_block, 0)


def _attention(q3, k3, v3, sink):
    bsz, s, _ = q3.shape
    tq = TQ_ATTN
    nq = s // tq
    hb = tq // BLOCK
    nb = s // BLOCK
    tk = tq + 2 * BLOCK
    kv_main = pl.BlockSpec((None, tq, KV_W), lambda b, j: (b, j, 0))
    kv_prev = pl.BlockSpec((None, BLOCK, KV_W), lambda b, j: (b, jnp.maximum(j * hb - 1, 0), 0))
    kv_next = pl.BlockSpec((None, BLOCK, KV_W), lambda b, j: (b, jnp.minimum((j + 1) * hb, nb - 1), 0))
    qo = pl.BlockSpec((None, tq, Q_W), lambda b, j: (b, j, 0))
    return pl.pallas_call(
        functools.partial(_attn_kernel, nb),
        grid=(bsz, nq),
        in_specs=[pl.BlockSpec(memory_space=pltpu.SMEM), qo,
                  kv_prev, kv_main, kv_next, kv_prev, kv_main, kv_next],
        out_specs=qo,
        out_shape=jax.ShapeDtypeStruct((bsz, s, Q_W), F32),
        scratch_shapes=[
            pltpu.VMEM((2 * N_KV_HEADS, tk, LANES), BF16),
            pltpu.VMEM((2 * N_KV_HEADS, tk, 2 * LANES), BF16),
            pltpu.VMEM((_N_BIAS, BLOCK, BLOCK), F32),
            pltpu.VMEM((N_HEADS, BLOCK, 3 * BLOCK), F32),
            pltpu.VMEM((N_HEADS, BLOCK, LANES), F32),
            pltpu.VMEM((N_HEADS, BLOCK, 3 * BLOCK), BF16),
        ],
        compiler_params=pltpu.CompilerParams(
            dimension_semantics=("arbitrary", "arbitrary"), vmem_limit_bytes=VMEM_LIMIT_BYTES),
        name="attention",
    )(sink, q3, k3, k3, k3, v3, v3, v3)


def _ffn_kernel(x_ref, ya_ref, yb_ref, z_ref, bg_ref, wo_ref, g2_ref, wi_ref, wd_ref, g3_ref, o_ref):
    gates = _sigmoid(z_ref[...] + bg_ref[...])
    merged = gates[:, :D_MODEL] * ya_ref[...] + gates[:, D_MODEL:] * yb_ref[...]
    x1 = x_ref[...] + jnp.dot(merged.astype(BF16), wo_ref[...], preferred_element_type=F32)
    xn = _rmsnorm(x1, g2_ref[...]).astype(BF16)
    gate = jnp.dot(xn, wi_ref[:, :D_FF], preferred_element_type=F32)
    up = jnp.dot(xn, wi_ref[:, D_FF:], preferred_element_type=F32)
    act = (gate * _sigmoid(gate) * up).astype(BF16)
    x2 = x1 + jnp.dot(act, wd_ref[...], preferred_element_type=F32)
    o_ref[...] = _rmsnorm(x2, g3_ref[...])


def _ffn(x2, ya, yb, z, bg, wo, g2, wi, wd, g3):
    t = x2.shape[0]
    tm = TM_FFN
    row = lambda w_: pl.BlockSpec((tm, w_), lambda i: (i, 0))
    return pl.pallas_call(
        _ffn_kernel,
        grid=(t // tm,),
        in_specs=[row(D_MODEL), row(D_MODEL), row(D_MODEL), row(2 * D_MODEL),
                  _resident((1, 2 * D_MODEL)), _resident((D_MODEL, D_MODEL)), _resident((1, D_MODEL)),
                  _resident((D_MODEL, 2 * D_FF)), _resident((D_FF, D_MODEL)), _resident((1, D_MODEL))],
        out_specs=row(D_MODEL),
        out_shape=jax.ShapeDtypeStruct((t, D_MODEL), F32),
        compiler_params=pltpu.CompilerParams(
            dimension_semantics=("arbitrary",), vmem_limit_bytes=VMEM_LIMIT_BYTES),
        name="merge_ffn",
    )(x2, ya, yb, z, bg, wo, g2, wi, wd, g3)


def _gate_weights(wa, wx):
    def blockdiag(w):
        w = 0.5 * w.reshape(N_GROUPS, LRU_GROUP // LRU_BLOCK, LRU_BLOCK, LRU_BLOCK)
        eye = jnp.eye(LRU_GROUP // LRU_BLOCK, dtype=w.dtype)
        return jnp.einsum("ghij,hk->ghikj", w, eye).reshape(N_GROUPS, LRU_GROUP, LRU_GROUP)
    return jnp.concatenate([blockdiag(wa), blockdiag(wx)], axis=-1).astype(BF16)


def kernel(x, norm_mix_g, w_in, b_gate, conv_w, conv_b, lru_lambda, lru_wa, lru_ba, lru_wx, lru_bx,
           attn_sink, w_out, norm_ffn_g, w_ffn_in, w_ffn_out, norm_final_g):
    bsz, s, d = x.shape
    depth = w_in.shape[0]
    assert depth == 1, "the merge/ffn kernel applies the final RMSNorm, so it must be the last layer"
    t = bsz * s
    row = lambda a: a.reshape(1, -1)
    time_major = lambda a: a.reshape(a.shape[:-1] + (LANE_TILES, LANES))
    x2 = x.reshape(t, d)
    for l in range(depth):
        u, g_lru, q, k, v, z = _in_proj(x2, row(norm_mix_g[l]), w_in[l].astype(BF16))
        seq = lambda a: a.reshape(bsz, s, a.shape[-1])
        gates = lambda dr: (row(lru_lambda[l, dr]), _gate_weights(lru_wa[l, dr], lru_wx[l, dr]),
                            row(lru_ba[l, dr]), row(lru_bx[l, dr]))
        h_fwd, uc = _lru_fwd(seq(u), time_major(conv_w[l]), time_major(conv_b[l]), *gates(0))
        y_a = _lru_bwd(uc, h_fwd, seq(g_lru), *gates(1))
        y_b = _attention(seq(q), seq(k), seq(v), attn_sink[l])
        x2 = _ffn(x2, y_a.reshape(t, d), y_b.reshape(t, d), z, row(b_gate[l]),
                  w_out[l].astype(BF16), row(norm_ffn_g[l]), w_ffn_in[l].astype(BF16),
                  w_ffn_out[l].astype(BF16), row(norm_final_g))
    return x2.reshape(bsz, s, d)
```

```python
import functools
import math

import numpy as np
import jax
import jax.numpy as jnp
from jax import lax
from jax.experimental import pallas as pl
from jax.experimental.pallas import tpu as pltpu

F32 = jnp.float32
BF16 = jnp.bfloat16

D_MODEL = 1024
LRU_HEADS = 16
LRU_BLOCK = D_MODEL // LRU_HEADS
CONV_WIDTH = 4
CONV_LEFT = 2
RGLRU_C = 8.0
N_HEADS = 16
N_KV_HEADS = 4
HEAD_DIM = 64
GROUP = N_HEADS // N_KV_HEADS
WINDOW = 128
BLOCK = 128
D_FF = 2816
Q_W = N_HEADS * HEAD_DIM
KV_W = N_KV_HEADS * HEAD_DIM
IN_W = 2 * D_MODEL + Q_W + 2 * KV_W + 2 * D_MODEL
EPS = 1e-6
NEG_INF = -1e30

LANES = 128
SUBLANES = 8
VMEM_LIMIT_BYTES = 56 * 1024 * 1024

TM_PROJ = 512
TS_LRU = 512
TQ_ATTN = 1024
TM_FFN = 256
LRU_GROUP = 256
N_GROUPS = D_MODEL // LRU_GROUP
LANE_TILES = D_MODEL // LANES
assert LANE_TILES == SUBLANES
SLAB = SUBLANES * LANE_TILES
TILES_PER_STEP = 4
N_SLABS = 3 * TILES_PER_STEP

_ALIBI_SLOPES = [float(v) for v in np.exp2(
    -8.0 * (np.arange(N_HEADS, dtype=np.float32) + 1.0) / N_HEADS).astype(np.float32)]


def _sigmoid(x):
    return 0.5 * jnp.tanh(0.5 * x) + 0.5


def _rmsnorm(x, g):
    return x * lax.rsqrt(jnp.mean(x * x, axis=-1, keepdims=True) + EPS) * g


def _resident(shape):
    nd = len(shape)
    return pl.BlockSpec(shape, lambda *_: (0,) * nd, pipeline_mode=pl.Buffered(1))


def _in_proj_kernel(x_ref, g_ref, w_ref, u_ref, gl_ref, q_ref, k_ref, v_ref, z_ref):
    xn = _rmsnorm(x_ref[...], g_ref[...]).astype(BF16)

    def proj(lo, hi):
        return jnp.dot(xn, w_ref[:, lo:hi], preferred_element_type=F32)

    o = 0
    u_ref[...] = proj(o, o + D_MODEL); o += D_MODEL
    gl_ref[...] = proj(o, o + D_MODEL); o += D_MODEL
    q_ref[...] = (proj(o, o + Q_W) * (HEAD_DIM ** -0.5)).astype(BF16); o += Q_W
    k_ref[...] = proj(o, o + KV_W).astype(BF16); o += KV_W
    v_ref[...] = proj(o, o + KV_W).astype(BF16); o += KV_W
    z_ref[...] = proj(o, o + 2 * D_MODEL)


def _in_proj(x2, g, w):
    t = x2.shape[0]
    tm = TM_PROJ
    row = lambda w_: pl.BlockSpec((tm, w_), lambda i: (i, 0))
    return pl.pallas_call(
        _in_proj_kernel,
        grid=(t // tm,),
        in_specs=[row(D_MODEL), _resident((1, D_MODEL)), _resident((D_MODEL, IN_W))],
        out_specs=[row(D_MODEL), row(D_MODEL), row(Q_W), row(KV_W), row(KV_W), row(2 * D_MODEL)],
        out_shape=[
            jax.ShapeDtypeStruct((t, D_MODEL), F32),
            jax.ShapeDtypeStruct((t, D_MODEL), F32),
            jax.ShapeDtypeStruct((t, Q_W), BF16),
            jax.ShapeDtypeStruct((t, KV_W), BF16),
            jax.ShapeDtypeStruct((t, KV_W), BF16),
            jax.ShapeDtypeStruct((t, 2 * D_MODEL), F32),
        ],
        compiler_params=pltpu.CompilerParams(
            dimension_semantics=("arbitrary",), vmem_limit_bytes=VMEM_LIMIT_BYTES),
        name="in_proj",
    )(x2, g, w)


def _split_time(tile, slab_ref, s):
    for j in range(LANE_TILES):
        slab_ref[s, j * SUBLANES:(j + 1) * SUBLANES, :] = tile[:, j * LANES:(j + 1) * LANES]
    return [slab_ref[s, pl.ds(r, LANE_TILES, stride=SUBLANES), :] for r in range(SUBLANES)]


def _join_time(steps, slab_ref, s):
    for r in range(SUBLANES):
        slab_ref[s, pl.ds(r, LANE_TILES, stride=SUBLANES), :] = steps[r]
    return jnp.concatenate(
        [slab_ref[s, j * SUBLANES:(j + 1) * SUBLANES, :] for j in range(LANE_TILES)], axis=1)


def _half_decay(lam):
    return (-0.5 * RGLRU_C) * (jnp.maximum(-lam, 0.0) + jnp.log1p(jnp.exp(-jnp.abs(lam))))


def _gate_terms(pre_r, pre_i, uc, hdecay, hba, hbx):
    log_a = hdecay * jnp.tanh(pre_r + hba) + hdecay
    a = jnp.exp(log_a)
    y = jnp.maximum(1.0 - a * a, 0.0)
    beta = jnp.where(y > 0.0, y * lax.rsqrt(y), 0.0)
    i = 0.5 * jnp.tanh(pre_i + hbx) + 0.5
    return a, beta * (i * uc)


def _gelu_tanh(g):
    c = math.sqrt(2.0 / math.pi)
    half = 0.5 * g
    return half * jnp.tanh(g * ((c * 0.044715) * (g * g) + c)) + half


def _gate_maps(uc_ref, wg_ref, ra_ref, ix_ref):
    for gi in range(N_GROUPS):
        cols = slice(gi * LRU_GROUP, (gi + 1) * LRU_GROUP)
        pre = jnp.dot(uc_ref[:, cols].astype(BF16), wg_ref[gi], preferred_element_type=F32)
        ra_ref[:, cols] = pre[:, :LRU_GROUP]
        ix_ref[:, cols] = pre[:, LRU_GROUP:]


def _recurrence(reverse, uc_ref, ra_ref, ix_ref, lam_ref, ba_ref, bx_ref, carry_ref, slab_ref, emit):
    nt = uc_ref.shape[0] // SUBLANES
    hdecay = _half_decay(lam_ref[...])
    hba = 0.5 * ba_ref[...]
    hbx = 0.5 * bx_ref[...]

    def group(i, h):
        for s in range(TILES_PER_STEP):
            k = TILES_PER_STEP * i + s
            kk = (nt - 1 - k) if reverse else k
            rows = pl.ds(pl.multiple_of(kk * SUBLANES, SUBLANES), SUBLANES)
            a, b = _gate_terms(ra_ref[rows, :], ix_ref[rows, :], uc_ref[rows, :], hdecay, hba, hbx)
            a_t = _split_time(a, slab_ref, 3 * s)
            b_t = _split_time(b, slab_ref, 3 * s + 1)
            hs = [None] * SUBLANES
            for r in (range(SUBLANES - 1, -1, -1) if reverse else range(SUBLANES)):
                h = a_t[r] * h + b_t[r]
                hs[r] = h
            emit(rows, _join_time(hs, slab_ref, 3 * s + 2))
        return h

    carry_ref[...] = lax.fori_loop(0, nt // TILES_PER_STEP, group, carry_ref[...])


def _lru_fwd_kernel(nblk, um_ref, up_ref, un_ref, cw_ref, cb_ref, lam_ref, wg_ref, ba_ref, bx_ref,
                    hf_ref, uc_ref, uext_ref, ra_ref, ix_ref, carry_ref, slab_ref):
    ts = um_ref.shape[0]
    nt = ts // SUBLANES
    j = pl.program_id(1)

    @pl.when(j == 0)
    def _():
        carry_ref[...] = jnp.zeros_like(carry_ref)

    uext_ref[0:SUBLANES, :] = jnp.where(j > 0, up_ref[...], 0.0)
    uext_ref[SUBLANES:SUBLANES + ts, :] = um_ref[...]
    uext_ref[SUBLANES + ts:, :] = jnp.where(j < nblk - 1, un_ref[...], 0.0)

    w = [cw_ref[k] for k in range(CONV_WIDTH)]
    cb = cb_ref[...]

    def ext_tile(e):
        return uext_ref[pl.ds(pl.multiple_of(e * SUBLANES, SUBLANES), SUBLANES), :]

    def conv_tiles(i, state):
        prev2, prev1, cur = state[0], state[1], list(state[2:])
        for s in range(2):
            t = 2 * i + s
            nxt = _split_time(ext_tile(t + 2), slab_ref, 2 * s)
            x = [prev2, prev1] + cur + [nxt[0]]
            out = [(x[r] * w[0] + x[r + 1] * w[1]) + (x[r + 2] * w[2] + x[r + 3] * w[3]) + cb
                   for r in range(SUBLANES)]
            uc_ref[pl.ds(pl.multiple_of(t * SUBLANES, SUBLANES), SUBLANES), :] = _join_time(
                out, slab_ref, 2 * s + 1)
            prev2, prev1, cur = cur[SUBLANES - 2], cur[SUBLANES - 1], nxt
        return (prev2, prev1, *cur)

    halo = _split_time(ext_tile(0), slab_ref, 4)
    first = _split_time(ext_tile(1), slab_ref, 5)
    lax.fori_loop(0, nt // 2, conv_tiles, (halo[SUBLANES - 2], halo[SUBLANES - 1], *first))

    _gate_maps(uc_ref, wg_ref, ra_ref, ix_ref)

    def emit(rows, h):
        hf_ref[rows, :] = h

    _recurrence(False, uc_ref, ra_ref, ix_ref, lam_ref, ba_ref, bx_ref, carry_ref, slab_ref, emit)


def _lru_bwd_kernel(uc_ref, hf_ref, g_ref, lam_ref, wg_ref, ba_ref, bx_ref,
                    ya_ref, ra_ref, ix_ref, carry_ref, slab_ref):
    @pl.when(pl.program_id(1) == 0)
    def _():
        carry_ref[...] = jnp.zeros_like(carry_ref)

    _gate_maps(uc_ref, wg_ref, ra_ref, ix_ref)

    def emit(rows, h):
        ya_ref[rows, :] = (hf_ref[rows, :] + h) * _gelu_tanh(g_ref[rows, :])

    _recurrence(True, uc_ref, ra_ref, ix_ref, lam_ref, ba_ref, bx_ref, carry_ref, slab_ref, emit)


_LRU_SCRATCH_TAIL = [
    pltpu.VMEM((TS_LRU, D_MODEL), F32),
    pltpu.VMEM((TS_LRU, D_MODEL), F32),
    pltpu.VMEM((LANE_TILES, LANES), F32),
    pltpu.VMEM((N_SLABS, SLAB, LANES), F32),
]


def _lru_fwd(u3, cw, cb, lam, wg, ba, bx):
    bsz, s, _ = u3.shape
    ts = TS_LRU
    nblk = s // ts
    hb = ts // SUBLANES
    nh = s // SUBLANES
    main = pl.BlockSpec((None, ts, D_MODEL), lambda b, j: (b, j, 0))
    prev = pl.BlockSpec((None, SUBLANES, D_MODEL), lambda b, j: (b, jnp.maximum(j * hb - 1, 0), 0))
    nxt = pl.BlockSpec((None, SUBLANES, D_MODEL), lambda b, j: (b, jnp.minimum((j + 1) * hb, nh - 1), 0))
    out = jax.ShapeDtypeStruct((bsz, s, D_MODEL), F32)
    return pl.pallas_call(
        functools.partial(_lru_fwd_kernel, nblk),
        grid=(bsz, nblk),
        in_specs=[main, prev, nxt, _resident(cw.shape), _resident(cb.shape), _resident((1, D_MODEL)),
                  _resident(wg.shape), _resident((1, D_MODEL)), _resident((1, D_MODEL))],
        out_specs=[main, main],
        out_shape=[out, out],
        scratch_shapes=[pltpu.VMEM((ts + 2 * SUBLANES, D_MODEL), F32)] + _LRU_SCRATCH_TAIL,
        compiler_params=pltpu.CompilerParams(
            dimension_semantics=("arbitrary", "arbitrary"), vmem_limit_bytes=VMEM_LIMIT_BYTES),
        name="lru_fwd",
    )(u3, u3, u3, cw, cb, lam, wg, ba, bx)


def _lru_bwd(uc3, hf3, g3, lam, wg, ba, bx):
    bsz, s, _ = uc3.shape
    ts = TS_LRU
    nblk = s // ts
    blk = pl.BlockSpec((None, ts, D_MODEL), lambda b, j: (b, nblk - 1 - j, 0))
    return pl.pallas_call(
        _lru_bwd_kernel,
        grid=(bsz, nblk),
        in_specs=[blk, blk, blk, _resident((1, D_MODEL)), _resident(wg.shape),
                  _resident((1, D_MODEL)), _resident((1, D_MODEL))],
        out_specs=blk,
        out_shape=jax.ShapeDtypeStruct((bsz, s, D_MODEL), F32),
        scratch_shapes=_LRU_SCRATCH_TAIL,
        compiler_params=pltpu.CompilerParams(
            dimension_semantics=("arbitrary", "arbitrary"), vmem_limit_bytes=VMEM_LIMIT_BYTES),
        name="lru_bwd",
    )(uc3, hf3, g3, lam, wg, ba, bx)


_N_BIAS = 3 * N_HEADS + 1
_MASKED_TILE = 3 * N_HEADS


def _attn_kernel(nb, sink_ref, q_ref, kp_ref, km_ref, kn_ref, vp_ref, vm_ref, vn_ref,
                 o_ref, kz_ref, vab_ref, bias_ref, s_ref, m_ref, p_ref):
    tq = q_ref.shape[0]
    tk = tq + 2 * BLOCK
    nsub = tq // BLOCK
    j = pl.program_id(1)

    @pl.when((pl.program_id(0) == 0) & (j == 0))
    def _():
        row = lax.broadcasted_iota(jnp.int32, (BLOCK, BLOCK), 0)
        col = lax.broadcasted_iota(jnp.int32, (BLOCK, BLOCK), 1)
        bias_ref[_MASKED_TILE] = jnp.ones((BLOCK, BLOCK), F32)
        for c in range(3):
            dist = jnp.abs(row + (1 - c) * BLOCK - col)
            absd = dist.astype(F32)
            for h in range(N_HEADS):
                bias_ref[3 * h + c] = jnp.where(dist <= WINDOW, -_ALIBI_SLOPES[h] * absd, 1.0)

    low = lax.broadcasted_iota(jnp.int32, (tk, LANES), 1) < HEAD_DIM
    ones_lo = jnp.where(low, 1.0, 0.0).astype(BF16)
    ones_hi = jnp.where(low, 0.0, 1.0).astype(BF16)
    for t in range(KV_W // LANES):
        cols = slice(t * LANES, (t + 1) * LANES)
        full = jnp.concatenate([r[:, cols] for r in (kp_ref, km_ref, kn_ref)], axis=0).astype(F32)
        swapped = pltpu.roll(full, HEAD_DIM, axis=1)
        kz_ref[4 * t + 0] = jnp.where(low, full, 0.0).astype(BF16)
        kz_ref[4 * t + 1] = jnp.where(low, 0.0, swapped).astype(BF16)
        kz_ref[4 * t + 2] = jnp.where(low, swapped, 0.0).astype(BF16)
        kz_ref[4 * t + 3] = jnp.where(low, 0.0, full).astype(BF16)
        full = jnp.concatenate([r[:, cols] for r in (vp_ref, vm_ref, vn_ref)], axis=0).astype(F32)
        swapped = pltpu.roll(full, HEAD_DIM, axis=1)
        vab_ref[4 * t + 0, :, :LANES] = jnp.where(low, full, 0.0).astype(BF16)
        vab_ref[4 * t + 1, :, :LANES] = jnp.where(low, 0.0, swapped).astype(BF16)
        vab_ref[4 * t + 2, :, :LANES] = jnp.where(low, swapped, 0.0).astype(BF16)
        vab_ref[4 * t + 3, :, :LANES] = jnp.where(low, 0.0, full).astype(BF16)
        for i in range(4):
            vab_ref[4 * t + i, :, LANES:] = ones_lo if i % 2 == 0 else ones_hi

    low_q = lax.broadcasted_iota(jnp.int32, (BLOCK, LANES), 1) < HEAD_DIM

    def scores(qi, slot):
        q0 = pl.multiple_of(qi * BLOCK, BLOCK)
        blk = j * nsub + qi
        win = pl.ds(q0, 3 * BLOCK)
        for h in range(N_HEADS):
            g, pp, half = h // GROUP, (h % GROUP) // 2, h % 2
            qt = q_ref[pl.ds(q0, BLOCK), (2 * g + pp) * LANES:(2 * g + pp + 1) * LANES]
            sc = lax.dot_general(qt, kz_ref[2 * g + half, win, :], (((1,), (1,)), ((), ())),
                                 preferred_element_type=F32)
            tiles = []
            for c in range(3):
                t = sc[:, c * BLOCK:(c + 1) * BLOCK]
                if c == 1:
                    t = t + bias_ref[3 * h + 1]
                else:
                    edge = (blk == 0) if c == 0 else (blk == nb - 1)
                    b = bias_ref[jnp.where(edge, _MASKED_TILE, 3 * h + c)]
                    t = jnp.where(b > 0.0, NEG_INF, t + b)
                s_ref[slot, h, :, c * BLOCK:(c + 1) * BLOCK] = t
                tiles.append(t)
            mx = jnp.max(jnp.maximum(jnp.maximum(tiles[0], tiles[1]), tiles[2]), axis=-1, keepdims=True)
            m_ref[slot, h] = jnp.broadcast_to(jnp.maximum(mx, sink_ref[h]), (BLOCK, LANES))

    def finish(qi, slot):
        q0 = pl.multiple_of(qi * BLOCK, BLOCK)
        win = pl.ds(q0, 3 * BLOCK)
        for h in range(N_HEADS):
            m = m_ref[slot, h]
            for c in range(3):
                cs = slice(c * BLOCK, (c + 1) * BLOCK)
                p_ref[slot, h, :, cs] = jnp.exp(s_ref[slot, h, :, cs] - m).astype(BF16)
        for pr in range(N_HEADS // 2):
            g = pr // 2
            h0, h1 = 2 * pr, 2 * pr + 1
            o = (jnp.dot(p_ref[slot, h0], vab_ref[2 * g, win, :], preferred_element_type=F32)
                 + jnp.dot(p_ref[slot, h1], vab_ref[2 * g + 1, win, :], preferred_element_type=F32))
            e = jnp.where(low_q, jnp.exp(sink_ref[h0] - m_ref[slot, h0]),
                          jnp.exp(sink_ref[h1] - m_ref[slot, h1]))
            o_ref[pl.ds(q0, BLOCK), pr * LANES:(pr + 1) * LANES] = o[:, :LANES] * (1.0 / (o[:, LANES:] + e))

    scores(0, 0)

    def two_sub_blocks(i, _):
        qa = 2 * i
        scores(qa + 1, 1)
        finish(qa, 0)
        scores(jnp.minimum(qa + 2, nsub - 1), 0)
        finish(qa + 1, 1)
        return 0

    lax.fori_loop(0, nsub // 2, two_sub_blocks, 0)


def _attention(q3, k3, v3, sink):
    bsz, s, _ = q3.shape
    tq = TQ_ATTN
    nq = s // tq
    hb = tq // BLOCK
    nb = s // BLOCK
    tk = tq + 2 * BLOCK
    kv_main = pl.BlockSpec((None, tq, KV_W), lambda b, j: (b, j, 0))
    kv_prev = pl.BlockSpec((None, BLOCK, KV_W), lambda b, j: (b, jnp.maximum(j * hb - 1, 0), 0))
    kv_next = pl.BlockSpec((None, BLOCK, KV_W), lambda b, j: (b, jnp.minimum((j + 1) * hb, nb - 1), 0))
    qo = pl.BlockSpec((None, tq, Q_W), lambda b, j: (b, j, 0))
    return pl.pallas_call(
        functools.partial(_attn_kernel, nb),
        grid=(bsz, nq),
        in_specs=[pl.BlockSpec(memory_space=pltpu.SMEM), qo,
                  kv_prev, kv_main, kv_next, kv_prev, kv_main, kv_next],
        out_specs=qo,
        out_shape=jax.ShapeDtypeStruct((bsz, s, Q_W), F32),
        scratch_shapes=[
            pltpu.VMEM((2 * N_KV_HEADS, tk, LANES), BF16),
            pltpu.VMEM((2 * N_KV_HEADS, tk, 2 * LANES), BF16),
            pltpu.VMEM((_N_BIAS, BLOCK, BLOCK), F32),
            pltpu.VMEM((2, N_HEADS, BLOCK, 3 * BLOCK), F32),
            pltpu.VMEM((2, N_HEADS, BLOCK, LANES), F32),
            pltpu.VMEM((2, N_HEADS, BLOCK, 3 * BLOCK), BF16),
        ],
        compiler_params=pltpu.CompilerParams(
            dimension_semantics=("arbitrary", "arbitrary"), vmem_limit_bytes=VMEM_LIMIT_BYTES),
        name="attention",
    )(sink, q3, k3, k3, k3, v3, v3, v3)


def _ffn_kernel(x_ref, ya_ref, yb_ref, z_ref, bg_ref, wo_ref, g2_ref, wi_ref, wd_ref, g3_ref, o_ref):
    gates = _sigmoid(z_ref[...] + bg_ref[...])
    merged = gates[:, :D_MODEL] * ya_ref[...] + gates[:, D_MODEL:] * yb_ref[...]
    x1 = x_ref[...] + jnp.dot(merged.astype(BF16), wo_ref[...], preferred_element_type=F32)
    xn = _rmsnorm(x1, g2_ref[...]).astype(BF16)
    gate = jnp.dot(xn, wi_ref[:, :D_FF], preferred_element_type=F32)
    up = jnp.dot(xn, wi_ref[:, D_FF:], preferred_element_type=F32)
    act = (gate * _sigmoid(gate) * up).astype(BF16)
    x2 = x1 + jnp.dot(act, wd_ref[...], preferred_element_type=F32)
    o_ref[...] = _rmsnorm(x2, g3_ref[...])


def _ffn(x2, ya, yb, z, bg, wo, g2, wi, wd, g3):
    t = x2.shape[0]
    tm = TM_FFN
    row = lambda w_: pl.BlockSpec((tm, w_), lambda i: (i, 0))
    return pl.pallas_call(
        _ffn_kernel,
        grid=(t // tm,),
        in_specs=[row(D_MODEL), row(D_MODEL), row(D_MODEL), row(2 * D_MODEL),
                  _resident((1, 2 * D_MODEL)), _resident((D_MODEL, D_MODEL)), _resident((1, D_MODEL)),
                  _resident((D_MODEL, 2 * D_FF)), _resident((D_FF, D_MODEL)), _resident((1, D_MODEL))],
        out_specs=row(D_MODEL),
        out_shape=jax.ShapeDtypeStruct((t, D_MODEL), F32),
        compiler_params=pltpu.CompilerParams(
            dimension_semantics=("arbitrary",), vmem_limit_bytes=VMEM_LIMIT_BYTES),
        name="merge_ffn",
    )(x2, ya, yb, z, bg, wo, g2, wi, wd, g3)


def _gate_weights(wa, wx):
    def blockdiag(w):
        w = 0.5 * w.reshape(N_GROUPS, LRU_GROUP // LRU_BLOCK, LRU_BLOCK, LRU_BLOCK)
        eye = jnp.eye(LRU_GROUP // LRU_BLOCK, dtype=w.dtype)
        return jnp.einsum("ghij,hk->ghikj", w, eye).reshape(N_GROUPS, LRU_GROUP, LRU_GROUP)
    return jnp.concatenate([blockdiag(wa), blockdiag(wx)], axis=-1).astype(BF16)


def kernel(x, norm_mix_g, w_in, b_gate, conv_w, conv_b, lru_lambda, lru_wa, lru_ba, lru_wx, lru_bx,
           attn_sink, w_out, norm_ffn_g, w_ffn_in, w_ffn_out, norm_final_g):
    bsz, s, d = x.shape
    depth = w_in.shape[0]
    assert depth == 1, "the merge/ffn kernel applies the final RMSNorm, so it must be the last layer"
    t = bsz * s
    row = lambda a: a.reshape(1, -1)
    time_major = lambda a: a.reshape(a.shape[:-1] + (LANE_TILES, LANES))
    x2 = x.reshape(t, d)
    for l in range(depth):
        u, g_lru, q, k, v, z = _in_proj(x2, row(norm_mix_g[l]), w_in[l].astype(BF16))
        seq = lambda a: a.reshape(bsz, s, a.shape[-1])
        gates = lambda dr: (row(lru_lambda[l, dr]), _gate_weights(lru_wa[l, dr], lru_wx[l, dr]),
                            row(lru_ba[l, dr]), row(lru_bx[l, dr]))
        h_fwd, uc = _lru_fwd(seq(u), time_major(conv_w[l]), time_major(conv_b[l]), *gates(0))
        y_a = _lru_bwd(uc, h_fwd, seq(g_lru), *gates(1))
        y_b = _attention(seq(q), seq(k), seq(v), attn_sink[l])
        x2 = _ffn(x2, y_a.reshape(t, d), y_b.reshape(t, d), z, row(b_gate[l]),
                  w_out[l].astype(BF16), row(norm_ffn_g[l]), w_ffn_in[l].astype(BF16),
                  w_ffn_out[l].astype(BF16), row(norm_final_g))
    return x2.reshape(bsz, s, d)
```

```python
import functools
import math

import numpy as np
import jax
import jax.numpy as jnp
from jax import lax
from jax.experimental import pallas as pl
from jax.experimental.pallas import tpu as pltpu

F32 = jnp.float32
BF16 = jnp.bfloat16

D_MODEL = 1024
LRU_HEADS = 16
LRU_BLOCK = D_MODEL // LRU_HEADS
CONV_WIDTH = 4
CONV_LEFT = 2
RGLRU_C = 8.0
N_HEADS = 16
N_KV_HEADS = 4
HEAD_DIM = 64
GROUP = N_HEADS // N_KV_HEADS
WINDOW = 128
BLOCK = 128
D_FF = 2816
Q_W = N_HEADS * HEAD_DIM
KV_W = N_KV_HEADS * HEAD_DIM
IN_W = 2 * D_MODEL + Q_W + 2 * KV_W + 2 * D_MODEL
EPS = 1e-6
NEG_INF = -1e30

LANES = 128
SUBLANES = 8
VMEM_LIMIT_BYTES = 56 * 1024 * 1024

TM_PROJ = 512
TS_LRU = 512
TQ_ATTN = 1024
TM_FFN = 256
LRU_GROUP = 256
N_GROUPS = D_MODEL // LRU_GROUP
LANE_TILES = D_MODEL // LANES
assert LANE_TILES == SUBLANES
SLAB = SUBLANES * LANE_TILES
TILES_PER_STEP = 4
N_SLABS = 3 * TILES_PER_STEP

_ALIBI_SLOPES = [float(v) for v in np.exp2(
    -8.0 * (np.arange(N_HEADS, dtype=np.float32) + 1.0) / N_HEADS).astype(np.float32)]


def _sigmoid(x):
    return 0.5 * jnp.tanh(0.5 * x) + 0.5


def _rmsnorm(x, g):
    return x * lax.rsqrt(jnp.mean(x * x, axis=-1, keepdims=True) + EPS) * g


def _resident(shape):
    nd = len(shape)
    return pl.BlockSpec(shape, lambda *_: (0,) * nd, pipeline_mode=pl.Buffered(1))


def _in_proj_kernel(x_ref, g_ref, w_ref, u_ref, gl_ref, q_ref, k_ref, v_ref, z_ref):
    xn = _rmsnorm(x_ref[...], g_ref[...]).astype(BF16)

    def proj(lo, hi):
        return jnp.dot(xn, w_ref[:, lo:hi], preferred_element_type=F32)

    o = 0
    u_ref[...] = proj(o, o + D_MODEL); o += D_MODEL
    gl_ref[...] = proj(o, o + D_MODEL); o += D_MODEL
    q_ref[...] = (proj(o, o + Q_W) * (HEAD_DIM ** -0.5)).astype(BF16); o += Q_W
    k_ref[...] = proj(o, o + KV_W).astype(BF16); o += KV_W
    v_ref[...] = proj(o, o + KV_W).astype(BF16); o += KV_W
    z_ref[...] = proj(o, o + 2 * D_MODEL)


def _in_proj(x2, g, w):
    t = x2.shape[0]
    tm = TM_PROJ
    row = lambda w_: pl.BlockSpec((tm, w_), lambda i: (i, 0))
    return pl.pallas_call(
        _in_proj_kernel,
        grid=(t // tm,),
        in_specs=[row(D_MODEL), _resident((1, D_MODEL)), _resident((D_MODEL, IN_W))],
        out_specs=[row(D_MODEL), row(D_MODEL), row(Q_W), row(KV_W), row(KV_W), row(2 * D_MODEL)],
        out_shape=[
            jax.ShapeDtypeStruct((t, D_MODEL), F32),
            jax.ShapeDtypeStruct((t, D_MODEL), F32),
            jax.ShapeDtypeStruct((t, Q_W), BF16),
            jax.ShapeDtypeStruct((t, KV_W), BF16),
            jax.ShapeDtypeStruct((t, KV_W), BF16),
            jax.ShapeDtypeStruct((t, 2 * D_MODEL), F32),
        ],
        compiler_params=pltpu.CompilerParams(
            dimension_semantics=("arbitrary",), vmem_limit_bytes=VMEM_LIMIT_BYTES),
        name="in_proj",
    )(x2, g, w)


def _split_time(tile, slab_ref, s):
    for j in range(LANE_TILES):
        slab_ref[s, j * SUBLANES:(j + 1) * SUBLANES, :] = tile[:, j * LANES:(j + 1) * LANES]
    return [slab_ref[s, pl.ds(r, LANE_TILES, stride=SUBLANES), :] for r in range(SUBLANES)]


def _join_time(steps, slab_ref, s):
    for r in range(SUBLANES):
        slab_ref[s, pl.ds(r, LANE_TILES, stride=SUBLANES), :] = steps[r]
    return jnp.concatenate(
        [slab_ref[s, j * SUBLANES:(j + 1) * SUBLANES, :] for j in range(LANE_TILES)], axis=1)


def _half_decay(lam):
    return (-0.5 * RGLRU_C) * (jnp.maximum(-lam, 0.0) + jnp.log1p(jnp.exp(-jnp.abs(lam))))


def _gate_terms(pre_r, pre_i, uc, hdecay, hba, hbx):
    log_a = hdecay * jnp.tanh(pre_r + hba) + hdecay
    a = jnp.exp(log_a)
    y = jnp.maximum(1.0 - a * a, 0.0)
    beta = jnp.where(y > 0.0, y * lax.rsqrt(y), 0.0)
    i = 0.5 * jnp.tanh(pre_i + hbx) + 0.5
    return a, beta * (i * uc)


def _gelu_tanh(g):
    c = math.sqrt(2.0 / math.pi)
    half = 0.5 * g
    return half * jnp.tanh(g * ((c * 0.044715) * (g * g) + c)) + half


def _gate_maps(uc_ref, wg_ref, ra_ref, ix_ref):
    for gi in range(N_GROUPS):
        cols = slice(gi * LRU_GROUP, (gi + 1) * LRU_GROUP)
        pre = jnp.dot(uc_ref[:, cols].astype(BF16), wg_ref[gi], preferred_element_type=F32)
        ra_ref[:, cols] = pre[:, :LRU_GROUP]
        ix_ref[:, cols] = pre[:, LRU_GROUP:]


def _recurrence(reverse, uc_ref, ra_ref, ix_ref, lam_ref, ba_ref, bx_ref, carry_ref, slab_ref, emit):
    nt = uc_ref.shape[0] // SUBLANES
    nsteps = nt // TILES_PER_STEP
    hdecay = _half_decay(lam_ref[...])
    hba = 0.5 * ba_ref[...]
    hbx = 0.5 * bx_ref[...]

    def group(i, h):
        for s in range(TILES_PER_STEP):
            k = TILES_PER_STEP * i + s
            kk = (nt - 1 - k) if reverse else k
            rows = pl.ds(pl.multiple_of(kk * SUBLANES, SUBLANES), SUBLANES)
            a, b = _gate_terms(ra_ref[rows, :], ix_ref[rows, :], uc_ref[rows, :], hdecay, hba, hbx)
            a_t = _split_time(a, slab_ref, 3 * s)
            b_t = _split_time(b, slab_ref, 3 * s + 1)
            hs = [None] * SUBLANES
            for r in (range(SUBLANES - 1, -1, -1) if reverse else range(SUBLANES)):
                h = a_t[r] * h + b_t[r]
                hs[r] = h
            emit(rows, _join_time(hs, slab_ref, 3 * s + 2))
        return h

    carry_ref[...] = lax.fori_loop(0, nsteps, group, carry_ref[...])


def _lru_fwd_kernel(nblk, um_ref, up_ref, un_ref, cw_ref, cb_ref, lam_ref, wg_ref, ba_ref, bx_ref,
                    hf_ref, uc_ref, uext_ref, ra_ref, ix_ref, carry_ref, slab_ref):
    ts = um_ref.shape[0]
    nt = ts // SUBLANES
    j = pl.program_id(1)

    @pl.when(j == 0)
    def _():
        carry_ref[...] = jnp.zeros_like(carry_ref)

    uext_ref[0:SUBLANES, :] = jnp.where(j > 0, up_ref[...], 0.0)
    uext_ref[SUBLANES:SUBLANES + ts, :] = um_ref[...]
    uext_ref[SUBLANES + ts:, :] = jnp.where(j < nblk - 1, un_ref[...], 0.0)

    w = [cw_ref[k] for k in range(CONV_WIDTH)]
    cb = cb_ref[...]

    def ext_tile(e):
        return uext_ref[pl.ds(pl.multiple_of(e * SUBLANES, SUBLANES), SUBLANES), :]

    def conv_tiles(i, state):
        prev2, prev1, cur = state[0], state[1], list(state[2:])
        for s in range(2):
            t = 2 * i + s
            nxt = _split_time(ext_tile(t + 2), slab_ref, 2 * s)
            x = [prev2, prev1] + cur + [nxt[0]]
            out = [(x[r] * w[0] + x[r + 1] * w[1]) + (x[r + 2] * w[2] + x[r + 3] * w[3]) + cb
                   for r in range(SUBLANES)]
            uc_ref[pl.ds(pl.multiple_of(t * SUBLANES, SUBLANES), SUBLANES), :] = _join_time(
                out, slab_ref, 2 * s + 1)
            prev2, prev1, cur = cur[SUBLANES - 2], cur[SUBLANES - 1], nxt
        return (prev2, prev1, *cur)

    halo = _split_time(ext_tile(0), slab_ref, 4)
    first = _split_time(ext_tile(1), slab_ref, 5)
    lax.fori_loop(0, nt // 2, conv_tiles, (halo[SUBLANES - 2], halo[SUBLANES - 1], *first))

    _gate_maps(uc_ref, wg_ref, ra_ref, ix_ref)

    def emit(rows, h):
        hf_ref[rows, :] = h

    _recurrence(False, uc_ref, ra_ref, ix_ref, lam_ref, ba_ref, bx_ref, carry_ref, slab_ref, emit)


def _lru_bwd_kernel(uc_ref, hf_ref, g_ref, lam_ref, wg_ref, ba_ref, bx_ref,
                    ya_ref, ra_ref, ix_ref, carry_ref, slab_ref):
    @pl.when(pl.program_id(1) == 0)
    def _():
        carry_ref[...] = jnp.zeros_like(carry_ref)

    _gate_maps(uc_ref, wg_ref, ra_ref, ix_ref)

    def emit(rows, h):
        ya_ref[rows, :] = (hf_ref[rows, :] + h) * _gelu_tanh(g_ref[rows, :])

    _recurrence(True, uc_ref, ra_ref, ix_ref, lam_ref, ba_ref, bx_ref, carry_ref, slab_ref, emit)


_LRU_SCRATCH_TAIL = [
    pltpu.VMEM((TS_LRU, D_MODEL), F32),
    pltpu.VMEM((TS_LRU, D_MODEL), F32),
    pltpu.VMEM((LANE_TILES, LANES), F32),
    pltpu.VMEM((N_SLABS, SLAB, LANES), F32),
]


def _lru_fwd(u3, cw, cb, lam, wg, ba, bx):
    bsz, s, _ = u3.shape
    ts = TS_LRU
    nblk = s // ts
    hb = ts // SUBLANES
    nh = s // SUBLANES
    main = pl.BlockSpec((None, ts, D_MODEL), lambda b, j: (b, j, 0))
    prev = pl.BlockSpec((None, SUBLANES, D_MODEL), lambda b, j: (b, jnp.maximum(j * hb - 1, 0), 0))
    nxt = pl.BlockSpec((None, SUBLANES, D_MODEL), lambda b, j: (b, jnp.minimum((j + 1) * hb, nh - 1), 0))
    out = jax.ShapeDtypeStruct((bsz, s, D_MODEL), F32)
    return pl.pallas_call(
        functools.partial(_lru_fwd_kernel, nblk),
        grid=(bsz, nblk),
        in_specs=[main, prev, nxt, _resident(cw.shape), _resident(cb.shape), _resident((1, D_MODEL)),
                  _resident(wg.shape), _resident((1, D_MODEL)), _resident((1, D_MODEL))],
        out_specs=[main, main],
        out_shape=[out, out],
        scratch_shapes=[pltpu.VMEM((ts + 2 * SUBLANES, D_MODEL), F32)] + _LRU_SCRATCH_TAIL,
        compiler_params=pltpu.CompilerParams(
            dimension_semantics=("arbitrary", "arbitrary"), vmem_limit_bytes=VMEM_LIMIT_BYTES),
        name="lru_fwd",
    )(u3, u3, u3, cw, cb, lam, wg, ba, bx)


def _lru_bwd(uc3, hf3, g3, lam, wg, ba, bx):
    bsz, s, _ = uc3.shape
    ts = TS_LRU
    nblk = s // ts
    blk = pl.BlockSpec((None, ts, D_MODEL), lambda b, j: (b, nblk - 1 - j, 0))
    return pl.pallas_call(
        _lru_bwd_kernel,
        grid=(bsz, nblk),
        in_specs=[blk, blk, blk, _resident((1, D_MODEL)), _resident(wg.shape),
                  _resident((1, D_MODEL)), _resident((1, D_MODEL))],
        out_specs=blk,
        out_shape=jax.ShapeDtypeStruct((bsz, s, D_MODEL), F32),
        scratch_shapes=_LRU_SCRATCH_TAIL,
        compiler_params=pltpu.CompilerParams(
            dimension_semantics=("arbitrary", "arbitrary"), vmem_limit_bytes=VMEM_LIMIT_BYTES),
        name="lru_bwd",
    )(uc3, hf3, g3, lam, wg, ba, bx)


_N_BIAS = 3 * N_HEADS + 1
_MASKED_TILE = 3 * N_HEADS


def _attn_kernel(nb, sink_ref, q_ref, kp_ref, km_ref, kn_ref, vp_ref, vm_ref, vn_ref,
                 o_ref, kz_ref, vab_ref, bias_ref, s_ref, m_ref, p_ref):
    tq = q_ref.shape[0]
    tk = tq + 2 * BLOCK
    nsub = tq // BLOCK
    j = pl.program_id(1)

    @pl.when((pl.program_id(0) == 0) & (j == 0))
    def _():
        row = lax.broadcasted_iota(jnp.int32, (BLOCK, BLOCK), 0)
        col = lax.broadcasted_iota(jnp.int32, (BLOCK, BLOCK), 1)
        bias_ref[_MASKED_TILE] = jnp.ones((BLOCK, BLOCK), F32)
        for c in range(3):
            dist = jnp.abs(row + (1 - c) * BLOCK - col)
            absd = dist.astype(F32)
            for h in range(N_HEADS):
                bias_ref[3 * h + c] = jnp.where(dist <= WINDOW, -_ALIBI_SLOPES[h] * absd, 1.0)

    low = lax.broadcasted_iota(jnp.int32, (tk, LANES), 1) < HEAD_DIM
    low_q = lax.broadcasted_iota(jnp.int32, (BLOCK, LANES), 1) < HEAD_DIM
    ones_q = [jnp.where(low_q, 1.0, 0.0).astype(BF16), jnp.where(low_q, 0.0, 1.0).astype(BF16)]
    for t in range(KV_W // LANES):
        cols = slice(t * LANES, (t + 1) * LANES)
        full = jnp.concatenate([r[:, cols] for r in (kp_ref, km_ref, kn_ref)], axis=0).astype(F32)
        swapped = pltpu.roll(full, HEAD_DIM, axis=1)
        kz_ref[4 * t + 0] = jnp.where(low, full, 0.0).astype(BF16)
        kz_ref[4 * t + 1] = jnp.where(low, 0.0, swapped).astype(BF16)
        kz_ref[4 * t + 2] = jnp.where(low, swapped, 0.0).astype(BF16)
        kz_ref[4 * t + 3] = jnp.where(low, 0.0, full).astype(BF16)
        nkb = tk // BLOCK
        for b in range(nkb):
            if b == 0:
                src = vp_ref[:, cols]
            elif b == nkb - 1:
                src = vn_ref[:, cols]
            else:
                src = vm_ref[(b - 1) * BLOCK:b * BLOCK, cols]
            full = src.astype(F32)
            swapped = pltpu.roll(full, HEAD_DIM, axis=1)
            placed = [jnp.where(low_q, full, 0.0), jnp.where(low_q, 0.0, swapped),
                      jnp.where(low_q, swapped, 0.0), jnp.where(low_q, 0.0, full)]
            for i in range(4):
                g, half = 2 * t + i // 2, i % 2
                dst = slice((2 * b + half) * BLOCK, (2 * b + half + 1) * BLOCK)
                vab_ref[g, dst, :LANES] = placed[i].astype(BF16)
                vab_ref[g, dst, LANES:] = ones_q[half]

    def scores(qi, slot):
        q0 = pl.multiple_of(qi * BLOCK, BLOCK)
        blk = j * nsub + qi
        win = pl.ds(q0, 3 * BLOCK)
        for h in range(N_HEADS):
            g, pp, half = h // GROUP, (h % GROUP) // 2, h % 2
            qt = q_ref[pl.ds(q0, BLOCK), (2 * g + pp) * LANES:(2 * g + pp + 1) * LANES]
            sc = lax.dot_general(qt, kz_ref[2 * g + half, win, :], (((1,), (1,)), ((), ())),
                                 preferred_element_type=F32)
            tiles = []
            for c in range(3):
                t = sc[:, c * BLOCK:(c + 1) * BLOCK]
                if c == 1:
                    t = t + bias_ref[3 * h + 1]
                else:
                    edge = (blk == 0) if c == 0 else (blk == nb - 1)
                    b = bias_ref[jnp.where(edge, _MASKED_TILE, 3 * h + c)]
                    t = jnp.where(b > 0.0, NEG_INF, t + b)
                s_ref[slot, h, :, c * BLOCK:(c + 1) * BLOCK] = t
                tiles.append(t)
            mx = jnp.max(jnp.maximum(jnp.maximum(tiles[0], tiles[1]), tiles[2]), axis=-1, keepdims=True)
            m_ref[slot, h] = jnp.broadcast_to(jnp.maximum(mx, sink_ref[h]), (BLOCK, LANES))

    def finish(qi, slot):
        q0 = pl.multiple_of(qi * BLOCK, BLOCK)
        win2 = pl.ds(pl.multiple_of(qi * (2 * BLOCK), 2 * BLOCK), 6 * BLOCK)
        for h in range(N_HEADS):
            m = m_ref[slot, h]
            for c in range(3):
                cs = slice(c * BLOCK, (c + 1) * BLOCK)
                dst = slice((2 * c + h % 2) * BLOCK, (2 * c + h % 2 + 1) * BLOCK)
                p_ref[slot, h // 2, :, dst] = jnp.exp(s_ref[slot, h, :, cs] - m).astype(BF16)
        for pr in range(N_HEADS // 2):
            g = pr // 2
            h0, h1 = 2 * pr, 2 * pr + 1
            o = jnp.dot(p_ref[slot, pr], vab_ref[g, win2, :], preferred_element_type=F32)
            e = jnp.where(low_q, jnp.exp(sink_ref[h0] - m_ref[slot, h0]),
                          jnp.exp(sink_ref[h1] - m_ref[slot, h1]))
            o_ref[pl.ds(q0, BLOCK), pr * LANES:(pr + 1) * LANES] = o[:, :LANES] * (1.0 / (o[:, LANES:] + e))

    scores(0, 0)

    def two_sub_blocks(i, _):
        qa = 2 * i
        scores(qa + 1, 1)
        finish(qa, 0)
        scores(jnp.minimum(qa + 2, nsub - 1), 0)
        finish(qa + 1, 1)
        return 0

    lax.fori_loop(0, nsub // 2, two_sub_blocks, 0)


def _attention(q3, k3, v3, sink):
    bsz, s, _ = q3.shape
    tq = TQ_ATTN
    nq = s // tq
    hb = tq // BLOCK
    nb = s // BLOCK
    tk = tq + 2 * BLOCK
    kv_main = pl.BlockSpec((None, tq, KV_W), lambda b, j: (b, j, 0))
    kv_prev = pl.BlockSpec((None, BLOCK, KV_W), lambda b, j: (b, jnp.maximum(j * hb - 1, 0), 0))
    kv_next = pl.BlockSpec((None, BLOCK, KV_W), lambda b, j: (b, jnp.minimum((j + 1) * hb, nb - 1), 0))
    qo = pl.BlockSpec((None, tq, Q_W), lambda b, j: (b, j, 0))
    return pl.pallas_call(
        functools.partial(_attn_kernel, nb),
        grid=(bsz, nq),
        in_specs=[pl.BlockSpec(memory_space=pltpu.SMEM), qo,
                  kv_prev, kv_main, kv_next, kv_prev, kv_main, kv_next],
        out_specs=qo,
        out_shape=jax.ShapeDtypeStruct((bsz, s, Q_W), F32),
        scratch_shapes=[
            pltpu.VMEM((2 * N_KV_HEADS, tk, LANES), BF16),
            pltpu.VMEM((N_KV_HEADS, 2 * tk, 2 * LANES), BF16),
            pltpu.VMEM((_N_BIAS, BLOCK, BLOCK), F32),
            pltpu.VMEM((2, N_HEADS, BLOCK, 3 * BLOCK), F32),
            pltpu.VMEM((2, N_HEADS, BLOCK, LANES), F32),
            pltpu.VMEM((2, N_HEADS // 2, BLOCK, 6 * BLOCK), BF16),
        ],
        compiler_params=pltpu.CompilerParams(
            dimension_semantics=("arbitrary", "arbitrary"), vmem_limit_bytes=VMEM_LIMIT_BYTES),
        name="attention",
    )(sink, q3, k3, k3, k3, v3, v3, v3)


def _ffn_kernel(x_ref, ya_ref, yb_ref, z_ref, bg_ref, wo_ref, g2_ref, wi_ref, wd_ref, g3_ref, o_ref):
    gates = _sigmoid(z_ref[...] + bg_ref[...])
    merged = gates[:, :D_MODEL] * ya_ref[...] + gates[:, D_MODEL:] * yb_ref[...]
    x1 = x_ref[...] + jnp.dot(merged.astype(BF16), wo_ref[...], preferred_element_type=F32)
    xn = _rmsnorm(x1, g2_ref[...]).astype(BF16)
    gate = jnp.dot(xn, wi_ref[:, :D_FF], preferred_element_type=F32)
    up = jnp.dot(xn, wi_ref[:, D_FF:], preferred_element_type=F32)
    act = (gate * _sigmoid(gate) * up).astype(BF16)
    x2 = x1 + jnp.dot(act, wd_ref[...], preferred_element_type=F32)
    o_ref[...] = _rmsnorm(x2, g3_ref[...])


def _ffn(x2, ya, yb, z, bg, wo, g2, wi, wd, g3):
    t = x2.shape[0]
    tm = TM_FFN
    row = lambda w_: pl.BlockSpec((tm, w_), lambda i: (i, 0))
    return pl.pallas_call(
        _ffn_kernel,
        grid=(t // tm,),
        in_specs=[row(D_MODEL), row(D_MODEL), row(D_MODEL), row(2 * D_MODEL),
                  _resident((1, 2 * D_MODEL)), _resident((D_MODEL, D_MODEL)), _resident((1, D_MODEL)),
                  _resident((D_MODEL, 2 * D_FF)), _resident((D_FF, D_MODEL)), _resident((1, D_MODEL))],
        out_specs=row(D_MODEL),
        out_shape=jax.ShapeDtypeStruct((t, D_MODEL), F32),
        compiler_params=pltpu.CompilerParams(
            dimension_semantics=("arbitrary",), vmem_limit_bytes=VMEM_LIMIT_BYTES),
        name="merge_ffn",
    )(x2, ya, yb, z, bg, wo, g2, wi, wd, g3)


def _gate_weights(wa, wx):
    def blockdiag(w):
        w = 0.5 * w.reshape(N_GROUPS, LRU_GROUP // LRU_BLOCK, LRU_BLOCK, LRU_BLOCK)
        eye = jnp.eye(LRU_GROUP // LRU_BLOCK, dtype=w.dtype)
        return jnp.einsum("ghij,hk->ghikj", w, eye).reshape(N_GROUPS, LRU_GROUP, LRU_GROUP)
    return jnp.concatenate([blockdiag(wa), blockdiag(wx)], axis=-1).astype(BF16)


def kernel(x, norm_mix_g, w_in, b_gate, conv_w, conv_b, lru_lambda, lru_wa, lru_ba, lru_wx, lru_bx,
           attn_sink, w_out, norm_ffn_g, w_ffn_in, w_ffn_out, norm_final_g):
    bsz, s, d = x.shape
    depth = w_in.shape[0]
    assert depth == 1, "the merge/ffn kernel applies the final RMSNorm, so it must be the last layer"
    t = bsz * s
    row = lambda a: a.reshape(1, -1)
    time_major = lambda a: a.reshape(a.shape[:-1] + (LANE_TILES, LANES))
    x2 = x.reshape(t, d)
    for l in range(depth):
        u, g_lru, q, k, v, z = _in_proj(x2, row(norm_mix_g[l]), w_in[l].astype(BF16))
        seq = lambda a: a.reshape(bsz, s, a.shape[-1])
        gates = lambda dr: (row(lru_lambda[l, dr]), _gate_weights(lru_wa[l, dr], lru_wx[l, dr]),
                            row(lru_ba[l, dr]), row(lru_bx[l, dr]))
        h_fwd, uc = _lru_fwd(seq(u), time_major(conv_w[l]), time_major(conv_b[l]), *gates(0))
        y_a = _lru_bwd(uc, h_fwd, seq(g_lru), *gates(1))
        y_b = _attention(seq(q), seq(k), seq(v), attn_sink[l])
        x2 = _ffn(x2, y_a.reshape(t, d), y_b.reshape(t, d), z, row(b_gate[l]),
                  w_out[l].astype(BF16), row(norm_ffn_g[l]), w_ffn_in[l].astype(BF16),
                  w_ffn_out[l].astype(BF16), row(norm_final_g))
    return x2.reshape(bsz, s, d)
```

```python
import functools
import math

import numpy as np
import jax
import jax.numpy as jnp
from jax import lax
from jax.experimental import pallas as pl
from jax.experimental.pallas import tpu as pltpu

F32 = jnp.float32
BF16 = jnp.bfloat16

D_MODEL = 1024
LRU_HEADS = 16
LRU_BLOCK = D_MODEL // LRU_HEADS
CONV_WIDTH = 4
CONV_LEFT = 2
RGLRU_C = 8.0
N_HEADS = 16
N_KV_HEADS = 4
HEAD_DIM = 64
GROUP = N_HEADS // N_KV_HEADS
WINDOW = 128
BLOCK = 128
D_FF = 2816
Q_W = N_HEADS * HEAD_DIM
KV_W = N_KV_HEADS * HEAD_DIM
IN_W = 2 * D_MODEL + Q_W + 2 * KV_W + 2 * D_MODEL
EPS = 1e-6
NEG_INF = -1e30

LANES = 128
SUBLANES = 8
VMEM_LIMIT_BYTES = 56 * 1024 * 1024

TM_PROJ = 512
TS_LRU = 512
TQ_ATTN = 1024
TM_FFN = 256
LRU_GROUP = 256
N_GROUPS = D_MODEL // LRU_GROUP
LANE_TILES = D_MODEL // LANES
assert LANE_TILES == SUBLANES
SLAB = SUBLANES * LANE_TILES
TILES_PER_STEP = 4
N_SLABS = 3 * TILES_PER_STEP

_ALIBI_SLOPES = [float(v) for v in np.exp2(
    -8.0 * (np.arange(N_HEADS, dtype=np.float32) + 1.0) / N_HEADS).astype(np.float32)]


def _sigmoid(x):
    return 0.5 * jnp.tanh(0.5 * x) + 0.5


def _rmsnorm(x, g):
    return x * lax.rsqrt(jnp.mean(x * x, axis=-1, keepdims=True) + EPS) * g


def _resident(shape):
    nd = len(shape)
    return pl.BlockSpec(shape, lambda *_: (0,) * nd, pipeline_mode=pl.Buffered(1))


def _in_proj_kernel(x_ref, g_ref, w_ref, u_ref, gl_ref, q_ref, k_ref, v_ref, z_ref):
    xn = _rmsnorm(x_ref[...], g_ref[...]).astype(BF16)

    def proj(lo, hi):
        return jnp.dot(xn, w_ref[:, lo:hi], preferred_element_type=F32)

    o = 0
    u_ref[...] = proj(o, o + D_MODEL); o += D_MODEL
    gl_ref[...] = proj(o, o + D_MODEL); o += D_MODEL
    q_ref[...] = (proj(o, o + Q_W) * (HEAD_DIM ** -0.5)).astype(BF16); o += Q_W
    k_ref[...] = proj(o, o + KV_W).astype(BF16); o += KV_W
    v_ref[...] = proj(o, o + KV_W).astype(BF16); o += KV_W
    z_ref[...] = proj(o, o + 2 * D_MODEL)


def _in_proj(x2, g, w):
    t = x2.shape[0]
    tm = TM_PROJ
    row = lambda w_: pl.BlockSpec((tm, w_), lambda i: (i, 0))
    return pl.pallas_call(
        _in_proj_kernel,
        grid=(t // tm,),
        in_specs=[row(D_MODEL), _resident((1, D_MODEL)), _resident((D_MODEL, IN_W))],
        out_specs=[row(D_MODEL), row(D_MODEL), row(Q_W), row(KV_W), row(KV_W), row(2 * D_MODEL)],
        out_shape=[
            jax.ShapeDtypeStruct((t, D_MODEL), F32),
            jax.ShapeDtypeStruct((t, D_MODEL), F32),
            jax.ShapeDtypeStruct((t, Q_W), BF16),
            jax.ShapeDtypeStruct((t, KV_W), BF16),
            jax.ShapeDtypeStruct((t, KV_W), BF16),
            jax.ShapeDtypeStruct((t, 2 * D_MODEL), F32),
        ],
        compiler_params=pltpu.CompilerParams(
            dimension_semantics=("arbitrary",), vmem_limit_bytes=VMEM_LIMIT_BYTES),
        name="in_proj",
    )(x2, g, w)


def _split_time(tile, slab_ref, s):
    for j in range(LANE_TILES):
        slab_ref[s, j * SUBLANES:(j + 1) * SUBLANES, :] = tile[:, j * LANES:(j + 1) * LANES]
    return [slab_ref[s, pl.ds(r, LANE_TILES, stride=SUBLANES), :] for r in range(SUBLANES)]


def _join_time(steps, slab_ref, s):
    for r in range(SUBLANES):
        slab_ref[s, pl.ds(r, LANE_TILES, stride=SUBLANES), :] = steps[r]
    return jnp.concatenate(
        [slab_ref[s, j * SUBLANES:(j + 1) * SUBLANES, :] for j in range(LANE_TILES)], axis=1)


def _half_decay(lam):
    return (-0.5 * RGLRU_C) * (jnp.maximum(-lam, 0.0) + jnp.log1p(jnp.exp(-jnp.abs(lam))))


def _gate_terms(pre_r, pre_i, uc, hdecay, hba, hbx):
    log_a = hdecay * jnp.tanh(pre_r + hba) + hdecay
    a = jnp.exp(log_a)
    y = jnp.maximum(1.0 - a * a, 0.0)
    beta = jnp.where(y > 0.0, y * lax.rsqrt(y), 0.0)
    i = 0.5 * jnp.tanh(pre_i + hbx) + 0.5
    return a, beta * (i * uc)


def _gelu_tanh(g):
    c = math.sqrt(2.0 / math.pi)
    half = 0.5 * g
    return half * jnp.tanh(g * ((c * 0.044715) * (g * g) + c)) + half


def _gate_maps(uc_ref, wg_ref, ra_ref, ix_ref):
    for gi in range(N_GROUPS):
        cols = slice(gi * LRU_GROUP, (gi + 1) * LRU_GROUP)
        pre = jnp.dot(uc_ref[:, cols].astype(BF16), wg_ref[gi], preferred_element_type=F32)
        ra_ref[:, cols] = pre[:, :LRU_GROUP]
        ix_ref[:, cols] = pre[:, LRU_GROUP:]


def _recurrence(reverse, uc_ref, ra_ref, ix_ref, lam_ref, ba_ref, bx_ref, carry_ref, slab_ref, emit):
    nt = uc_ref.shape[0] // SUBLANES
    nsteps = nt // TILES_PER_STEP
    hdecay = _half_decay(lam_ref[...])
    hba = 0.5 * ba_ref[...]
    hbx = 0.5 * bx_ref[...]

    def group(i, h):
        for s in range(TILES_PER_STEP):
            k = TILES_PER_STEP * i + s
            kk = (nt - 1 - k) if reverse else k
            rows = pl.ds(pl.multiple_of(kk * SUBLANES, SUBLANES), SUBLANES)
            a, b = _gate_terms(ra_ref[rows, :], ix_ref[rows, :], uc_ref[rows, :], hdecay, hba, hbx)
            a_t = _split_time(a, slab_ref, 3 * s)
            b_t = _split_time(b, slab_ref, 3 * s + 1)
            hs = [None] * SUBLANES
            for r in (range(SUBLANES - 1, -1, -1) if reverse else range(SUBLANES)):
                h = a_t[r] * h + b_t[r]
                hs[r] = h
            emit(rows, _join_time(hs, slab_ref, 3 * s + 2))
        return h

    carry_ref[...] = lax.fori_loop(0, nsteps, group, carry_ref[...])


def _lru_fwd_kernel(nblk, um_ref, up_ref, un_ref, cw_ref, cb_ref, lam_ref, wg_ref, ba_ref, bx_ref,
                    hf_ref, uc_ref, uext_ref, ra_ref, ix_ref, carry_ref, slab_ref):
    ts = um_ref.shape[0]
    nt = ts // SUBLANES
    j = pl.program_id(1)

    @pl.when(j == 0)
    def _():
        carry_ref[...] = jnp.zeros_like(carry_ref)

    uext_ref[0:SUBLANES, :] = jnp.where(j > 0, up_ref[...], 0.0)
    uext_ref[SUBLANES:SUBLANES + ts, :] = um_ref[...]
    uext_ref[SUBLANES + ts:, :] = jnp.where(j < nblk - 1, un_ref[...], 0.0)

    w = [cw_ref[k] for k in range(CONV_WIDTH)]
    cb = cb_ref[...]

    def ext_tile(e):
        return uext_ref[pl.ds(pl.multiple_of(e * SUBLANES, SUBLANES), SUBLANES), :]

    def conv_tiles(i, state):
        prev2, prev1, cur = state[0], state[1], list(state[2:])
        for s in range(2):
            t = 2 * i + s
            nxt = _split_time(ext_tile(t + 2), slab_ref, 2 * s)
            x = [prev2, prev1] + cur + [nxt[0]]
            out = [(x[r] * w[0] + x[r + 1] * w[1]) + (x[r + 2] * w[2] + x[r + 3] * w[3]) + cb
                   for r in range(SUBLANES)]
            uc_ref[pl.ds(pl.multiple_of(t * SUBLANES, SUBLANES), SUBLANES), :] = _join_time(
                out, slab_ref, 2 * s + 1)
            prev2, prev1, cur = cur[SUBLANES - 2], cur[SUBLANES - 1], nxt
        return (prev2, prev1, *cur)

    halo = _split_time(ext_tile(0), slab_ref, 4)
    first = _split_time(ext_tile(1), slab_ref, 5)
    lax.fori_loop(0, nt // 2, conv_tiles, (halo[SUBLANES - 2], halo[SUBLANES - 1], *first))

    _gate_maps(uc_ref, wg_ref, ra_ref, ix_ref)

    def emit(rows, h):
        hf_ref[rows, :] = h

    _recurrence(False, uc_ref, ra_ref, ix_ref, lam_ref, ba_ref, bx_ref, carry_ref, slab_ref, emit)


def _lru_bwd_kernel(uc_ref, hf_ref, g_ref, lam_ref, wg_ref, ba_ref, bx_ref,
                    ya_ref, ra_ref, ix_ref, carry_ref, slab_ref):
    @pl.when(pl.program_id(1) == 0)
    def _():
        carry_ref[...] = jnp.zeros_like(carry_ref)

    _gate_maps(uc_ref, wg_ref, ra_ref, ix_ref)

    def emit(rows, h):
        ya_ref[rows, :] = (hf_ref[rows, :] + h) * _gelu_tanh(g_ref[rows, :])

    _recurrence(True, uc_ref, ra_ref, ix_ref, lam_ref, ba_ref, bx_ref, carry_ref, slab_ref, emit)


_LRU_SCRATCH_TAIL = [
    pltpu.VMEM((TS_LRU, D_MODEL), F32),
    pltpu.VMEM((TS_LRU, D_MODEL), F32),
    pltpu.VMEM((LANE_TILES, LANES), F32),
    pltpu.VMEM((N_SLABS, SLAB, LANES), F32),
]


def _lru_fwd(u3, cw, cb, lam, wg, ba, bx):
    bsz, s, _ = u3.shape
    ts = TS_LRU
    nblk = s // ts
    hb = ts // SUBLANES
    nh = s // SUBLANES
    main = pl.BlockSpec((None, ts, D_MODEL), lambda b, j: (b, j, 0))
    prev = pl.BlockSpec((None, SUBLANES, D_MODEL), lambda b, j: (b, jnp.maximum(j * hb - 1, 0), 0))
    nxt = pl.BlockSpec((None, SUBLANES, D_MODEL), lambda b, j: (b, jnp.minimum((j + 1) * hb, nh - 1), 0))
    out = jax.ShapeDtypeStruct((bsz, s, D_MODEL), F32)
    return pl.pallas_call(
        functools.partial(_lru_fwd_kernel, nblk),
        grid=(bsz, nblk),
        in_specs=[main, prev, nxt, _resident(cw.shape), _resident(cb.shape), _resident((1, D_MODEL)),
                  _resident(wg.shape), _resident((1, D_MODEL)), _resident((1, D_MODEL))],
        out_specs=[main, main],
        out_shape=[out, out],
        scratch_shapes=[pltpu.VMEM((ts + 2 * SUBLANES, D_MODEL), F32)] + _LRU_SCRATCH_TAIL,
        compiler_params=pltpu.CompilerParams(
            dimension_semantics=("arbitrary", "arbitrary"), vmem_limit_bytes=VMEM_LIMIT_BYTES),
        name="lru_fwd",
    )(u3, u3, u3, cw, cb, lam, wg, ba, bx)


def _lru_bwd(uc3, hf3, g3, lam, wg, ba, bx):
    bsz, s, _ = uc3.shape
    ts = TS_LRU
    nblk = s // ts
    blk = pl.BlockSpec((None, ts, D_MODEL), lambda b, j: (b, nblk - 1 - j, 0))
    return pl.pallas_call(
        _lru_bwd_kernel,
        grid=(bsz, nblk),
        in_specs=[blk, blk, blk, _resident((1, D_MODEL)), _resident(wg.shape),
                  _resident((1, D_MODEL)), _resident((1, D_MODEL))],
        out_specs=blk,
        out_shape=jax.ShapeDtypeStruct((bsz, s, D_MODEL), F32),
        scratch_shapes=_LRU_SCRATCH_TAIL,
        compiler_params=pltpu.CompilerParams(
            dimension_semantics=("arbitrary", "arbitrary"), vmem_limit_bytes=VMEM_LIMIT_BYTES),
        name="lru_bwd",
    )(uc3, hf3, g3, lam, wg, ba, bx)


_N_BIAS = 3 * N_HEADS + 1
_MASKED_TILE = 3 * N_HEADS


def _attn_kernel(nb, sink_ref, q_ref, kp_ref, km_ref, kn_ref, vp_ref, vm_ref, vn_ref,
                 o_ref, kz_ref, vab_ref, bias_ref, s_ref, m_ref, p_ref):
    tq = q_ref.shape[0]
    tk = tq + 2 * BLOCK
    nsub = tq // BLOCK
    j = pl.program_id(1)

    @pl.when((pl.program_id(0) == 0) & (j == 0))
    def _():
        row = lax.broadcasted_iota(jnp.int32, (BLOCK, BLOCK), 0)
        col = lax.broadcasted_iota(jnp.int32, (BLOCK, BLOCK), 1)
        bias_ref[_MASKED_TILE] = jnp.ones((BLOCK, BLOCK), F32)
        for c in range(3):
            dist = jnp.abs(row + (1 - c) * BLOCK - col)
            absd = dist.astype(F32)
            for h in range(N_HEADS):
                bias_ref[3 * h + c] = jnp.where(dist <= WINDOW, -_ALIBI_SLOPES[h] * absd, 1.0)

    low = lax.broadcasted_iota(jnp.int32, (tk, LANES), 1) < HEAD_DIM
    low_q = lax.broadcasted_iota(jnp.int32, (BLOCK, LANES), 1) < HEAD_DIM
    ones_q = [jnp.where(low_q, 1.0, 0.0).astype(BF16), jnp.where(low_q, 0.0, 1.0).astype(BF16)]
    for t in range(KV_W // LANES):
        cols = slice(t * LANES, (t + 1) * LANES)
        full = jnp.concatenate([r[:, cols] for r in (kp_ref, km_ref, kn_ref)], axis=0).astype(F32)
        swapped = pltpu.roll(full, HEAD_DIM, axis=1)
        kz_ref[4 * t + 0] = jnp.where(low, full, 0.0).astype(BF16)
        kz_ref[4 * t + 1] = jnp.where(low, 0.0, swapped).astype(BF16)
        kz_ref[4 * t + 2] = jnp.where(low, swapped, 0.0).astype(BF16)
        kz_ref[4 * t + 3] = jnp.where(low, 0.0, full).astype(BF16)
        nkb = tk // BLOCK
        for b in range(nkb):
            if b == 0:
                src = vp_ref[:, cols]
            elif b == nkb - 1:
                src = vn_ref[:, cols]
            else:
                src = vm_ref[(b - 1) * BLOCK:b * BLOCK, cols]
            full = src.astype(F32)
            swapped = pltpu.roll(full, HEAD_DIM, axis=1)
            placed = [jnp.where(low_q, full, 0.0), jnp.where(low_q, 0.0, swapped),
                      jnp.where(low_q, swapped, 0.0), jnp.where(low_q, 0.0, full)]
            for i in range(4):
                g, half = 2 * t + i // 2, i % 2
                dst = slice((2 * b + half) * BLOCK, (2 * b + half + 1) * BLOCK)
                vab_ref[g, dst, :LANES] = placed[i].astype(BF16)
                vab_ref[g, dst, LANES:] = ones_q[half]

    def scores(qi, slot):
        q0 = pl.multiple_of(qi * BLOCK, BLOCK)
        blk = j * nsub + qi
        win = pl.ds(q0, 3 * BLOCK)
        sc2 = [None, None]
        for h in range(N_HEADS):
            g, pp, half = h // GROUP, (h % GROUP) // 2, h % 2
            if pp == 0:
                q2 = jnp.concatenate(
                    [q_ref[pl.ds(q0, BLOCK), (2 * g + i) * LANES:(2 * g + i + 1) * LANES] for i in range(2)],
                    axis=0)
                sc2[half] = lax.dot_general(q2, kz_ref[2 * g + half, win, :], (((1,), (1,)), ((), ())),
                                            preferred_element_type=F32)
            sc = sc2[half][pp * BLOCK:(pp + 1) * BLOCK]
            tiles = []
            for c in range(3):
                t = sc[:, c * BLOCK:(c + 1) * BLOCK]
                if c == 1:
                    t = t + bias_ref[3 * h + 1]
                else:
                    edge = (blk == 0) if c == 0 else (blk == nb - 1)
                    b = bias_ref[jnp.where(edge, _MASKED_TILE, 3 * h + c)]
                    t = jnp.where(b > 0.0, NEG_INF, t + b)
                s_ref[slot, h, :, c * BLOCK:(c + 1) * BLOCK] = t
                tiles.append(t)
            mx = jnp.max(jnp.maximum(jnp.maximum(tiles[0], tiles[1]), tiles[2]), axis=-1, keepdims=True)
            m_ref[slot, h] = jnp.broadcast_to(jnp.maximum(mx, sink_ref[h]), (BLOCK, LANES))

    def finish(qi, slot):
        q0 = pl.multiple_of(qi * BLOCK, BLOCK)
        win2 = pl.ds(pl.multiple_of(qi * (2 * BLOCK), 2 * BLOCK), 6 * BLOCK)
        for h in range(N_HEADS):
            m = m_ref[slot, h]
            for c in range(3):
                cs = slice(c * BLOCK, (c + 1) * BLOCK)
                dst = slice((2 * c + h % 2) * BLOCK, (2 * c + h % 2 + 1) * BLOCK)
                prow = slice(((h // 2) % 2) * BLOCK, ((h // 2) % 2 + 1) * BLOCK)
                p_ref[slot, h // GROUP, prow, dst] = jnp.exp(s_ref[slot, h, :, cs] - m).astype(BF16)
        for g in range(N_KV_HEADS):
            o2 = jnp.dot(p_ref[slot, g], vab_ref[g, win2, :], preferred_element_type=F32)
            for pp in range(2):
                pr = 2 * g + pp
                h0, h1 = 2 * pr, 2 * pr + 1
                o = o2[pp * BLOCK:(pp + 1) * BLOCK]
                e = jnp.where(low_q, jnp.exp(sink_ref[h0] - m_ref[slot, h0]),
                              jnp.exp(sink_ref[h1] - m_ref[slot, h1]))
                o_ref[pl.ds(q0, BLOCK), pr * LANES:(pr + 1) * LANES] = (
                    o[:, :LANES] * (1.0 / (o[:, LANES:] + e)))

    scores(0, 0)

    def two_sub_blocks(i, _):
        qa = 2 * i
        scores(qa + 1, 1)
        finish(qa, 0)
        scores(jnp.minimum(qa + 2, nsub - 1), 0)
        finish(qa + 1, 1)
        return 0

    lax.fori_loop(0, nsub // 2, two_sub_blocks, 0)


def _attention(q3, k3, v3, sink):
    bsz, s, _ = q3.shape
    tq = TQ_ATTN
    nq = s // tq
    hb = tq // BLOCK
    nb = s // BLOCK
    tk = tq + 2 * BLOCK
    kv_main = pl.BlockSpec((None, tq, KV_W), lambda b, j: (b, j, 0))
    kv_prev = pl.BlockSpec((None, BLOCK, KV_W), lambda b, j: (b, jnp.maximum(j * hb - 1, 0), 0))
    kv_next = pl.BlockSpec((None, BLOCK, KV_W), lambda b, j: (b, jnp.minimum((j + 1) * hb, nb - 1), 0))
    qo = pl.BlockSpec((None, tq, Q_W), lambda b, j: (b, j, 0))
    return pl.pallas_call(
        functools.partial(_attn_kernel, nb),
        grid=(bsz, nq),
        in_specs=[pl.BlockSpec(memory_space=pltpu.SMEM), qo,
                  kv_prev, kv_main, kv_next, kv_prev, kv_main, kv_next],
        out_specs=qo,
        out_shape=jax.ShapeDtypeStruct((bsz, s, Q_W), F32),
        scratch_shapes=[
            pltpu.VMEM((2 * N_KV_HEADS, tk, LANES), BF16),
            pltpu.VMEM((N_KV_HEADS, 2 * tk, 2 * LANES), BF16),
            pltpu.VMEM((_N_BIAS, BLOCK, BLOCK), F32),
            pltpu.VMEM((2, N_HEADS, BLOCK, 3 * BLOCK), F32),
            pltpu.VMEM((2, N_HEADS, BLOCK, LANES), F32),
            pltpu.VMEM((2, N_KV_HEADS, 2 * BLOCK, 6 * BLOCK), BF16),
        ],
        compiler_params=pltpu.CompilerParams(
            dimension_semantics=("arbitrary", "arbitrary"), vmem_limit_bytes=VMEM_LIMIT_BYTES),
        name="attention",
    )(sink, q3, k3, k3, k3, v3, v3, v3)


def _ffn_kernel(x_ref, ya_ref, yb_ref, z_ref, bg_ref, wo_ref, g2_ref, wi_ref, wd_ref, g3_ref, o_ref):
    gates = _sigmoid(z_ref[...] + bg_ref[...])
    merged = gates[:, :D_MODEL] * ya_ref[...] + gates[:, D_MODEL:] * yb_ref[...]
    x1 = x_ref[...] + jnp.dot(merged.astype(BF16), wo_ref[...], preferred_element_type=F32)
    xn = _rmsnorm(x1, g2_ref[...]).astype(BF16)
    gate = jnp.dot(xn, wi_ref[:, :D_FF], preferred_element_type=F32)
    up = jnp.dot(xn, wi_ref[:, D_FF:], preferred_element_type=F32)
    act = (gate * _sigmoid(gate) * up).astype(BF16)
    x2 = x1 + jnp.dot(act, wd_ref[...], preferred_element_type=F32)
    o_ref[...] = _rmsnorm(x2, g3_ref[...])


def _ffn(x2, ya, yb, z, bg, wo, g2, wi, wd, g3):
    t = x2.shape[0]
    tm = TM_FFN
    row = lambda w_: pl.BlockSpec((tm, w_), lambda i: (i, 0))
    return pl.pallas_call(
        _ffn_kernel,
        grid=(t // tm,),
        in_specs=[row(D_MODEL), row(D_MODEL), row(D_MODEL), row(2 * D_MODEL),
                  _resident((1, 2 * D_MODEL)), _resident((D_MODEL, D_MODEL)), _resident((1, D_MODEL)),
                  _resident((D_MODEL, 2 * D_FF)), _resident((D_FF, D_MODEL)), _resident((1, D_MODEL))],
        out_specs=row(D_MODEL),
        out_shape=jax.ShapeDtypeStruct((t, D_MODEL), F32),
        compiler_params=pltpu.CompilerParams(
            dimension_semantics=("arbitrary",), vmem_limit_bytes=VMEM_LIMIT_BYTES),
        name="merge_ffn",
    )(x2, ya, yb, z, bg, wo, g2, wi, wd, g3)


def _gate_weights(wa, wx):
    def blockdiag(w):
        w = 0.5 * w.reshape(N_GROUPS, LRU_GROUP // LRU_BLOCK, LRU_BLOCK, LRU_BLOCK)
        eye = jnp.eye(LRU_GROUP // LRU_BLOCK, dtype=w.dtype)
        return jnp.einsum("ghij,hk->ghikj", w, eye).reshape(N_GROUPS, LRU_GROUP, LRU_GROUP)
    return jnp.concatenate([blockdiag(wa), blockdiag(wx)], axis=-1).astype(BF16)


def kernel(x, norm_mix_g, w_in, b_gate, conv_w, conv_b, lru_lambda, lru_wa, lru_ba, lru_wx, lru_bx,
           attn_sink, w_out, norm_ffn_g, w_ffn_in, w_ffn_out, norm_final_g):
    bsz, s, d = x.shape
    depth = w_in.shape[0]
    assert depth == 1, "the merge/ffn kernel applies the final RMSNorm, so it must be the last layer"
    t = bsz * s
    row = lambda a: a.reshape(1, -1)
    time_major = lambda a: a.reshape(a.shape[:-1] + (LANE_TILES, LANES))
    x2 = x.reshape(t, d)
    for l in range(depth):
        u, g_lru, q, k, v, z = _in_proj(x2, row(norm_mix_g[l]), w_in[l].astype(BF16))
        seq = lambda a: a.reshape(bsz, s, a.shape[-1])
        gates = lambda dr: (row(lru_lambda[l, dr]), _gate_weights(lru_wa[l, dr], lru_wx[l, dr]),
                            row(lru_ba[l, dr]), row(lru_bx[l, dr]))
        h_fwd, uc = _lru_fwd(seq(u), time_major(conv_w[l]), time_major(conv_b[l]), *gates(0))
        y_a = _lru_bwd(uc, h_fwd, seq(g_lru), *gates(1))
        y_b = _attention(seq(q), seq(k), seq(v), attn_sink[l])
        x2 = _ffn(x2, y_a.reshape(t, d), y_b.reshape(t, d), z, row(b_gate[l]),
                  w_out[l].astype(BF16), row(norm_ffn_g[l]), w_ffn_in[l].astype(BF16),
                  w_ffn_out[l].astype(BF16), row(norm_final_g))
    return x2.reshape(bsz, s, d)
```

```python
import functools
import math

import numpy as np
import jax
import jax.numpy as jnp
from jax import lax
from jax.experimental import pallas as pl
from jax.experimental.pallas import tpu as pltpu

F32 = jnp.float32
BF16 = jnp.bfloat16

D_MODEL = 1024
LRU_HEADS = 16
LRU_BLOCK = D_MODEL // LRU_HEADS
CONV_WIDTH = 4
CONV_LEFT = 2
RGLRU_C = 8.0
N_HEADS = 16
N_KV_HEADS = 4
HEAD_DIM = 64
GROUP = N_HEADS // N_KV_HEADS
WINDOW = 128
BLOCK = 128
D_FF = 2816
Q_W = N_HEADS * HEAD_DIM
KV_W = N_KV_HEADS * HEAD_DIM
IN_W = 2 * D_MODEL + Q_W + 2 * KV_W + 2 * D_MODEL
EPS = 1e-6
NEG_INF = -1e30

LANES = 128
SUBLANES = 8
VMEM_LIMIT_BYTES = 56 * 1024 * 1024

TM_PROJ = 512
TS_LRU = 512
TQ_ATTN = 1024
TM_FFN = 512
LRU_GROUP = 256
N_GROUPS = D_MODEL // LRU_GROUP
LANE_TILES = D_MODEL // LANES
assert LANE_TILES == SUBLANES
SLAB = SUBLANES * LANE_TILES
TILES_PER_STEP = 4
N_SLABS = 3 * TILES_PER_STEP

_ALIBI_SLOPES = [float(v) for v in np.exp2(
    -8.0 * (np.arange(N_HEADS, dtype=np.float32) + 1.0) / N_HEADS).astype(np.float32)]


def _sigmoid(x):
    return 0.5 * jnp.tanh(0.5 * x) + 0.5


def _rmsnorm(x, g):
    return x * lax.rsqrt(jnp.mean(x * x, axis=-1, keepdims=True) + EPS) * g


def _resident(shape):
    nd = len(shape)
    return pl.BlockSpec(shape, lambda *_: (0,) * nd, pipeline_mode=pl.Buffered(1))


def _in_proj_kernel(x_ref, g_ref, w_ref, u_ref, gl_ref, q_ref, k_ref, v_ref, z_ref):
    xn = _rmsnorm(x_ref[...], g_ref[...]).astype(BF16)

    def proj(lo, hi):
        return jnp.dot(xn, w_ref[:, lo:hi], preferred_element_type=F32)

    o = 0
    u_ref[...] = proj(o, o + D_MODEL); o += D_MODEL
    gl_ref[...] = proj(o, o + D_MODEL); o += D_MODEL
    q_ref[...] = (proj(o, o + Q_W) * (HEAD_DIM ** -0.5)).astype(BF16); o += Q_W
    k_ref[...] = proj(o, o + KV_W).astype(BF16); o += KV_W
    v_ref[...] = proj(o, o + KV_W).astype(BF16); o += KV_W
    z_ref[...] = proj(o, o + 2 * D_MODEL)


def _in_proj(x2, g, w):
    t = x2.shape[0]
    tm = TM_PROJ
    row = lambda w_: pl.BlockSpec((tm, w_), lambda i: (i, 0))
    return pl.pallas_call(
        _in_proj_kernel,
        grid=(t // tm,),
        in_specs=[row(D_MODEL), _resident((1, D_MODEL)), _resident((D_MODEL, IN_W))],
        out_specs=[row(D_MODEL), row(D_MODEL), row(Q_W), row(KV_W), row(KV_W), row(2 * D_MODEL)],
        out_shape=[
            jax.ShapeDtypeStruct((t, D_MODEL), F32),
            jax.ShapeDtypeStruct((t, D_MODEL), F32),
            jax.ShapeDtypeStruct((t, Q_W), BF16),
            jax.ShapeDtypeStruct((t, KV_W), BF16),
            jax.ShapeDtypeStruct((t, KV_W), BF16),
            jax.ShapeDtypeStruct((t, 2 * D_MODEL), F32),
        ],
        compiler_params=pltpu.CompilerParams(
            dimension_semantics=("arbitrary",), vmem_limit_bytes=VMEM_LIMIT_BYTES),
        name="in_proj",
    )(x2, g, w)


def _split_time(tile, slab_ref, s):
    for j in range(LANE_TILES):
        slab_ref[s, j * SUBLANES:(j + 1) * SUBLANES, :] = tile[:, j * LANES:(j + 1) * LANES]
    return [slab_ref[s, pl.ds(r, LANE_TILES, stride=SUBLANES), :] for r in range(SUBLANES)]


def _join_time(steps, slab_ref, s):
    for r in range(SUBLANES):
        slab_ref[s, pl.ds(r, LANE_TILES, stride=SUBLANES), :] = steps[r]
    return jnp.concatenate(
        [slab_ref[s, j * SUBLANES:(j + 1) * SUBLANES, :] for j in range(LANE_TILES)], axis=1)


def _half_decay(lam):
    return (-0.5 * RGLRU_C) * (jnp.maximum(-lam, 0.0) + jnp.log1p(jnp.exp(-jnp.abs(lam))))


def _gate_terms(pre_r, pre_i, uc, hdecay, hba, hbx):
    log_a = hdecay * jnp.tanh(pre_r + hba) + hdecay
    a = jnp.exp(log_a)
    y = jnp.maximum(1.0 - a * a, 0.0)
    beta = jnp.where(y > 0.0, y * lax.rsqrt(y), 0.0)
    i = 0.5 * jnp.tanh(pre_i + hbx) + 0.5
    return a, beta * (i * uc)


def _gelu_tanh(g):
    c = math.sqrt(2.0 / math.pi)
    half = 0.5 * g
    return half * jnp.tanh(g * ((c * 0.044715) * (g * g) + c)) + half


def _gate_maps(uc_ref, wg_ref, ra_ref, ix_ref):
    for gi in range(N_GROUPS):
        cols = slice(gi * LRU_GROUP, (gi + 1) * LRU_GROUP)
        pre = jnp.dot(uc_ref[:, cols].astype(BF16), wg_ref[gi], preferred_element_type=F32)
        ra_ref[:, cols] = pre[:, :LRU_GROUP]
        ix_ref[:, cols] = pre[:, LRU_GROUP:]


def _recurrence(reverse, uc_ref, ra_ref, ix_ref, lam_ref, ba_ref, bx_ref, carry_ref, slab_ref, emit):
    nt = uc_ref.shape[0] // SUBLANES
    nsteps = nt // TILES_PER_STEP
    hdecay = _half_decay(lam_ref[...])
    hba = 0.5 * ba_ref[...]
    hbx = 0.5 * bx_ref[...]

    def group(i, h):
        for s in range(TILES_PER_STEP):
            k = TILES_PER_STEP * i + s
            kk = (nt - 1 - k) if reverse else k
            rows = pl.ds(pl.multiple_of(kk * SUBLANES, SUBLANES), SUBLANES)
            a, b = _gate_terms(ra_ref[rows, :], ix_ref[rows, :], uc_ref[rows, :], hdecay, hba, hbx)
            a_t = _split_time(a, slab_ref, 3 * s)
            b_t = _split_time(b, slab_ref, 3 * s + 1)
            hs = [None] * SUBLANES
            for r in (range(SUBLANES - 1, -1, -1) if reverse else range(SUBLANES)):
                h = a_t[r] * h + b_t[r]
                hs[r] = h
            emit(rows, _join_time(hs, slab_ref, 3 * s + 2))
        return h

    carry_ref[...] = lax.fori_loop(0, nsteps, group, carry_ref[...])


def _lru_fwd_kernel(nblk, um_ref, up_ref, un_ref, cw_ref, cb_ref, lam_ref, wg_ref, ba_ref, bx_ref,
                    hf_ref, uc_ref, uext_ref, ra_ref, ix_ref, carry_ref, slab_ref):
    ts = um_ref.shape[0]
    nt = ts // SUBLANES
    j = pl.program_id(1)

    @pl.when(j == 0)
    def _():
        carry_ref[...] = jnp.zeros_like(carry_ref)

    uext_ref[0:SUBLANES, :] = jnp.where(j > 0, up_ref[...], 0.0)
    uext_ref[SUBLANES:SUBLANES + ts, :] = um_ref[...]
    uext_ref[SUBLANES + ts:, :] = jnp.where(j < nblk - 1, un_ref[...], 0.0)

    w = [cw_ref[k] for k in range(CONV_WIDTH)]
    cb = cb_ref[...]

    def ext_tile(e):
        return uext_ref[pl.ds(pl.multiple_of(e * SUBLANES, SUBLANES), SUBLANES), :]

    def conv_tiles(i, state):
        prev2, prev1, cur = state[0], state[1], list(state[2:])
        for s in range(2):
            t = 2 * i + s
            nxt = _split_time(ext_tile(t + 2), slab_ref, 2 * s)
            x = [prev2, prev1] + cur + [nxt[0]]
            out = [(x[r] * w[0] + x[r + 1] * w[1]) + (x[r + 2] * w[2] + x[r + 3] * w[3]) + cb
                   for r in range(SUBLANES)]
            uc_ref[pl.ds(pl.multiple_of(t * SUBLANES, SUBLANES), SUBLANES), :] = _join_time(
                out, slab_ref, 2 * s + 1)
            prev2, prev1, cur = cur[SUBLANES - 2], cur[SUBLANES - 1], nxt
        return (prev2, prev1, *cur)

    halo = _split_time(ext_tile(0), slab_ref, 4)
    first = _split_time(ext_tile(1), slab_ref, 5)
    lax.fori_loop(0, nt // 2, conv_tiles, (halo[SUBLANES - 2], halo[SUBLANES - 1], *first))

    _gate_maps(uc_ref, wg_ref, ra_ref, ix_ref)

    def emit(rows, h):
        hf_ref[rows, :] = h

    _recurrence(False, uc_ref, ra_ref, ix_ref, lam_ref, ba_ref, bx_ref, carry_ref, slab_ref, emit)


def _lru_bwd_kernel(uc_ref, hf_ref, g_ref, lam_ref, wg_ref, ba_ref, bx_ref,
                    ya_ref, ra_ref, ix_ref, carry_ref, slab_ref):
    @pl.when(pl.program_id(1) == 0)
    def _():
        carry_ref[...] = jnp.zeros_like(carry_ref)

    _gate_maps(uc_ref, wg_ref, ra_ref, ix_ref)

    def emit(rows, h):
        ya_ref[rows, :] = (hf_ref[rows, :] + h) * _gelu_tanh(g_ref[rows, :])

    _recurrence(True, uc_ref, ra_ref, ix_ref, lam_ref, ba_ref, bx_ref, carry_ref, slab_ref, emit)


_LRU_SCRATCH_TAIL = [
    pltpu.VMEM((TS_LRU, D_MODEL), F32),
    pltpu.VMEM((TS_LRU, D_MODEL), F32),
    pltpu.VMEM((LANE_TILES, LANES), F32),
    pltpu.VMEM((N_SLABS, SLAB, LANES), F32),
]


def _lru_fwd(u3, cw, cb, lam, wg, ba, bx):
    bsz, s, _ = u3.shape
    ts = TS_LRU
    nblk = s // ts
    hb = ts // SUBLANES
    nh = s // SUBLANES
    main = pl.BlockSpec((None, ts, D_MODEL), lambda b, j: (b, j, 0))
    prev = pl.BlockSpec((None, SUBLANES, D_MODEL), lambda b, j: (b, jnp.maximum(j * hb - 1, 0), 0))
    nxt = pl.BlockSpec((None, SUBLANES, D_MODEL), lambda b, j: (b, jnp.minimum((j + 1) * hb, nh - 1), 0))
    out = jax.ShapeDtypeStruct((bsz, s, D_MODEL), F32)
    return pl.pallas_call(
        functools.partial(_lru_fwd_kernel, nblk),
        grid=(bsz, nblk),
        in_specs=[main, prev, nxt, _resident(cw.shape), _resident(cb.shape), _resident((1, D_MODEL)),
                  _resident(wg.shape), _resident((1, D_MODEL)), _resident((1, D_MODEL))],
        out_specs=[main, main],
        out_shape=[out, out],
        scratch_shapes=[pltpu.VMEM((ts + 2 * SUBLANES, D_MODEL), F32)] + _LRU_SCRATCH_TAIL,
        compiler_params=pltpu.CompilerParams(
            dimension_semantics=("arbitrary", "arbitrary"), vmem_limit_bytes=VMEM_LIMIT_BYTES),
        name="lru_fwd",
    )(u3, u3, u3, cw, cb, lam, wg, ba, bx)


def _lru_bwd(uc3, hf3, g3, lam, wg, ba, bx):
    bsz, s, _ = uc3.shape
    ts = TS_LRU
    nblk = s // ts
    blk = pl.BlockSpec((None, ts, D_MODEL), lambda b, j: (b, nblk - 1 - j, 0))
    return pl.pallas_call(
        _lru_bwd_kernel,
        grid=(bsz, nblk),
        in_specs=[blk, blk, blk, _resident((1, D_MODEL)), _resident(wg.shape),
                  _resident((1, D_MODEL)), _resident((1, D_MODEL))],
        out_specs=blk,
        out_shape=jax.ShapeDtypeStruct((bsz, s, D_MODEL), F32),
        scratch_shapes=_LRU_SCRATCH_TAIL,
        compiler_params=pltpu.CompilerParams(
            dimension_semantics=("arbitrary", "arbitrary"), vmem_limit_bytes=VMEM_LIMIT_BYTES),
        name="lru_bwd",
    )(uc3, hf3, g3, lam, wg, ba, bx)


_N_BIAS = 3 * N_HEADS + 1
_MASKED_TILE = 3 * N_HEADS


def _attn_kernel(nb, sink_ref, q_ref, kp_ref, km_ref, kn_ref, vp_ref, vm_ref, vn_ref,
                 o_ref, kz_ref, vab_ref, bias_ref, s_ref, m_ref, p_ref):
    tq = q_ref.shape[0]
    tk = tq + 2 * BLOCK
    nsub = tq // BLOCK
    j = pl.program_id(1)

    @pl.when((pl.program_id(0) == 0) & (j == 0))
    def _():
        row = lax.broadcasted_iota(jnp.int32, (BLOCK, BLOCK), 0)
        col = lax.broadcasted_iota(jnp.int32, (BLOCK, BLOCK), 1)
        bias_ref[_MASKED_TILE] = jnp.ones((BLOCK, BLOCK), F32)
        for c in range(3):
            dist = jnp.abs(row + (1 - c) * BLOCK - col)
            absd = dist.astype(F32)
            for h in range(N_HEADS):
                bias_ref[3 * h + c] = jnp.where(dist <= WINDOW, -_ALIBI_SLOPES[h] * absd, 1.0)

    low = lax.broadcasted_iota(jnp.int32, (tk, LANES), 1) < HEAD_DIM
    low_q = lax.broadcasted_iota(jnp.int32, (BLOCK, LANES), 1) < HEAD_DIM
    ones_q = [jnp.where(low_q, 1.0, 0.0).astype(BF16), jnp.where(low_q, 0.0, 1.0).astype(BF16)]
    for t in range(KV_W // LANES):
        cols = slice(t * LANES, (t + 1) * LANES)
        full = jnp.concatenate([r[:, cols] for r in (kp_ref, km_ref, kn_ref)], axis=0).astype(F32)
        swapped = pltpu.roll(full, HEAD_DIM, axis=1)
        kz_ref[4 * t + 0] = jnp.where(low, full, 0.0).astype(BF16)
        kz_ref[4 * t + 1] = jnp.where(low, 0.0, swapped).astype(BF16)
        kz_ref[4 * t + 2] = jnp.where(low, swapped, 0.0).astype(BF16)
        kz_ref[4 * t + 3] = jnp.where(low, 0.0, full).astype(BF16)
        nkb = tk // BLOCK
        for b in range(nkb):
            if b == 0:
                src = vp_ref[:, cols]
            elif b == nkb - 1:
                src = vn_ref[:, cols]
            else:
                src = vm_ref[(b - 1) * BLOCK:b * BLOCK, cols]
            full = src.astype(F32)
            swapped = pltpu.roll(full, HEAD_DIM, axis=1)
            placed = [jnp.where(low_q, full, 0.0), jnp.where(low_q, 0.0, swapped),
                      jnp.where(low_q, swapped, 0.0), jnp.where(low_q, 0.0, full)]
            for i in range(4):
                g, half = 2 * t + i // 2, i % 2
                dst = slice((2 * b + half) * BLOCK, (2 * b + half + 1) * BLOCK)
                vab_ref[g, dst, :LANES] = placed[i].astype(BF16)
                vab_ref[g, dst, LANES:] = ones_q[half]

    def scores(qi, slot):
        q0 = pl.multiple_of(qi * BLOCK, BLOCK)
        blk = j * nsub + qi
        win = pl.ds(q0, 3 * BLOCK)
        sc2 = [None, None]
        for h in range(N_HEADS):
            g, pp, half = h // GROUP, (h % GROUP) // 2, h % 2
            if pp == 0:
                q2 = jnp.concatenate(
                    [q_ref[pl.ds(q0, BLOCK), (2 * g + i) * LANES:(2 * g + i + 1) * LANES] for i in range(2)],
                    axis=0)
                sc2[half] = lax.dot_general(q2, kz_ref[2 * g + half, win, :], (((1,), (1,)), ((), ())),
                                            preferred_element_type=F32)
            sc = sc2[half][pp * BLOCK:(pp + 1) * BLOCK]
            tiles = []
            for c in range(3):
                t = sc[:, c * BLOCK:(c + 1) * BLOCK]
                if c == 1:
                    t = t + bias_ref[3 * h + 1]
                else:
                    edge = (blk == 0) if c == 0 else (blk == nb - 1)
                    b = bias_ref[jnp.where(edge, _MASKED_TILE, 3 * h + c)]
                    t = jnp.where(b > 0.0, NEG_INF, t + b)
                s_ref[slot, h, :, c * BLOCK:(c + 1) * BLOCK] = t
                tiles.append(t)
            mx = jnp.max(jnp.maximum(jnp.maximum(tiles[0], tiles[1]), tiles[2]), axis=-1, keepdims=True)
            m_ref[slot, h] = jnp.broadcast_to(jnp.maximum(mx, sink_ref[h]), (BLOCK, LANES))

    def finish(qi, slot):
        q0 = pl.multiple_of(qi * BLOCK, BLOCK)
        win2 = pl.ds(pl.multiple_of(qi * (2 * BLOCK), 2 * BLOCK), 6 * BLOCK)
        for h in range(N_HEADS):
            m = m_ref[slot, h]
            for c in range(3):
                cs = slice(c * BLOCK, (c + 1) * BLOCK)
                dst = slice((2 * c + h % 2) * BLOCK, (2 * c + h % 2 + 1) * BLOCK)
                prow = slice(((h // 2) % 2) * BLOCK, ((h // 2) % 2 + 1) * BLOCK)
                p_ref[slot, h // GROUP, prow, dst] = jnp.exp(s_ref[slot, h, :, cs] - m).astype(BF16)
        for g in range(N_KV_HEADS):
            o2 = jnp.dot(p_ref[slot, g], vab_ref[g, win2, :], preferred_element_type=F32)
            for pp in range(2):
                pr = 2 * g + pp
                h0, h1 = 2 * pr, 2 * pr + 1
                o = o2[pp * BLOCK:(pp + 1) * BLOCK]
                e = jnp.where(low_q, jnp.exp(sink_ref[h0] - m_ref[slot, h0]),
                              jnp.exp(sink_ref[h1] - m_ref[slot, h1]))
                o_ref[pl.ds(q0, BLOCK), pr * LANES:(pr + 1) * LANES] = (
                    o[:, :LANES] * (1.0 / (o[:, LANES:] + e)))

    scores(0, 0)

    def two_sub_blocks(i, _):
        qa = 2 * i
        scores(qa + 1, 1)
        finish(qa, 0)
        scores(jnp.minimum(qa + 2, nsub - 1), 0)
        finish(qa + 1, 1)
        return 0

    lax.fori_loop(0, nsub // 2, two_sub_blocks, 0)


def _attention(q3, k3, v3, sink):
    bsz, s, _ = q3.shape
    tq = TQ_ATTN
    nq = s // tq
    hb = tq // BLOCK
    nb = s // BLOCK
    tk = tq + 2 * BLOCK
    kv_main = pl.BlockSpec((None, tq, KV_W), lambda b, j: (b, j, 0))
    kv_prev = pl.BlockSpec((None, BLOCK, KV_W), lambda b, j: (b, jnp.maximum(j * hb - 1, 0), 0))
    kv_next = pl.BlockSpec((None, BLOCK, KV_W), lambda b, j: (b, jnp.minimum((j + 1) * hb, nb - 1), 0))
    qo = pl.BlockSpec((None, tq, Q_W), lambda b, j: (b, j, 0))
    return pl.pallas_call(
        functools.partial(_attn_kernel, nb),
        grid=(bsz, nq),
        in_specs=[pl.BlockSpec(memory_space=pltpu.SMEM), qo,
                  kv_prev, kv_main, kv_next, kv_prev, kv_main, kv_next],
        out_specs=qo,
        out_shape=jax.ShapeDtypeStruct((bsz, s, Q_W), F32),
        scratch_shapes=[
            pltpu.VMEM((2 * N_KV_HEADS, tk, LANES), BF16),
            pltpu.VMEM((N_KV_HEADS, 2 * tk, 2 * LANES), BF16),
            pltpu.VMEM((_N_BIAS, BLOCK, BLOCK), F32),
            pltpu.VMEM((2, N_HEADS, BLOCK, 3 * BLOCK), F32),
            pltpu.VMEM((2, N_HEADS, BLOCK, LANES), F32),
            pltpu.VMEM((2, N_KV_HEADS, 2 * BLOCK, 6 * BLOCK), BF16),
        ],
        compiler_params=pltpu.CompilerParams(
            dimension_semantics=("arbitrary", "arbitrary"), vmem_limit_bytes=VMEM_LIMIT_BYTES),
        name="attention",
    )(sink, q3, k3, k3, k3, v3, v3, v3)


def _ffn_kernel(x_ref, ya_ref, yb_ref, z_ref, bg_ref, wo_ref, g2_ref, wi_ref, wd_ref, g3_ref, o_ref):
    gates = _sigmoid(z_ref[...] + bg_ref[...])
    merged = gates[:, :D_MODEL] * ya_ref[...] + gates[:, D_MODEL:] * yb_ref[...]
    x1 = x_ref[...] + jnp.dot(merged.astype(BF16), wo_ref[...], preferred_element_type=F32)
    xn = _rmsnorm(x1, g2_ref[...]).astype(BF16)
    gate = jnp.dot(xn, wi_ref[:, :D_FF], preferred_element_type=F32)
    up = jnp.dot(xn, wi_ref[:, D_FF:], preferred_element_type=F32)
    act = (gate * _sigmoid(gate) * up).astype(BF16)
    x2 = x1 + jnp.dot(act, wd_ref[...], preferred_element_type=F32)
    o_ref[...] = _rmsnorm(x2, g3_ref[...])


def _ffn(x2, ya, yb, z, bg, wo, g2, wi, wd, g3):
    t = x2.shape[0]
    tm = TM_FFN
    row = lambda w_: pl.BlockSpec((tm, w_), lambda i: (i, 0))
    return pl.pallas_call(
        _ffn_kernel,
        grid=(t // tm,),
        in_specs=[row(D_MODEL), row(D_MODEL), row(D_MODEL), row(2 * D_MODEL),
                  _resident((1, 2 * D_MODEL)), _resident((D_MODEL, D_MODEL)), _resident((1, D_MODEL)),
                  _resident((D_MODEL, 2 * D_FF)), _resident((D_FF, D_MODEL)), _resident((1, D_MODEL))],
        out_specs=row(D_MODEL),
        out_shape=jax.ShapeDtypeStruct((t, D_MODEL), F32),
        compiler_params=pltpu.CompilerParams(
            dimension_semantics=("arbitrary",), vmem_limit_bytes=VMEM_LIMIT_BYTES),
        name="merge_ffn",
    )(x2, ya, yb, z, bg, wo, g2, wi, wd, g3)


def _gate_weights(wa, wx):
    def blockdiag(w):
        w = 0.5 * w.reshape(N_GROUPS, LRU_GROUP // LRU_BLOCK, LRU_BLOCK, LRU_BLOCK)
        eye = jnp.eye(LRU_GROUP // LRU_BLOCK, dtype=w.dtype)
        return jnp.einsum("ghij,hk->ghikj", w, eye).reshape(N_GROUPS, LRU_GROUP, LRU_GROUP)
    return jnp.concatenate([blockdiag(wa), blockdiag(wx)], axis=-1).astype(BF16)


def kernel(x, norm_mix_g, w_in, b_gate, conv_w, conv_b, lru_lambda, lru_wa, lru_ba, lru_wx, lru_bx,
           attn_sink, w_out, norm_ffn_g, w_ffn_in, w_ffn_out, norm_final_g):
    bsz, s, d = x.shape
    depth = w_in.shape[0]
    assert depth == 1, "the merge/ffn kernel applies the final RMSNorm, so it must be the last layer"
    t = bsz * s
    row = lambda a: a.reshape(1, -1)
    time_major = lambda a: a.reshape(a.shape[:-1] + (LANE_TILES, LANES))
    x2 = x.reshape(t, d)
    for l in range(depth):
        u, g_lru, q, k, v, z = _in_proj(x2, row(norm_mix_g[l]), w_in[l].astype(BF16))
        seq = lambda a: a.reshape(bsz, s, a.shape[-1])
        gates = lambda dr: (row(lru_lambda[l, dr]), _gate_weights(lru_wa[l, dr], lru_wx[l, dr]),
                            row(lru_ba[l, dr]), row(lru_bx[l, dr]))
        h_fwd, uc = _lru_fwd(seq(u), time_major(conv_w[l]), time_major(conv_b[l]), *gates(0))
        y_a = _lru_bwd(uc, h_fwd, seq(g_lru), *gates(1))
        y_b = _attention(seq(q), seq(k), seq(v), attn_sink[l])
        x2 = _ffn(x2, y_a.reshape(t, d), y_b.reshape(t, d), z, row(b_gate[l]),
                  w_out[l].astype(BF16), row(norm_ffn_g[l]), w_ffn_in[l].astype(BF16),
                  w_ffn_out[l].astype(BF16), row(norm_final_g))
    return x2.reshape(bsz, s, d)
```

```python
import functools
import math

import numpy as np
import jax
import jax.numpy as jnp
from jax import lax
from jax.experimental import pallas as pl
from jax.experimental.pallas import tpu as pltpu

F32 = jnp.float32
BF16 = jnp.bfloat16

D_MODEL = 1024
LRU_HEADS = 16
LRU_BLOCK = D_MODEL // LRU_HEADS
CONV_WIDTH = 4
CONV_LEFT = 2
RGLRU_C = 8.0
N_HEADS = 16
N_KV_HEADS = 4
HEAD_DIM = 64
GROUP = N_HEADS // N_KV_HEADS
WINDOW = 128
BLOCK = 128
D_FF = 2816
Q_W = N_HEADS * HEAD_DIM
KV_W = N_KV_HEADS * HEAD_DIM
IN_W = 2 * D_MODEL + Q_W + 2 * KV_W + 2 * D_MODEL
EPS = 1e-6
NEG_INF = -1e30

LANES = 128
SUBLANES = 8
VMEM_LIMIT_BYTES = 56 * 1024 * 1024

TM_PROJ = 512
TS_LRU = 512
TQ_ATTN = 1024
TM_FFN = 512
LRU_GROUP = 256
N_GROUPS = D_MODEL // LRU_GROUP
LANE_TILES = D_MODEL // LANES
assert LANE_TILES == SUBLANES
SLAB = SUBLANES * LANE_TILES
TILES_PER_STEP = 4
N_SLABS = 3 * TILES_PER_STEP

_ALIBI_SLOPES = [float(v) for v in np.exp2(
    -8.0 * (np.arange(N_HEADS, dtype=np.float32) + 1.0) / N_HEADS).astype(np.float32)]


def _sigmoid(x):
    return 0.5 * jnp.tanh(0.5 * x) + 0.5


def _rmsnorm(x, g):
    return x * lax.rsqrt(jnp.mean(x * x, axis=-1, keepdims=True) + EPS) * g


def _resident(shape):
    nd = len(shape)
    return pl.BlockSpec(shape, lambda *_: (0,) * nd, pipeline_mode=pl.Buffered(1))


def _in_proj_kernel(x_ref, g_ref, w_ref, bg_ref, u_ref, gg_ref, q_ref, k_ref, v_ref, gate_ref):
    xn = _rmsnorm(x_ref[...], g_ref[...]).astype(BF16)

    def proj(lo, hi):
        return jnp.dot(xn, w_ref[:, lo:hi], preferred_element_type=F32)

    o_g = D_MODEL
    o_q = o_g + D_MODEL
    o_k = o_q + Q_W
    o_v = o_k + KV_W
    o_z = o_v + KV_W
    gate_ref[...] = _sigmoid(proj(o_z, o_z + 2 * D_MODEL) + bg_ref[...])
    gg_ref[...] = _gelu_tanh(proj(o_g, o_g + D_MODEL))
    q_ref[...] = (proj(o_q, o_q + Q_W) * (HEAD_DIM ** -0.5)).astype(BF16)
    k_ref[...] = proj(o_k, o_k + KV_W).astype(BF16)
    v_ref[...] = proj(o_v, o_v + KV_W).astype(BF16)
    u_ref[...] = proj(0, D_MODEL)


def _in_proj(x2, g, w, bg):
    t = x2.shape[0]
    tm = TM_PROJ
    row = lambda w_: pl.BlockSpec((tm, w_), lambda i: (i, 0))
    return pl.pallas_call(
        _in_proj_kernel,
        grid=(t // tm,),
        in_specs=[row(D_MODEL), _resident((1, D_MODEL)), _resident((D_MODEL, IN_W)),
                  _resident((1, 2 * D_MODEL))],
        out_specs=[row(D_MODEL), row(D_MODEL), row(Q_W), row(KV_W), row(KV_W), row(2 * D_MODEL)],
        out_shape=[
            jax.ShapeDtypeStruct((t, D_MODEL), F32),
            jax.ShapeDtypeStruct((t, D_MODEL), F32),
            jax.ShapeDtypeStruct((t, Q_W), BF16),
            jax.ShapeDtypeStruct((t, KV_W), BF16),
            jax.ShapeDtypeStruct((t, KV_W), BF16),
            jax.ShapeDtypeStruct((t, 2 * D_MODEL), F32),
        ],
        compiler_params=pltpu.CompilerParams(
            dimension_semantics=("arbitrary",), vmem_limit_bytes=VMEM_LIMIT_BYTES),
        name="in_proj",
    )(x2, g, w, bg)


def _split_time(tile, slab_ref, s):
    for j in range(LANE_TILES):
        slab_ref[s, j * SUBLANES:(j + 1) * SUBLANES, :] = tile[:, j * LANES:(j + 1) * LANES]
    return [slab_ref[s, pl.ds(r, LANE_TILES, stride=SUBLANES), :] for r in range(SUBLANES)]


def _join_time(steps, slab_ref, s):
    for r in range(SUBLANES):
        slab_ref[s, pl.ds(r, LANE_TILES, stride=SUBLANES), :] = steps[r]
    return jnp.concatenate(
        [slab_ref[s, j * SUBLANES:(j + 1) * SUBLANES, :] for j in range(LANE_TILES)], axis=1)


def _half_decay(lam):
    return (-0.5 * RGLRU_C) * (jnp.maximum(-lam, 0.0) + jnp.log1p(jnp.exp(-jnp.abs(lam))))


def _gate_terms(pre_r, pre_i, uc, hdecay, hba, hbx):
    log_a = hdecay * jnp.tanh(pre_r + hba) + hdecay
    a = jnp.exp(log_a)
    y = jnp.maximum(1.0 - a * a, 0.0)
    beta = jnp.where(y > 0.0, y * lax.rsqrt(y), 0.0)
    i = 0.5 * jnp.tanh(pre_i + hbx) + 0.5
    return a, beta * (i * uc)


def _gelu_tanh(g):
    c = math.sqrt(2.0 / math.pi)
    half = 0.5 * g
    return half * jnp.tanh(g * ((c * 0.044715) * (g * g) + c)) + half


def _gate_maps(uc_ref, wg_ref, ra_ref, ix_ref):
    for gi in range(N_GROUPS):
        cols = slice(gi * LRU_GROUP, (gi + 1) * LRU_GROUP)
        pre = jnp.dot(uc_ref[:, cols].astype(BF16), wg_ref[gi], preferred_element_type=F32)
        ra_ref[:, cols] = pre[:, :LRU_GROUP]
        ix_ref[:, cols] = pre[:, LRU_GROUP:]


def _recurrence(reverse, uc_ref, ra_ref, ix_ref, lam_ref, ba_ref, bx_ref, carry_ref, slab_ref, emit):
    nt = uc_ref.shape[0] // SUBLANES
    nsteps = nt // TILES_PER_STEP
    hdecay = _half_decay(lam_ref[...])
    hba = 0.5 * ba_ref[...]
    hbx = 0.5 * bx_ref[...]

    def group(i, h):
        for s in range(TILES_PER_STEP):
            k = TILES_PER_STEP * i + s
            kk = (nt - 1 - k) if reverse else k
            rows = pl.ds(pl.multiple_of(kk * SUBLANES, SUBLANES), SUBLANES)
            a, b = _gate_terms(ra_ref[rows, :], ix_ref[rows, :], uc_ref[rows, :], hdecay, hba, hbx)
            a_t = _split_time(a, slab_ref, 3 * s)
            b_t = _split_time(b, slab_ref, 3 * s + 1)
            hs = [None] * SUBLANES
            for r in (range(SUBLANES - 1, -1, -1) if reverse else range(SUBLANES)):
                h = a_t[r] * h + b_t[r]
                hs[r] = h
            emit(rows, _join_time(hs, slab_ref, 3 * s + 2))
        return h

    carry_ref[...] = lax.fori_loop(0, nsteps, group, carry_ref[...])


def _lru_fwd_kernel(nblk, um_ref, up_ref, un_ref, cw_ref, cb_ref, lam_ref, wg_ref, ba_ref, bx_ref,
                    hf_ref, uc_ref, uext_ref, ra_ref, ix_ref, carry_ref, slab_ref):
    ts = um_ref.shape[0]
    nt = ts // SUBLANES
    j = pl.program_id(1)

    @pl.when(j == 0)
    def _():
        carry_ref[...] = jnp.zeros_like(carry_ref)

    uext_ref[0:SUBLANES, :] = jnp.where(j > 0, up_ref[...], 0.0)
    uext_ref[SUBLANES:SUBLANES + ts, :] = um_ref[...]
    uext_ref[SUBLANES + ts:, :] = jnp.where(j < nblk - 1, un_ref[...], 0.0)

    w = [cw_ref[k] for k in range(CONV_WIDTH)]
    cb = cb_ref[...]

    def ext_tile(e):
        return uext_ref[pl.ds(pl.multiple_of(e * SUBLANES, SUBLANES), SUBLANES), :]

    def conv_tiles(i, state):
        prev2, prev1, cur = state[0], state[1], list(state[2:])
        for s in range(2):
            t = 2 * i + s
            nxt = _split_time(ext_tile(t + 2), slab_ref, 2 * s)
            x = [prev2, prev1] + cur + [nxt[0]]
            out = [(x[r] * w[0] + x[r + 1] * w[1]) + (x[r + 2] * w[2] + x[r + 3] * w[3]) + cb
                   for r in range(SUBLANES)]
            uc_ref[pl.ds(pl.multiple_of(t * SUBLANES, SUBLANES), SUBLANES), :] = _join_time(
                out, slab_ref, 2 * s + 1)
            prev2, prev1, cur = cur[SUBLANES - 2], cur[SUBLANES - 1], nxt
        return (prev2, prev1, *cur)

    halo = _split_time(ext_tile(0), slab_ref, 4)
    first = _split_time(ext_tile(1), slab_ref, 5)
    lax.fori_loop(0, nt // 2, conv_tiles, (halo[SUBLANES - 2], halo[SUBLANES - 1], *first))

    _gate_maps(uc_ref, wg_ref, ra_ref, ix_ref)

    def emit(rows, h):
        hf_ref[rows, :] = h

    _recurrence(False, uc_ref, ra_ref, ix_ref, lam_ref, ba_ref, bx_ref, carry_ref, slab_ref, emit)


def _lru_bwd_kernel(uc_ref, hf_ref, g_ref, lam_ref, wg_ref, ba_ref, bx_ref,
                    ya_ref, ra_ref, ix_ref, carry_ref, slab_ref):
    @pl.when(pl.program_id(1) == 0)
    def _():
        carry_ref[...] = jnp.zeros_like(carry_ref)

    _gate_maps(uc_ref, wg_ref, ra_ref, ix_ref)

    def emit(rows, h):
        ya_ref[rows, :] = (hf_ref[rows, :] + h) * g_ref[rows, :]

    _recurrence(True, uc_ref, ra_ref, ix_ref, lam_ref, ba_ref, bx_ref, carry_ref, slab_ref, emit)


_LRU_SCRATCH_TAIL = [
    pltpu.VMEM((TS_LRU, D_MODEL), F32),
    pltpu.VMEM((TS_LRU, D_MODEL), F32),
    pltpu.VMEM((LANE_TILES, LANES), F32),
    pltpu.VMEM((N_SLABS, SLAB, LANES), F32),
]


def _lru_fwd(u3, cw, cb, lam, wg, ba, bx):
    bsz, s, _ = u3.shape
    ts = TS_LRU
    nblk = s // ts
    hb = ts // SUBLANES
    nh = s // SUBLANES
    main = pl.BlockSpec((None, ts, D_MODEL), lambda b, j: (b, j, 0))
    prev = pl.BlockSpec((None, SUBLANES, D_MODEL), lambda b, j: (b, jnp.maximum(j * hb - 1, 0), 0))
    nxt = pl.BlockSpec((None, SUBLANES, D_MODEL), lambda b, j: (b, jnp.minimum((j + 1) * hb, nh - 1), 0))
    out = jax.ShapeDtypeStruct((bsz, s, D_MODEL), F32)
    return pl.pallas_call(
        functools.partial(_lru_fwd_kernel, nblk),
        grid=(bsz, nblk),
        in_specs=[main, prev, nxt, _resident(cw.shape), _resident(cb.shape), _resident((1, D_MODEL)),
                  _resident(wg.shape), _resident((1, D_MODEL)), _resident((1, D_MODEL))],
        out_specs=[main, main],
        out_shape=[out, out],
        scratch_shapes=[pltpu.VMEM((ts + 2 * SUBLANES, D_MODEL), F32)] + _LRU_SCRATCH_TAIL,
        compiler_params=pltpu.CompilerParams(
            dimension_semantics=("arbitrary", "arbitrary"), vmem_limit_bytes=VMEM_LIMIT_BYTES),
        name="lru_fwd",
    )(u3, u3, u3, cw, cb, lam, wg, ba, bx)


def _lru_bwd(uc3, hf3, g3, lam, wg, ba, bx):
    bsz, s, _ = uc3.shape
    ts = TS_LRU
    nblk = s // ts
    blk = pl.BlockSpec((None, ts, D_MODEL), lambda b, j: (b, nblk - 1 - j, 0))
    return pl.pallas_call(
        _lru_bwd_kernel,
        grid=(bsz, nblk),
        in_specs=[blk, blk, blk, _resident((1, D_MODEL)), _resident(wg.shape),
                  _resident((1, D_MODEL)), _resident((1, D_MODEL))],
        out_specs=blk,
        out_shape=jax.ShapeDtypeStruct((bsz, s, D_MODEL), F32),
        scratch_shapes=_LRU_SCRATCH_TAIL,
        compiler_params=pltpu.CompilerParams(
            dimension_semantics=("arbitrary", "arbitrary"), vmem_limit_bytes=VMEM_LIMIT_BYTES),
        name="lru_bwd",
    )(uc3, hf3, g3, lam, wg, ba, bx)


_N_BIAS = 3 * N_HEADS + 1
_MASKED_TILE = 3 * N_HEADS


def _attn_kernel(nb, sink_ref, q_ref, kp_ref, km_ref, kn_ref, vp_ref, vm_ref, vn_ref,
                 o_ref, kz_ref, vab_ref, bias_ref, s_ref, m_ref, p_ref):
    tq = q_ref.shape[0]
    tk = tq + 2 * BLOCK
    nsub = tq // BLOCK
    j = pl.program_id(1)

    @pl.when((pl.program_id(0) == 0) & (j == 0))
    def _():
        row = lax.broadcasted_iota(jnp.int32, (BLOCK, BLOCK), 0)
        col = lax.broadcasted_iota(jnp.int32, (BLOCK, BLOCK), 1)
        bias_ref[_MASKED_TILE] = jnp.ones((BLOCK, BLOCK), F32)
        for c in range(3):
            dist = jnp.abs(row + (1 - c) * BLOCK - col)
            absd = dist.astype(F32)
            for h in range(N_HEADS):
                bias_ref[3 * h + c] = jnp.where(dist <= WINDOW, -_ALIBI_SLOPES[h] * absd, 1.0)

    low = lax.broadcasted_iota(jnp.int32, (tk, LANES), 1) < HEAD_DIM
    low_q = lax.broadcasted_iota(jnp.int32, (BLOCK, LANES), 1) < HEAD_DIM
    ones_q = [jnp.where(low_q, 1.0, 0.0).astype(BF16), jnp.where(low_q, 0.0, 1.0).astype(BF16)]
    for t in range(KV_W // LANES):
        cols = slice(t * LANES, (t + 1) * LANES)
        full = jnp.concatenate([r[:, cols] for r in (kp_ref, km_ref, kn_ref)], axis=0).astype(F32)
        swapped = pltpu.roll(full, HEAD_DIM, axis=1)
        kz_ref[4 * t + 0] = jnp.where(low, full, 0.0).astype(BF16)
        kz_ref[4 * t + 1] = jnp.where(low, 0.0, swapped).astype(BF16)
        kz_ref[4 * t + 2] = jnp.where(low, swapped, 0.0).astype(BF16)
        kz_ref[4 * t + 3] = jnp.where(low, 0.0, full).astype(BF16)
        nkb = tk // BLOCK
        for b in range(nkb):
            if b == 0:
                src = vp_ref[:, cols]
            elif b == nkb - 1:
                src = vn_ref[:, cols]
            else:
                src = vm_ref[(b - 1) * BLOCK:b * BLOCK, cols]
            full = src.astype(F32)
            swapped = pltpu.roll(full, HEAD_DIM, axis=1)
            placed = [jnp.where(low_q, full, 0.0), jnp.where(low_q, 0.0, swapped),
                      jnp.where(low_q, swapped, 0.0), jnp.where(low_q, 0.0, full)]
            for i in range(4):
                g, half = 2 * t + i // 2, i % 2
                dst = slice((2 * b + half) * BLOCK, (2 * b + half + 1) * BLOCK)
                vab_ref[g, dst, :LANES] = placed[i].astype(BF16)
                vab_ref[g, dst, LANES:] = ones_q[half]

    def scores(qi, slot):
        q0 = pl.multiple_of(qi * BLOCK, BLOCK)
        blk = j * nsub + qi
        win = pl.ds(q0, 3 * BLOCK)
        sc2 = [None, None]
        for h in range(N_HEADS):
            g, pp, half = h // GROUP, (h % GROUP) // 2, h % 2
            if pp == 0:
                q2 = jnp.concatenate(
                    [q_ref[pl.ds(q0, BLOCK), (2 * g + i) * LANES:(2 * g + i + 1) * LANES] for i in range(2)],
                    axis=0)
                sc2[half] = lax.dot_general(q2, kz_ref[2 * g + half, win, :], (((1,), (1,)), ((), ())),
                                            preferred_element_type=F32)
            sc = sc2[half][pp * BLOCK:(pp + 1) * BLOCK]
            tiles = []
            for c in range(3):
                t = sc[:, c * BLOCK:(c + 1) * BLOCK]
                if c == 1:
                    t = t + bias_ref[3 * h + 1]
                else:
                    edge = (blk == 0) if c == 0 else (blk == nb - 1)
                    b = bias_ref[jnp.where(edge, _MASKED_TILE, 3 * h + c)]
                    t = jnp.where(b > 0.0, NEG_INF, t + b)
                s_ref[slot, h, :, c * BLOCK:(c + 1) * BLOCK] = t
                tiles.append(t)
            mx = jnp.max(jnp.maximum(jnp.maximum(tiles[0], tiles[1]), tiles[2]), axis=-1, keepdims=True)
            m_ref[slot, h] = jnp.broadcast_to(jnp.maximum(mx, sink_ref[h]), (BLOCK, LANES))

    def finish(qi, slot):
        q0 = pl.multiple_of(qi * BLOCK, BLOCK)
        win2 = pl.ds(pl.multiple_of(qi * (2 * BLOCK), 2 * BLOCK), 6 * BLOCK)
        for h in range(N_HEADS):
            m = m_ref[slot, h]
            for c in range(3):
                cs = slice(c * BLOCK, (c + 1) * BLOCK)
                dst = slice((2 * c + h % 2) * BLOCK, (2 * c + h % 2 + 1) * BLOCK)
                prow = slice(((h // 2) % 2) * BLOCK, ((h // 2) % 2 + 1) * BLOCK)
                p_ref[slot, h // GROUP, prow, dst] = jnp.exp(s_ref[slot, h, :, cs] - m).astype(BF16)
        for g in range(N_KV_HEADS):
            o2 = jnp.dot(p_ref[slot, g], vab_ref[g, win2, :], preferred_element_type=F32)
            for pp in range(2):
                pr = 2 * g + pp
                h0, h1 = 2 * pr, 2 * pr + 1
                o = o2[pp * BLOCK:(pp + 1) * BLOCK]
                e = jnp.where(low_q, jnp.exp(sink_ref[h0] - m_ref[slot, h0]),
                              jnp.exp(sink_ref[h1] - m_ref[slot, h1]))
                o_ref[pl.ds(q0, BLOCK), pr * LANES:(pr + 1) * LANES] = (
                    o[:, :LANES] * (1.0 / (o[:, LANES:] + e)))

    scores(0, 0)

    def two_sub_blocks(i, _):
        qa = 2 * i
        scores(qa + 1, 1)
        finish(qa, 0)
        scores(jnp.minimum(qa + 2, nsub - 1), 0)
        finish(qa + 1, 1)
        return 0

    lax.fori_loop(0, nsub // 2, two_sub_blocks, 0)


def _attention(q3, k3, v3, sink):
    bsz, s, _ = q3.shape
    tq = TQ_ATTN
    nq = s // tq
    hb = tq // BLOCK
    nb = s // BLOCK
    tk = tq + 2 * BLOCK
    kv_main = pl.BlockSpec((None, tq, KV_W), lambda b, j: (b, j, 0))
    kv_prev = pl.BlockSpec((None, BLOCK, KV_W), lambda b, j: (b, jnp.maximum(j * hb - 1, 0), 0))
    kv_next = pl.BlockSpec((None, BLOCK, KV_W), lambda b, j: (b, jnp.minimum((j + 1) * hb, nb - 1), 0))
    qo = pl.BlockSpec((None, tq, Q_W), lambda b, j: (b, j, 0))
    return pl.pallas_call(
        functools.partial(_attn_kernel, nb),
        grid=(bsz, nq),
        in_specs=[pl.BlockSpec(memory_space=pltpu.SMEM), qo,
                  kv_prev, kv_main, kv_next, kv_prev, kv_main, kv_next],
        out_specs=qo,
        out_shape=jax.ShapeDtypeStruct((bsz, s, Q_W), F32),
        scratch_shapes=[
            pltpu.VMEM((2 * N_KV_HEADS, tk, LANES), BF16),
            pltpu.VMEM((N_KV_HEADS, 2 * tk, 2 * LANES), BF16),
            pltpu.VMEM((_N_BIAS, BLOCK, BLOCK), F32),
            pltpu.VMEM((2, N_HEADS, BLOCK, 3 * BLOCK), F32),
            pltpu.VMEM((2, N_HEADS, BLOCK, LANES), F32),
            pltpu.VMEM((2, N_KV_HEADS, 2 * BLOCK, 6 * BLOCK), BF16),
        ],
        compiler_params=pltpu.CompilerParams(
            dimension_semantics=("arbitrary", "arbitrary"), vmem_limit_bytes=VMEM_LIMIT_BYTES),
        name="attention",
    )(sink, q3, k3, k3, k3, v3, v3, v3)


def _ffn_kernel(x_ref, ya_ref, yb_ref, gate_ref, wo_ref, g2_ref, wi_ref, wd_ref, g3_ref, o_ref):
    merged = gate_ref[:, :D_MODEL] * ya_ref[...] + gate_ref[:, D_MODEL:] * yb_ref[...]
    x1 = x_ref[...] + jnp.dot(merged.astype(BF16), wo_ref[...], preferred_element_type=F32)
    xn = _rmsnorm(x1, g2_ref[...]).astype(BF16)
    gate = jnp.dot(xn, wi_ref[:, :D_FF], preferred_element_type=F32)
    up = jnp.dot(xn, wi_ref[:, D_FF:], preferred_element_type=F32)
    act = (gate * _sigmoid(gate) * up).astype(BF16)
    x2 = x1 + jnp.dot(act, wd_ref[...], preferred_element_type=F32)
    o_ref[...] = _rmsnorm(x2, g3_ref[...])


def _ffn(x2, ya, yb, gates, wo, g2, wi, wd, g3):
    t = x2.shape[0]
    tm = TM_FFN
    row = lambda w_: pl.BlockSpec((tm, w_), lambda i: (i, 0))
    return pl.pallas_call(
        _ffn_kernel,
        grid=(t // tm,),
        in_specs=[row(D_MODEL), row(D_MODEL), row(D_MODEL), row(2 * D_MODEL),
                  _resident((D_MODEL, D_MODEL)), _resident((1, D_MODEL)),
                  _resident((D_MODEL, 2 * D_FF)), _resident((D_FF, D_MODEL)), _resident((1, D_MODEL))],
        out_specs=row(D_MODEL),
        out_shape=jax.ShapeDtypeStruct((t, D_MODEL), F32),
        compiler_params=pltpu.CompilerParams(
            dimension_semantics=("arbitrary",), vmem_limit_bytes=VMEM_LIMIT_BYTES),
        name="merge_ffn",
    )(x2, ya, yb, gates, wo, g2, wi, wd, g3)


def _gate_weights(wa, wx):
    def blockdiag(w):
        w = 0.5 * w.reshape(N_GROUPS, LRU_GROUP // LRU_BLOCK, LRU_BLOCK, LRU_BLOCK)
        eye = jnp.eye(LRU_GROUP // LRU_BLOCK, dtype=w.dtype)
        return jnp.einsum("ghij,hk->ghikj", w, eye).reshape(N_GROUPS, LRU_GROUP, LRU_GROUP)
    return jnp.concatenate([blockdiag(wa), blockdiag(wx)], axis=-1).astype(BF16)


def kernel(x, norm_mix_g, w_in, b_gate, conv_w, conv_b, lru_lambda, lru_wa, lru_ba, lru_wx, lru_bx,
           attn_sink, w_out, norm_ffn_g, w_ffn_in, w_ffn_out, norm_final_g):
    bsz, s, d = x.shape
    depth = w_in.shape[0]
    assert depth == 1, "the merge/ffn kernel applies the final RMSNorm, so it must be the last layer"
    t = bsz * s
    row = lambda a: a.reshape(1, -1)
    time_major = lambda a: a.reshape(a.shape[:-1] + (LANE_TILES, LANES))
    x2 = x.reshape(t, d)
    for l in range(depth):
        u, gelu_g, q, k, v, merge_gates = _in_proj(x2, row(norm_mix_g[l]), w_in[l].astype(BF16),
                                                   row(b_gate[l]))
        seq = lambda a: a.reshape(bsz, s, a.shape[-1])
        lru = lambda dr: (row(lru_lambda[l, dr]), _gate_weights(lru_wa[l, dr], lru_wx[l, dr]),
                          row(lru_ba[l, dr]), row(lru_bx[l, dr]))
        h_fwd, uc = _lru_fwd(seq(u), time_major(conv_w[l]), time_major(conv_b[l]), *lru(0))
        y_a = _lru_bwd(uc, h_fwd, seq(gelu_g), *lru(1))
        y_b = _attention(seq(q), seq(k), seq(v), attn_sink[l])
        x2 = _ffn(x2, y_a.reshape(t, d), y_b.reshape(t, d), merge_gates,
                  w_out[l].astype(BF16), row(norm_ffn_g[l]), w_ffn_in[l].astype(BF16),
                  w_ffn_out[l].astype(BF16), row(norm_final_g))
    return x2.reshape(bsz, s, d)
```

```python
import functools
import math

import numpy as np
import jax
import jax.numpy as jnp
from jax import lax
from jax.experimental import pallas as pl
from jax.experimental.pallas import tpu as pltpu

F32 = jnp.float32
BF16 = jnp.bfloat16

D_MODEL = 1024
LRU_HEADS = 16
LRU_BLOCK = D_MODEL // LRU_HEADS
CONV_WIDTH = 4
CONV_LEFT = 2
RGLRU_C = 8.0
N_HEADS = 16
N_KV_HEADS = 4
HEAD_DIM = 64
GROUP = N_HEADS // N_KV_HEADS
WINDOW = 128
BLOCK = 128
D_FF = 2816
Q_W = N_HEADS * HEAD_DIM
KV_W = N_KV_HEADS * HEAD_DIM
IN_W = 2 * D_MODEL + Q_W + 2 * KV_W + 2 * D_MODEL
EPS = 1e-6
NEG_INF = -1e30

LANES = 128
SUBLANES = 8
VMEM_LIMIT_BYTES = 56 * 1024 * 1024

TM_PROJ = 512
TS_LRU = 512
TQ_ATTN = 1024
TM_FFN = 512
LRU_GROUP = 256
N_GROUPS = D_MODEL // LRU_GROUP
LANE_TILES = D_MODEL // LANES
assert LANE_TILES == SUBLANES
SLAB = SUBLANES * LANE_TILES
TILES_PER_STEP = 4
N_SLABS = 3 * TILES_PER_STEP

_ALIBI_SLOPES = [float(v) for v in np.exp2(
    -8.0 * (np.arange(N_HEADS, dtype=np.float32) + 1.0) / N_HEADS).astype(np.float32)]


def _sigmoid(x):
    return 0.5 * jnp.tanh(0.5 * x) + 0.5


def _rmsnorm(x, g):
    return x * lax.rsqrt(jnp.mean(x * x, axis=-1, keepdims=True) + EPS) * g


def _resident(shape):
    nd = len(shape)
    return pl.BlockSpec(shape, lambda *_: (0,) * nd, pipeline_mode=pl.Buffered(1))


def _in_proj_kernel(nblk, x_ref, xp_ref, xnx_ref, g_ref, w_ref, bg_ref, cw_ref, cb_ref,
                    uc_ref, gg_ref, q_ref, k_ref, v_ref, gate_ref, uext_ref, slab_ref):
    tm = x_ref.shape[0]
    nt = tm // SUBLANES
    jb = pl.program_id(0) % nblk
    gn = g_ref[...]
    xn = _rmsnorm(x_ref[...], gn).astype(BF16)
    halo = jnp.concatenate([xp_ref[...], xnx_ref[...]], axis=0)
    xh = _rmsnorm(halo, gn).astype(BF16)

    def proj(lo, hi):
        return jnp.dot(xn, w_ref[:, lo:hi], preferred_element_type=F32)

    ue = jnp.dot(jnp.concatenate([xn, xh], axis=0), w_ref[:, :D_MODEL], preferred_element_type=F32)
    uext_ref[SUBLANES:SUBLANES + tm, :] = ue[:tm]
    uext_ref[0:SUBLANES, :] = jnp.where(jb > 0, ue[tm:tm + SUBLANES], 0.0)
    uext_ref[SUBLANES + tm:, :] = jnp.where(jb < nblk - 1, ue[tm + SUBLANES:], 0.0)

    o_g = D_MODEL
    o_q = o_g + D_MODEL
    o_k = o_q + Q_W
    o_v = o_k + KV_W
    o_z = o_v + KV_W
    pw = 2 * LANES
    todo = []
    for dst, lo, width, post in (
            (gate_ref, o_z, 2 * D_MODEL, lambda v, c: _sigmoid(v + bg_ref[:, c:c + pw])),
            (gg_ref, o_g, D_MODEL, lambda v, c: _gelu_tanh(v)),
            (q_ref, o_q, Q_W, lambda v, c: (v * (HEAD_DIM ** -0.5)).astype(BF16)),
            (k_ref, o_k, KV_W, lambda v, c: v.astype(BF16)),
            (v_ref, o_v, KV_W, lambda v, c: v.astype(BF16))):
        todo += [(dst, c, lo + c, post) for c in range(0, width, pw)]
    every = -(-nt // len(todo))

    def project_block():
        dst, c, wo, post = todo.pop(0)
        dst[:, c:c + pw] = post(proj(wo, wo + pw), c)

    w = [cw_ref[k] for k in range(CONV_WIDTH)]
    cb = cb_ref[...]
    ring = slab_ref.shape[0] // 2

    def ext_steps(e, s):
        return _split_time(uext_ref[e * SUBLANES:(e + 1) * SUBLANES, :], slab_ref, s)

    tail = ext_steps(0, 0)
    prev2, prev1 = tail[SUBLANES - 2], tail[SUBLANES - 1]
    cur = ext_steps(1, 1)
    for t in range(nt):
        if t % every == 0 and todo:
            project_block()
        nxt = ext_steps(t + 2, 2 * (t % ring))
        x = [prev2, prev1] + cur + [nxt[0]]
        out = [(x[r] * w[0] + x[r + 1] * w[1]) + (x[r + 2] * w[2] + x[r + 3] * w[3]) + cb
               for r in range(SUBLANES)]
        uc_ref[t * SUBLANES:(t + 1) * SUBLANES, :] = _join_time(out, slab_ref, 2 * (t % ring) + 1)
        prev2, prev1, cur = cur[SUBLANES - 2], cur[SUBLANES - 1], nxt
    while todo:
        project_block()


def _in_proj(x2, seq, g, w, bg, cw, cb):
    t = x2.shape[0]
    tm = TM_PROJ
    nblk = seq // tm
    hb = tm // SUBLANES
    nh = t // SUBLANES
    row = lambda w_: pl.BlockSpec((tm, w_), lambda i: (i, 0))
    prev = pl.BlockSpec((SUBLANES, D_MODEL), lambda i: (jnp.maximum(i * hb - 1, 0), 0))
    nxt = pl.BlockSpec((SUBLANES, D_MODEL), lambda i: (jnp.minimum((i + 1) * hb, nh - 1), 0))
    return pl.pallas_call(
        functools.partial(_in_proj_kernel, nblk),
        grid=(t // tm,),
        in_specs=[row(D_MODEL), prev, nxt, _resident((1, D_MODEL)), _resident((D_MODEL, IN_W)),
                  _resident((1, 2 * D_MODEL)), _resident(cw.shape), _resident(cb.shape)],
        out_specs=[row(D_MODEL), row(D_MODEL), row(Q_W), row(KV_W), row(KV_W), row(2 * D_MODEL)],
        out_shape=[
            jax.ShapeDtypeStruct((t, D_MODEL), F32),
            jax.ShapeDtypeStruct((t, D_MODEL), F32),
            jax.ShapeDtypeStruct((t, Q_W), BF16),
            jax.ShapeDtypeStruct((t, KV_W), BF16),
            jax.ShapeDtypeStruct((t, KV_W), BF16),
            jax.ShapeDtypeStruct((t, 2 * D_MODEL), F32),
        ],
        scratch_shapes=[
            pltpu.VMEM((tm + 2 * SUBLANES, D_MODEL), F32),
            pltpu.VMEM((16, SLAB, LANES), F32),
        ],
        compiler_params=pltpu.CompilerParams(
            dimension_semantics=("arbitrary",), vmem_limit_bytes=VMEM_LIMIT_BYTES),
        name="in_proj",
    )(x2, x2, x2, g, w, bg, cw, cb)


def _split_time(tile, slab_ref, s):
    for j in range(LANE_TILES):
        slab_ref[s, j * SUBLANES:(j + 1) * SUBLANES, :] = tile[:, j * LANES:(j + 1) * LANES]
    return [slab_ref[s, pl.ds(r, LANE_TILES, stride=SUBLANES), :] for r in range(SUBLANES)]


def _join_time(steps, slab_ref, s):
    for r in range(SUBLANES):
        slab_ref[s, pl.ds(r, LANE_TILES, stride=SUBLANES), :] = steps[r]
    return jnp.concatenate(
        [slab_ref[s, j * SUBLANES:(j + 1) * SUBLANES, :] for j in range(LANE_TILES)], axis=1)


def _half_decay(lam):
    return (-0.5 * RGLRU_C) * (jnp.maximum(-lam, 0.0) + jnp.log1p(jnp.exp(-jnp.abs(lam))))


def _gate_terms(pre_r, pre_i, uc, hdecay, hba, hbx):
    log_a = hdecay * jnp.tanh(pre_r + hba) + hdecay
    a = jnp.exp(log_a)
    y = jnp.maximum(1.0 - a * a, 0.0)
    beta = jnp.where(y > 0.0, y * lax.rsqrt(y), 0.0)
    i = 0.5 * jnp.tanh(pre_i + hbx) + 0.5
    return a, beta * (i * uc)


def _gelu_tanh(g):
    c = math.sqrt(2.0 / math.pi)
    half = 0.5 * g
    return half * jnp.tanh(g * ((c * 0.044715) * (g * g) + c)) + half


def _gate_maps(uc_ref, wg_ref, ra_ref, ix_ref):
    for gi in range(N_GROUPS):
        cols = slice(gi * LRU_GROUP, (gi + 1) * LRU_GROUP)
        pre = jnp.dot(uc_ref[:, cols].astype(BF16), wg_ref[gi], preferred_element_type=F32)
        ra_ref[:, cols] = pre[:, :LRU_GROUP]
        ix_ref[:, cols] = pre[:, LRU_GROUP:]


def _recurrence(reverse, uc_ref, ra_ref, ix_ref, lam_ref, ba_ref, bx_ref, carry_ref, slab_ref, emit):
    nt = uc_ref.shape[0] // SUBLANES
    nsteps = nt // TILES_PER_STEP
    hdecay = _half_decay(lam_ref[...])
    hba = 0.5 * ba_ref[...]
    hbx = 0.5 * bx_ref[...]

    def group(i, h):
        for s in range(TILES_PER_STEP):
            k = TILES_PER_STEP * i + s
            kk = (nt - 1 - k) if reverse else k
            rows = pl.ds(pl.multiple_of(kk * SUBLANES, SUBLANES), SUBLANES)
            a, b = _gate_terms(ra_ref[rows, :], ix_ref[rows, :], uc_ref[rows, :], hdecay, hba, hbx)
            a_t = _split_time(a, slab_ref, 3 * s)
            b_t = _split_time(b, slab_ref, 3 * s + 1)
            hs = [None] * SUBLANES
            for r in (range(SUBLANES - 1, -1, -1) if reverse else range(SUBLANES)):
                h = a_t[r] * h + b_t[r]
                hs[r] = h
            emit(rows, _join_time(hs, slab_ref, 3 * s + 2))
        return h

    carry_ref[...] = lax.fori_loop(0, nsteps, group, carry_ref[...])


def _lru_fwd_kernel(uc_ref, lam_ref, wg_ref, ba_ref, bx_ref,
                    hf_ref, ra_ref, ix_ref, carry_ref, slab_ref):
    @pl.when(pl.program_id(1) == 0)
    def _():
        carry_ref[...] = jnp.zeros_like(carry_ref)

    _gate_maps(uc_ref, wg_ref, ra_ref, ix_ref)

    def emit(rows, h):
        hf_ref[rows, :] = h

    _recurrence(False, uc_ref, ra_ref, ix_ref, lam_ref, ba_ref, bx_ref, carry_ref, slab_ref, emit)


def _lru_bwd_kernel(uc_ref, hf_ref, g_ref, lam_ref, wg_ref, ba_ref, bx_ref,
                    ya_ref, ra_ref, ix_ref, carry_ref, slab_ref):
    @pl.when(pl.program_id(1) == 0)
    def _():
        carry_ref[...] = jnp.zeros_like(carry_ref)

    _gate_maps(uc_ref, wg_ref, ra_ref, ix_ref)

    def emit(rows, h):
        ya_ref[rows, :] = (hf_ref[rows, :] + h) * g_ref[rows, :]

    _recurrence(True, uc_ref, ra_ref, ix_ref, lam_ref, ba_ref, bx_ref, carry_ref, slab_ref, emit)


_LRU_SCRATCH_TAIL = [
    pltpu.VMEM((TS_LRU, D_MODEL), F32),
    pltpu.VMEM((TS_LRU, D_MODEL), F32),
    pltpu.VMEM((LANE_TILES, LANES), F32),
    pltpu.VMEM((N_SLABS, SLAB, LANES), F32),
]


def _lru_fwd(uc3, lam, wg, ba, bx):
    bsz, s, _ = uc3.shape
    ts = TS_LRU
    blk = pl.BlockSpec((None, ts, D_MODEL), lambda b, j: (b, j, 0))
    return pl.pallas_call(
        _lru_fwd_kernel,
        grid=(bsz, s // ts),
        in_specs=[blk, _resident((1, D_MODEL)), _resident(wg.shape),
                  _resident((1, D_MODEL)), _resident((1, D_MODEL))],
        out_specs=blk,
        out_shape=jax.ShapeDtypeStruct((bsz, s, D_MODEL), F32),
        scratch_shapes=_LRU_SCRATCH_TAIL,
        compiler_params=pltpu.CompilerParams(
            dimension_semantics=("arbitrary", "arbitrary"), vmem_limit_bytes=VMEM_LIMIT_BYTES),
        name="lru_fwd",
    )(uc3, lam, wg, ba, bx)


def _lru_bwd(uc3, hf3, g3, lam, wg, ba, bx):
    bsz, s, _ = uc3.shape
    ts = TS_LRU
    nblk = s // ts
    blk = pl.BlockSpec((None, ts, D_MODEL), lambda b, j: (b, nblk - 1 - j, 0))
    return pl.pallas_call(
        _lru_bwd_kernel,
        grid=(bsz, nblk),
        in_specs=[blk, blk, blk, _resident((1, D_MODEL)), _resident(wg.shape),
                  _resident((1, D_MODEL)), _resident((1, D_MODEL))],
        out_specs=blk,
        out_shape=jax.ShapeDtypeStruct((bsz, s, D_MODEL), F32),
        scratch_shapes=_LRU_SCRATCH_TAIL,
        compiler_params=pltpu.CompilerParams(
            dimension_semantics=("arbitrary", "arbitrary"), vmem_limit_bytes=VMEM_LIMIT_BYTES),
        name="lru_bwd",
    )(uc3, hf3, g3, lam, wg, ba, bx)


_N_BIAS = 3 * N_HEADS + 1
_MASKED_TILE = 3 * N_HEADS


def _attn_kernel(nb, sink_ref, q_ref, kp_ref, km_ref, kn_ref, vp_ref, vm_ref, vn_ref,
                 o_ref, kz_ref, vab_ref, bias_ref, s_ref, m_ref, p_ref):
    tq = q_ref.shape[0]
    tk = tq + 2 * BLOCK
    nsub = tq // BLOCK
    j = pl.program_id(1)

    @pl.when((pl.program_id(0) == 0) & (j == 0))
    def _():
        row = lax.broadcasted_iota(jnp.int32, (BLOCK, BLOCK), 0)
        col = lax.broadcasted_iota(jnp.int32, (BLOCK, BLOCK), 1)
        bias_ref[_MASKED_TILE] = jnp.ones((BLOCK, BLOCK), F32)
        for c in range(3):
            dist = jnp.abs(row + (1 - c) * BLOCK - col)
            absd = dist.astype(F32)
            for h in range(N_HEADS):
                bias_ref[3 * h + c] = jnp.where(dist <= WINDOW, -_ALIBI_SLOPES[h] * absd, 1.0)

    low = lax.broadcasted_iota(jnp.int32, (tk, LANES), 1) < HEAD_DIM
    low_q = lax.broadcasted_iota(jnp.int32, (BLOCK, LANES), 1) < HEAD_DIM
    ones_q = [jnp.where(low_q, 1.0, 0.0).astype(BF16), jnp.where(low_q, 0.0, 1.0).astype(BF16)]
    for t in range(KV_W // LANES):
        cols = slice(t * LANES, (t + 1) * LANES)
        full = jnp.concatenate([r[:, cols] for r in (kp_ref, km_ref, kn_ref)], axis=0).astype(F32)
        swapped = pltpu.roll(full, HEAD_DIM, axis=1)
        kz_ref[4 * t + 0] = jnp.where(low, full, 0.0).astype(BF16)
        kz_ref[4 * t + 1] = jnp.where(low, 0.0, swapped).astype(BF16)
        kz_ref[4 * t + 2] = jnp.where(low, swapped, 0.0).astype(BF16)
        kz_ref[4 * t + 3] = jnp.where(low, 0.0, full).astype(BF16)
        nkb = tk // BLOCK
        for b in range(nkb):
            if b == 0:
                src = vp_ref[:, cols]
            elif b == nkb - 1:
                src = vn_ref[:, cols]
            else:
                src = vm_ref[(b - 1) * BLOCK:b * BLOCK, cols]
            full = src.astype(F32)
            swapped = pltpu.roll(full, HEAD_DIM, axis=1)
            placed = [jnp.where(low_q, full, 0.0), jnp.where(low_q, 0.0, swapped),
                      jnp.where(low_q, swapped, 0.0), jnp.where(low_q, 0.0, full)]
            for i in range(4):
                g, half = 2 * t + i // 2, i % 2
                dst = slice((2 * b + half) * BLOCK, (2 * b + half + 1) * BLOCK)
                vab_ref[g, dst, :LANES] = placed[i].astype(BF16)
                vab_ref[g, dst, LANES:] = ones_q[half]

    def scores(qi, slot):
        q0 = pl.multiple_of(qi * BLOCK, BLOCK)
        blk = j * nsub + qi
        win = pl.ds(q0, 3 * BLOCK)
        sc2 = [None, None]
        for h in range(N_HEADS):
            g, pp, half = h // GROUP, (h % GROUP) // 2, h % 2
            if pp == 0:
                q2 = jnp.concatenate(
                    [q_ref[pl.ds(q0, BLOCK), (2 * g + i) * LANES:(2 * g + i + 1) * LANES] for i in range(2)],
                    axis=0)
                sc2[half] = lax.dot_general(q2, kz_ref[2 * g + half, win, :], (((1,), (1,)), ((), ())),
                                            preferred_element_type=F32)
            sc = sc2[half][pp * BLOCK:(pp + 1) * BLOCK]
            tiles = []
            for c in range(3):
                t = sc[:, c * BLOCK:(c + 1) * BLOCK]
                if c == 1:
                    t = t + bias_ref[3 * h + 1]
                else:
                    edge = (blk == 0) if c == 0 else (blk == nb - 1)
                    b = bias_ref[jnp.where(edge, _MASKED_TILE, 3 * h + c)]
                    t = jnp.where(b > 0.0, NEG_INF, t + b)
                s_ref[slot, h, :, c * BLOCK:(c + 1) * BLOCK] = t
                tiles.append(t)
            mx = jnp.max(jnp.maximum(jnp.maximum(tiles[0], tiles[1]), tiles[2]), axis=-1, keepdims=True)
            m_ref[slot, h] = jnp.broadcast_to(jnp.maximum(mx, sink_ref[h]), (BLOCK, LANES))

    def finish(qi, slot):
        q0 = pl.multiple_of(qi * BLOCK, BLOCK)
        win2 = pl.ds(pl.multiple_of(qi * (2 * BLOCK), 2 * BLOCK), 6 * BLOCK)
        for h in range(N_HEADS):
            m = m_ref[slot, h]
            for c in range(3):
                cs = slice(c * BLOCK, (c + 1) * BLOCK)
                dst = slice((2 * c + h % 2) * BLOCK, (2 * c + h % 2 + 1) * BLOCK)
                prow = slice(((h // 2) % 2) * BLOCK, ((h // 2) % 2 + 1) * BLOCK)
                p_ref[slot, h // GROUP, prow, dst] = jnp.exp(s_ref[slot, h, :, cs] - m).astype(BF16)
        for g in range(N_KV_HEADS):
            o2 = jnp.dot(p_ref[slot, g], vab_ref[g, win2, :], preferred_element_type=F32)
            for pp in range(2):
                pr = 2 * g + pp
                h0, h1 = 2 * pr, 2 * pr + 1
                o = o2[pp * BLOCK:(pp + 1) * BLOCK]
                e = jnp.where(low_q, jnp.exp(sink_ref[h0] - m_ref[slot, h0]),
                              jnp.exp(sink_ref[h1] - m_ref[slot, h1]))
                o_ref[pl.ds(q0, BLOCK), pr * LANES:(pr + 1) * LANES] = (
                    o[:, :LANES] * (1.0 / (o[:, LANES:] + e)))

    scores(0, 0)

    def two_sub_blocks(i, _):
        qa = 2 * i
        scores(qa + 1, 1)
        finish(qa, 0)
        scores(jnp.minimum(qa + 2, nsub - 1), 0)
        finish(qa + 1, 1)
        return 0

    lax.fori_loop(0, nsub // 2, two_sub_blocks, 0)


def _attention(q3, k3, v3, sink):
    bsz, s, _ = q3.shape
    tq = TQ_ATTN
    nq = s // tq
    hb = tq // BLOCK
    nb = s // BLOCK
    tk = tq + 2 * BLOCK
    kv_main = pl.BlockSpec((None, tq, KV_W), lambda b, j: (b, j, 0))
    kv_prev = pl.BlockSpec((None, BLOCK, KV_W), lambda b, j: (b, jnp.maximum(j * hb - 1, 0), 0))
    kv_next = pl.BlockSpec((None, BLOCK, KV_W), lambda b, j: (b, jnp.minimum((j + 1) * hb, nb - 1), 0))
    qo = pl.BlockSpec((None, tq, Q_W), lambda b, j: (b, j, 0))
    return pl.pallas_call(
        functools.partial(_attn_kernel, nb),
        grid=(bsz, nq),
        in_specs=[pl.BlockSpec(memory_space=pltpu.SMEM), qo,
                  kv_prev, kv_main, kv_next, kv_prev, kv_main, kv_next],
        out_specs=qo,
        out_shape=jax.ShapeDtypeStruct((bsz, s, Q_W), F32),
        scratch_shapes=[
            pltpu.VMEM((2 * N_KV_HEADS, tk, LANES), BF16),
            pltpu.VMEM((N_KV_HEADS, 2 * tk, 2 * LANES), BF16),
            pltpu.VMEM((_N_BIAS, BLOCK, BLOCK), F32),
            pltpu.VMEM((2, N_HEADS, BLOCK, 3 * BLOCK), F32),
            pltpu.VMEM((2, N_HEADS, BLOCK, LANES), F32),
            pltpu.VMEM((2, N_KV_HEADS, 2 * BLOCK, 6 * BLOCK), BF16),
        ],
        compiler_params=pltpu.CompilerParams(
            dimension_semantics=("arbitrary", "arbitrary"), vmem_limit_bytes=VMEM_LIMIT_BYTES),
        name="attention",
    )(sink, q3, k3, k3, k3, v3, v3, v3)


def _ffn_kernel(x_ref, ya_ref, yb_ref, gate_ref, wo_ref, g2_ref, wi_ref, wd_ref, g3_ref, o_ref):
    merged = gate_ref[:, :D_MODEL] * ya_ref[...] + gate_ref[:, D_MODEL:] * yb_ref[...]
    x1 = x_ref[...] + jnp.dot(merged.astype(BF16), wo_ref[...], preferred_element_type=F32)
    xn = _rmsnorm(x1, g2_ref[...]).astype(BF16)
    gate = jnp.dot(xn, wi_ref[:, :D_FF], preferred_element_type=F32)
    up = jnp.dot(xn, wi_ref[:, D_FF:], preferred_element_type=F32)
    act = (gate * _sigmoid(gate) * up).astype(BF16)
    x2 = x1 + jnp.dot(act, wd_ref[...], preferred_element_type=F32)
    o_ref[...] = _rmsnorm(x2, g3_ref[...])


def _ffn(x2, ya, yb, gates, wo, g2, wi, wd, g3):
    t = x2.shape[0]
    tm = TM_FFN
    row = lambda w_: pl.BlockSpec((tm, w_), lambda i: (i, 0))
    return pl.pallas_call(
        _ffn_kernel,
        grid=(t // tm,),
        in_specs=[row(D_MODEL), row(D_MODEL), row(D_MODEL), row(2 * D_MODEL),
                  _resident((D_MODEL, D_MODEL)), _resident((1, D_MODEL)),
                  _resident((D_MODEL, 2 * D_FF)), _resident((D_FF, D_MODEL)), _resident((1, D_MODEL))],
        out_specs=row(D_MODEL),
        out_shape=jax.ShapeDtypeStruct((t, D_MODEL), F32),
        compiler_params=pltpu.CompilerParams(
            dimension_semantics=("arbitrary",), vmem_limit_bytes=VMEM_LIMIT_BYTES),
        name="merge_ffn",
    )(x2, ya, yb, gates, wo, g2, wi, wd, g3)


def _gate_weights(wa, wx):
    def blockdiag(w):
        w = 0.5 * w.reshape(N_GROUPS, LRU_GROUP // LRU_BLOCK, LRU_BLOCK, LRU_BLOCK)
        eye = jnp.eye(LRU_GROUP // LRU_BLOCK, dtype=w.dtype)
        return jnp.einsum("ghij,hk->ghikj", w, eye).reshape(N_GROUPS, LRU_GROUP, LRU_GROUP)
    return jnp.concatenate([blockdiag(wa), blockdiag(wx)], axis=-1).astype(BF16)


def kernel(x, norm_mix_g, w_in, b_gate, conv_w, conv_b, lru_lambda, lru_wa, lru_ba, lru_wx, lru_bx,
           attn_sink, w_out, norm_ffn_g, w_ffn_in, w_ffn_out, norm_final_g):
    bsz, s, d = x.shape
    depth = w_in.shape[0]
    assert depth == 1, "the merge/ffn kernel applies the final RMSNorm, so it must be the last layer"
    t = bsz * s
    row = lambda a: a.reshape(1, -1)
    time_major = lambda a: a.reshape(a.shape[:-1] + (LANE_TILES, LANES))
    x2 = x.reshape(t, d)
    for l in range(depth):
        uc, gelu_g, q, k, v, merge_gates = _in_proj(
            x2, s, row(norm_mix_g[l]), w_in[l].astype(BF16), row(b_gate[l]),
            time_major(conv_w[l]), time_major(conv_b[l]))
        seq = lambda a: a.reshape(bsz, s, a.shape[-1])
        lru = lambda dr: (row(lru_lambda[l, dr]), _gate_weights(lru_wa[l, dr], lru_wx[l, dr]),
                          row(lru_ba[l, dr]), row(lru_bx[l, dr]))
        h_fwd = _lru_fwd(seq(uc), *lru(0))
        y_a = _lru_bwd(seq(uc), h_fwd, seq(gelu_g), *lru(1))
        y_b = _attention(seq(q), seq(k), seq(v), attn_sink[l])
        x2 = _ffn(x2, y_a.reshape(t, d), y_b.reshape(t, d), merge_gates,
                  w_out[l].astype(BF16), row(norm_ffn_g[l]), w_ffn_in[l].astype(BF16),
                  w_ffn_out[l].astype(BF16), row(norm_final_g))
    return x2.reshape(bsz, s, d)
```

```python
import functools
import math

import numpy as np
import jax
import jax.numpy as jnp
from jax import lax
from jax.experimental import pallas as pl
from jax.experimental.pallas import tpu as pltpu

F32 = jnp.float32
BF16 = jnp.bfloat16

D_MODEL = 1024
LRU_HEADS = 16
LRU_BLOCK = D_MODEL // LRU_HEADS
CONV_WIDTH = 4
CONV_LEFT = 2
RGLRU_C = 8.0
N_HEADS = 16
N_KV_HEADS = 4
HEAD_DIM = 64
GROUP = N_HEADS // N_KV_HEADS
WINDOW = 128
BLOCK = 128
D_FF = 2816
Q_W = N_HEADS * HEAD_DIM
KV_W = N_KV_HEADS * HEAD_DIM
IN_W = 2 * D_MODEL + Q_W + 2 * KV_W + 2 * D_MODEL
EPS = 1e-6
NEG_INF = -1e30

LANES = 128
SUBLANES = 8
VMEM_LIMIT_BYTES = 56 * 1024 * 1024

TM_PROJ = 512
TS_LRU = 512
TQ_ATTN = 1024
TM_FFN = 512
LRU_GROUP = 256
N_GROUPS = D_MODEL // LRU_GROUP
LANE_TILES = D_MODEL // LANES
assert LANE_TILES == SUBLANES
SLAB = SUBLANES * LANE_TILES
TILES_PER_STEP = 16
N_SLABS = 3 * TILES_PER_STEP

_ALIBI_SLOPES = [float(v) for v in np.exp2(
    -8.0 * (np.arange(N_HEADS, dtype=np.float32) + 1.0) / N_HEADS).astype(np.float32)]


def _sigmoid(x):
    return 0.5 * jnp.tanh(0.5 * x) + 0.5


def _rmsnorm(x, g):
    return x * lax.rsqrt(jnp.mean(x * x, axis=-1, keepdims=True) + EPS) * g


def _resident(shape):
    nd = len(shape)
    return pl.BlockSpec(shape, lambda *_: (0,) * nd, pipeline_mode=pl.Buffered(1))


def _in_proj_kernel(nblk, x_ref, xp_ref, xnx_ref, g_ref, w_ref, bg_ref, cw_ref, cb_ref,
                    uc_ref, gg_ref, q_ref, k_ref, v_ref, gate_ref, uext_ref, slab_ref):
    tm = x_ref.shape[0]
    nt = tm // SUBLANES
    jb = pl.program_id(0) % nblk
    gn = g_ref[...]
    xn = _rmsnorm(x_ref[...], gn).astype(BF16)
    halo = jnp.concatenate([xp_ref[...], xnx_ref[...]], axis=0)
    xh = _rmsnorm(halo, gn).astype(BF16)

    def proj(lo, hi):
        return jnp.dot(xn, w_ref[:, lo:hi], preferred_element_type=F32)

    ue = jnp.dot(jnp.concatenate([xn, xh], axis=0), w_ref[:, :D_MODEL], preferred_element_type=F32)
    uext_ref[SUBLANES:SUBLANES + tm, :] = ue[:tm]
    uext_ref[0:SUBLANES, :] = jnp.where(jb > 0, ue[tm:tm + SUBLANES], 0.0)
    uext_ref[SUBLANES + tm:, :] = jnp.where(jb < nblk - 1, ue[tm + SUBLANES:], 0.0)

    o_g = D_MODEL
    o_q = o_g + D_MODEL
    o_k = o_q + Q_W
    o_v = o_k + KV_W
    o_z = o_v + KV_W
    pw = 2 * LANES
    todo = []
    for dst, lo, width, post in (
            (gate_ref, o_z, 2 * D_MODEL, lambda v, c: _sigmoid(v + bg_ref[:, c:c + pw])),
            (gg_ref, o_g, D_MODEL, lambda v, c: _gelu_tanh(v)),
            (q_ref, o_q, Q_W, lambda v, c: (v * (HEAD_DIM ** -0.5)).astype(BF16)),
            (k_ref, o_k, KV_W, lambda v, c: v.astype(BF16)),
            (v_ref, o_v, KV_W, lambda v, c: v.astype(BF16))):
        todo += [(dst, c, lo + c, post) for c in range(0, width, pw)]
    every = -(-nt // len(todo))

    def project_block():
        dst, c, wo, post = todo.pop(0)
        dst[:, c:c + pw] = post(proj(wo, wo + pw), c)

    w = [cw_ref[k] for k in range(CONV_WIDTH)]
    cb = cb_ref[...]
    ring = slab_ref.shape[0] // 2

    def ext_steps(e, s):
        return _split_time(uext_ref[e * SUBLANES:(e + 1) * SUBLANES, :], slab_ref, s)

    tail = ext_steps(0, 0)
    prev2, prev1 = tail[SUBLANES - 2], tail[SUBLANES - 1]
    cur = ext_steps(1, 1)
    for t in range(nt):
        if t % every == 0 and todo:
            project_block()
        nxt = ext_steps(t + 2, 2 * (t % ring))
        x = [prev2, prev1] + cur + [nxt[0]]
        out = [(x[r] * w[0] + x[r + 1] * w[1]) + (x[r + 2] * w[2] + x[r + 3] * w[3]) + cb
               for r in range(SUBLANES)]
        uc_ref[t * SUBLANES:(t + 1) * SUBLANES, :] = _join_time(out, slab_ref, 2 * (t % ring) + 1)
        prev2, prev1, cur = cur[SUBLANES - 2], cur[SUBLANES - 1], nxt
    while todo:
        project_block()


def _in_proj(x2, seq, g, w, bg, cw, cb):
    t = x2.shape[0]
    tm = TM_PROJ
    nblk = seq // tm
    hb = tm // SUBLANES
    nh = t // SUBLANES
    row = lambda w_: pl.BlockSpec((tm, w_), lambda i: (i, 0))
    prev = pl.BlockSpec((SUBLANES, D_MODEL), lambda i: (jnp.maximum(i * hb - 1, 0), 0))
    nxt = pl.BlockSpec((SUBLANES, D_MODEL), lambda i: (jnp.minimum((i + 1) * hb, nh - 1), 0))
    return pl.pallas_call(
        functools.partial(_in_proj_kernel, nblk),
        grid=(t // tm,),
        in_specs=[row(D_MODEL), prev, nxt, _resident((1, D_MODEL)), _resident((D_MODEL, IN_W)),
                  _resident((1, 2 * D_MODEL)), _resident(cw.shape), _resident(cb.shape)],
        out_specs=[row(D_MODEL), row(D_MODEL), row(Q_W), row(KV_W), row(KV_W), row(2 * D_MODEL)],
        out_shape=[
            jax.ShapeDtypeStruct((t, D_MODEL), F32),
            jax.ShapeDtypeStruct((t, D_MODEL), F32),
            jax.ShapeDtypeStruct((t, Q_W), BF16),
            jax.ShapeDtypeStruct((t, KV_W), BF16),
            jax.ShapeDtypeStruct((t, KV_W), BF16),
            jax.ShapeDtypeStruct((t, 2 * D_MODEL), F32),
        ],
        scratch_shapes=[
            pltpu.VMEM((tm + 2 * SUBLANES, D_MODEL), F32),
            pltpu.VMEM((16, SLAB, LANES), F32),
        ],
        compiler_params=pltpu.CompilerParams(
            dimension_semantics=("arbitrary",), vmem_limit_bytes=VMEM_LIMIT_BYTES),
        name="in_proj",
    )(x2, x2, x2, g, w, bg, cw, cb)


def _split_time(tile, slab_ref, s):
    for j in range(LANE_TILES):
        slab_ref[s, j * SUBLANES:(j + 1) * SUBLANES, :] = tile[:, j * LANES:(j + 1) * LANES]
    return [slab_ref[s, pl.ds(r, LANE_TILES, stride=SUBLANES), :] for r in range(SUBLANES)]


def _join_time(steps, slab_ref, s):
    for r in range(SUBLANES):
        slab_ref[s, pl.ds(r, LANE_TILES, stride=SUBLANES), :] = steps[r]
    return jnp.concatenate(
        [slab_ref[s, j * SUBLANES:(j + 1) * SUBLANES, :] for j in range(LANE_TILES)], axis=1)


def _half_decay_log2(lam):
    return (-0.5 * RGLRU_C * math.log2(math.e)) * (
        jnp.maximum(-lam, 0.0) + jnp.log1p(jnp.exp(-jnp.abs(lam))))


def _gate_terms(pre_r, pre_i, huc, hd2, hba, hbx):
    a = jnp.exp2(hd2 * jnp.tanh(pre_r + hba) + hd2)
    y = jnp.maximum(1.0 - a * a, 0.0)
    beta = jnp.where(y > 0.0, y * lax.rsqrt(y), 0.0)
    return a, beta * (jnp.tanh(pre_i + hbx) * huc + huc)


def _gelu_tanh(g):
    c = math.sqrt(2.0 / math.pi)
    half = 0.5 * g
    return half * jnp.tanh(g * ((c * 0.044715) * (g * g) + c)) + half


def _gate_maps(uc_ref, wg_ref, ra_ref, ix_ref):
    for gi in range(N_GROUPS):
        cols = slice(gi * LRU_GROUP, (gi + 1) * LRU_GROUP)
        pre = jnp.dot(uc_ref[:, cols].astype(BF16), wg_ref[gi], preferred_element_type=F32)
        ra_ref[:, cols] = pre[:, :LRU_GROUP]
        ix_ref[:, cols] = pre[:, LRU_GROUP:]


def _recurrence(reverse, uc_ref, ra_ref, ix_ref, lam_ref, ba_ref, bx_ref, carry_ref, slab_ref, emit):
    nt = uc_ref.shape[0] // SUBLANES
    nsteps = nt // TILES_PER_STEP
    hd2 = _half_decay_log2(lam_ref[...])
    hba = 0.5 * ba_ref[...]
    hbx = 0.5 * bx_ref[...]

    def group(i, h):
        for s in range(TILES_PER_STEP):
            k = TILES_PER_STEP * i + s
            kk = (nt - 1 - k) if reverse else k
            rows = pl.ds(pl.multiple_of(kk * SUBLANES, SUBLANES), SUBLANES)
            a, b = _gate_terms(ra_ref[rows, :], ix_ref[rows, :], uc_ref[rows, :], hd2, hba, hbx)
            a_t = _split_time(a, slab_ref, 3 * s)
            b_t = _split_time(b, slab_ref, 3 * s + 1)
            hs = [None] * SUBLANES
            for r in (range(SUBLANES - 1, -1, -1) if reverse else range(SUBLANES)):
                h = a_t[r] * h + b_t[r]
                hs[r] = h
            emit(rows, _join_time(hs, slab_ref, 3 * s + 2))
        return h

    carry_ref[...] = lax.fori_loop(0, nsteps, group, carry_ref[...])


def _lru_fwd_kernel(uc_ref, lam_ref, wg_ref, ba_ref, bx_ref,
                    hf_ref, ra_ref, ix_ref, carry_ref, slab_ref):
    @pl.when(pl.program_id(1) == 0)
    def _():
        carry_ref[...] = jnp.zeros_like(carry_ref)

    _gate_maps(uc_ref, wg_ref, ra_ref, ix_ref)

    def emit(rows, h):
        hf_ref[rows, :] = h

    _recurrence(False, uc_ref, ra_ref, ix_ref, lam_ref, ba_ref, bx_ref, carry_ref, slab_ref, emit)


def _lru_bwd_kernel(uc_ref, hf_ref, g_ref, lam_ref, wg_ref, ba_ref, bx_ref,
                    ya_ref, ra_ref, ix_ref, carry_ref, slab_ref):
    @pl.when(pl.program_id(1) == 0)
    def _():
        carry_ref[...] = jnp.zeros_like(carry_ref)

    _gate_maps(uc_ref, wg_ref, ra_ref, ix_ref)

    def emit(rows, h):
        ya_ref[rows, :] = (hf_ref[rows, :] + h) * g_ref[rows, :]

    _recurrence(True, uc_ref, ra_ref, ix_ref, lam_ref, ba_ref, bx_ref, carry_ref, slab_ref, emit)


_LRU_SCRATCH_TAIL = [
    pltpu.VMEM((TS_LRU, D_MODEL), F32),
    pltpu.VMEM((TS_LRU, D_MODEL), F32),
    pltpu.VMEM((LANE_TILES, LANES), F32),
    pltpu.VMEM((N_SLABS, SLAB, LANES), F32),
]


def _lru_fwd(uc3, lam, wg, ba, bx):
    bsz, s, _ = uc3.shape
    ts = TS_LRU
    blk = pl.BlockSpec((None, ts, D_MODEL), lambda b, j: (b, j, 0))
    return pl.pallas_call(
        _lru_fwd_kernel,
        grid=(bsz, s // ts),
        in_specs=[blk, _resident((1, D_MODEL)), _resident(wg.shape),
                  _resident((1, D_MODEL)), _resident((1, D_MODEL))],
        out_specs=blk,
        out_shape=jax.ShapeDtypeStruct((bsz, s, D_MODEL), F32),
        scratch_shapes=_LRU_SCRATCH_TAIL,
        compiler_params=pltpu.CompilerParams(
            dimension_semantics=("arbitrary", "arbitrary"), vmem_limit_bytes=VMEM_LIMIT_BYTES),
        name="lru_fwd",
    )(uc3, lam, wg, ba, bx)


def _lru_bwd(uc3, hf3, g3, lam, wg, ba, bx):
    bsz, s, _ = uc3.shape
    ts = TS_LRU
    nblk = s // ts
    blk = pl.BlockSpec((None, ts, D_MODEL), lambda b, j: (b, nblk - 1 - j, 0))
    return pl.pallas_call(
        _lru_bwd_kernel,
        grid=(bsz, nblk),
        in_specs=[blk, blk, blk, _resident((1, D_MODEL)), _resident(wg.shape),
                  _resident((1, D_MODEL)), _resident((1, D_MODEL))],
        out_specs=blk,
        out_shape=jax.ShapeDtypeStruct((bsz, s, D_MODEL), F32),
        scratch_shapes=_LRU_SCRATCH_TAIL,
        compiler_params=pltpu.CompilerParams(
            dimension_semantics=("arbitrary", "arbitrary"), vmem_limit_bytes=VMEM_LIMIT_BYTES),
        name="lru_bwd",
    )(uc3, hf3, g3, lam, wg, ba, bx)


_N_BIAS = 3 * N_HEADS + 1
_MASKED_TILE = 3 * N_HEADS


def _attn_kernel(nb, sink_ref, q_ref, kp_ref, km_ref, kn_ref, vp_ref, vm_ref, vn_ref,
                 o_ref, kz_ref, vab_ref, bias_ref, s_ref, m_ref, p_ref):
    tq = q_ref.shape[0]
    tk = tq + 2 * BLOCK
    nsub = tq // BLOCK
    j = pl.program_id(1)

    @pl.when((pl.program_id(0) == 0) & (j == 0))
    def _():
        row = lax.broadcasted_iota(jnp.int32, (BLOCK, BLOCK), 0)
        col = lax.broadcasted_iota(jnp.int32, (BLOCK, BLOCK), 1)
        bias_ref[_MASKED_TILE] = jnp.ones((BLOCK, BLOCK), F32)
        for c in range(3):
            dist = jnp.abs(row + (1 - c) * BLOCK - col)
            absd = dist.astype(F32)
            for h in range(N_HEADS):
                bias_ref[3 * h + c] = jnp.where(dist <= WINDOW, -_ALIBI_SLOPES[h] * absd, 1.0)

    low = lax.broadcasted_iota(jnp.int32, (tk, LANES), 1) < HEAD_DIM
    low_q = lax.broadcasted_iota(jnp.int32, (BLOCK, LANES), 1) < HEAD_DIM
    ones_q = [jnp.where(low_q, 1.0, 0.0).astype(BF16), jnp.where(low_q, 0.0, 1.0).astype(BF16)]
    for t in range(KV_W // LANES):
        cols = slice(t * LANES, (t + 1) * LANES)
        full = jnp.concatenate([r[:, cols] for r in (kp_ref, km_ref, kn_ref)], axis=0).astype(F32)
        swapped = pltpu.roll(full, HEAD_DIM, axis=1)
        kz_ref[4 * t + 0] = jnp.where(low, full, 0.0).astype(BF16)
        kz_ref[4 * t + 1] = jnp.where(low, 0.0, swapped).astype(BF16)
        kz_ref[4 * t + 2] = jnp.where(low, swapped, 0.0).astype(BF16)
        kz_ref[4 * t + 3] = jnp.where(low, 0.0, full).astype(BF16)
        nkb = tk // BLOCK
        for b in range(nkb):
            if b == 0:
                src = vp_ref[:, cols]
            elif b == nkb - 1:
                src = vn_ref[:, cols]
            else:
                src = vm_ref[(b - 1) * BLOCK:b * BLOCK, cols]
            full = src.astype(F32)
            swapped = pltpu.roll(full, HEAD_DIM, axis=1)
            placed = [jnp.where(low_q, full, 0.0), jnp.where(low_q, 0.0, swapped),
                      jnp.where(low_q, swapped, 0.0), jnp.where(low_q, 0.0, full)]
            for i in range(4):
                g, half = 2 * t + i // 2, i % 2
                dst = slice((2 * b + half) * BLOCK, (2 * b + half + 1) * BLOCK)
                vab_ref[g, dst, :LANES] = placed[i].astype(BF16)
                vab_ref[g, dst, LANES:] = ones_q[half]

    def scores(qi, slot):
        q0 = pl.multiple_of(qi * BLOCK, BLOCK)
        blk = j * nsub + qi
        win = pl.ds(q0, 3 * BLOCK)
        sc2 = [None, None]
        for h in range(N_HEADS):
            g, pp, half = h // GROUP, (h % GROUP) // 2, h % 2
            if pp == 0:
                q2 = jnp.concatenate(
                    [q_ref[pl.ds(q0, BLOCK), (2 * g + i) * LANES:(2 * g + i + 1) * LANES] for i in range(2)],
                    axis=0)
                sc2[half] = lax.dot_general(q2, kz_ref[2 * g + half, win, :], (((1,), (1,)), ((), ())),
                                            preferred_element_type=F32)
            sc = sc2[half][pp * BLOCK:(pp + 1) * BLOCK]
            tiles = []
            for c in range(3):
                t = sc[:, c * BLOCK:(c + 1) * BLOCK]
                if c == 1:
                    t = t + bias_ref[3 * h + 1]
                else:
                    edge = (blk == 0) if c == 0 else (blk == nb - 1)
                    b = bias_ref[jnp.where(edge, _MASKED_TILE, 3 * h + c)]
                    t = jnp.where(b > 0.0, NEG_INF, t + b)
                s_ref[slot, h, :, c * BLOCK:(c + 1) * BLOCK] = t
                tiles.append(t)
            mx = jnp.max(jnp.maximum(jnp.maximum(tiles[0], tiles[1]), tiles[2]), axis=-1, keepdims=True)
            m_ref[slot, h] = jnp.broadcast_to(jnp.maximum(mx, sink_ref[h]), (BLOCK, LANES))

    def finish(qi, slot):
        q0 = pl.multiple_of(qi * BLOCK, BLOCK)
        win2 = pl.ds(pl.multiple_of(qi * (2 * BLOCK), 2 * BLOCK), 6 * BLOCK)
        for h in range(N_HEADS):
            m = m_ref[slot, h]
            for c in range(3):
                cs = slice(c * BLOCK, (c + 1) * BLOCK)
                dst = slice((2 * c + h % 2) * BLOCK, (2 * c + h % 2 + 1) * BLOCK)
                prow = slice(((h // 2) % 2) * BLOCK, ((h // 2) % 2 + 1) * BLOCK)
                p_ref[slot, h // GROUP, prow, dst] = jnp.exp(s_ref[slot, h, :, cs] - m).astype(BF16)
        for g in range(N_KV_HEADS):
            o2 = jnp.dot(p_ref[slot, g], vab_ref[g, win2, :], preferred_element_type=F32)
            for pp in range(2):
                pr = 2 * g + pp
                h0, h1 = 2 * pr, 2 * pr + 1
                o = o2[pp * BLOCK:(pp + 1) * BLOCK]
                e = jnp.where(low_q, jnp.exp(sink_ref[h0] - m_ref[slot, h0]),
                              jnp.exp(sink_ref[h1] - m_ref[slot, h1]))
                o_ref[pl.ds(q0, BLOCK), pr * LANES:(pr + 1) * LANES] = (
                    o[:, :LANES] * (1.0 / (o[:, LANES:] + e)))

    scores(0, 0)

    def two_sub_blocks(i, _):
        qa = 2 * i
        scores(qa + 1, 1)
        finish(qa, 0)
        scores(jnp.minimum(qa + 2, nsub - 1), 0)
        finish(qa + 1, 1)
        return 0

    lax.fori_loop(0, nsub // 2, two_sub_blocks, 0)


def _attention(q3, k3, v3, sink):
    bsz, s, _ = q3.shape
    tq = TQ_ATTN
    nq = s // tq
    hb = tq // BLOCK
    nb = s // BLOCK
    tk = tq + 2 * BLOCK
    kv_main = pl.BlockSpec((None, tq, KV_W), lambda b, j: (b, j, 0))
    kv_prev = pl.BlockSpec((None, BLOCK, KV_W), lambda b, j: (b, jnp.maximum(j * hb - 1, 0), 0))
    kv_next = pl.BlockSpec((None, BLOCK, KV_W), lambda b, j: (b, jnp.minimum((j + 1) * hb, nb - 1), 0))
    qo = pl.BlockSpec((None, tq, Q_W), lambda b, j: (b, j, 0))
    return pl.pallas_call(
        functools.partial(_attn_kernel, nb),
        grid=(bsz, nq),
        in_specs=[pl.BlockSpec(memory_space=pltpu.SMEM), qo,
                  kv_prev, kv_main, kv_next, kv_prev, kv_main, kv_next],
        out_specs=qo,
        out_shape=jax.ShapeDtypeStruct((bsz, s, Q_W), F32),
        scratch_shapes=[
            pltpu.VMEM((2 * N_KV_HEADS, tk, LANES), BF16),
            pltpu.VMEM((N_KV_HEADS, 2 * tk, 2 * LANES), BF16),
            pltpu.VMEM((_N_BIAS, BLOCK, BLOCK), F32),
            pltpu.VMEM((2, N_HEADS, BLOCK, 3 * BLOCK), F32),
            pltpu.VMEM((2, N_HEADS, BLOCK, LANES), F32),
            pltpu.VMEM((2, N_KV_HEADS, 2 * BLOCK, 6 * BLOCK), BF16),
        ],
        compiler_params=pltpu.CompilerParams(
            dimension_semantics=("arbitrary", "arbitrary"), vmem_limit_bytes=VMEM_LIMIT_BYTES),
        name="attention",
    )(sink, q3, k3, k3, k3, v3, v3, v3)


def _ffn_kernel(x_ref, ya_ref, yb_ref, gate_ref, wo_ref, g2_ref, wi_ref, wd_ref, g3_ref, o_ref):
    merged = gate_ref[:, :D_MODEL] * ya_ref[...] + gate_ref[:, D_MODEL:] * yb_ref[...]
    x1 = x_ref[...] + jnp.dot(merged.astype(BF16), wo_ref[...], preferred_element_type=F32)
    xn = _rmsnorm(x1, g2_ref[...]).astype(BF16)
    gate = jnp.dot(xn, wi_ref[:, :D_FF], preferred_element_type=F32)
    up = jnp.dot(xn, wi_ref[:, D_FF:], preferred_element_type=F32)
    act = (gate * _sigmoid(gate) * up).astype(BF16)
    x2 = x1 + jnp.dot(act, wd_ref[...], preferred_element_type=F32)
    o_ref[...] = _rmsnorm(x2, g3_ref[...])


def _ffn(x2, ya, yb, gates, wo, g2, wi, wd, g3):
    t = x2.shape[0]
    tm = TM_FFN
    row = lambda w_: pl.BlockSpec((tm, w_), lambda i: (i, 0))
    return pl.pallas_call(
        _ffn_kernel,
        grid=(t // tm,),
        in_specs=[row(D_MODEL), row(D_MODEL), row(D_MODEL), row(2 * D_MODEL),
                  _resident((D_MODEL, D_MODEL)), _resident((1, D_MODEL)),
                  _resident((D_MODEL, 2 * D_FF)), _resident((D_FF, D_MODEL)), _resident((1, D_MODEL))],
        out_specs=row(D_MODEL),
        out_shape=jax.ShapeDtypeStruct((t, D_MODEL), F32),
        compiler_params=pltpu.CompilerParams(
            dimension_semantics=("arbitrary",), vmem_limit_bytes=VMEM_LIMIT_BYTES),
        name="merge_ffn",
    )(x2, ya, yb, gates, wo, g2, wi, wd, g3)


def _gate_weights(wa, wx):
    def blockdiag(w):
        w = w.reshape(N_GROUPS, LRU_GROUP // LRU_BLOCK, LRU_BLOCK, LRU_BLOCK)
        eye = jnp.eye(LRU_GROUP // LRU_BLOCK, dtype=w.dtype)
        return jnp.einsum("ghij,hk->ghikj", w, eye).reshape(N_GROUPS, LRU_GROUP, LRU_GROUP)
    return jnp.concatenate([blockdiag(wa), blockdiag(wx)], axis=-1).astype(BF16)


def kernel(x, norm_mix_g, w_in, b_gate, conv_w, conv_b, lru_lambda, lru_wa, lru_ba, lru_wx, lru_bx,
           attn_sink, w_out, norm_ffn_g, w_ffn_in, w_ffn_out, norm_final_g):
    bsz, s, d = x.shape
    depth = w_in.shape[0]
    assert depth == 1, "the merge/ffn kernel applies the final RMSNorm, so it must be the last layer"
    t = bsz * s
    row = lambda a: a.reshape(1, -1)
    time_major = lambda a: a.reshape(a.shape[:-1] + (LANE_TILES, LANES))
    x2 = x.reshape(t, d)
    for l in range(depth):
        huc, gelu_g, q, k, v, merge_gates = _in_proj(
            x2, s, row(norm_mix_g[l]), w_in[l].astype(BF16), row(b_gate[l]),
            time_major(0.5 * conv_w[l]), time_major(0.5 * conv_b[l]))
        seq = lambda a: a.reshape(bsz, s, a.shape[-1])
        lru = lambda dr: (row(lru_lambda[l, dr]), _gate_weights(lru_wa[l, dr], lru_wx[l, dr]),
                          row(lru_ba[l, dr]), row(lru_bx[l, dr]))
        h_fwd = _lru_fwd(seq(huc), *lru(0))
        y_a = _lru_bwd(seq(huc), h_fwd, seq(gelu_g), *lru(1))
        y_b = _attention(seq(q), seq(k), seq(v), attn_sink[l])
        x2 = _ffn(x2, y_a.reshape(t, d), y_b.reshape(t, d), merge_gates,
                  w_out[l].astype(BF16), row(norm_ffn_g[l]), w_ffn_in[l].astype(BF16),
                  w_ffn_out[l].astype(BF16), row(norm_final_g))
    return x2.reshape(bsz, s, d)
```

```python
import functools
import math

import numpy as np
import jax
import jax.numpy as jnp
from jax import lax
from jax.experimental import pallas as pl
from jax.experimental.pallas import tpu as pltpu

F32 = jnp.float32
BF16 = jnp.bfloat16

D_MODEL = 1024
LRU_HEADS = 16
LRU_BLOCK = D_MODEL // LRU_HEADS
CONV_WIDTH = 4
CONV_LEFT = 2
RGLRU_C = 8.0
N_HEADS = 16
N_KV_HEADS = 4
HEAD_DIM = 64
GROUP = N_HEADS // N_KV_HEADS
WINDOW = 128
BLOCK = 128
D_FF = 2816
Q_W = N_HEADS * HEAD_DIM
KV_W = N_KV_HEADS * HEAD_DIM
IN_W = 2 * D_MODEL + Q_W + 2 * KV_W + 2 * D_MODEL
EPS = 1e-6
NEG_INF = -1e30

LANES = 128
SUBLANES = 8
VMEM_LIMIT_BYTES = 56 * 1024 * 1024

TM_PROJ = 512
TS_LRU = 512
TQ_ATTN = 1024
TM_FFN = 512
LRU_GROUP = 256
N_GROUPS = D_MODEL // LRU_GROUP
LANE_TILES = D_MODEL // LANES
assert LANE_TILES == SUBLANES
SLAB = SUBLANES * LANE_TILES
TILES_PER_STEP = 16
N_SLABS = 3 * TILES_PER_STEP

_ALIBI_SLOPES = [float(v) for v in np.exp2(
    -8.0 * (np.arange(N_HEADS, dtype=np.float32) + 1.0) / N_HEADS).astype(np.float32)]
LOG2_E = math.log2(math.e)
Q_SCALE = HEAD_DIM ** -0.5 * LOG2_E


def _sigmoid(x):
    return 0.5 * jnp.tanh(0.5 * x) + 0.5


def _rmsnorm(x, g):
    return x * lax.rsqrt(jnp.mean(x * x, axis=-1, keepdims=True) + EPS) * g


def _resident(shape):
    nd = len(shape)
    return pl.BlockSpec(shape, lambda *_: (0,) * nd, pipeline_mode=pl.Buffered(1))


def _in_proj_kernel(nblk, x_ref, xp_ref, xnx_ref, g_ref, w_ref, bg_ref, cw_ref, cb_ref,
                    uc_ref, gg_ref, q_ref, k_ref, v_ref, gate_ref, uext_ref, slab_ref):
    tm = x_ref.shape[0]
    nt = tm // SUBLANES
    jb = pl.program_id(0) % nblk
    gn = g_ref[...]
    xn = _rmsnorm(x_ref[...], gn).astype(BF16)
    halo = jnp.concatenate([xp_ref[...], xnx_ref[...]], axis=0)
    xh = _rmsnorm(halo, gn).astype(BF16)

    def proj(lo, hi):
        return jnp.dot(xn, w_ref[:, lo:hi], preferred_element_type=F32)

    ue = jnp.dot(jnp.concatenate([xn, xh], axis=0), w_ref[:, :D_MODEL], preferred_element_type=F32)
    uext_ref[SUBLANES:SUBLANES + tm, :] = ue[:tm]
    uext_ref[0:SUBLANES, :] = jnp.where(jb > 0, ue[tm:tm + SUBLANES], 0.0)
    uext_ref[SUBLANES + tm:, :] = jnp.where(jb < nblk - 1, ue[tm + SUBLANES:], 0.0)

    o_g = D_MODEL
    o_q = o_g + D_MODEL
    o_k = o_q + Q_W
    o_v = o_k + KV_W
    o_z = o_v + KV_W
    pw = 2 * LANES
    todo = []
    for dst, lo, width, post in (
            (gate_ref, o_z, 2 * D_MODEL, lambda v, c: _sigmoid(v + bg_ref[:, c:c + pw])),
            (gg_ref, o_g, D_MODEL, lambda v, c: _gelu_tanh(v)),
            (q_ref, o_q, Q_W, lambda v, c: (v * Q_SCALE).astype(BF16)),
            (k_ref, o_k, KV_W, lambda v, c: v.astype(BF16)),
            (v_ref, o_v, KV_W, lambda v, c: v.astype(BF16))):
        todo += [(dst, c, lo + c, post) for c in range(0, width, pw)]
    every = -(-nt // len(todo))

    def project_block():
        dst, c, wo, post = todo.pop(0)
        dst[:, c:c + pw] = post(proj(wo, wo + pw), c)

    w = [cw_ref[k] for k in range(CONV_WIDTH)]
    cb = cb_ref[...]
    ring = slab_ref.shape[0] // 2

    def ext_steps(e, s):
        return _split_time(uext_ref[e * SUBLANES:(e + 1) * SUBLANES, :], slab_ref, s)

    tail = ext_steps(0, 0)
    prev2, prev1 = tail[SUBLANES - 2], tail[SUBLANES - 1]
    cur = ext_steps(1, 1)
    for t in range(nt):
        if t % every == 0 and todo:
            project_block()
        nxt = ext_steps(t + 2, 2 * (t % ring))
        x = [prev2, prev1] + cur + [nxt[0]]
        out = [(x[r] * w[0] + x[r + 1] * w[1]) + (x[r + 2] * w[2] + x[r + 3] * w[3]) + cb
               for r in range(SUBLANES)]
        uc_ref[t * SUBLANES:(t + 1) * SUBLANES, :] = _join_time(out, slab_ref, 2 * (t % ring) + 1)
        prev2, prev1, cur = cur[SUBLANES - 2], cur[SUBLANES - 1], nxt
    while todo:
        project_block()


def _in_proj(x2, seq, g, w, bg, cw, cb):
    t = x2.shape[0]
    tm = TM_PROJ
    nblk = seq // tm
    hb = tm // SUBLANES
    nh = t // SUBLANES
    row = lambda w_: pl.BlockSpec((tm, w_), lambda i: (i, 0))
    prev = pl.BlockSpec((SUBLANES, D_MODEL), lambda i: (jnp.maximum(i * hb - 1, 0), 0))
    nxt = pl.BlockSpec((SUBLANES, D_MODEL), lambda i: (jnp.minimum((i + 1) * hb, nh - 1), 0))
    return pl.pallas_call(
        functools.partial(_in_proj_kernel, nblk),
        grid=(t // tm,),
        in_specs=[row(D_MODEL), prev, nxt, _resident((1, D_MODEL)), _resident((D_MODEL, IN_W)),
                  _resident((1, 2 * D_MODEL)), _resident(cw.shape), _resident(cb.shape)],
        out_specs=[row(D_MODEL), row(D_MODEL), row(Q_W), row(KV_W), row(KV_W), row(2 * D_MODEL)],
        out_shape=[
            jax.ShapeDtypeStruct((t, D_MODEL), F32),
            jax.ShapeDtypeStruct((t, D_MODEL), F32),
            jax.ShapeDtypeStruct((t, Q_W), BF16),
            jax.ShapeDtypeStruct((t, KV_W), BF16),
            jax.ShapeDtypeStruct((t, KV_W), BF16),
            jax.ShapeDtypeStruct((t, 2 * D_MODEL), F32),
        ],
        scratch_shapes=[
            pltpu.VMEM((tm + 2 * SUBLANES, D_MODEL), F32),
            pltpu.VMEM((16, SLAB, LANES), F32),
        ],
        compiler_params=pltpu.CompilerParams(
            dimension_semantics=("arbitrary",), vmem_limit_bytes=VMEM_LIMIT_BYTES),
        name="in_proj",
    )(x2, x2, x2, g, w, bg, cw, cb)


def _split_time(tile, slab_ref, s):
    for j in range(LANE_TILES):
        slab_ref[s, j * SUBLANES:(j + 1) * SUBLANES, :] = tile[:, j * LANES:(j + 1) * LANES]
    return [slab_ref[s, pl.ds(r, LANE_TILES, stride=SUBLANES), :] for r in range(SUBLANES)]


def _join_time(steps, slab_ref, s):
    for r in range(SUBLANES):
        slab_ref[s, pl.ds(r, LANE_TILES, stride=SUBLANES), :] = steps[r]
    return jnp.concatenate(
        [slab_ref[s, j * SUBLANES:(j + 1) * SUBLANES, :] for j in range(LANE_TILES)], axis=1)


def _half_decay_log2(lam):
    return (-0.5 * RGLRU_C * math.log2(math.e)) * (
        jnp.maximum(-lam, 0.0) + jnp.log1p(jnp.exp(-jnp.abs(lam))))


def _gate_terms(pre_r, pre_i, huc, hd2, hba, hbx):
    a = jnp.exp2(hd2 * jnp.tanh(pre_r + hba) + hd2)
    y = jnp.maximum(1.0 - a * a, 0.0)
    beta = jnp.where(y > 0.0, y * lax.rsqrt(y), 0.0)
    return a, beta * (jnp.tanh(pre_i + hbx) * huc + huc)


def _gelu_tanh(g):
    c = math.sqrt(2.0 / math.pi)
    half = 0.5 * g
    return half * jnp.tanh(g * ((c * 0.044715) * (g * g) + c)) + half


def _gate_maps(uc_ref, wg_ref, ra_ref, ix_ref, rows):
    for gi in range(N_GROUPS):
        cols = slice(gi * LRU_GROUP, (gi + 1) * LRU_GROUP)
        pre = jnp.dot(uc_ref[rows, cols].astype(BF16), wg_ref[gi], preferred_element_type=F32)
        ra_ref[rows, cols] = pre[:, :LRU_GROUP]
        ix_ref[rows, cols] = pre[:, LRU_GROUP:]


def _recurrence(reverse, uc_ref, wg_ref, ra_ref, ix_ref, lam_ref, ba_ref, bx_ref, carry_ref, slab_ref,
                emit):
    ts = uc_ref.shape[0]
    nt = ts // SUBLANES
    nsteps = nt // TILES_PER_STEP
    hd2 = _half_decay_log2(lam_ref[...])
    hba = 0.5 * ba_ref[...]
    hbx = 0.5 * bx_ref[...]

    def tile(k, s, h):
        kk = (nt - 1 - k) if reverse else k
        rows = pl.ds(pl.multiple_of(kk * SUBLANES, SUBLANES), SUBLANES)
        a, b = _gate_terms(ra_ref[rows, :], ix_ref[rows, :], uc_ref[rows, :], hd2, hba, hbx)
        a_t = _split_time(a, slab_ref, 3 * s)
        b_t = _split_time(b, slab_ref, 3 * s + 1)
        hs = [None] * SUBLANES
        for r in (range(SUBLANES - 1, -1, -1) if reverse else range(SUBLANES)):
            h = a_t[r] * h + b_t[r]
            hs[r] = h
        emit(rows, _join_time(hs, slab_ref, 3 * s + 2))
        return h

    def group(i, h):
        for s in range(TILES_PER_STEP):
            h = tile(TILES_PER_STEP * i + s, s, h)
        return h

    _gate_maps(uc_ref, wg_ref, ra_ref, ix_ref, slice(0, ts))
    carry_ref[...] = lax.fori_loop(0, nsteps, group, carry_ref[...])


def _lru_fwd_kernel(uc_ref, lam_ref, wg_ref, ba_ref, bx_ref,
                    hf_ref, ra_ref, ix_ref, carry_ref, slab_ref):
    @pl.when(pl.program_id(1) == 0)
    def _():
        carry_ref[...] = jnp.zeros_like(carry_ref)

    def emit(rows, h):
        hf_ref[rows, :] = h

    _recurrence(False, uc_ref, wg_ref, ra_ref, ix_ref, lam_ref, ba_ref, bx_ref, carry_ref, slab_ref, emit)


def _lru_bwd_kernel(uc_ref, hf_ref, g_ref, lam_ref, wg_ref, ba_ref, bx_ref,
                    ya_ref, ra_ref, ix_ref, carry_ref, slab_ref):
    @pl.when(pl.program_id(1) == 0)
    def _():
        carry_ref[...] = jnp.zeros_like(carry_ref)

    def emit(rows, h):
        ya_ref[rows, :] = (hf_ref[rows, :] + h) * g_ref[rows, :]

    _recurrence(True, uc_ref, wg_ref, ra_ref, ix_ref, lam_ref, ba_ref, bx_ref, carry_ref, slab_ref, emit)


_LRU_SCRATCH_TAIL = [
    pltpu.VMEM((TS_LRU, D_MODEL), F32),
    pltpu.VMEM((TS_LRU, D_MODEL), F32),
    pltpu.VMEM((LANE_TILES, LANES), F32),
    pltpu.VMEM((N_SLABS, SLAB, LANES), F32),
]


def _lru_fwd(uc3, lam, wg, ba, bx):
    bsz, s, _ = uc3.shape
    ts = TS_LRU
    blk = pl.BlockSpec((None, ts, D_MODEL), lambda b, j: (b, j, 0))
    return pl.pallas_call(
        _lru_fwd_kernel,
        grid=(bsz, s // ts),
        in_specs=[blk, _resident((1, D_MODEL)), _resident(wg.shape),
                  _resident((1, D_MODEL)), _resident((1, D_MODEL))],
        out_specs=blk,
        out_shape=jax.ShapeDtypeStruct((bsz, s, D_MODEL), F32),
        scratch_shapes=_LRU_SCRATCH_TAIL,
        compiler_params=pltpu.CompilerParams(
            dimension_semantics=("arbitrary", "arbitrary"), vmem_limit_bytes=VMEM_LIMIT_BYTES),
        name="lru_fwd",
    )(uc3, lam, wg, ba, bx)


def _lru_bwd(uc3, hf3, g3, lam, wg, ba, bx):
    bsz, s, _ = uc3.shape
    ts = TS_LRU
    nblk = s // ts
    blk = pl.BlockSpec((None, ts, D_MODEL), lambda b, j: (b, nblk - 1 - j, 0))
    return pl.pallas_call(
        _lru_bwd_kernel,
        grid=(bsz, nblk),
        in_specs=[blk, blk, blk, _resident((1, D_MODEL)), _resident(wg.shape),
                  _resident((1, D_MODEL)), _resident((1, D_MODEL))],
        out_specs=blk,
        out_shape=jax.ShapeDtypeStruct((bsz, s, D_MODEL), F32),
        scratch_shapes=_LRU_SCRATCH_TAIL,
        compiler_params=pltpu.CompilerParams(
            dimension_semantics=("arbitrary", "arbitrary"), vmem_limit_bytes=VMEM_LIMIT_BYTES),
        name="lru_bwd",
    )(uc3, hf3, g3, lam, wg, ba, bx)


_N_BIAS = 3 * N_HEADS + 1
_MASKED_TILE = 3 * N_HEADS


def _attn_kernel(nb, sink_ref, q_ref, kp_ref, km_ref, kn_ref, vp_ref, vm_ref, vn_ref,
                 o_ref, kz_ref, vab_ref, bias_ref, s_ref, m_ref, p_ref):
    tq = q_ref.shape[0]
    tk = tq + 2 * BLOCK
    nsub = tq // BLOCK
    j = pl.program_id(1)

    @pl.when((pl.program_id(0) == 0) & (j == 0))
    def _():
        row = lax.broadcasted_iota(jnp.int32, (BLOCK, BLOCK), 0)
        col = lax.broadcasted_iota(jnp.int32, (BLOCK, BLOCK), 1)
        bias_ref[_MASKED_TILE] = jnp.ones((BLOCK, BLOCK), F32)
        for c in range(3):
            dist = jnp.abs(row + (1 - c) * BLOCK - col)
            absd = dist.astype(F32)
            for h in range(N_HEADS):
                bias_ref[3 * h + c] = jnp.where(dist <= WINDOW, (-_ALIBI_SLOPES[h] * absd) * LOG2_E, 1.0)

    low = lax.broadcasted_iota(jnp.int32, (tk, LANES), 1) < HEAD_DIM
    low_q = lax.broadcasted_iota(jnp.int32, (BLOCK, LANES), 1) < HEAD_DIM
    ones_q = [jnp.where(low_q, 1.0, 0.0).astype(BF16), jnp.where(low_q, 0.0, 1.0).astype(BF16)]
    for t in range(KV_W // LANES):
        cols = slice(t * LANES, (t + 1) * LANES)
        full = jnp.concatenate([r[:, cols] for r in (kp_ref, km_ref, kn_ref)], axis=0).astype(F32)
        swapped = pltpu.roll(full, HEAD_DIM, axis=1)
        kz_ref[4 * t + 0] = jnp.where(low, full, 0.0).astype(BF16)
        kz_ref[4 * t + 1] = jnp.where(low, 0.0, swapped).astype(BF16)
        kz_ref[4 * t + 2] = jnp.where(low, swapped, 0.0).astype(BF16)
        kz_ref[4 * t + 3] = jnp.where(low, 0.0, full).astype(BF16)
        nkb = tk // BLOCK
        for b in range(nkb):
            if b == 0:
                src = vp_ref[:, cols]
            elif b == nkb - 1:
                src = vn_ref[:, cols]
            else:
                src = vm_ref[(b - 1) * BLOCK:b * BLOCK, cols]
            full = src.astype(F32)
            swapped = pltpu.roll(full, HEAD_DIM, axis=1)
            placed = [jnp.where(low_q, full, 0.0), jnp.where(low_q, 0.0, swapped),
                      jnp.where(low_q, swapped, 0.0), jnp.where(low_q, 0.0, full)]
            for i in range(4):
                g, half = 2 * t + i // 2, i % 2
                dst = slice((2 * b + half) * BLOCK, (2 * b + half + 1) * BLOCK)
                vab_ref[g, dst, :LANES] = placed[i].astype(BF16)
                vab_ref[g, dst, LANES:] = ones_q[half]

    def scores(qi, slot):
        q0 = pl.multiple_of(qi * BLOCK, BLOCK)
        blk = j * nsub + qi
        win = pl.ds(q0, 3 * BLOCK)
        sc2 = [None, None]
        for h in range(N_HEADS):
            g, pp, half = h // GROUP, (h % GROUP) // 2, h % 2
            if pp == 0:
                q2 = jnp.concatenate(
                    [q_ref[pl.ds(q0, BLOCK), (2 * g + i) * LANES:(2 * g + i + 1) * LANES] for i in range(2)],
                    axis=0)
                sc2[half] = lax.dot_general(q2, kz_ref[2 * g + half, win, :], (((1,), (1,)), ((), ())),
                                            preferred_element_type=F32)
            sc = sc2[half][pp * BLOCK:(pp + 1) * BLOCK]
            tiles = []
            for c in range(3):
                t = sc[:, c * BLOCK:(c + 1) * BLOCK]
                if c == 1:
                    t = t + bias_ref[3 * h + 1]
                else:
                    edge = (blk == 0) if c == 0 else (blk == nb - 1)
                    b = bias_ref[jnp.where(edge, _MASKED_TILE, 3 * h + c)]
                    t = jnp.where(b > 0.0, NEG_INF, t + b)
                s_ref[slot, h, :, c * BLOCK:(c + 1) * BLOCK] = t
                tiles.append(t)
            mx = jnp.max(jnp.maximum(jnp.maximum(tiles[0], tiles[1]), tiles[2]), axis=-1, keepdims=True)
            m_ref[slot, h] = jnp.broadcast_to(jnp.maximum(mx, sink_ref[h] * LOG2_E), (BLOCK, LANES))

    def finish(qi, slot):
        q0 = pl.multiple_of(qi * BLOCK, BLOCK)
        win2 = pl.ds(pl.multiple_of(qi * (2 * BLOCK), 2 * BLOCK), 6 * BLOCK)
        for h in range(N_HEADS):
            m = m_ref[slot, h]
            for c in range(3):
                cs = slice(c * BLOCK, (c + 1) * BLOCK)
                dst = slice((2 * c + h % 2) * BLOCK, (2 * c + h % 2 + 1) * BLOCK)
                prow = slice(((h // 2) % 2) * BLOCK, ((h // 2) % 2 + 1) * BLOCK)
                p_ref[slot, h // GROUP, prow, dst] = jnp.exp2(s_ref[slot, h, :, cs] - m).astype(BF16)
        for g in range(N_KV_HEADS):
            o2 = jnp.dot(p_ref[slot, g], vab_ref[g, win2, :], preferred_element_type=F32)
            for pp in range(2):
                pr = 2 * g + pp
                h0, h1 = 2 * pr, 2 * pr + 1
                o = o2[pp * BLOCK:(pp + 1) * BLOCK]
                e = jnp.where(low_q, jnp.exp2(sink_ref[h0] * LOG2_E - m_ref[slot, h0]),
                              jnp.exp2(sink_ref[h1] * LOG2_E - m_ref[slot, h1]))
                o_ref[pl.ds(q0, BLOCK), pr * LANES:(pr + 1) * LANES] = (
                    o[:, :LANES] * (1.0 / (o[:, LANES:] + e)))

    scores(0, 0)

    def two_sub_blocks(i, _):
        qa = 2 * i
        scores(qa + 1, 1)
        finish(qa, 0)
        scores(jnp.minimum(qa + 2, nsub - 1), 0)
        finish(qa + 1, 1)
        return 0

    lax.fori_loop(0, nsub // 2, two_sub_blocks, 0)


def _attention(q3, k3, v3, sink):
    bsz, s, _ = q3.shape
    tq = TQ_ATTN
    nq = s // tq
    hb = tq // BLOCK
    nb = s // BLOCK
    tk = tq + 2 * BLOCK
    kv_main = pl.BlockSpec((None, tq, KV_W), lambda b, j: (b, j, 0))
    kv_prev = pl.BlockSpec((None, BLOCK, KV_W), lambda b, j: (b, jnp.maximum(j * hb - 1, 0), 0))
    kv_next = pl.BlockSpec((None, BLOCK, KV_W), lambda b, j: (b, jnp.minimum((j + 1) * hb, nb - 1), 0))
    qo = pl.BlockSpec((None, tq, Q_W), lambda b, j: (b, j, 0))
    return pl.pallas_call(
        functools.partial(_attn_kernel, nb),
        grid=(bsz, nq),
        in_specs=[pl.BlockSpec(memory_space=pltpu.SMEM), qo,
                  kv_prev, kv_main, kv_next, kv_prev, kv_main, kv_next],
        out_specs=qo,
        out_shape=jax.ShapeDtypeStruct((bsz, s, Q_W), F32),
        scratch_shapes=[
            pltpu.VMEM((2 * N_KV_HEADS, tk, LANES), BF16),
            pltpu.VMEM((N_KV_HEADS, 2 * tk, 2 * LANES), BF16),
            pltpu.VMEM((_N_BIAS, BLOCK, BLOCK), F32),
            pltpu.VMEM((2, N_HEADS, BLOCK, 3 * BLOCK), F32),
            pltpu.VMEM((2, N_HEADS, BLOCK, LANES), F32),
            pltpu.VMEM((2, N_KV_HEADS, 2 * BLOCK, 6 * BLOCK), BF16),
        ],
        compiler_params=pltpu.CompilerParams(
            dimension_semantics=("arbitrary", "arbitrary"), vmem_limit_bytes=VMEM_LIMIT_BYTES),
        name="attention",
    )(sink, q3, k3, k3, k3, v3, v3, v3)


def _ffn_kernel(x_ref, ya_ref, yb_ref, gate_ref, wo_ref, g2_ref, wi_ref, wd_ref, g3_ref, o_ref):
    merged = gate_ref[:, :D_MODEL] * ya_ref[...] + gate_ref[:, D_MODEL:] * yb_ref[...]
    x1 = x_ref[...] + jnp.dot(merged.astype(BF16), wo_ref[...], preferred_element_type=F32)
    xn = _rmsnorm(x1, g2_ref[...]).astype(BF16)
    gate = jnp.dot(xn, wi_ref[:, :D_FF], preferred_element_type=F32)
    up = jnp.dot(xn, wi_ref[:, D_FF:], preferred_element_type=F32)
    act = (gate * _sigmoid(gate) * up).astype(BF16)
    x2 = x1 + jnp.dot(act, wd_ref[...], preferred_element_type=F32)
    o_ref[...] = _rmsnorm(x2, g3_ref[...])


def _ffn(x2, ya, yb, gates, wo, g2, wi, wd, g3):
    t = x2.shape[0]
    tm = TM_FFN
    row = lambda w_: pl.BlockSpec((tm, w_), lambda i: (i, 0))
    return pl.pallas_call(
        _ffn_kernel,
        grid=(t // tm,),
        in_specs=[row(D_MODEL), row(D_MODEL), row(D_MODEL), row(2 * D_MODEL),
                  _resident((D_MODEL, D_MODEL)), _resident((1, D_MODEL)),
                  _resident((D_MODEL, 2 * D_FF)), _resident((D_FF, D_MODEL)), _resident((1, D_MODEL))],
        out_specs=row(D_MODEL),
        out_shape=jax.ShapeDtypeStruct((t, D_MODEL), F32),
        compiler_params=pltpu.CompilerParams(
            dimension_semantics=("arbitrary",), vmem_limit_bytes=VMEM_LIMIT_BYTES),
        name="merge_ffn",
    )(x2, ya, yb, gates, wo, g2, wi, wd, g3)


def _gate_weights(wa, wx):
    def blockdiag(w):
        w = w.reshape(N_GROUPS, LRU_GROUP // LRU_BLOCK, LRU_BLOCK, LRU_BLOCK)
        eye = jnp.eye(LRU_GROUP // LRU_BLOCK, dtype=w.dtype)
        return jnp.einsum("ghij,hk->ghikj", w, eye).reshape(N_GROUPS, LRU_GROUP, LRU_GROUP)
    return jnp.concatenate([blockdiag(wa), blockdiag(wx)], axis=-1).astype(BF16)


def kernel(x, norm_mix_g, w_in, b_gate, conv_w, conv_b, lru_lambda, lru_wa, lru_ba, lru_wx, lru_bx,
           attn_sink, w_out, norm_ffn_g, w_ffn_in, w_ffn_out, norm_final_g):
    bsz, s, d = x.shape
    depth = w_in.shape[0]
    assert depth == 1, "the merge/ffn kernel applies the final RMSNorm, so it must be the last layer"
    t = bsz * s
    row = lambda a: a.reshape(1, -1)
    time_major = lambda a: a.reshape(a.shape[:-1] + (LANE_TILES, LANES))
    x2 = x.reshape(t, d)
    for l in range(depth):
        huc, gelu_g, q, k, v, merge_gates = _in_proj(
            x2, s, row(norm_mix_g[l]), w_in[l].astype(BF16), row(b_gate[l]),
            time_major(0.5 * conv_w[l]), time_major(0.5 * conv_b[l]))
        seq = lambda a: a.reshape(bsz, s, a.shape[-1])
        lru = lambda dr: (row(lru_lambda[l, dr]), _gate_weights(lru_wa[l, dr], lru_wx[l, dr]),
                          row(lru_ba[l, dr]), row(lru_bx[l, dr]))
        h_fwd = _lru_fwd(seq(huc), *lru(0))
        y_a = _lru_bwd(seq(huc), h_fwd, seq(gelu_g), *lru(1))
        y_b = _attention(seq(q), seq(k), seq(v), attn_sink[l])
        x2 = _ffn(x2, y_a.reshape(t, d), y_b.reshape(t, d), merge_gates,
                  w_out[l].astype(BF16), row(norm_ffn_g[l]), w_ffn_in[l].astype(BF16),
                  w_ffn_out[l].astype(BF16), row(norm_final_g))
    return x2.reshape(bsz, s, d)
```

```python
import functools
import math

import numpy as np
import jax
import jax.numpy as jnp
from jax import lax
from jax.experimental import pallas as pl
from jax.experimental.pallas import tpu as pltpu

F32 = jnp.float32
BF16 = jnp.bfloat16

D_MODEL = 1024
LRU_HEADS = 16
LRU_BLOCK = D_MODEL // LRU_HEADS
CONV_WIDTH = 4
CONV_LEFT = 2
RGLRU_C = 8.0
N_HEADS = 16
N_KV_HEADS = 4
HEAD_DIM = 64
GROUP = N_HEADS // N_KV_HEADS
WINDOW = 128
BLOCK = 128
D_FF = 2816
Q_W = N_HEADS * HEAD_DIM
KV_W = N_KV_HEADS * HEAD_DIM
IN_W = 2 * D_MODEL + Q_W + 2 * KV_W + 2 * D_MODEL
EPS = 1e-6
NEG_INF = -1e30

LANES = 128
SUBLANES = 8
VMEM_LIMIT_BYTES = 56 * 1024 * 1024

TM_PROJ = 512
TS_LRU = 512
TQ_ATTN = 1024
TM_FFN = 512
LRU_GROUP = 256
N_GROUPS = D_MODEL // LRU_GROUP
LANE_TILES = D_MODEL // LANES
assert LANE_TILES == SUBLANES
SLAB = SUBLANES * LANE_TILES
TILES_PER_STEP = 16
N_SLABS = 3 * TILES_PER_STEP

_ALIBI_SLOPES = [float(v) for v in np.exp2(
    -8.0 * (np.arange(N_HEADS, dtype=np.float32) + 1.0) / N_HEADS).astype(np.float32)]
LOG2_E = math.log2(math.e)
Q_SCALE = HEAD_DIM ** -0.5 * LOG2_E


def _sigmoid(x):
    return 0.5 * jnp.tanh(0.5 * x) + 0.5


def _rmsnorm(x, g):
    return x * lax.rsqrt(jnp.mean(x * x, axis=-1, keepdims=True) + EPS) * g


def _resident(shape):
    nd = len(shape)
    return pl.BlockSpec(shape, lambda *_: (0,) * nd, pipeline_mode=pl.Buffered(1))


def _in_proj_kernel(nblk, x_ref, xp_ref, xnx_ref, g_ref, w_ref, bg_ref, cw_ref, cb_ref,
                    uc_ref, gg_ref, q_ref, k_ref, v_ref, gate_ref, uext_ref, slab_ref):
    tm = x_ref.shape[0]
    nt = tm // SUBLANES
    jb = pl.program_id(0) % nblk
    gn = g_ref[...]
    xn = _rmsnorm(x_ref[...], gn).astype(BF16)
    halo = jnp.concatenate([xp_ref[...], xnx_ref[...]], axis=0)
    xh = _rmsnorm(halo, gn).astype(BF16)

    def proj(lo, hi):
        return jnp.dot(xn, w_ref[:, lo:hi], preferred_element_type=F32)

    ue = jnp.dot(jnp.concatenate([xn, xh], axis=0), w_ref[:, :D_MODEL], preferred_element_type=F32)
    uext_ref[SUBLANES:SUBLANES + tm, :] = ue[:tm]
    uext_ref[0:SUBLANES, :] = jnp.where(jb > 0, ue[tm:tm + SUBLANES], 0.0)
    uext_ref[SUBLANES + tm:, :] = jnp.where(jb < nblk - 1, ue[tm + SUBLANES:], 0.0)

    o_g = D_MODEL
    o_q = o_g + D_MODEL
    o_k = o_q + Q_W
    o_v = o_k + KV_W
    o_z = o_v + KV_W
    pw = 2 * LANES
    todo = []
    for dst, lo, width, post in (
            (gate_ref, o_z, 2 * D_MODEL, lambda v, c: _sigmoid(v + bg_ref[:, c:c + pw])),
            (gg_ref, o_g, D_MODEL, lambda v, c: _gelu_tanh(v)),
            (q_ref, o_q, Q_W, lambda v, c: (v * Q_SCALE).astype(BF16)),
            (k_ref, o_k, KV_W, lambda v, c: v.astype(BF16)),
            (v_ref, o_v, KV_W, lambda v, c: v.astype(BF16))):
        todo += [(dst, c, lo + c, post) for c in range(0, width, pw)]
    every = -(-nt // len(todo))

    def project_block():
        dst, c, wo, post = todo.pop(0)
        dst[:, c:c + pw] = post(proj(wo, wo + pw), c)

    w = [cw_ref[k] for k in range(CONV_WIDTH)]
    cb = cb_ref[...]
    ring = slab_ref.shape[0] // 2

    def ext_steps(e, s):
        return _split_time(uext_ref[e * SUBLANES:(e + 1) * SUBLANES, :], slab_ref, s)

    tail = ext_steps(0, 0)
    prev2, prev1 = tail[SUBLANES - 2], tail[SUBLANES - 1]
    cur = ext_steps(1, 1)
    for t in range(nt):
        if t % every == 0 and todo:
            project_block()
        nxt = ext_steps(t + 2, 2 * (t % ring))
        x = [prev2, prev1] + cur + [nxt[0]]
        out = [(x[r] * w[0] + x[r + 1] * w[1]) + (x[r + 2] * w[2] + x[r + 3] * w[3]) + cb
               for r in range(SUBLANES)]
        uc_ref[t * SUBLANES:(t + 1) * SUBLANES, :] = _join_time(out, slab_ref, 2 * (t % ring) + 1)
        prev2, prev1, cur = cur[SUBLANES - 2], cur[SUBLANES - 1], nxt
    while todo:
        project_block()


def _in_proj(x2, seq, g, w, bg, cw, cb):
    t = x2.shape[0]
    tm = TM_PROJ
    nblk = seq // tm
    hb = tm // SUBLANES
    nh = t // SUBLANES
    row = lambda w_: pl.BlockSpec((tm, w_), lambda i: (i, 0))
    prev = pl.BlockSpec((SUBLANES, D_MODEL), lambda i: (jnp.maximum(i * hb - 1, 0), 0))
    nxt = pl.BlockSpec((SUBLANES, D_MODEL), lambda i: (jnp.minimum((i + 1) * hb, nh - 1), 0))
    return pl.pallas_call(
        functools.partial(_in_proj_kernel, nblk),
        grid=(t // tm,),
        in_specs=[row(D_MODEL), prev, nxt, _resident((1, D_MODEL)), _resident((D_MODEL, IN_W)),
                  _resident((1, 2 * D_MODEL)), _resident(cw.shape), _resident(cb.shape)],
        out_specs=[row(D_MODEL), row(D_MODEL), row(Q_W), row(KV_W), row(KV_W), row(2 * D_MODEL)],
        out_shape=[
            jax.ShapeDtypeStruct((t, D_MODEL), F32),
            jax.ShapeDtypeStruct((t, D_MODEL), F32),
            jax.ShapeDtypeStruct((t, Q_W), BF16),
            jax.ShapeDtypeStruct((t, KV_W), BF16),
            jax.ShapeDtypeStruct((t, KV_W), BF16),
            jax.ShapeDtypeStruct((t, 2 * D_MODEL), F32),
        ],
        scratch_shapes=[
            pltpu.VMEM((tm + 2 * SUBLANES, D_MODEL), F32),
            pltpu.VMEM((16, SLAB, LANES), F32),
        ],
        compiler_params=pltpu.CompilerParams(
            dimension_semantics=("arbitrary",), vmem_limit_bytes=VMEM_LIMIT_BYTES),
        name="in_proj",
    )(x2, x2, x2, g, w, bg, cw, cb)


def _split_time(tile, slab_ref, s):
    for j in range(LANE_TILES):
        slab_ref[s, j * SUBLANES:(j + 1) * SUBLANES, :] = tile[:, j * LANES:(j + 1) * LANES]
    return [slab_ref[s, pl.ds(r, LANE_TILES, stride=SUBLANES), :] for r in range(SUBLANES)]


def _join_time(steps, slab_ref, s):
    for r in range(SUBLANES):
        slab_ref[s, pl.ds(r, LANE_TILES, stride=SUBLANES), :] = steps[r]
    return jnp.concatenate(
        [slab_ref[s, j * SUBLANES:(j + 1) * SUBLANES, :] for j in range(LANE_TILES)], axis=1)


def _half_decay_log2(lam):
    return (-0.5 * RGLRU_C * math.log2(math.e)) * (
        jnp.maximum(-lam, 0.0) + jnp.log1p(jnp.exp(-jnp.abs(lam))))


def _gate_terms(pre_r, pre_i, huc, hd2, hba, hbx):
    a = jnp.exp2(hd2 * jnp.tanh(pre_r + hba) + hd2)
    y = jnp.maximum(1.0 - a * a, 0.0)
    beta = jnp.where(y > 0.0, y * lax.rsqrt(y), 0.0)
    return a, beta * (jnp.tanh(pre_i + hbx) * huc + huc)


def _gelu_tanh(g):
    c = math.sqrt(2.0 / math.pi)
    half = 0.5 * g
    return half * jnp.tanh(g * ((c * 0.044715) * (g * g) + c)) + half


def _gate_maps(uc_ref, wg_ref, ra_ref, ix_ref, rows):
    for gi in range(N_GROUPS):
        cols = slice(gi * LRU_GROUP, (gi + 1) * LRU_GROUP)
        pre = jnp.dot(uc_ref[rows, cols].astype(BF16), wg_ref[gi], preferred_element_type=F32)
        ra_ref[rows, cols] = pre[:, :LRU_GROUP]
        ix_ref[rows, cols] = pre[:, LRU_GROUP:]


def _recurrence(reverse, uc_ref, wg_ref, ra_ref, ix_ref, lam_ref, ba_ref, bx_ref, carry_ref, slab_ref,
                emit):
    ts = uc_ref.shape[0]
    nt = ts // SUBLANES
    nsteps = nt // TILES_PER_STEP
    hd2 = _half_decay_log2(lam_ref[...])
    hba = 0.5 * ba_ref[...]
    hbx = 0.5 * bx_ref[...]

    def tile(k, s, h):
        kk = (nt - 1 - k) if reverse else k
        rows = pl.ds(pl.multiple_of(kk * SUBLANES, SUBLANES), SUBLANES)
        a, b = _gate_terms(ra_ref[rows, :], ix_ref[rows, :], uc_ref[rows, :], hd2, hba, hbx)
        a_t = _split_time(a, slab_ref, 3 * s)
        b_t = _split_time(b, slab_ref, 3 * s + 1)
        hs = [None] * SUBLANES
        for r in (range(SUBLANES - 1, -1, -1) if reverse else range(SUBLANES)):
            h = a_t[r] * h + b_t[r]
            hs[r] = h
        emit(rows, _join_time(hs, slab_ref, 3 * s + 2))
        return h

    def group(i, h):
        for s in range(TILES_PER_STEP):
            h = tile(TILES_PER_STEP * i + s, s, h)
        return h

    _gate_maps(uc_ref, wg_ref, ra_ref, ix_ref, slice(0, ts))
    carry_ref[...] = lax.fori_loop(0, nsteps, group, carry_ref[...])


def _lru_fwd_kernel(uc_ref, lam_ref, wg_ref, ba_ref, bx_ref,
                    hf_ref, ra_ref, ix_ref, carry_ref, slab_ref):
    @pl.when(pl.program_id(1) == 0)
    def _():
        carry_ref[...] = jnp.zeros_like(carry_ref)

    def emit(rows, h):
        hf_ref[rows, :] = h

    _recurrence(False, uc_ref, wg_ref, ra_ref, ix_ref, lam_ref, ba_ref, bx_ref, carry_ref, slab_ref, emit)


def _lru_bwd_kernel(uc_ref, hf_ref, g_ref, lam_ref, wg_ref, ba_ref, bx_ref,
                    ya_ref, ra_ref, ix_ref, carry_ref, slab_ref):
    @pl.when(pl.program_id(1) == 0)
    def _():
        carry_ref[...] = jnp.zeros_like(carry_ref)

    def emit(rows, h):
        ya_ref[rows, :] = (hf_ref[rows, :] + h) * g_ref[rows, :]

    _recurrence(True, uc_ref, wg_ref, ra_ref, ix_ref, lam_ref, ba_ref, bx_ref, carry_ref, slab_ref, emit)


_LRU_SCRATCH_TAIL = [
    pltpu.VMEM((TS_LRU, D_MODEL), F32),
    pltpu.VMEM((TS_LRU, D_MODEL), F32),
    pltpu.VMEM((LANE_TILES, LANES), F32),
    pltpu.VMEM((N_SLABS, SLAB, LANES), F32),
]


def _lru_fwd(uc3, lam, wg, ba, bx):
    bsz, s, _ = uc3.shape
    ts = TS_LRU
    blk = pl.BlockSpec((None, ts, D_MODEL), lambda b, j: (b, j, 0))
    return pl.pallas_call(
        _lru_fwd_kernel,
        grid=(bsz, s // ts),
        in_specs=[blk, _resident((1, D_MODEL)), _resident(wg.shape),
                  _resident((1, D_MODEL)), _resident((1, D_MODEL))],
        out_specs=blk,
        out_shape=jax.ShapeDtypeStruct((bsz, s, D_MODEL), F32),
        scratch_shapes=_LRU_SCRATCH_TAIL,
        compiler_params=pltpu.CompilerParams(
            dimension_semantics=("arbitrary", "arbitrary"), vmem_limit_bytes=VMEM_LIMIT_BYTES),
        name="lru_fwd",
    )(uc3, lam, wg, ba, bx)


def _lru_bwd(uc3, hf3, g3, lam, wg, ba, bx):
    bsz, s, _ = uc3.shape
    ts = TS_LRU
    nblk = s // ts
    blk = pl.BlockSpec((None, ts, D_MODEL), lambda b, j: (b, nblk - 1 - j, 0))
    return pl.pallas_call(
        _lru_bwd_kernel,
        grid=(bsz, nblk),
        in_specs=[blk, blk, blk, _resident((1, D_MODEL)), _resident(wg.shape),
                  _resident((1, D_MODEL)), _resident((1, D_MODEL))],
        out_specs=blk,
        out_shape=jax.ShapeDtypeStruct((bsz, s, D_MODEL), F32),
        scratch_shapes=_LRU_SCRATCH_TAIL,
        compiler_params=pltpu.CompilerParams(
            dimension_semantics=("arbitrary", "arbitrary"), vmem_limit_bytes=VMEM_LIMIT_BYTES),
        name="lru_bwd",
    )(uc3, hf3, g3, lam, wg, ba, bx)


_N_BIAS = 3 * N_HEADS + 1
_MASKED_TILE = 3 * N_HEADS


def _attn_kernel(nb, sink_ref, q_ref, kp_ref, km_ref, kn_ref, vp_ref, vm_ref, vn_ref,
                 o_ref, kz_ref, vab_ref, bias_ref, s_ref, m_ref, p_ref):
    tq = q_ref.shape[0]
    tk = tq + 2 * BLOCK
    nsub = tq // BLOCK
    j = pl.program_id(1)

    @pl.when((pl.program_id(0) == 0) & (j == 0))
    def _():
        row = lax.broadcasted_iota(jnp.int32, (BLOCK, BLOCK), 0)
        col = lax.broadcasted_iota(jnp.int32, (BLOCK, BLOCK), 1)
        bias_ref[_MASKED_TILE] = jnp.ones((BLOCK, BLOCK), F32)
        for c in range(3):
            dist = jnp.abs(row + (1 - c) * BLOCK - col)
            absd = dist.astype(F32)
            for h in range(N_HEADS):
                bias_ref[3 * h + c] = jnp.where(dist <= WINDOW, (-_ALIBI_SLOPES[h] * absd) * LOG2_E, 1.0)

    low = lax.broadcasted_iota(jnp.int32, (tk, LANES), 1) < HEAD_DIM
    low_q = lax.broadcasted_iota(jnp.int32, (BLOCK, LANES), 1) < HEAD_DIM
    ones_q = [jnp.where(low_q, 1.0, 0.0).astype(BF16), jnp.where(low_q, 0.0, 1.0).astype(BF16)]
    for t in range(KV_W // LANES):
        cols = slice(t * LANES, (t + 1) * LANES)
        full = jnp.concatenate([r[:, cols] for r in (kp_ref, km_ref, kn_ref)], axis=0).astype(F32)
        swapped = pltpu.roll(full, HEAD_DIM, axis=1)
        kz_ref[4 * t + 0] = jnp.where(low, full, 0.0).astype(BF16)
        kz_ref[4 * t + 1] = jnp.where(low, 0.0, swapped).astype(BF16)
        kz_ref[4 * t + 2] = jnp.where(low, swapped, 0.0).astype(BF16)
        kz_ref[4 * t + 3] = jnp.where(low, 0.0, full).astype(BF16)
        nkb = tk // BLOCK
        for b in range(nkb):
            if b == 0:
                src = vp_ref[:, cols]
            elif b == nkb - 1:
                src = vn_ref[:, cols]
            else:
                src = vm_ref[(b - 1) * BLOCK:b * BLOCK, cols]
            full = src.astype(F32)
            swapped = pltpu.roll(full, HEAD_DIM, axis=1)
            placed = [jnp.where(low_q, full, 0.0), jnp.where(low_q, 0.0, swapped),
                      jnp.where(low_q, swapped, 0.0), jnp.where(low_q, 0.0, full)]
            for i in range(4):
                g, half = 2 * t + i // 2, i % 2
                dst = slice((2 * b + half) * BLOCK, (2 * b + half + 1) * BLOCK)
                vab_ref[g, dst, :LANES] = placed[i].astype(BF16)
                vab_ref[g, dst, LANES:] = ones_q[half]

    def scores(qi, slot):
        q0 = pl.multiple_of(qi * BLOCK, BLOCK)
        blk = j * nsub + qi
        win = pl.ds(q0, 3 * BLOCK)
        sc2 = [None, None]
        for h in range(N_HEADS):
            g, pp, half = h // GROUP, (h % GROUP) // 2, h % 2
            if pp == 0:
                q2 = jnp.concatenate(
                    [q_ref[pl.ds(q0, BLOCK), (2 * g + i) * LANES:(2 * g + i + 1) * LANES] for i in range(2)],
                    axis=0)
                sc2[half] = lax.dot_general(q2, kz_ref[2 * g + half, win, :], (((1,), (1,)), ((), ())),
                                            preferred_element_type=F32)
            sc = sc2[half][pp * BLOCK:(pp + 1) * BLOCK]
            tiles = []
            for c in range(3):
                t = sc[:, c * BLOCK:(c + 1) * BLOCK]
                if c == 1:
                    t = t + bias_ref[3 * h + 1]
                else:
                    edge = (blk == 0) if c == 0 else (blk == nb - 1)
                    b = bias_ref[jnp.where(edge, _MASKED_TILE, 3 * h + c)]
                    t = jnp.where(b > 0.0, NEG_INF, t + b)
                s_ref[slot, h, :, c * BLOCK:(c + 1) * BLOCK] = t
                tiles.append(t)
            mx = jnp.max(jnp.maximum(jnp.maximum(tiles[0], tiles[1]), tiles[2]), axis=-1, keepdims=True)
            m_ref[slot, h] = jnp.broadcast_to(jnp.maximum(mx, sink_ref[h] * LOG2_E), (BLOCK, LANES))

    def finish(qi, slot):
        q0 = pl.multiple_of(qi * BLOCK, BLOCK)
        win2 = pl.ds(pl.multiple_of(qi * (2 * BLOCK), 2 * BLOCK), 6 * BLOCK)
        for h in range(N_HEADS):
            m = m_ref[slot, h]
            for c in range(3):
                cs = slice(c * BLOCK, (c + 1) * BLOCK)
                dst = slice((2 * c + h % 2) * BLOCK, (2 * c + h % 2 + 1) * BLOCK)
                prow = slice(((h // 2) % 2) * BLOCK, ((h // 2) % 2 + 1) * BLOCK)
                p_ref[slot, h // GROUP, prow, dst] = jnp.exp2(s_ref[slot, h, :, cs] - m).astype(BF16)
        for g in range(N_KV_HEADS):
            o2 = jnp.dot(p_ref[slot, g], vab_ref[g, win2, :], preferred_element_type=F32)
            for pp in range(2):
                pr = 2 * g + pp
                h0, h1 = 2 * pr, 2 * pr + 1
                o = o2[pp * BLOCK:(pp + 1) * BLOCK]
                e = jnp.where(low_q, jnp.exp2(sink_ref[h0] * LOG2_E - m_ref[slot, h0]),
                              jnp.exp2(sink_ref[h1] * LOG2_E - m_ref[slot, h1]))
                o_ref[pl.ds(q0, BLOCK), pr * LANES:(pr + 1) * LANES] = (
                    o[:, :LANES] * (1.0 / (o[:, LANES:] + e)))

    scores(0, 0)

    def two_sub_blocks(i, _):
        qa = 2 * i
        scores(qa + 1, 1)
        finish(qa, 0)
        scores(jnp.minimum(qa + 2, nsub - 1), 0)
        finish(qa + 1, 1)
        return 0

    lax.fori_loop(0, nsub // 2, two_sub_blocks, 0)


def _attention(q3, k3, v3, sink):
    bsz, s, _ = q3.shape
    tq = TQ_ATTN
    nq = s // tq
    hb = tq // BLOCK
    nb = s // BLOCK
    tk = tq + 2 * BLOCK
    kv_main = pl.BlockSpec((None, tq, KV_W), lambda b, j: (b, j, 0))
    kv_prev = pl.BlockSpec((None, BLOCK, KV_W), lambda b, j: (b, jnp.maximum(j * hb - 1, 0), 0))
    kv_next = pl.BlockSpec((None, BLOCK, KV_W), lambda b, j: (b, jnp.minimum((j + 1) * hb, nb - 1), 0))
    qo = pl.BlockSpec((None, tq, Q_W), lambda b, j: (b, j, 0))
    return pl.pallas_call(
        functools.partial(_attn_kernel, nb),
        grid=(bsz, nq),
        in_specs=[pl.BlockSpec(memory_space=pltpu.SMEM), qo,
                  kv_prev, kv_main, kv_next, kv_prev, kv_main, kv_next],
        out_specs=qo,
        out_shape=jax.ShapeDtypeStruct((bsz, s, Q_W), F32),
        scratch_shapes=[
            pltpu.VMEM((2 * N_KV_HEADS, tk, LANES), BF16),
            pltpu.VMEM((N_KV_HEADS, 2 * tk, 2 * LANES), BF16),
            pltpu.VMEM((_N_BIAS, BLOCK, BLOCK), F32),
            pltpu.VMEM((2, N_HEADS, BLOCK, 3 * BLOCK), F32),
            pltpu.VMEM((2, N_HEADS, BLOCK, LANES), F32),
            pltpu.VMEM((2, N_KV_HEADS, 2 * BLOCK, 6 * BLOCK), BF16),
        ],
        compiler_params=pltpu.CompilerParams(
            dimension_semantics=("arbitrary", "arbitrary"), vmem_limit_bytes=VMEM_LIMIT_BYTES),
        name="attention",
    )(sink, q3, k3, k3, k3, v3, v3, v3)


def _ffn_kernel(x_ref, ya_ref, yb_ref, gate_ref, wo_ref, g2_ref, wi_ref, wd_ref, g3_ref, o_ref):
    merged = gate_ref[:, :D_MODEL] * ya_ref[...] + gate_ref[:, D_MODEL:] * yb_ref[...]
    x1 = x_ref[...] + jnp.dot(merged.astype(BF16), wo_ref[...], preferred_element_type=F32)
    xn = _rmsnorm(x1, g2_ref[...]).astype(BF16)
    gate = jnp.dot(xn, wi_ref[:, :D_FF], preferred_element_type=F32)
    up = jnp.dot(xn, wi_ref[:, D_FF:], preferred_element_type=F32)
    act = ((gate * jnp.tanh(gate) + gate) * up).astype(BF16)
    x2 = x1 + jnp.dot(act, wd_ref[...], preferred_element_type=F32)
    o_ref[...] = _rmsnorm(x2, g3_ref[...])


def _ffn(x2, ya, yb, gates, wo, g2, wi, wd, g3):
    t = x2.shape[0]
    tm = TM_FFN
    row = lambda w_: pl.BlockSpec((tm, w_), lambda i: (i, 0))
    return pl.pallas_call(
        _ffn_kernel,
        grid=(t // tm,),
        in_specs=[row(D_MODEL), row(D_MODEL), row(D_MODEL), row(2 * D_MODEL),
                  _resident((D_MODEL, D_MODEL)), _resident((1, D_MODEL)),
                  _resident((D_MODEL, 2 * D_FF)), _resident((D_FF, D_MODEL)), _resident((1, D_MODEL))],
        out_specs=row(D_MODEL),
        out_shape=jax.ShapeDtypeStruct((t, D_MODEL), F32),
        compiler_params=pltpu.CompilerParams(
            dimension_semantics=("arbitrary",), vmem_limit_bytes=VMEM_LIMIT_BYTES),
        name="merge_ffn",
    )(x2, ya, yb, gates, wo, g2, wi, wd, g3)


def _gate_weights(wa, wx):
    def blockdiag(w):
        w = w.reshape(N_GROUPS, LRU_GROUP // LRU_BLOCK, LRU_BLOCK, LRU_BLOCK)
        eye = jnp.eye(LRU_GROUP // LRU_BLOCK, dtype=w.dtype)
        return jnp.einsum("ghij,hk->ghikj", w, eye).reshape(N_GROUPS, LRU_GROUP, LRU_GROUP)
    return jnp.concatenate([blockdiag(wa), blockdiag(wx)], axis=-1).astype(BF16)


def kernel(x, norm_mix_g, w_in, b_gate, conv_w, conv_b, lru_lambda, lru_wa, lru_ba, lru_wx, lru_bx,
           attn_sink, w_out, norm_ffn_g, w_ffn_in, w_ffn_out, norm_final_g):
    bsz, s, d = x.shape
    depth = w_in.shape[0]
    assert depth == 1, "the merge/ffn kernel applies the final RMSNorm, so it must be the last layer"
    t = bsz * s
    row = lambda a: a.reshape(1, -1)
    time_major = lambda a: a.reshape(a.shape[:-1] + (LANE_TILES, LANES))
    x2 = x.reshape(t, d)
    for l in range(depth):
        huc, gelu_g, q, k, v, merge_gates = _in_proj(
            x2, s, row(norm_mix_g[l]), w_in[l].astype(BF16), row(b_gate[l]),
            time_major(0.5 * conv_w[l]), time_major(0.5 * conv_b[l]))
        seq = lambda a: a.reshape(bsz, s, a.shape[-1])
        lru = lambda dr: (row(lru_lambda[l, dr]), _gate_weights(lru_wa[l, dr], lru_wx[l, dr]),
                          row(lru_ba[l, dr]), row(lru_bx[l, dr]))
        h_fwd = _lru_fwd(seq(huc), *lru(0))
        y_a = _lru_bwd(seq(huc), h_fwd, seq(gelu_g), *lru(1))
        y_b = _attention(seq(q), seq(k), seq(v), attn_sink[l])
        col_scale = jnp.where(jnp.arange(2 * D_FF) < D_FF, 0.5, 1.0).astype(F32)
        x2 = _ffn(x2, y_a.reshape(t, d), y_b.reshape(t, d), merge_gates,
                  w_out[l].astype(BF16), row(norm_ffn_g[l]), (w_ffn_in[l] * col_scale).astype(BF16),
                  w_ffn_out[l].astype(BF16), row(norm_final_g))
    return x2.reshape(bsz, s, d)
```

```python
import functools
import math

import numpy as np
import jax
import jax.numpy as jnp
from jax import lax
from jax.experimental import pallas as pl
from jax.experimental.pallas import tpu as pltpu

F32 = jnp.float32
BF16 = jnp.bfloat16

D_MODEL = 1024
LRU_HEADS = 16
LRU_BLOCK = D_MODEL // LRU_HEADS
CONV_WIDTH = 4
CONV_LEFT = 2
RGLRU_C = 8.0
N_HEADS = 16
N_KV_HEADS = 4
HEAD_DIM = 64
GROUP = N_HEADS // N_KV_HEADS
WINDOW = 128
BLOCK = 128
D_FF = 2816
Q_W = N_HEADS * HEAD_DIM
KV_W = N_KV_HEADS * HEAD_DIM
IN_W = 2 * D_MODEL + Q_W + 2 * KV_W + 2 * D_MODEL
EPS = 1e-6
NEG_INF = -1e30

LANES = 128
SUBLANES = 8
VMEM_LIMIT_BYTES = 56 * 1024 * 1024

TM_PROJ = 512
TS_LRU = 1024
TQ_ATTN = 1024
TM_FFN = 512
LRU_GROUP = 256
N_GROUPS = D_MODEL // LRU_GROUP
LANE_TILES = D_MODEL // LANES
assert LANE_TILES == SUBLANES
SLAB = SUBLANES * LANE_TILES
TILES_PER_STEP = 16
N_SLABS = 3 * TILES_PER_STEP

_ALIBI_SLOPES = [float(v) for v in np.exp2(
    -8.0 * (np.arange(N_HEADS, dtype=np.float32) + 1.0) / N_HEADS).astype(np.float32)]
LOG2_E = math.log2(math.e)
Q_SCALE = HEAD_DIM ** -0.5 * LOG2_E


def _sigmoid(x):
    return 0.5 * jnp.tanh(0.5 * x) + 0.5


def _rmsnorm(x, g):
    return x * lax.rsqrt(jnp.mean(x * x, axis=-1, keepdims=True) + EPS) * g


def _resident(shape):
    nd = len(shape)
    return pl.BlockSpec(shape, lambda *_: (0,) * nd, pipeline_mode=pl.Buffered(1))


def _in_proj_kernel(nblk, x_ref, xp_ref, xnx_ref, g_ref, w_ref, bg_ref, cw_ref, cb_ref,
                    uc_ref, gg_ref, q_ref, k_ref, v_ref, gate_ref, uext_ref, slab_ref):
    tm = x_ref.shape[0]
    nt = tm // SUBLANES
    jb = pl.program_id(0) % nblk
    gn = g_ref[...]
    xn = _rmsnorm(x_ref[...], gn).astype(BF16)
    halo = jnp.concatenate([xp_ref[...], xnx_ref[...]], axis=0)
    xh = _rmsnorm(halo, gn).astype(BF16)

    def proj(lo, hi):
        return jnp.dot(xn, w_ref[:, lo:hi], preferred_element_type=F32)

    ue = jnp.dot(jnp.concatenate([xn, xh], axis=0), w_ref[:, :D_MODEL], preferred_element_type=F32)
    uext_ref[SUBLANES:SUBLANES + tm, :] = ue[:tm]
    uext_ref[0:SUBLANES, :] = jnp.where(jb > 0, ue[tm:tm + SUBLANES], 0.0)
    uext_ref[SUBLANES + tm:, :] = jnp.where(jb < nblk - 1, ue[tm + SUBLANES:], 0.0)

    o_g = D_MODEL
    o_q = o_g + D_MODEL
    o_k = o_q + Q_W
    o_v = o_k + KV_W
    o_z = o_v + KV_W
    pw = 2 * LANES
    todo = []
    for dst, lo, width, post in (
            (gate_ref, o_z, 2 * D_MODEL, lambda v, c: _sigmoid(v + bg_ref[:, c:c + pw])),
            (gg_ref, o_g, D_MODEL, lambda v, c: _gelu_tanh(v)),
            (q_ref, o_q, Q_W, lambda v, c: (v * Q_SCALE).astype(BF16)),
            (k_ref, o_k, KV_W, lambda v, c: v.astype(BF16)),
            (v_ref, o_v, KV_W, lambda v, c: v.astype(BF16))):
        todo += [(dst, c, lo + c, post) for c in range(0, width, pw)]
    every = -(-nt // len(todo))

    def project_block():
        dst, c, wo, post = todo.pop(0)
        dst[:, c:c + pw] = post(proj(wo, wo + pw), c)

    w = [cw_ref[k] for k in range(CONV_WIDTH)]
    cb = cb_ref[...]
    ring = slab_ref.shape[0] // 2

    def ext_steps(e, s):
        return _split_time(uext_ref[e * SUBLANES:(e + 1) * SUBLANES, :], slab_ref, s)

    tail = ext_steps(0, 0)
    prev2, prev1 = tail[SUBLANES - 2], tail[SUBLANES - 1]
    cur = ext_steps(1, 1)
    for t in range(nt):
        if t % every == 0 and todo:
            project_block()
        nxt = ext_steps(t + 2, 2 * (t % ring))
        x = [prev2, prev1] + cur + [nxt[0]]
        out = [(x[r] * w[0] + x[r + 1] * w[1]) + (x[r + 2] * w[2] + x[r + 3] * w[3]) + cb
               for r in range(SUBLANES)]
        uc_ref[t * SUBLANES:(t + 1) * SUBLANES, :] = _join_time(out, slab_ref, 2 * (t % ring) + 1)
        prev2, prev1, cur = cur[SUBLANES - 2], cur[SUBLANES - 1], nxt
    while todo:
        project_block()


def _in_proj(x2, seq, g, w, bg, cw, cb):
    t = x2.shape[0]
    tm = TM_PROJ
    nblk = seq // tm
    hb = tm // SUBLANES
    nh = t // SUBLANES
    row = lambda w_: pl.BlockSpec((tm, w_), lambda i: (i, 0))
    prev = pl.BlockSpec((SUBLANES, D_MODEL), lambda i: (jnp.maximum(i * hb - 1, 0), 0))
    nxt = pl.BlockSpec((SUBLANES, D_MODEL), lambda i: (jnp.minimum((i + 1) * hb, nh - 1), 0))
    return pl.pallas_call(
        functools.partial(_in_proj_kernel, nblk),
        grid=(t // tm,),
        in_specs=[row(D_MODEL), prev, nxt, _resident((1, D_MODEL)), _resident((D_MODEL, IN_W)),
                  _resident((1, 2 * D_MODEL)), _resident(cw.shape), _resident(cb.shape)],
        out_specs=[row(D_MODEL), row(D_MODEL), row(Q_W), row(KV_W), row(KV_W), row(2 * D_MODEL)],
        out_shape=[
            jax.ShapeDtypeStruct((t, D_MODEL), F32),
            jax.ShapeDtypeStruct((t, D_MODEL), F32),
            jax.ShapeDtypeStruct((t, Q_W), BF16),
            jax.ShapeDtypeStruct((t, KV_W), BF16),
            jax.ShapeDtypeStruct((t, KV_W), BF16),
            jax.ShapeDtypeStruct((t, 2 * D_MODEL), F32),
        ],
        scratch_shapes=[
            pltpu.VMEM((tm + 2 * SUBLANES, D_MODEL), F32),
            pltpu.VMEM((16, SLAB, LANES), F32),
        ],
        compiler_params=pltpu.CompilerParams(
            dimension_semantics=("arbitrary",), vmem_limit_bytes=VMEM_LIMIT_BYTES),
        name="in_proj",
    )(x2, x2, x2, g, w, bg, cw, cb)


def _split_time(tile, slab_ref, s):
    for j in range(LANE_TILES):
        slab_ref[s, j * SUBLANES:(j + 1) * SUBLANES, :] = tile[:, j * LANES:(j + 1) * LANES]
    return [slab_ref[s, pl.ds(r, LANE_TILES, stride=SUBLANES), :] for r in range(SUBLANES)]


def _join_time(steps, slab_ref, s):
    for r in range(SUBLANES):
        slab_ref[s, pl.ds(r, LANE_TILES, stride=SUBLANES), :] = steps[r]
    return jnp.concatenate(
        [slab_ref[s, j * SUBLANES:(j + 1) * SUBLANES, :] for j in range(LANE_TILES)], axis=1)


def _half_decay_log2(lam):
    return (-0.5 * RGLRU_C * math.log2(math.e)) * (
        jnp.maximum(-lam, 0.0) + jnp.log1p(jnp.exp(-jnp.abs(lam))))


def _gate_terms(pre_r, pre_i, huc, hd2, hba, hbx):
    a = jnp.exp2(hd2 * jnp.tanh(pre_r + hba) + hd2)
    y = jnp.maximum(1.0 - a * a, 0.0)
    beta = jnp.where(y > 0.0, y * lax.rsqrt(y), 0.0)
    return a, beta * (jnp.tanh(pre_i + hbx) * huc + huc)


def _gelu_tanh(g):
    c = math.sqrt(2.0 / math.pi)
    half = 0.5 * g
    return half * jnp.tanh(g * ((c * 0.044715) * (g * g) + c)) + half


def _gate_maps(uc_ref, wg_ref, ra_ref, ix_ref, rows):
    for gi in range(N_GROUPS):
        cols = slice(gi * LRU_GROUP, (gi + 1) * LRU_GROUP)
        pre = jnp.dot(uc_ref[rows, cols].astype(BF16), wg_ref[gi], preferred_element_type=F32)
        ra_ref[rows, cols] = pre[:, :LRU_GROUP]
        ix_ref[rows, cols] = pre[:, LRU_GROUP:]


def _recurrence(reverse, uc_ref, wg_ref, ra_ref, ix_ref, lam_ref, ba_ref, bx_ref, carry_ref, slab_ref,
                emit):
    ts = uc_ref.shape[0]
    nt = ts // SUBLANES
    nsteps = nt // TILES_PER_STEP
    hd2 = _half_decay_log2(lam_ref[...])
    hba = 0.5 * ba_ref[...]
    hbx = 0.5 * bx_ref[...]

    def tile(k, s, h):
        kk = (nt - 1 - k) if reverse else k
        rows = pl.ds(pl.multiple_of(kk * SUBLANES, SUBLANES), SUBLANES)
        a, b = _gate_terms(ra_ref[rows, :], ix_ref[rows, :], uc_ref[rows, :], hd2, hba, hbx)
        a_t = _split_time(a, slab_ref, 3 * s)
        b_t = _split_time(b, slab_ref, 3 * s + 1)
        hs = [None] * SUBLANES
        for r in (range(SUBLANES - 1, -1, -1) if reverse else range(SUBLANES)):
            h = a_t[r] * h + b_t[r]
            hs[r] = h
        emit(rows, _join_time(hs, slab_ref, 3 * s + 2))
        return h

    def group(i, h):
        for s in range(TILES_PER_STEP):
            h = tile(TILES_PER_STEP * i + s, s, h)
        return h

    _gate_maps(uc_ref, wg_ref, ra_ref, ix_ref, slice(0, ts))
    carry_ref[...] = lax.fori_loop(0, nsteps, group, carry_ref[...])


def _lru_fwd_kernel(uc_ref, lam_ref, wg_ref, ba_ref, bx_ref,
                    hf_ref, ra_ref, ix_ref, carry_ref, slab_ref):
    @pl.when(pl.program_id(1) == 0)
    def _():
        carry_ref[...] = jnp.zeros_like(carry_ref)

    def emit(rows, h):
        hf_ref[rows, :] = h

    _recurrence(False, uc_ref, wg_ref, ra_ref, ix_ref, lam_ref, ba_ref, bx_ref, carry_ref, slab_ref, emit)


def _lru_bwd_kernel(uc_ref, hf_ref, g_ref, lam_ref, wg_ref, ba_ref, bx_ref,
                    ya_ref, ra_ref, ix_ref, carry_ref, slab_ref):
    @pl.when(pl.program_id(1) == 0)
    def _():
        carry_ref[...] = jnp.zeros_like(carry_ref)

    def emit(rows, h):
        ya_ref[rows, :] = (hf_ref[rows, :] + h) * g_ref[rows, :]

    _recurrence(True, uc_ref, wg_ref, ra_ref, ix_ref, lam_ref, ba_ref, bx_ref, carry_ref, slab_ref, emit)


_LRU_SCRATCH_TAIL = [
    pltpu.VMEM((TS_LRU, D_MODEL), F32),
    pltpu.VMEM((TS_LRU, D_MODEL), F32),
    pltpu.VMEM((LANE_TILES, LANES), F32),
    pltpu.VMEM((N_SLABS, SLAB, LANES), F32),
]


def _lru_fwd(uc3, lam, wg, ba, bx):
    bsz, s, _ = uc3.shape
    ts = TS_LRU
    blk = pl.BlockSpec((None, ts, D_MODEL), lambda b, j: (b, j, 0))
    return pl.pallas_call(
        _lru_fwd_kernel,
        grid=(bsz, s // ts),
        in_specs=[blk, _resident((1, D_MODEL)), _resident(wg.shape),
                  _resident((1, D_MODEL)), _resident((1, D_MODEL))],
        out_specs=blk,
        out_shape=jax.ShapeDtypeStruct((bsz, s, D_MODEL), F32),
        scratch_shapes=_LRU_SCRATCH_TAIL,
        compiler_params=pltpu.CompilerParams(
            dimension_semantics=("arbitrary", "arbitrary"), vmem_limit_bytes=VMEM_LIMIT_BYTES),
        name="lru_fwd",
    )(uc3, lam, wg, ba, bx)


def _lru_bwd(uc3, hf3, g3, lam, wg, ba, bx):
    bsz, s, _ = uc3.shape
    ts = TS_LRU
    nblk = s // ts
    blk = pl.BlockSpec((None, ts, D_MODEL), lambda b, j: (b, nblk - 1 - j, 0))
    return pl.pallas_call(
        _lru_bwd_kernel,
        grid=(bsz, nblk),
        in_specs=[blk, blk, blk, _resident((1, D_MODEL)), _resident(wg.shape),
                  _resident((1, D_MODEL)), _resident((1, D_MODEL))],
        out_specs=blk,
        out_shape=jax.ShapeDtypeStruct((bsz, s, D_MODEL), F32),
        scratch_shapes=_LRU_SCRATCH_TAIL,
        compiler_params=pltpu.CompilerParams(
            dimension_semantics=("arbitrary", "arbitrary"), vmem_limit_bytes=VMEM_LIMIT_BYTES),
        name="lru_bwd",
    )(uc3, hf3, g3, lam, wg, ba, bx)


_N_BIAS = 3 * N_HEADS + 1
_MASKED_TILE = 3 * N_HEADS


def _attn_kernel(nb, sink_ref, q_ref, kp_ref, km_ref, kn_ref, vp_ref, vm_ref, vn_ref,
                 o_ref, kz_ref, vab_ref, bias_ref, s_ref, m_ref, p_ref):
    tq = q_ref.shape[0]
    tk = tq + 2 * BLOCK
    nsub = tq // BLOCK
    j = pl.program_id(1)

    @pl.when((pl.program_id(0) == 0) & (j == 0))
    def _():
        row = lax.broadcasted_iota(jnp.int32, (BLOCK, BLOCK), 0)
        col = lax.broadcasted_iota(jnp.int32, (BLOCK, BLOCK), 1)
        bias_ref[_MASKED_TILE] = jnp.ones((BLOCK, BLOCK), F32)
        for c in range(3):
            dist = jnp.abs(row + (1 - c) * BLOCK - col)
            absd = dist.astype(F32)
            for h in range(N_HEADS):
                bias_ref[3 * h + c] = jnp.where(dist <= WINDOW, (-_ALIBI_SLOPES[h] * absd) * LOG2_E, 1.0)

    low = lax.broadcasted_iota(jnp.int32, (tk, LANES), 1) < HEAD_DIM
    low_q = lax.broadcasted_iota(jnp.int32, (BLOCK, LANES), 1) < HEAD_DIM
    ones_q = [jnp.where(low_q, 1.0, 0.0).astype(BF16), jnp.where(low_q, 0.0, 1.0).astype(BF16)]
    for t in range(KV_W // LANES):
        cols = slice(t * LANES, (t + 1) * LANES)
        full = jnp.concatenate([r[:, cols] for r in (kp_ref, km_ref, kn_ref)], axis=0).astype(F32)
        swapped = pltpu.roll(full, HEAD_DIM, axis=1)
        kz_ref[4 * t + 0] = jnp.where(low, full, 0.0).astype(BF16)
        kz_ref[4 * t + 1] = jnp.where(low, 0.0, swapped).astype(BF16)
        kz_ref[4 * t + 2] = jnp.where(low, swapped, 0.0).astype(BF16)
        kz_ref[4 * t + 3] = jnp.where(low, 0.0, full).astype(BF16)
        nkb = tk // BLOCK
        for b in range(nkb):
            if b == 0:
                src = vp_ref[:, cols]
            elif b == nkb - 1:
                src = vn_ref[:, cols]
            else:
                src = vm_ref[(b - 1) * BLOCK:b * BLOCK, cols]
            full = src.astype(F32)
            swapped = pltpu.roll(full, HEAD_DIM, axis=1)
            placed = [jnp.where(low_q, full, 0.0), jnp.where(low_q, 0.0, swapped),
                      jnp.where(low_q, swapped, 0.0), jnp.where(low_q, 0.0, full)]
            for i in range(4):
                g, half = 2 * t + i // 2, i % 2
                dst = slice((2 * b + half) * BLOCK, (2 * b + half + 1) * BLOCK)
                vab_ref[g, dst, :LANES] = placed[i].astype(BF16)
                vab_ref[g, dst, LANES:] = ones_q[half]

    def scores(qi, slot):
        q0 = pl.multiple_of(qi * BLOCK, BLOCK)
        blk = j * nsub + qi
        win = pl.ds(q0, 3 * BLOCK)
        sc2 = [None, None]
        for h in range(N_HEADS):
            g, pp, half = h // GROUP, (h % GROUP) // 2, h % 2
            if pp == 0:
                q2 = jnp.concatenate(
                    [q_ref[pl.ds(q0, BLOCK), (2 * g + i) * LANES:(2 * g + i + 1) * LANES] for i in range(2)],
                    axis=0)
                sc2[half] = lax.dot_general(q2, kz_ref[2 * g + half, win, :], (((1,), (1,)), ((), ())),
                                            preferred_element_type=F32)
            sc = sc2[half][pp * BLOCK:(pp + 1) * BLOCK]
            tiles = []
            for c in range(3):
                t = sc[:, c * BLOCK:(c + 1) * BLOCK]
                if c == 1:
                    t = t + bias_ref[3 * h + 1]
                else:
                    edge = (blk == 0) if c == 0 else (blk == nb - 1)
                    b = bias_ref[jnp.where(edge, _MASKED_TILE, 3 * h + c)]
                    t = jnp.where(b > 0.0, NEG_INF, t + b)
                s_ref[slot, h, :, c * BLOCK:(c + 1) * BLOCK] = t
                tiles.append(t)
            mx = jnp.max(jnp.maximum(jnp.maximum(tiles[0], tiles[1]), tiles[2]), axis=-1, keepdims=True)
            m_ref[slot, h] = jnp.broadcast_to(jnp.maximum(mx, sink_ref[h] * LOG2_E), (BLOCK, LANES))

    def finish(qi, slot):
        q0 = pl.multiple_of(qi * BLOCK, BLOCK)
        win2 = pl.ds(pl.multiple_of(qi * (2 * BLOCK), 2 * BLOCK), 6 * BLOCK)
        for h in range(N_HEADS):
            m = m_ref[slot, h]
            for c in range(3):
                cs = slice(c * BLOCK, (c + 1) * BLOCK)
                dst = slice((2 * c + h % 2) * BLOCK, (2 * c + h % 2 + 1) * BLOCK)
                prow = slice(((h // 2) % 2) * BLOCK, ((h // 2) % 2 + 1) * BLOCK)
                p_ref[slot, h // GROUP, prow, dst] = jnp.exp2(s_ref[slot, h, :, cs] - m).astype(BF16)
        for g in range(N_KV_HEADS):
            o2 = jnp.dot(p_ref[slot, g], vab_ref[g, win2, :], preferred_element_type=F32)
            for pp in range(2):
                pr = 2 * g + pp
                h0, h1 = 2 * pr, 2 * pr + 1
                o = o2[pp * BLOCK:(pp + 1) * BLOCK]
                e = jnp.where(low_q, jnp.exp2(sink_ref[h0] * LOG2_E - m_ref[slot, h0]),
                              jnp.exp2(sink_ref[h1] * LOG2_E - m_ref[slot, h1]))
                o_ref[pl.ds(q0, BLOCK), pr * LANES:(pr + 1) * LANES] = (
                    o[:, :LANES] * (1.0 / (o[:, LANES:] + e)))

    scores(0, 0)

    def two_sub_blocks(i, _):
        qa = 2 * i
        scores(qa + 1, 1)
        finish(qa, 0)
        scores(jnp.minimum(qa + 2, nsub - 1), 0)
        finish(qa + 1, 1)
        return 0

    lax.fori_loop(0, nsub // 2, two_sub_blocks, 0)


def _attention(q3, k3, v3, sink):
    bsz, s, _ = q3.shape
    tq = TQ_ATTN
    nq = s // tq
    hb = tq // BLOCK
    nb = s // BLOCK
    tk = tq + 2 * BLOCK
    kv_main = pl.BlockSpec((None, tq, KV_W), lambda b, j: (b, j, 0))
    kv_prev = pl.BlockSpec((None, BLOCK, KV_W), lambda b, j: (b, jnp.maximum(j * hb - 1, 0), 0))
    kv_next = pl.BlockSpec((None, BLOCK, KV_W), lambda b, j: (b, jnp.minimum((j + 1) * hb, nb - 1), 0))
    qo = pl.BlockSpec((None, tq, Q_W), lambda b, j: (b, j, 0))
    return pl.pallas_call(
        functools.partial(_attn_kernel, nb),
        grid=(bsz, nq),
        in_specs=[pl.BlockSpec(memory_space=pltpu.SMEM), qo,
                  kv_prev, kv_main, kv_next, kv_prev, kv_main, kv_next],
        out_specs=qo,
        out_shape=jax.ShapeDtypeStruct((bsz, s, Q_W), F32),
        scratch_shapes=[
            pltpu.VMEM((2 * N_KV_HEADS, tk, LANES), BF16),
            pltpu.VMEM((N_KV_HEADS, 2 * tk, 2 * LANES), BF16),
            pltpu.VMEM((_N_BIAS, BLOCK, BLOCK), F32),
            pltpu.VMEM((2, N_HEADS, BLOCK, 3 * BLOCK), F32),
            pltpu.VMEM((2, N_HEADS, BLOCK, LANES), F32),
            pltpu.VMEM((2, N_KV_HEADS, 2 * BLOCK, 6 * BLOCK), BF16),
        ],
        compiler_params=pltpu.CompilerParams(
            dimension_semantics=("arbitrary", "arbitrary"), vmem_limit_bytes=VMEM_LIMIT_BYTES),
        name="attention",
    )(sink, q3, k3, k3, k3, v3, v3, v3)


def _ffn_kernel(x_ref, ya_ref, yb_ref, gate_ref, wo_ref, g2_ref, wi_ref, wd_ref, g3_ref, o_ref):
    merged = gate_ref[:, :D_MODEL] * ya_ref[...] + gate_ref[:, D_MODEL:] * yb_ref[...]
    x1 = x_ref[...] + jnp.dot(merged.astype(BF16), wo_ref[...], preferred_element_type=F32)
    xn = _rmsnorm(x1, g2_ref[...]).astype(BF16)
    gate = jnp.dot(xn, wi_ref[:, :D_FF], preferred_element_type=F32)
    up = jnp.dot(xn, wi_ref[:, D_FF:], preferred_element_type=F32)
    act = ((gate * jnp.tanh(gate) + gate) * up).astype(BF16)
    x2 = x1 + jnp.dot(act, wd_ref[...], preferred_element_type=F32)
    o_ref[...] = _rmsnorm(x2, g3_ref[...])


def _ffn(x2, ya, yb, gates, wo, g2, wi, wd, g3):
    t = x2.shape[0]
    tm = TM_FFN
    row = lambda w_: pl.BlockSpec((tm, w_), lambda i: (i, 0))
    return pl.pallas_call(
        _ffn_kernel,
        grid=(t // tm,),
        in_specs=[row(D_MODEL), row(D_MODEL), row(D_MODEL), row(2 * D_MODEL),
                  _resident((D_MODEL, D_MODEL)), _resident((1, D_MODEL)),
                  _resident((D_MODEL, 2 * D_FF)), _resident((D_FF, D_MODEL)), _resident((1, D_MODEL))],
        out_specs=row(D_MODEL),
        out_shape=jax.ShapeDtypeStruct((t, D_MODEL), F32),
        compiler_params=pltpu.CompilerParams(
            dimension_semantics=("arbitrary",), vmem_limit_bytes=VMEM_LIMIT_BYTES),
        name="merge_ffn",
    )(x2, ya, yb, gates, wo, g2, wi, wd, g3)


def _gate_weights(wa, wx):
    def blockdiag(w):
        w = w.reshape(N_GROUPS, LRU_GROUP // LRU_BLOCK, LRU_BLOCK, LRU_BLOCK)
        eye = jnp.eye(LRU_GROUP // LRU_BLOCK, dtype=w.dtype)
        return jnp.einsum("ghij,hk->ghikj", w, eye).reshape(N_GROUPS, LRU_GROUP, LRU_GROUP)
    return jnp.concatenate([blockdiag(wa), blockdiag(wx)], axis=-1).astype(BF16)


def kernel(x, norm_mix_g, w_in, b_gate, conv_w, conv_b, lru_lambda, lru_wa, lru_ba, lru_wx, lru_bx,
           attn_sink, w_out, norm_ffn_g, w_ffn_in, w_ffn_out, norm_final_g):
    bsz, s, d = x.shape
    depth = w_in.shape[0]
    assert depth == 1, "the merge/ffn kernel applies the final RMSNorm, so it must be the last layer"
    t = bsz * s
    row = lambda a: a.reshape(1, -1)
    time_major = lambda a: a.reshape(a.shape[:-1] + (LANE_TILES, LANES))
    x2 = x.reshape(t, d)
    for l in range(depth):
        huc, gelu_g, q, k, v, merge_gates = _in_proj(
            x2, s, row(norm_mix_g[l]), w_in[l].astype(BF16), row(b_gate[l]),
            time_major(0.5 * conv_w[l]), time_major(0.5 * conv_b[l]))
        seq = lambda a: a.reshape(bsz, s, a.shape[-1])
        lru = lambda dr: (row(lru_lambda[l, dr]), _gate_weights(lru_wa[l, dr], lru_wx[l, dr]),
                          row(lru_ba[l, dr]), row(lru_bx[l, dr]))
        h_fwd = _lru_fwd(seq(huc), *lru(0))
        y_a = _lru_bwd(seq(huc), h_fwd, seq(gelu_g), *lru(1))
        y_b = _attention(seq(q), seq(k), seq(v), attn_sink[l])
        col_scale = jnp.where(jnp.arange(2 * D_FF) < D_FF, 0.5, 1.0).astype(F32)
        x2 = _ffn(x2, y_a.reshape(t, d), y_b.reshape(t, d), merge_gates,
                  w_out[l].astype(BF16), row(norm_ffn_g[l]), (w_ffn_in[l] * col_scale).astype(BF16),
                  w_ffn_out[l].astype(BF16), row(norm_final_g))
    return x2.reshape(bsz, s, d)
```

```python
import functools
import math

import numpy as np
import jax
import jax.numpy as jnp
from jax import lax
from jax.experimental import pallas as pl
from jax.experimental.pallas import tpu as pltpu

F32 = jnp.float32
BF16 = jnp.bfloat16

D_MODEL = 1024
LRU_HEADS = 16
LRU_BLOCK = D_MODEL // LRU_HEADS
CONV_WIDTH = 4
CONV_LEFT = 2
RGLRU_C = 8.0
N_HEADS = 16
N_KV_HEADS = 4
HEAD_DIM = 64
GROUP = N_HEADS // N_KV_HEADS
WINDOW = 128
BLOCK = 128
D_FF = 2816
Q_W = N_HEADS * HEAD_DIM
KV_W = N_KV_HEADS * HEAD_DIM
IN_W = 2 * D_MODEL + Q_W + 2 * KV_W + 2 * D_MODEL
EPS = 1e-6
NEG_INF = -1e30

LANES = 128
SUBLANES = 8
VMEM_LIMIT_BYTES = 56 * 1024 * 1024

TM_PROJ = 512
TS_LRU = 1024
TQ_ATTN = 1024
TM_FFN = 512
LRU_GROUP = 256
N_GROUPS = D_MODEL // LRU_GROUP
LANE_TILES = D_MODEL // LANES
assert LANE_TILES == SUBLANES
SLAB = SUBLANES * LANE_TILES
TILES_PER_STEP = 16
N_SLABS = 3 * TILES_PER_STEP

_ALIBI_SLOPES = [float(v) for v in np.exp2(
    -8.0 * (np.arange(N_HEADS, dtype=np.float32) + 1.0) / N_HEADS).astype(np.float32)]
LOG2_E = math.log2(math.e)
Q_SCALE = HEAD_DIM ** -0.5 * LOG2_E


def _sigmoid(x):
    return 0.5 * jnp.tanh(0.5 * x) + 0.5


def _rmsnorm(x, g):
    return x * lax.rsqrt(jnp.mean(x * x, axis=-1, keepdims=True) + EPS) * g


def _resident(shape):
    nd = len(shape)
    return pl.BlockSpec(shape, lambda *_: (0,) * nd, pipeline_mode=pl.Buffered(1))


def _in_proj_kernel(nblk, x_ref, xp_ref, xnx_ref, g_ref, w_ref, bg_ref, cw_ref, cb_ref,
                    uc_ref, q_ref, k_ref, v_ref, coef_ref, uext_ref, slab_ref):
    tm = x_ref.shape[0]
    nt = tm // SUBLANES
    jb = pl.program_id(0) % nblk
    gn = g_ref[...]
    xn = _rmsnorm(x_ref[...], gn).astype(BF16)
    halo = jnp.concatenate([xp_ref[...], xnx_ref[...]], axis=0)
    xh = _rmsnorm(halo, gn).astype(BF16)

    def proj(lo, hi):
        return jnp.dot(xn, w_ref[:, lo:hi], preferred_element_type=F32)

    ue = jnp.dot(jnp.concatenate([xn, xh], axis=0), w_ref[:, :D_MODEL], preferred_element_type=F32)
    uext_ref[SUBLANES:SUBLANES + tm, :] = ue[:tm]
    uext_ref[0:SUBLANES, :] = jnp.where(jb > 0, ue[tm:tm + SUBLANES], 0.0)
    uext_ref[SUBLANES + tm:, :] = jnp.where(jb < nblk - 1, ue[tm + SUBLANES:], 0.0)

    o_g = D_MODEL
    o_q = o_g + D_MODEL
    o_k = o_q + Q_W
    o_v = o_k + KV_W
    o_z = o_v + KV_W
    pw = 2 * LANES
    todo = []

    def coef_a(c):
        gate = _sigmoid(proj(o_z + c, o_z + c + pw) + bg_ref[:, c:c + pw])
        coef_ref[:, c:c + pw] = gate * _gelu_tanh(proj(o_g + c, o_g + c + pw))

    def coef_b(c):
        cz = D_MODEL + c
        coef_ref[:, cz:cz + pw] = _sigmoid(proj(o_z + cz, o_z + cz + pw) + bg_ref[:, cz:cz + pw])

    def plain(dst, lo, post):
        def run(c):
            dst[:, c:c + pw] = post(proj(lo + c, lo + c + pw))
        return run

    for fn, width in ((coef_a, D_MODEL), (coef_b, D_MODEL),
                      (plain(q_ref, o_q, lambda v: (v * Q_SCALE).astype(BF16)), Q_W),
                      (plain(k_ref, o_k, lambda v: v.astype(BF16)), KV_W),
                      (plain(v_ref, o_v, lambda v: v.astype(BF16)), KV_W)):
        todo += [(fn, c) for c in range(0, width, pw)]
    every = -(-nt // len(todo))

    def project_block():
        fn, c = todo.pop(0)
        fn(c)

    w = [cw_ref[k] for k in range(CONV_WIDTH)]
    cb = cb_ref[...]
    ring = slab_ref.shape[0] // 2

    def ext_steps(e, s):
        return _split_time(uext_ref[e * SUBLANES:(e + 1) * SUBLANES, :], slab_ref, s)

    tail = ext_steps(0, 0)
    prev2, prev1 = tail[SUBLANES - 2], tail[SUBLANES - 1]
    cur = ext_steps(1, 1)
    for t in range(nt):
        if t % every == 0 and todo:
            project_block()
        nxt = ext_steps(t + 2, 2 * (t % ring))
        x = [prev2, prev1] + cur + [nxt[0]]
        out = [(x[r] * w[0] + x[r + 1] * w[1]) + (x[r + 2] * w[2] + x[r + 3] * w[3]) + cb
               for r in range(SUBLANES)]
        uc_ref[t * SUBLANES:(t + 1) * SUBLANES, :] = _join_time(out, slab_ref, 2 * (t % ring) + 1)
        prev2, prev1, cur = cur[SUBLANES - 2], cur[SUBLANES - 1], nxt
    while todo:
        project_block()


def _in_proj(x2, seq, g, w, bg, cw, cb):
    t = x2.shape[0]
    tm = TM_PROJ
    nblk = seq // tm
    hb = tm // SUBLANES
    nh = t // SUBLANES
    row = lambda w_: pl.BlockSpec((tm, w_), lambda i: (i, 0))
    prev = pl.BlockSpec((SUBLANES, D_MODEL), lambda i: (jnp.maximum(i * hb - 1, 0), 0))
    nxt = pl.BlockSpec((SUBLANES, D_MODEL), lambda i: (jnp.minimum((i + 1) * hb, nh - 1), 0))
    return pl.pallas_call(
        functools.partial(_in_proj_kernel, nblk),
        grid=(t // tm,),
        in_specs=[row(D_MODEL), prev, nxt, _resident((1, D_MODEL)), _resident((D_MODEL, IN_W)),
                  _resident((1, 2 * D_MODEL)), _resident(cw.shape), _resident(cb.shape)],
        out_specs=[row(D_MODEL), row(Q_W), row(KV_W), row(KV_W), row(2 * D_MODEL)],
        out_shape=[
            jax.ShapeDtypeStruct((t, D_MODEL), F32),
            jax.ShapeDtypeStruct((t, Q_W), BF16),
            jax.ShapeDtypeStruct((t, KV_W), BF16),
            jax.ShapeDtypeStruct((t, KV_W), BF16),
            jax.ShapeDtypeStruct((t, 2 * D_MODEL), F32),
        ],
        scratch_shapes=[
            pltpu.VMEM((tm + 2 * SUBLANES, D_MODEL), F32),
            pltpu.VMEM((16, SLAB, LANES), F32),
        ],
        compiler_params=pltpu.CompilerParams(
            dimension_semantics=("arbitrary",), vmem_limit_bytes=VMEM_LIMIT_BYTES),
        name="in_proj",
    )(x2, x2, x2, g, w, bg, cw, cb)


def _split_time(tile, slab_ref, s):
    for j in range(LANE_TILES):
        slab_ref[s, j * SUBLANES:(j + 1) * SUBLANES, :] = tile[:, j * LANES:(j + 1) * LANES]
    return [slab_ref[s, pl.ds(r, LANE_TILES, stride=SUBLANES), :] for r in range(SUBLANES)]


def _join_time(steps, slab_ref, s):
    for r in range(SUBLANES):
        slab_ref[s, pl.ds(r, LANE_TILES, stride=SUBLANES), :] = steps[r]
    return jnp.concatenate(
        [slab_ref[s, j * SUBLANES:(j + 1) * SUBLANES, :] for j in range(LANE_TILES)], axis=1)


def _half_decay_log2(lam):
    return (-0.5 * RGLRU_C * math.log2(math.e)) * (
        jnp.maximum(-lam, 0.0) + jnp.log1p(jnp.exp(-jnp.abs(lam))))


def _gate_terms(pre_r, pre_i, huc, hd2, hba, hbx):
    a = jnp.exp2(hd2 * jnp.tanh(pre_r + hba) + hd2)
    y = jnp.maximum(1.0 - a * a, 0.0)
    beta = jnp.where(y > 0.0, y * lax.rsqrt(y), 0.0)
    return a, beta * (jnp.tanh(pre_i + hbx) * huc + huc)


def _gelu_tanh(g):
    c = math.sqrt(2.0 / math.pi)
    half = 0.5 * g
    return half * jnp.tanh(g * ((c * 0.044715) * (g * g) + c)) + half


def _gate_maps(uc_ref, wg_ref, ra_ref, ix_ref, rows):
    for gi in range(N_GROUPS):
        cols = slice(gi * LRU_GROUP, (gi + 1) * LRU_GROUP)
        pre = jnp.dot(uc_ref[rows, cols].astype(BF16), wg_ref[gi], preferred_element_type=F32)
        ra_ref[rows, cols] = pre[:, :LRU_GROUP]
        ix_ref[rows, cols] = pre[:, LRU_GROUP:]


def _recurrence(reverse, uc_ref, wg_ref, ra_ref, ix_ref, lam_ref, ba_ref, bx_ref, carry_ref, slab_ref,
                emit):
    ts = uc_ref.shape[0]
    nt = ts // SUBLANES
    nsteps = nt // TILES_PER_STEP
    hd2 = _half_decay_log2(lam_ref[...])
    hba = 0.5 * ba_ref[...]
    hbx = 0.5 * bx_ref[...]

    def tile(k, s, h):
        kk = (nt - 1 - k) if reverse else k
        rows = pl.ds(pl.multiple_of(kk * SUBLANES, SUBLANES), SUBLANES)
        a, b = _gate_terms(ra_ref[rows, :], ix_ref[rows, :], uc_ref[rows, :], hd2, hba, hbx)
        a_t = _split_time(a, slab_ref, 3 * s)
        b_t = _split_time(b, slab_ref, 3 * s + 1)
        hs = [None] * SUBLANES
        for r in (range(SUBLANES - 1, -1, -1) if reverse else range(SUBLANES)):
            h = a_t[r] * h + b_t[r]
            hs[r] = h
        emit(rows, _join_time(hs, slab_ref, 3 * s + 2))
        return h

    def group(i, h):
        for s in range(TILES_PER_STEP):
            h = tile(TILES_PER_STEP * i + s, s, h)
        return h

    _gate_maps(uc_ref, wg_ref, ra_ref, ix_ref, slice(0, ts))
    carry_ref[...] = lax.fori_loop(0, nsteps, group, carry_ref[...])


def _lru_kernel(reverse, uc_ref, lam_ref, wg_ref, ba_ref, bx_ref, h_ref, ra_ref, ix_ref, carry_ref,
                slab_ref):
    @pl.when(pl.program_id(1) == 0)
    def _():
        carry_ref[...] = jnp.zeros_like(carry_ref)

    def emit(rows, h):
        h_ref[rows, :] = h

    _recurrence(reverse, uc_ref, wg_ref, ra_ref, ix_ref, lam_ref, ba_ref, bx_ref, carry_ref, slab_ref,
                emit)


def _lru(reverse, uc3, lam, wg, ba, bx):
    bsz, s, _ = uc3.shape
    ts = TS_LRU
    nblk = s // ts
    blk = pl.BlockSpec((None, ts, D_MODEL), lambda b, j: (b, (nblk - 1 - j) if reverse else j, 0))
    return pl.pallas_call(
        functools.partial(_lru_kernel, reverse),
        grid=(bsz, nblk),
        in_specs=[blk, _resident((1, D_MODEL)), _resident(wg.shape),
                  _resident((1, D_MODEL)), _resident((1, D_MODEL))],
        out_specs=blk,
        out_shape=jax.ShapeDtypeStruct((bsz, s, D_MODEL), F32),
        scratch_shapes=[
            pltpu.VMEM((ts, D_MODEL), F32),
            pltpu.VMEM((ts, D_MODEL), F32),
            pltpu.VMEM((LANE_TILES, LANES), F32),
            pltpu.VMEM((N_SLABS, SLAB, LANES), F32),
        ],
        compiler_params=pltpu.CompilerParams(
            dimension_semantics=("arbitrary", "arbitrary"), vmem_limit_bytes=VMEM_LIMIT_BYTES),
        name="lru_bwd" if reverse else "lru_fwd",
    )(uc3, lam, wg, ba, bx)


_N_BIAS = 3 * N_HEADS + 1
_MASKED_TILE = 3 * N_HEADS


def _attn_kernel(nb, sink_ref, q_ref, kp_ref, km_ref, kn_ref, vp_ref, vm_ref, vn_ref,
                 o_ref, kz_ref, vab_ref, bias_ref, s_ref, m_ref, p_ref):
    tq = q_ref.shape[0]
    tk = tq + 2 * BLOCK
    nsub = tq // BLOCK
    j = pl.program_id(1)

    @pl.when((pl.program_id(0) == 0) & (j == 0))
    def _():
        row = lax.broadcasted_iota(jnp.int32, (BLOCK, BLOCK), 0)
        col = lax.broadcasted_iota(jnp.int32, (BLOCK, BLOCK), 1)
        bias_ref[_MASKED_TILE] = jnp.ones((BLOCK, BLOCK), F32)
        for c in range(3):
            dist = jnp.abs(row + (1 - c) * BLOCK - col)
            absd = dist.astype(F32)
            for h in range(N_HEADS):
                bias_ref[3 * h + c] = jnp.where(dist <= WINDOW, (-_ALIBI_SLOPES[h] * absd) * LOG2_E, 1.0)

    low = lax.broadcasted_iota(jnp.int32, (tk, LANES), 1) < HEAD_DIM
    low_q = lax.broadcasted_iota(jnp.int32, (BLOCK, LANES), 1) < HEAD_DIM
    ones_q = [jnp.where(low_q, 1.0, 0.0).astype(BF16), jnp.where(low_q, 0.0, 1.0).astype(BF16)]
    for t in range(KV_W // LANES):
        cols = slice(t * LANES, (t + 1) * LANES)
        full = jnp.concatenate([r[:, cols] for r in (kp_ref, km_ref, kn_ref)], axis=0).astype(F32)
        swapped = pltpu.roll(full, HEAD_DIM, axis=1)
        kz_ref[4 * t + 0] = jnp.where(low, full, 0.0).astype(BF16)
        kz_ref[4 * t + 1] = jnp.where(low, 0.0, swapped).astype(BF16)
        kz_ref[4 * t + 2] = jnp.where(low, swapped, 0.0).astype(BF16)
        kz_ref[4 * t + 3] = jnp.where(low, 0.0, full).astype(BF16)
        nkb = tk // BLOCK
        for b in range(nkb):
            if b == 0:
                src = vp_ref[:, cols]
            elif b == nkb - 1:
                src = vn_ref[:, cols]
            else:
                src = vm_ref[(b - 1) * BLOCK:b * BLOCK, cols]
            full = src.astype(F32)
            swapped = pltpu.roll(full, HEAD_DIM, axis=1)
            placed = [jnp.where(low_q, full, 0.0), jnp.where(low_q, 0.0, swapped),
                      jnp.where(low_q, swapped, 0.0), jnp.where(low_q, 0.0, full)]
            for i in range(4):
                g, half = 2 * t + i // 2, i % 2
                dst = slice((2 * b + half) * BLOCK, (2 * b + half + 1) * BLOCK)
                vab_ref[g, dst, :LANES] = placed[i].astype(BF16)
                vab_ref[g, dst, LANES:] = ones_q[half]

    def scores(qi, slot):
        q0 = pl.multiple_of(qi * BLOCK, BLOCK)
        blk = j * nsub + qi
        win = pl.ds(q0, 3 * BLOCK)
        sc2 = [None, None]
        for h in range(N_HEADS):
            g, pp, half = h // GROUP, (h % GROUP) // 2, h % 2
            if pp == 0:
                q2 = jnp.concatenate(
                    [q_ref[pl.ds(q0, BLOCK), (2 * g + i) * LANES:(2 * g + i + 1) * LANES] for i in range(2)],
                    axis=0)
                sc2[half] = lax.dot_general(q2, kz_ref[2 * g + half, win, :], (((1,), (1,)), ((), ())),
                                            preferred_element_type=F32)
            sc = sc2[half][pp * BLOCK:(pp + 1) * BLOCK]
            tiles = []
            for c in range(3):
                t = sc[:, c * BLOCK:(c + 1) * BLOCK]
                if c == 1:
                    t = t + bias_ref[3 * h + 1]
                else:
                    edge = (blk == 0) if c == 0 else (blk == nb - 1)
                    b = bias_ref[jnp.where(edge, _MASKED_TILE, 3 * h + c)]
                    t = jnp.where(b > 0.0, NEG_INF, t + b)
                s_ref[slot, h, :, c * BLOCK:(c + 1) * BLOCK] = t
                tiles.append(t)
            mx = jnp.max(jnp.maximum(jnp.maximum(tiles[0], tiles[1]), tiles[2]), axis=-1, keepdims=True)
            m_ref[slot, h] = jnp.broadcast_to(jnp.maximum(mx, sink_ref[h] * LOG2_E), (BLOCK, LANES))

    def finish(qi, slot):
        q0 = pl.multiple_of(qi * BLOCK, BLOCK)
        win2 = pl.ds(pl.multiple_of(qi * (2 * BLOCK), 2 * BLOCK), 6 * BLOCK)
        for h in range(N_HEADS):
            m = m_ref[slot, h]
            for c in range(3):
                cs = slice(c * BLOCK, (c + 1) * BLOCK)
                dst = slice((2 * c + h % 2) * BLOCK, (2 * c + h % 2 + 1) * BLOCK)
                prow = slice(((h // 2) % 2) * BLOCK, ((h // 2) % 2 + 1) * BLOCK)
                p_ref[slot, h // GROUP, prow, dst] = jnp.exp2(s_ref[slot, h, :, cs] - m).astype(BF16)
        for g in range(N_KV_HEADS):
            o2 = jnp.dot(p_ref[slot, g], vab_ref[g, win2, :], preferred_element_type=F32)
            for pp in range(2):
                pr = 2 * g + pp
                h0, h1 = 2 * pr, 2 * pr + 1
                o = o2[pp * BLOCK:(pp + 1) * BLOCK]
                e = jnp.where(low_q, jnp.exp2(sink_ref[h0] * LOG2_E - m_ref[slot, h0]),
                              jnp.exp2(sink_ref[h1] * LOG2_E - m_ref[slot, h1]))
                o_ref[pl.ds(q0, BLOCK), pr * LANES:(pr + 1) * LANES] = (
                    o[:, :LANES] * (1.0 / (o[:, LANES:] + e)))

    scores(0, 0)

    def two_sub_blocks(i, _):
        qa = 2 * i
        scores(qa + 1, 1)
        finish(qa, 0)
        scores(jnp.minimum(qa + 2, nsub - 1), 0)
        finish(qa + 1, 1)
        return 0

    lax.fori_loop(0, nsub // 2, two_sub_blocks, 0)


def _attention(q3, k3, v3, sink):
    bsz, s, _ = q3.shape
    tq = TQ_ATTN
    nq = s // tq
    hb = tq // BLOCK
    nb = s // BLOCK
    tk = tq + 2 * BLOCK
    kv_main = pl.BlockSpec((None, tq, KV_W), lambda b, j: (b, j, 0))
    kv_prev = pl.BlockSpec((None, BLOCK, KV_W), lambda b, j: (b, jnp.maximum(j * hb - 1, 0), 0))
    kv_next = pl.BlockSpec((None, BLOCK, KV_W), lambda b, j: (b, jnp.minimum((j + 1) * hb, nb - 1), 0))
    qo = pl.BlockSpec((None, tq, Q_W), lambda b, j: (b, j, 0))
    return pl.pallas_call(
        functools.partial(_attn_kernel, nb),
        grid=(bsz, nq),
        in_specs=[pl.BlockSpec(memory_space=pltpu.SMEM), qo,
                  kv_prev, kv_main, kv_next, kv_prev, kv_main, kv_next],
        out_specs=qo,
        out_shape=jax.ShapeDtypeStruct((bsz, s, Q_W), F32),
        scratch_shapes=[
            pltpu.VMEM((2 * N_KV_HEADS, tk, LANES), BF16),
            pltpu.VMEM((N_KV_HEADS, 2 * tk, 2 * LANES), BF16),
            pltpu.VMEM((_N_BIAS, BLOCK, BLOCK), F32),
            pltpu.VMEM((2, N_HEADS, BLOCK, 3 * BLOCK), F32),
            pltpu.VMEM((2, N_HEADS, BLOCK, LANES), F32),
            pltpu.VMEM((2, N_KV_HEADS, 2 * BLOCK, 6 * BLOCK), BF16),
        ],
        compiler_params=pltpu.CompilerParams(
            dimension_semantics=("arbitrary", "arbitrary"), vmem_limit_bytes=VMEM_LIMIT_BYTES),
        name="attention",
    )(sink, q3, k3, k3, k3, v3, v3, v3)


def _ffn_kernel(x_ref, hf_ref, hb_ref, yb_ref, coef_ref, wo_ref, g2_ref, wi_ref, wd_ref, g3_ref, o_ref):
    merged = coef_ref[:, :D_MODEL] * (hf_ref[...] + hb_ref[...]) + coef_ref[:, D_MODEL:] * yb_ref[...]
    x1 = x_ref[...] + jnp.dot(merged.astype(BF16), wo_ref[...], preferred_element_type=F32)
    xn = _rmsnorm(x1, g2_ref[...]).astype(BF16)
    gate = jnp.dot(xn, wi_ref[:, :D_FF], preferred_element_type=F32)
    up = jnp.dot(xn, wi_ref[:, D_FF:], preferred_element_type=F32)
    act = ((gate * jnp.tanh(gate) + gate) * up).astype(BF16)
    x2 = x1 + jnp.dot(act, wd_ref[...], preferred_element_type=F32)
    o_ref[...] = _rmsnorm(x2, g3_ref[...])


def _ffn(x2, hf, hb, yb, coef, wo, g2, wi, wd, g3):
    t = x2.shape[0]
    tm = TM_FFN
    row = lambda w_: pl.BlockSpec((tm, w_), lambda i: (i, 0))
    return pl.pallas_call(
        _ffn_kernel,
        grid=(t // tm,),
        in_specs=[row(D_MODEL), row(D_MODEL), row(D_MODEL), row(D_MODEL), row(2 * D_MODEL),
                  _resident((D_MODEL, D_MODEL)), _resident((1, D_MODEL)),
                  _resident((D_MODEL, 2 * D_FF)), _resident((D_FF, D_MODEL)), _resident((1, D_MODEL))],
        out_specs=row(D_MODEL),
        out_shape=jax.ShapeDtypeStruct((t, D_MODEL), F32),
        compiler_params=pltpu.CompilerParams(
            dimension_semantics=("arbitrary",), vmem_limit_bytes=VMEM_LIMIT_BYTES),
        name="merge_ffn",
    )(x2, hf, hb, yb, coef, wo, g2, wi, wd, g3)


def _gate_weights(wa, wx):
    def blockdiag(w):
        w = w.reshape(N_GROUPS, LRU_GROUP // LRU_BLOCK, LRU_BLOCK, LRU_BLOCK)
        eye = jnp.eye(LRU_GROUP // LRU_BLOCK, dtype=w.dtype)
        return jnp.einsum("ghij,hk->ghikj", w, eye).reshape(N_GROUPS, LRU_GROUP, LRU_GROUP)
    return jnp.concatenate([blockdiag(wa), blockdiag(wx)], axis=-1).astype(BF16)


def kernel(x, norm_mix_g, w_in, b_gate, conv_w, conv_b, lru_lambda, lru_wa, lru_ba, lru_wx, lru_bx,
           attn_sink, w_out, norm_ffn_g, w_ffn_in, w_ffn_out, norm_final_g):
    bsz, s, d = x.shape
    depth = w_in.shape[0]
    assert depth == 1, "the merge/ffn kernel applies the final RMSNorm, so it must be the last layer"
    t = bsz * s
    row = lambda a: a.reshape(1, -1)
    time_major = lambda a: a.reshape(a.shape[:-1] + (LANE_TILES, LANES))
    x2 = x.reshape(t, d)
    for l in range(depth):
        huc, q, k, v, merge_coef = _in_proj(
            x2, s, row(norm_mix_g[l]), w_in[l].astype(BF16), row(b_gate[l]),
            time_major(0.5 * conv_w[l]), time_major(0.5 * conv_b[l]))
        seq = lambda a: a.reshape(bsz, s, a.shape[-1])
        lru = lambda dr: (row(lru_lambda[l, dr]), _gate_weights(lru_wa[l, dr], lru_wx[l, dr]),
                          row(lru_ba[l, dr]), row(lru_bx[l, dr]))
        h_fwd = _lru(False, seq(huc), *lru(0))
        h_bwd = _lru(True, seq(huc), *lru(1))
        y_b = _attention(seq(q), seq(k), seq(v), attn_sink[l])
        col_scale = jnp.where(jnp.arange(2 * D_FF) < D_FF, 0.5, 1.0).astype(F32)
        x2 = _ffn(x2, h_fwd.reshape(t, d), h_bwd.reshape(t, d), y_b.reshape(t, d), merge_coef,
                  w_out[l].astype(BF16), row(norm_ffn_g[l]), (w_ffn_in[l] * col_scale).astype(BF16),
                  w_ffn_out[l].astype(BF16), row(norm_final_g))
    return x2.reshape(bsz, s, d)
```

```python
import functools
import math

import numpy as np
import jax
import jax.numpy as jnp
from jax import lax
from jax.experimental import pallas as pl
from jax.experimental.pallas import tpu as pltpu

F32 = jnp.float32
BF16 = jnp.bfloat16

D_MODEL = 1024
LRU_HEADS = 16
LRU_BLOCK = D_MODEL // LRU_HEADS
CONV_WIDTH = 4
CONV_LEFT = 2
RGLRU_C = 8.0
N_HEADS = 16
N_KV_HEADS = 4
HEAD_DIM = 64
GROUP = N_HEADS // N_KV_HEADS
WINDOW = 128
BLOCK = 128
D_FF = 2816
Q_W = N_HEADS * HEAD_DIM
KV_W = N_KV_HEADS * HEAD_DIM
IN_W = 2 * D_MODEL + Q_W + 2 * KV_W + 2 * D_MODEL
EPS = 1e-6
NEG_INF = -1e30

LANES = 128
SUBLANES = 8
VMEM_LIMIT_BYTES = 56 * 1024 * 1024

TM_PROJ = 512
TS_LRU = 1024
TQ_ATTN = 1024
TM_FFN = 512
LRU_GROUP = 256
N_GROUPS = D_MODEL // LRU_GROUP
LANE_TILES = D_MODEL // LANES
assert LANE_TILES == SUBLANES
SLAB = SUBLANES * LANE_TILES
TILES_PER_STEP = 16
N_SLABS = 3 * TILES_PER_STEP

_ALIBI_SLOPES = [float(v) for v in np.exp2(
    -8.0 * (np.arange(N_HEADS, dtype=np.float32) + 1.0) / N_HEADS).astype(np.float32)]
LOG2_E = math.log2(math.e)
Q_SCALE = HEAD_DIM ** -0.5 * LOG2_E


def _sigmoid(x):
    return 0.5 * jnp.tanh(0.5 * x) + 0.5


def _rmsnorm(x, g):
    return x * lax.rsqrt(jnp.mean(x * x, axis=-1, keepdims=True) + EPS) * g


def _resident(shape):
    nd = len(shape)
    return pl.BlockSpec(shape, lambda *_: (0,) * nd, pipeline_mode=pl.Buffered(1))


def _in_proj_kernel(nblk, x_ref, xp_ref, xnx_ref, g_ref, w_ref, bg_ref, cw_ref, cb_ref,
                    uc_ref, q_ref, k_ref, v_ref, coef_ref, uext_ref, slab_ref):
    tm = x_ref.shape[0]
    nt = tm // SUBLANES
    jb = pl.program_id(0) % nblk
    gn = g_ref[...]
    xn = _rmsnorm(x_ref[...], gn).astype(BF16)
    halo = jnp.concatenate([xp_ref[...], xnx_ref[...]], axis=0)
    xh = _rmsnorm(halo, gn).astype(BF16)

    def proj(lo, hi):
        return jnp.dot(xn, w_ref[:, lo:hi], preferred_element_type=F32)

    ue = jnp.dot(jnp.concatenate([xn, xh], axis=0), w_ref[:, :D_MODEL], preferred_element_type=F32)
    uext_ref[SUBLANES:SUBLANES + tm, :] = ue[:tm]
    uext_ref[0:SUBLANES, :] = jnp.where(jb > 0, ue[tm:tm + SUBLANES], 0.0)
    uext_ref[SUBLANES + tm:, :] = jnp.where(jb < nblk - 1, ue[tm + SUBLANES:], 0.0)

    o_g = D_MODEL
    o_q = o_g + D_MODEL
    o_k = o_q + Q_W
    o_v = o_k + KV_W
    o_z = o_v + KV_W
    pw = 2 * LANES
    todo = []

    def coef_a(c):
        gate = _sigmoid(proj(o_z + c, o_z + c + pw) + bg_ref[:, c:c + pw])
        coef_ref[:, c:c + pw] = gate * _gelu_tanh(proj(o_g + c, o_g + c + pw))

    def coef_b(c):
        cz = D_MODEL + c
        coef_ref[:, cz:cz + pw] = _sigmoid(proj(o_z + cz, o_z + cz + pw) + bg_ref[:, cz:cz + pw])

    def plain(dst, lo, post):
        def run(c):
            dst[:, c:c + pw] = post(proj(lo + c, lo + c + pw))
        return run

    for fn, width in ((coef_a, D_MODEL), (coef_b, D_MODEL),
                      (plain(q_ref, o_q, lambda v: (v * Q_SCALE).astype(BF16)), Q_W),
                      (plain(k_ref, o_k, lambda v: v.astype(BF16)), KV_W),
                      (plain(v_ref, o_v, lambda v: v.astype(BF16)), KV_W)):
        todo += [(fn, c) for c in range(0, width, pw)]
    every = -(-nt // len(todo))

    def project_block():
        fn, c = todo.pop(0)
        fn(c)

    w = [cw_ref[k] for k in range(CONV_WIDTH)]
    cb = cb_ref[...]
    ring = slab_ref.shape[0] // 2

    def ext_steps(e, s):
        return _split_time(uext_ref[e * SUBLANES:(e + 1) * SUBLANES, :], slab_ref, s)

    tail = ext_steps(0, 0)
    prev2, prev1 = tail[SUBLANES - 2], tail[SUBLANES - 1]
    cur = ext_steps(1, 1)
    for t in range(nt):
        if t % every == 0 and todo:
            project_block()
        nxt = ext_steps(t + 2, 2 * (t % ring))
        x = [prev2, prev1] + cur + [nxt[0]]
        out = [(x[r] * w[0] + x[r + 1] * w[1]) + (x[r + 2] * w[2] + x[r + 3] * w[3]) + cb
               for r in range(SUBLANES)]
        uc_ref[t * SUBLANES:(t + 1) * SUBLANES, :] = _join_time(out, slab_ref, 2 * (t % ring) + 1)
        prev2, prev1, cur = cur[SUBLANES - 2], cur[SUBLANES - 1], nxt
    while todo:
        project_block()


def _in_proj(x2, seq, g, w, bg, cw, cb):
    t = x2.shape[0]
    tm = TM_PROJ
    nblk = seq // tm
    hb = tm // SUBLANES
    nh = t // SUBLANES
    row = lambda w_: pl.BlockSpec((tm, w_), lambda i: (i, 0))
    prev = pl.BlockSpec((SUBLANES, D_MODEL), lambda i: (jnp.maximum(i * hb - 1, 0), 0))
    nxt = pl.BlockSpec((SUBLANES, D_MODEL), lambda i: (jnp.minimum((i + 1) * hb, nh - 1), 0))
    return pl.pallas_call(
        functools.partial(_in_proj_kernel, nblk),
        grid=(t // tm,),
        in_specs=[row(D_MODEL), prev, nxt, _resident((1, D_MODEL)), _resident((D_MODEL, IN_W)),
                  _resident((1, 2 * D_MODEL)), _resident(cw.shape), _resident(cb.shape)],
        out_specs=[row(D_MODEL), row(Q_W), row(KV_W), row(KV_W), row(2 * D_MODEL)],
        out_shape=[
            jax.ShapeDtypeStruct((t, D_MODEL), F32),
            jax.ShapeDtypeStruct((t, Q_W), BF16),
            jax.ShapeDtypeStruct((t, KV_W), BF16),
            jax.ShapeDtypeStruct((t, KV_W), BF16),
            jax.ShapeDtypeStruct((t, 2 * D_MODEL), F32),
        ],
        scratch_shapes=[
            pltpu.VMEM((tm + 2 * SUBLANES, D_MODEL), F32),
            pltpu.VMEM((16, SLAB, LANES), F32),
        ],
        compiler_params=pltpu.CompilerParams(
            dimension_semantics=("arbitrary",), vmem_limit_bytes=VMEM_LIMIT_BYTES),
        name="in_proj",
    )(x2, x2, x2, g, w, bg, cw, cb)


def _split_time(tile, slab_ref, s):
    for j in range(LANE_TILES):
        slab_ref[s, j * SUBLANES:(j + 1) * SUBLANES, :] = tile[:, j * LANES:(j + 1) * LANES]
    return [slab_ref[s, pl.ds(r, LANE_TILES, stride=SUBLANES), :] for r in range(SUBLANES)]


def _join_time(steps, slab_ref, s):
    for r in range(SUBLANES):
        slab_ref[s, pl.ds(r, LANE_TILES, stride=SUBLANES), :] = steps[r]
    return jnp.concatenate(
        [slab_ref[s, j * SUBLANES:(j + 1) * SUBLANES, :] for j in range(LANE_TILES)], axis=1)


def _half_decay_log2(lam):
    return (-0.5 * RGLRU_C * math.log2(math.e)) * (
        jnp.maximum(-lam, 0.0) + jnp.log1p(jnp.exp(-jnp.abs(lam))))


def _gate_terms(pre_r, pre_i, huc, hd2, hba, hbx):
    a = jnp.exp2(hd2 * jnp.tanh(pre_r + hba) + hd2)
    y = jnp.maximum(1.0 - a * a, 0.0)
    beta = jnp.where(y > 0.0, y * lax.rsqrt(y), 0.0)
    return a, beta * (jnp.tanh(pre_i + hbx) * huc + huc)


def _gelu_tanh(g):
    c = math.sqrt(2.0 / math.pi)
    half = 0.5 * g
    return half * jnp.tanh(g * ((c * 0.044715) * (g * g) + c)) + half


def _gate_maps(uc_ref, wg_ref, ra_ref, ix_ref, rows):
    for gi in range(N_GROUPS):
        cols = slice(gi * LRU_GROUP, (gi + 1) * LRU_GROUP)
        pre = jnp.dot(uc_ref[rows, cols].astype(BF16), wg_ref[gi], preferred_element_type=F32)
        ra_ref[rows, cols] = pre[:, :LRU_GROUP]
        ix_ref[rows, cols] = pre[:, LRU_GROUP:]


def _recurrence(reverse, uc_ref, wg_ref, ra_ref, ix_ref, lam_ref, ba_ref, bx_ref, carry_ref, slab_ref,
                emit):
    ts = uc_ref.shape[0]
    nt = ts // SUBLANES
    nsteps = nt // TILES_PER_STEP
    hd2 = _half_decay_log2(lam_ref[...])
    hba = 0.5 * ba_ref[...]
    hbx = 0.5 * bx_ref[...]

    def tile(k, s, h):
        kk = (nt - 1 - k) if reverse else k
        rows = pl.ds(pl.multiple_of(kk * SUBLANES, SUBLANES), SUBLANES)
        a, b = _gate_terms(ra_ref[rows, :], ix_ref[rows, :], uc_ref[rows, :], hd2, hba, hbx)
        a_t = _split_time(a, slab_ref, 3 * s)
        b_t = _split_time(b, slab_ref, 3 * s + 1)
        hs = [None] * SUBLANES
        for r in (range(SUBLANES - 1, -1, -1) if reverse else range(SUBLANES)):
            h = a_t[r] * h + b_t[r]
            hs[r] = h
        emit(rows, _join_time(hs, slab_ref, 3 * s + 2))
        return h

    def group(i, h):
        for s in range(TILES_PER_STEP):
            h = tile(TILES_PER_STEP * i + s, s, h)
        return h

    _gate_maps(uc_ref, wg_ref, ra_ref, ix_ref, slice(0, ts))
    carry_ref[...] = lax.fori_loop(0, nsteps, group, carry_ref[...])


def _lru_kernel(reverse, uc_ref, lam_ref, wg_ref, ba_ref, bx_ref, h_ref, ra_ref, ix_ref, carry_ref,
                slab_ref):
    @pl.when(pl.program_id(1) == 0)
    def _():
        carry_ref[...] = jnp.zeros_like(carry_ref)

    def emit(rows, h):
        h_ref[rows, :] = h

    _recurrence(reverse, uc_ref, wg_ref, ra_ref, ix_ref, lam_ref, ba_ref, bx_ref, carry_ref, slab_ref,
                emit)


def _lru(reverse, uc3, lam, wg, ba, bx):
    bsz, s, _ = uc3.shape
    ts = TS_LRU
    nblk = s // ts
    blk = pl.BlockSpec((None, ts, D_MODEL), lambda b, j: (b, (nblk - 1 - j) if reverse else j, 0))
    return pl.pallas_call(
        functools.partial(_lru_kernel, reverse),
        grid=(bsz, nblk),
        in_specs=[blk, _resident((1, D_MODEL)), _resident(wg.shape),
                  _resident((1, D_MODEL)), _resident((1, D_MODEL))],
        out_specs=blk,
        out_shape=jax.ShapeDtypeStruct((bsz, s, D_MODEL), F32),
        scratch_shapes=[
            pltpu.VMEM((ts, D_MODEL), F32),
            pltpu.VMEM((ts, D_MODEL), F32),
            pltpu.VMEM((LANE_TILES, LANES), F32),
            pltpu.VMEM((N_SLABS, SLAB, LANES), F32),
        ],
        compiler_params=pltpu.CompilerParams(
            dimension_semantics=("arbitrary", "arbitrary"), vmem_limit_bytes=VMEM_LIMIT_BYTES),
        name="lru_bwd" if reverse else "lru_fwd",
    )(uc3, lam, wg, ba, bx)


_N_BIAS = 3 * N_HEADS + 1
_MASKED_TILE = 3 * N_HEADS


def _attn_kernel(nb, sink_ref, q_ref, kp_ref, km_ref, kn_ref, vp_ref, vm_ref, vn_ref,
                 wo_ref, wi_ref, wd_ref, o_ref, wob_ref, wib_ref, wdb_ref,
                 kz_ref, vab_ref, bias_ref, s_ref, m_ref, p_ref):
    tq = q_ref.shape[0]
    tk = tq + 2 * BLOCK
    nsub = tq // BLOCK
    j = pl.program_id(1)

    wob_ref[...] = wo_ref[...].astype(BF16)
    wdb_ref[...] = wd_ref[...].astype(BF16)
    gate_col = lax.broadcasted_iota(jnp.int32, (1, 2 * D_FF), 1) < D_FF
    wib_ref[...] = (wi_ref[...] * jnp.where(gate_col, 0.5, 1.0)).astype(BF16)

    @pl.when((pl.program_id(0) == 0) & (j == 0))
    def _():
        row = lax.broadcasted_iota(jnp.int32, (BLOCK, BLOCK), 0)
        col = lax.broadcasted_iota(jnp.int32, (BLOCK, BLOCK), 1)
        bias_ref[_MASKED_TILE] = jnp.ones((BLOCK, BLOCK), F32)
        for c in range(3):
            dist = jnp.abs(row + (1 - c) * BLOCK - col)
            absd = dist.astype(F32)
            for h in range(N_HEADS):
                bias_ref[3 * h + c] = jnp.where(dist <= WINDOW, (-_ALIBI_SLOPES[h] * absd) * LOG2_E, 1.0)

    low = lax.broadcasted_iota(jnp.int32, (tk, LANES), 1) < HEAD_DIM
    low_q = lax.broadcasted_iota(jnp.int32, (BLOCK, LANES), 1) < HEAD_DIM
    ones_q = [jnp.where(low_q, 1.0, 0.0).astype(BF16), jnp.where(low_q, 0.0, 1.0).astype(BF16)]
    for t in range(KV_W // LANES):
        cols = slice(t * LANES, (t + 1) * LANES)
        full = jnp.concatenate([r[:, cols] for r in (kp_ref, km_ref, kn_ref)], axis=0).astype(F32)
        swapped = pltpu.roll(full, HEAD_DIM, axis=1)
        kz_ref[4 * t + 0] = jnp.where(low, full, 0.0).astype(BF16)
        kz_ref[4 * t + 1] = jnp.where(low, 0.0, swapped).astype(BF16)
        kz_ref[4 * t + 2] = jnp.where(low, swapped, 0.0).astype(BF16)
        kz_ref[4 * t + 3] = jnp.where(low, 0.0, full).astype(BF16)
        nkb = tk // BLOCK
        for b in range(nkb):
            if b == 0:
                src = vp_ref[:, cols]
            elif b == nkb - 1:
                src = vn_ref[:, cols]
            else:
                src = vm_ref[(b - 1) * BLOCK:b * BLOCK, cols]
            full = src.astype(F32)
            swapped = pltpu.roll(full, HEAD_DIM, axis=1)
            placed = [jnp.where(low_q, full, 0.0), jnp.where(low_q, 0.0, swapped),
                      jnp.where(low_q, swapped, 0.0), jnp.where(low_q, 0.0, full)]
            for i in range(4):
                g, half = 2 * t + i // 2, i % 2
                dst = slice((2 * b + half) * BLOCK, (2 * b + half + 1) * BLOCK)
                vab_ref[g, dst, :LANES] = placed[i].astype(BF16)
                vab_ref[g, dst, LANES:] = ones_q[half]

    def scores(qi, slot):
        q0 = pl.multiple_of(qi * BLOCK, BLOCK)
        blk = j * nsub + qi
        win = pl.ds(q0, 3 * BLOCK)
        sc2 = [None, None]
        for h in range(N_HEADS):
            g, pp, half = h // GROUP, (h % GROUP) // 2, h % 2
            if pp == 0:
                q2 = jnp.concatenate(
                    [q_ref[pl.ds(q0, BLOCK), (2 * g + i) * LANES:(2 * g + i + 1) * LANES] for i in range(2)],
                    axis=0)
                sc2[half] = lax.dot_general(q2, kz_ref[2 * g + half, win, :], (((1,), (1,)), ((), ())),
                                            preferred_element_type=F32)
            sc = sc2[half][pp * BLOCK:(pp + 1) * BLOCK]
            tiles = []
            for c in range(3):
                t = sc[:, c * BLOCK:(c + 1) * BLOCK]
                if c == 1:
                    t = t + bias_ref[3 * h + 1]
                else:
                    edge = (blk == 0) if c == 0 else (blk == nb - 1)
                    b = bias_ref[jnp.where(edge, _MASKED_TILE, 3 * h + c)]
                    t = jnp.where(b > 0.0, NEG_INF, t + b)
                s_ref[slot, h, :, c * BLOCK:(c + 1) * BLOCK] = t
                tiles.append(t)
            mx = jnp.max(jnp.maximum(jnp.maximum(tiles[0], tiles[1]), tiles[2]), axis=-1, keepdims=True)
            m_ref[slot, h] = jnp.broadcast_to(jnp.maximum(mx, sink_ref[h] * LOG2_E), (BLOCK, LANES))

    def finish(qi, slot):
        q0 = pl.multiple_of(qi * BLOCK, BLOCK)
        win2 = pl.ds(pl.multiple_of(qi * (2 * BLOCK), 2 * BLOCK), 6 * BLOCK)
        for h in range(N_HEADS):
            m = m_ref[slot, h]
            for c in range(3):
                cs = slice(c * BLOCK, (c + 1) * BLOCK)
                dst = slice((2 * c + h % 2) * BLOCK, (2 * c + h % 2 + 1) * BLOCK)
                prow = slice(((h // 2) % 2) * BLOCK, ((h // 2) % 2 + 1) * BLOCK)
                p_ref[slot, h // GROUP, prow, dst] = jnp.exp2(s_ref[slot, h, :, cs] - m).astype(BF16)
        for g in range(N_KV_HEADS):
            o2 = jnp.dot(p_ref[slot, g], vab_ref[g, win2, :], preferred_element_type=F32)
            for pp in range(2):
                pr = 2 * g + pp
                h0, h1 = 2 * pr, 2 * pr + 1
                o = o2[pp * BLOCK:(pp + 1) * BLOCK]
                e = jnp.where(low_q, jnp.exp2(sink_ref[h0] * LOG2_E - m_ref[slot, h0]),
                              jnp.exp2(sink_ref[h1] * LOG2_E - m_ref[slot, h1]))
                o_ref[pl.ds(q0, BLOCK), pr * LANES:(pr + 1) * LANES] = (
                    o[:, :LANES] * (1.0 / (o[:, LANES:] + e)))

    scores(0, 0)

    def two_sub_blocks(i, _):
        qa = 2 * i
        scores(qa + 1, 1)
        finish(qa, 0)
        scores(jnp.minimum(qa + 2, nsub - 1), 0)
        finish(qa + 1, 1)
        return 0

    lax.fori_loop(0, nsub // 2, two_sub_blocks, 0)


def _attention(q3, k3, v3, sink, ffn_weights):
    bsz, s, _ = q3.shape
    tq = TQ_ATTN
    nq = s // tq
    hb = tq // BLOCK
    nb = s // BLOCK
    tk = tq + 2 * BLOCK
    kv_main = pl.BlockSpec((None, tq, KV_W), lambda b, j: (b, j, 0))
    kv_prev = pl.BlockSpec((None, BLOCK, KV_W), lambda b, j: (b, jnp.maximum(j * hb - 1, 0), 0))
    kv_next = pl.BlockSpec((None, BLOCK, KV_W), lambda b, j: (b, jnp.minimum((j + 1) * hb, nb - 1), 0))
    qo = pl.BlockSpec((None, tq, Q_W), lambda b, j: (b, j, 0))
    nsteps = bsz * nq
    w_slice = lambda w: pl.BlockSpec((w.shape[0] // nsteps, w.shape[1]), lambda b, j: (b * nq + j, 0))
    w_specs = [w_slice(w) for w in ffn_weights]
    return pl.pallas_call(
        functools.partial(_attn_kernel, nb),
        grid=(bsz, nq),
        in_specs=[pl.BlockSpec(memory_space=pltpu.SMEM), qo,
                  kv_prev, kv_main, kv_next, kv_prev, kv_main, kv_next] + w_specs,
        out_specs=[qo] + w_specs,
        out_shape=[jax.ShapeDtypeStruct((bsz, s, Q_W), F32)]
        + [jax.ShapeDtypeStruct(w.shape, BF16) for w in ffn_weights],
        scratch_shapes=[
            pltpu.VMEM((2 * N_KV_HEADS, tk, LANES), BF16),
            pltpu.VMEM((N_KV_HEADS, 2 * tk, 2 * LANES), BF16),
            pltpu.VMEM((_N_BIAS, BLOCK, BLOCK), F32),
            pltpu.VMEM((2, N_HEADS, BLOCK, 3 * BLOCK), F32),
            pltpu.VMEM((2, N_HEADS, BLOCK, LANES), F32),
            pltpu.VMEM((2, N_KV_HEADS, 2 * BLOCK, 6 * BLOCK), BF16),
        ],
        compiler_params=pltpu.CompilerParams(
            dimension_semantics=("arbitrary", "arbitrary"), vmem_limit_bytes=VMEM_LIMIT_BYTES),
        name="attention",
    )(sink, q3, k3, k3, k3, v3, v3, v3, *ffn_weights)


def _ffn_kernel(x_ref, hf_ref, hb_ref, yb_ref, coef_ref, wo_ref, g2_ref, wi_ref, wd_ref, g3_ref, o_ref):
    merged = coef_ref[:, :D_MODEL] * (hf_ref[...] + hb_ref[...]) + coef_ref[:, D_MODEL:] * yb_ref[...]
    x1 = x_ref[...] + jnp.dot(merged.astype(BF16), wo_ref[...], preferred_element_type=F32)
    xn = _rmsnorm(x1, g2_ref[...]).astype(BF16)
    gate = jnp.dot(xn, wi_ref[:, :D_FF], preferred_element_type=F32)
    up = jnp.dot(xn, wi_ref[:, D_FF:], preferred_element_type=F32)
    act = ((gate * jnp.tanh(gate) + gate) * up).astype(BF16)
    x2 = x1 + jnp.dot(act, wd_ref[...], preferred_element_type=F32)
    o_ref[...] = _rmsnorm(x2, g3_ref[...])


def _ffn(x2, hf, hb, yb, coef, wo, g2, wi, wd, g3):
    t = x2.shape[0]
    tm = TM_FFN
    row = lambda w_: pl.BlockSpec((tm, w_), lambda i: (i, 0))
    return pl.pallas_call(
        _ffn_kernel,
        grid=(t // tm,),
        in_specs=[row(D_MODEL), row(D_MODEL), row(D_MODEL), row(D_MODEL), row(2 * D_MODEL),
                  _resident((D_MODEL, D_MODEL)), _resident((1, D_MODEL)),
                  _resident((D_MODEL, 2 * D_FF)), _resident((D_FF, D_MODEL)), _resident((1, D_MODEL))],
        out_specs=row(D_MODEL),
        out_shape=jax.ShapeDtypeStruct((t, D_MODEL), F32),
        compiler_params=pltpu.CompilerParams(
            dimension_semantics=("arbitrary",), vmem_limit_bytes=VMEM_LIMIT_BYTES),
        name="merge_ffn",
    )(x2, hf, hb, yb, coef, wo, g2, wi, wd, g3)


def _gate_weights(wa, wx):
    def blockdiag(w):
        w = w.reshape(N_GROUPS, LRU_GROUP // LRU_BLOCK, LRU_BLOCK, LRU_BLOCK)
        eye = jnp.eye(LRU_GROUP // LRU_BLOCK, dtype=w.dtype)
        return jnp.einsum("ghij,hk->ghikj", w, eye).reshape(N_GROUPS, LRU_GROUP, LRU_GROUP)
    return jnp.concatenate([blockdiag(wa), blockdiag(wx)], axis=-1).astype(BF16)


def kernel(x, norm_mix_g, w_in, b_gate, conv_w, conv_b, lru_lambda, lru_wa, lru_ba, lru_wx, lru_bx,
           attn_sink, w_out, norm_ffn_g, w_ffn_in, w_ffn_out, norm_final_g):
    bsz, s, d = x.shape
    depth = w_in.shape[0]
    assert depth == 1, "the merge/ffn kernel applies the final RMSNorm, so it must be the last layer"
    t = bsz * s
    row = lambda a: a.reshape(1, -1)
    time_major = lambda a: a.reshape(a.shape[:-1] + (LANE_TILES, LANES))
    x2 = x.reshape(t, d)
    for l in range(depth):
        huc, q, k, v, merge_coef = _in_proj(
            x2, s, row(norm_mix_g[l]), w_in[l].astype(BF16), row(b_gate[l]),
            time_major(0.5 * conv_w[l]), time_major(0.5 * conv_b[l]))
        seq = lambda a: a.reshape(bsz, s, a.shape[-1])
        lru = lambda dr: (row(lru_lambda[l, dr]), _gate_weights(lru_wa[l, dr], lru_wx[l, dr]),
                          row(lru_ba[l, dr]), row(lru_bx[l, dr]))
        h_fwd = _lru(False, seq(huc), *lru(0))
        h_bwd = _lru(True, seq(huc), *lru(1))
        y_b, wo_b, wi_b, wd_b = _attention(seq(q), seq(k), seq(v), attn_sink[l],
                                           (w_out[l], w_ffn_in[l], w_ffn_out[l]))
        x2 = _ffn(x2, h_fwd.reshape(t, d), h_bwd.reshape(t, d), y_b.reshape(t, d), merge_coef,
                  wo_b, row(norm_ffn_g[l]), wi_b, wd_b, row(norm_final_g))
    return x2.reshape(bsz, s, d)
```

```python
import functools
import math

import numpy as np
import jax
import jax.numpy as jnp
from jax import lax
from jax.experimental import pallas as pl
from jax.experimental.pallas import tpu as pltpu

F32 = jnp.float32
BF16 = jnp.bfloat16

D_MODEL = 1024
LRU_HEADS = 16
LRU_BLOCK = D_MODEL // LRU_HEADS
CONV_WIDTH = 4
CONV_LEFT = 2
RGLRU_C = 8.0
N_HEADS = 16
N_KV_HEADS = 4
HEAD_DIM = 64
GROUP = N_HEADS // N_KV_HEADS
WINDOW = 128
BLOCK = 128
D_FF = 2816
Q_W = N_HEADS * HEAD_DIM
KV_W = N_KV_HEADS * HEAD_DIM
IN_W = 2 * D_MODEL + Q_W + 2 * KV_W + 2 * D_MODEL
EPS = 1e-6
NEG_INF = -1e30

LANES = 128
SUBLANES = 8
VMEM_LIMIT_BYTES = 56 * 1024 * 1024

TM_PROJ = 512
TS_LRU = 1024
TQ_ATTN = 1024
TM_FFN = 512
LRU_GROUP = 256
N_GROUPS = D_MODEL // LRU_GROUP
LANE_TILES = D_MODEL // LANES
assert LANE_TILES == SUBLANES
SLAB = SUBLANES * LANE_TILES
TILES_PER_STEP = 16
N_SLABS = 3 * TILES_PER_STEP

_ALIBI_SLOPES = [float(v) for v in np.exp2(
    -8.0 * (np.arange(N_HEADS, dtype=np.float32) + 1.0) / N_HEADS).astype(np.float32)]
LOG2_E = math.log2(math.e)
Q_SCALE = HEAD_DIM ** -0.5 * LOG2_E


def _sigmoid(x):
    return 0.5 * jnp.tanh(0.5 * x) + 0.5


def _rmsnorm(x, g):
    return x * lax.rsqrt(jnp.mean(x * x, axis=-1, keepdims=True) + EPS) * g


def _resident(shape):
    nd = len(shape)
    return pl.BlockSpec(shape, lambda *_: (0,) * nd, pipeline_mode=pl.Buffered(1))


def _in_proj_kernel(nblk, x_ref, xp_ref, xnx_ref, g_ref, w_ref, bg_ref, cw_ref, cb_ref,
                    uc_ref, q_ref, k_ref, v_ref, coef_ref, uext_ref, slab_ref):
    tm = x_ref.shape[0]
    nt = tm // SUBLANES
    jb = pl.program_id(0) % nblk
    gn = g_ref[...]
    xn = _rmsnorm(x_ref[...], gn).astype(BF16)
    halo = jnp.concatenate([xp_ref[...], xnx_ref[...]], axis=0)
    xh = _rmsnorm(halo, gn).astype(BF16)

    def proj(lo, hi):
        return jnp.dot(xn, w_ref[:, lo:hi], preferred_element_type=F32)

    ue = jnp.dot(jnp.concatenate([xn, xh], axis=0), w_ref[:, :D_MODEL], preferred_element_type=F32)
    uext_ref[SUBLANES:SUBLANES + tm, :] = ue[:tm]
    uext_ref[0:SUBLANES, :] = jnp.where(jb > 0, ue[tm:tm + SUBLANES], 0.0)
    uext_ref[SUBLANES + tm:, :] = jnp.where(jb < nblk - 1, ue[tm + SUBLANES:], 0.0)

    o_g = D_MODEL
    o_q = o_g + D_MODEL
    o_k = o_q + Q_W
    o_v = o_k + KV_W
    o_z = o_v + KV_W
    pw = 2 * LANES
    todo = []

    def coef_a(c):
        gate = _sigmoid(proj(o_z + c, o_z + c + pw) + bg_ref[:, c:c + pw])
        coef_ref[:, c:c + pw] = gate * _gelu_tanh(proj(o_g + c, o_g + c + pw))

    def coef_b(c):
        cz = D_MODEL + c
        coef_ref[:, cz:cz + pw] = _sigmoid(proj(o_z + cz, o_z + cz + pw) + bg_ref[:, cz:cz + pw])

    def plain(dst, lo, post):
        def run(c):
            dst[:, c:c + pw] = post(proj(lo + c, lo + c + pw))
        return run

    for fn, width in ((coef_a, D_MODEL), (coef_b, D_MODEL),
                      (plain(q_ref, o_q, lambda v: (v * Q_SCALE).astype(BF16)), Q_W),
                      (plain(k_ref, o_k, lambda v: v.astype(BF16)), KV_W),
                      (plain(v_ref, o_v, lambda v: v.astype(BF16)), KV_W)):
        todo += [(fn, c) for c in range(0, width, pw)]
    every = -(-nt // len(todo))

    def project_block():
        fn, c = todo.pop(0)
        fn(c)

    w = [cw_ref[k] for k in range(CONV_WIDTH)]
    cb = cb_ref[...]
    ring = slab_ref.shape[0] // 2

    def ext_steps(e, s):
        return _split_time(uext_ref[e * SUBLANES:(e + 1) * SUBLANES, :], slab_ref, s)

    tail = ext_steps(0, 0)
    prev2, prev1 = tail[SUBLANES - 2], tail[SUBLANES - 1]
    cur = ext_steps(1, 1)
    for t in range(nt):
        if t % every == 0 and todo:
            project_block()
        nxt = ext_steps(t + 2, 2 * (t % ring))
        x = [prev2, prev1] + cur + [nxt[0]]
        out = [(x[r] * w[0] + x[r + 1] * w[1]) + (x[r + 2] * w[2] + x[r + 3] * w[3]) + cb
               for r in range(SUBLANES)]
        uc_ref[t * SUBLANES:(t + 1) * SUBLANES, :] = _join_time(out, slab_ref, 2 * (t % ring) + 1)
        prev2, prev1, cur = cur[SUBLANES - 2], cur[SUBLANES - 1], nxt
    while todo:
        project_block()


def _in_proj(x2, seq, g, w, bg, cw, cb):
    t = x2.shape[0]
    tm = TM_PROJ
    nblk = seq // tm
    hb = tm // SUBLANES
    nh = t // SUBLANES
    row = lambda w_: pl.BlockSpec((tm, w_), lambda i: (i, 0))
    prev = pl.BlockSpec((SUBLANES, D_MODEL), lambda i: (jnp.maximum(i * hb - 1, 0), 0))
    nxt = pl.BlockSpec((SUBLANES, D_MODEL), lambda i: (jnp.minimum((i + 1) * hb, nh - 1), 0))
    return pl.pallas_call(
        functools.partial(_in_proj_kernel, nblk),
        grid=(t // tm,),
        in_specs=[row(D_MODEL), prev, nxt, _resident((1, D_MODEL)), _resident((D_MODEL, IN_W)),
                  _resident((1, 2 * D_MODEL)), _resident(cw.shape), _resident(cb.shape)],
        out_specs=[row(D_MODEL), row(Q_W), row(KV_W), row(KV_W), row(2 * D_MODEL)],
        out_shape=[
            jax.ShapeDtypeStruct((t, D_MODEL), F32),
            jax.ShapeDtypeStruct((t, Q_W), BF16),
            jax.ShapeDtypeStruct((t, KV_W), BF16),
            jax.ShapeDtypeStruct((t, KV_W), BF16),
            jax.ShapeDtypeStruct((t, 2 * D_MODEL), F32),
        ],
        scratch_shapes=[
            pltpu.VMEM((tm + 2 * SUBLANES, D_MODEL), F32),
            pltpu.VMEM((16, SLAB, LANES), F32),
        ],
        compiler_params=pltpu.CompilerParams(
            dimension_semantics=("arbitrary",), vmem_limit_bytes=VMEM_LIMIT_BYTES),
        name="in_proj",
    )(x2, x2, x2, g, w, bg, cw, cb)


def _split_time(tile, slab_ref, s):
    for j in range(LANE_TILES):
        slab_ref[s, j * SUBLANES:(j + 1) * SUBLANES, :] = tile[:, j * LANES:(j + 1) * LANES]
    return [slab_ref[s, pl.ds(r, LANE_TILES, stride=SUBLANES), :] for r in range(SUBLANES)]


def _join_time(steps, slab_ref, s):
    for r in range(SUBLANES):
        slab_ref[s, pl.ds(r, LANE_TILES, stride=SUBLANES), :] = steps[r]
    return jnp.concatenate(
        [slab_ref[s, j * SUBLANES:(j + 1) * SUBLANES, :] for j in range(LANE_TILES)], axis=1)


def _half_decay_log2(lam):
    return (-0.5 * RGLRU_C * math.log2(math.e)) * (
        jnp.maximum(-lam, 0.0) + jnp.log1p(jnp.exp(-jnp.abs(lam))))


def _gate_terms(pre_r, pre_i, huc, hd2, hba, hbx):
    a = jnp.exp2(hd2 * jnp.tanh(pre_r + hba) + hd2)
    y = jnp.maximum(1.0 - a * a, 0.0)
    beta = jnp.where(y > 0.0, y * lax.rsqrt(y), 0.0)
    return a, beta * (jnp.tanh(pre_i + hbx) * huc + huc)


def _gelu_tanh(g):
    c = math.sqrt(2.0 / math.pi)
    half = 0.5 * g
    return half * jnp.tanh(g * ((c * 0.044715) * (g * g) + c)) + half


def _gate_maps(uc_ref, wg_ref, ra_ref, ix_ref, rows):
    for gi in range(N_GROUPS):
        cols = slice(gi * LRU_GROUP, (gi + 1) * LRU_GROUP)
        pre = jnp.dot(uc_ref[rows, cols].astype(BF16), wg_ref[gi], preferred_element_type=F32)
        ra_ref[rows, cols] = pre[:, :LRU_GROUP]
        ix_ref[rows, cols] = pre[:, LRU_GROUP:]


def _recurrence(reverse, uc_ref, wg_ref, ra_ref, ix_ref, lam_ref, ba_ref, bx_ref, carry_ref, slab_ref,
                emit):
    ts = uc_ref.shape[0]
    nt = ts // SUBLANES
    nsteps = nt // TILES_PER_STEP
    hd2 = _half_decay_log2(lam_ref[...])
    hba = 0.5 * ba_ref[...]
    hbx = 0.5 * bx_ref[...]

    def tile(k, s, h):
        kk = (nt - 1 - k) if reverse else k
        rows = pl.ds(pl.multiple_of(kk * SUBLANES, SUBLANES), SUBLANES)
        a, b = _gate_terms(ra_ref[rows, :], ix_ref[rows, :], uc_ref[rows, :], hd2, hba, hbx)
        a_t = _split_time(a, slab_ref, 3 * s)
        b_t = _split_time(b, slab_ref, 3 * s + 1)
        hs = [None] * SUBLANES
        for r in (range(SUBLANES - 1, -1, -1) if reverse else range(SUBLANES)):
            h = a_t[r] * h + b_t[r]
            hs[r] = h
        emit(rows, _join_time(hs, slab_ref, 3 * s + 2))
        return h

    def group(i, h):
        for s in range(TILES_PER_STEP):
            h = tile(TILES_PER_STEP * i + s, s, h)
        return h

    _gate_maps(uc_ref, wg_ref, ra_ref, ix_ref, slice(0, ts))
    carry_ref[...] = lax.fori_loop(0, nsteps, group, carry_ref[...])


def _lru_kernel(reverse, uc_ref, lam_ref, wg_ref, ba_ref, bx_ref, h_ref, ra_ref, ix_ref, carry_ref,
                slab_ref):
    @pl.when(pl.program_id(1) == 0)
    def _():
        carry_ref[...] = jnp.zeros_like(carry_ref)

    def emit(rows, h):
        h_ref[rows, :] = h

    _recurrence(reverse, uc_ref, wg_ref, ra_ref, ix_ref, lam_ref, ba_ref, bx_ref, carry_ref, slab_ref,
                emit)


def _lru(reverse, uc3, lam, wg, ba, bx):
    bsz, s, _ = uc3.shape
    ts = TS_LRU
    nblk = s // ts
    blk = pl.BlockSpec((None, ts, D_MODEL), lambda b, j: (b, (nblk - 1 - j) if reverse else j, 0))
    return pl.pallas_call(
        functools.partial(_lru_kernel, reverse),
        grid=(bsz, nblk),
        in_specs=[blk, _resident((1, D_MODEL)), _resident(wg.shape),
                  _resident((1, D_MODEL)), _resident((1, D_MODEL))],
        out_specs=blk,
        out_shape=jax.ShapeDtypeStruct((bsz, s, D_MODEL), F32),
        scratch_shapes=[
            pltpu.VMEM((ts, D_MODEL), F32),
            pltpu.VMEM((ts, D_MODEL), F32),
            pltpu.VMEM((LANE_TILES, LANES), F32),
            pltpu.VMEM((N_SLABS, SLAB, LANES), F32),
        ],
        compiler_params=pltpu.CompilerParams(
            dimension_semantics=("arbitrary", "arbitrary"), vmem_limit_bytes=VMEM_LIMIT_BYTES),
        name="lru_bwd" if reverse else "lru_fwd",
    )(uc3, lam, wg, ba, bx)


_N_BIAS = 3 * N_HEADS + 1
_MASKED_TILE = 3 * N_HEADS


def _attn_kernel(nb, sink_ref, q_ref, kp_ref, km_ref, kn_ref, vp_ref, vm_ref, vn_ref,
                 wo_ref, wi_ref, wd_ref, o_ref, wob_ref, wib_ref, wdb_ref,
                 kz_ref, vab_ref, bias_ref, s_ref, m_ref, p_ref):
    tq = q_ref.shape[0]
    tk = tq + 2 * BLOCK
    nsub = tq // BLOCK
    j = pl.program_id(1)

    wob_ref[...] = wo_ref[...].astype(BF16)
    wdb_ref[...] = wd_ref[...].astype(BF16)
    gate_col = lax.broadcasted_iota(jnp.int32, (1, 2 * D_FF), 1) < D_FF
    wib_ref[...] = (wi_ref[...] * jnp.where(gate_col, 0.5, 1.0)).astype(BF16)

    @pl.when((pl.program_id(0) == 0) & (j == 0))
    def _():
        row = lax.broadcasted_iota(jnp.int32, (BLOCK, BLOCK), 0)
        col = lax.broadcasted_iota(jnp.int32, (BLOCK, BLOCK), 1)
        bias_ref[_MASKED_TILE] = jnp.ones((BLOCK, BLOCK), F32)
        for c in range(3):
            dist = jnp.abs(row + (1 - c) * BLOCK - col)
            absd = dist.astype(F32)
            for h in range(N_HEADS):
                bias_ref[3 * h + c] = jnp.where(dist <= WINDOW, (-_ALIBI_SLOPES[h] * absd) * LOG2_E, 1.0)

    low = lax.broadcasted_iota(jnp.int32, (tk, LANES), 1) < HEAD_DIM
    low_q = lax.broadcasted_iota(jnp.int32, (BLOCK, LANES), 1) < HEAD_DIM
    ones_q = [jnp.where(low_q, 1.0, 0.0).astype(BF16), jnp.where(low_q, 0.0, 1.0).astype(BF16)]
    for t in range(KV_W // LANES):
        cols = slice(t * LANES, (t + 1) * LANES)
        full = jnp.concatenate([r[:, cols] for r in (kp_ref, km_ref, kn_ref)], axis=0).astype(F32)
        swapped = pltpu.roll(full, HEAD_DIM, axis=1)
        kz_ref[4 * t + 0] = jnp.where(low, full, 0.0).astype(BF16)
        kz_ref[4 * t + 1] = jnp.where(low, 0.0, swapped).astype(BF16)
        kz_ref[4 * t + 2] = jnp.where(low, swapped, 0.0).astype(BF16)
        kz_ref[4 * t + 3] = jnp.where(low, 0.0, full).astype(BF16)
        nkb = tk // BLOCK
        for b in range(nkb):
            if b == 0:
                src = vp_ref[:, cols]
            elif b == nkb - 1:
                src = vn_ref[:, cols]
            else:
                src = vm_ref[(b - 1) * BLOCK:b * BLOCK, cols]
            full = src.astype(F32)
            swapped = pltpu.roll(full, HEAD_DIM, axis=1)
            placed = [jnp.where(low_q, full, 0.0), jnp.where(low_q, 0.0, swapped),
                      jnp.where(low_q, swapped, 0.0), jnp.where(low_q, 0.0, full)]
            for i in range(4):
                g, half = 2 * t + i // 2, i % 2
                dst = slice((2 * b + half) * BLOCK, (2 * b + half + 1) * BLOCK)
                vab_ref[g, dst, :LANES] = placed[i].astype(BF16)
                vab_ref[g, dst, LANES:] = ones_q[half]

    def scores(qi, slot):
        q0 = pl.multiple_of(qi * BLOCK, BLOCK)
        blk = j * nsub + qi
        win = pl.ds(q0, 3 * BLOCK)
        sc2 = [None, None]
        for h in range(N_HEADS):
            g, pp, half = h // GROUP, (h % GROUP) // 2, h % 2
            if pp == 0:
                q2 = jnp.concatenate(
                    [q_ref[pl.ds(q0, BLOCK), (2 * g + i) * LANES:(2 * g + i + 1) * LANES] for i in range(2)],
                    axis=0)
                sc2[half] = lax.dot_general(q2, kz_ref[2 * g + half, win, :], (((1,), (1,)), ((), ())),
                                            preferred_element_type=F32)
            sc = sc2[half][pp * BLOCK:(pp + 1) * BLOCK]
            tiles = []
            for c in range(3):
                t = sc[:, c * BLOCK:(c + 1) * BLOCK]
                if c == 1:
                    t = t + bias_ref[3 * h + 1]
                else:
                    edge = (blk == 0) if c == 0 else (blk == nb - 1)
                    b = bias_ref[jnp.where(edge, _MASKED_TILE, 3 * h + c)]
                    t = jnp.where(b > 0.0, NEG_INF, t + b)
                s_ref[slot, h, :, c * BLOCK:(c + 1) * BLOCK] = t
                tiles.append(t)
            mx = jnp.max(jnp.maximum(jnp.maximum(tiles[0], tiles[1]), tiles[2]), axis=-1, keepdims=True)
            m_ref[slot, h] = jnp.broadcast_to(jnp.maximum(mx, sink_ref[h] * LOG2_E), (BLOCK, LANES))

    def finish(qi, slot):
        q0 = pl.multiple_of(qi * BLOCK, BLOCK)
        win2 = pl.ds(pl.multiple_of(qi * (2 * BLOCK), 2 * BLOCK), 6 * BLOCK)
        for h in range(N_HEADS):
            m = m_ref[slot, h]
            for c in range(3):
                cs = slice(c * BLOCK, (c + 1) * BLOCK)
                dst = slice((2 * c + h % 2) * BLOCK, (2 * c + h % 2 + 1) * BLOCK)
                prow = slice(((h // 2) % 2) * BLOCK, ((h // 2) % 2 + 1) * BLOCK)
                p_ref[slot, h // GROUP, prow, dst] = jnp.exp2(s_ref[slot, h, :, cs] - m).astype(BF16)
        for g in range(N_KV_HEADS):
            o2 = jnp.dot(p_ref[slot, g], vab_ref[g, win2, :], preferred_element_type=F32)
            for pp in range(2):
                pr = 2 * g + pp
                h0, h1 = 2 * pr, 2 * pr + 1
                o = o2[pp * BLOCK:(pp + 1) * BLOCK]
                e = jnp.where(low_q, jnp.exp2(sink_ref[h0] * LOG2_E - m_ref[slot, h0]),
                              jnp.exp2(sink_ref[h1] * LOG2_E - m_ref[slot, h1]))
                o_ref[pl.ds(q0, BLOCK), pr * LANES:(pr + 1) * LANES] = (
                    o[:, :LANES] * (1.0 / (o[:, LANES:] + e)))

    scores(0, 0)

    def two_sub_blocks(i, _):
        qa = 2 * i
        scores(qa + 1, 1)
        finish(qa, 0)
        scores(jnp.minimum(qa + 2, nsub - 1), 0)
        finish(qa + 1, 1)
        return 0

    lax.fori_loop(0, nsub // 2, two_sub_blocks, 0)


def _attention(q3, k3, v3, sink, ffn_weights):
    bsz, s, _ = q3.shape
    tq = TQ_ATTN
    nq = s // tq
    hb = tq // BLOCK
    nb = s // BLOCK
    tk = tq + 2 * BLOCK
    kv_main = pl.BlockSpec((None, tq, KV_W), lambda b, j: (b, j, 0))
    kv_prev = pl.BlockSpec((None, BLOCK, KV_W), lambda b, j: (b, jnp.maximum(j * hb - 1, 0), 0))
    kv_next = pl.BlockSpec((None, BLOCK, KV_W), lambda b, j: (b, jnp.minimum((j + 1) * hb, nb - 1), 0))
    qo = pl.BlockSpec((None, tq, Q_W), lambda b, j: (b, j, 0))
    nsteps = bsz * nq
    w_slice = lambda w: pl.BlockSpec((w.shape[0] // nsteps, w.shape[1]), lambda b, j: (b * nq + j, 0))
    w_specs = [w_slice(w) for w in ffn_weights]
    return pl.pallas_call(
        functools.partial(_attn_kernel, nb),
        grid=(bsz, nq),
        in_specs=[pl.BlockSpec(memory_space=pltpu.SMEM), qo,
                  kv_prev, kv_main, kv_next, kv_prev, kv_main, kv_next] + w_specs,
        out_specs=[qo] + w_specs,
        out_shape=[jax.ShapeDtypeStruct((bsz, s, Q_W), F32)]
        + [jax.ShapeDtypeStruct(w.shape, BF16) for w in ffn_weights],
        scratch_shapes=[
            pltpu.VMEM((2 * N_KV_HEADS, tk, LANES), BF16),
            pltpu.VMEM((N_KV_HEADS, 2 * tk, 2 * LANES), BF16),
            pltpu.VMEM((_N_BIAS, BLOCK, BLOCK), F32),
            pltpu.VMEM((2, N_HEADS, BLOCK, 3 * BLOCK), F32),
            pltpu.VMEM((2, N_HEADS, BLOCK, LANES), F32),
            pltpu.VMEM((2, N_KV_HEADS, 2 * BLOCK, 6 * BLOCK), BF16),
        ],
        compiler_params=pltpu.CompilerParams(
            dimension_semantics=("arbitrary", "arbitrary"), vmem_limit_bytes=VMEM_LIMIT_BYTES),
        name="attention",
    )(sink, q3, k3, k3, k3, v3, v3, v3, *ffn_weights)


def _ffn_kernel(x_ref, hf_ref, hb_ref, yb_ref, coef_ref, wo_ref, g2_ref, wi_ref, wd_ref, g3_ref, o_ref):
    merged = coef_ref[:, :D_MODEL] * (hf_ref[...] + hb_ref[...]) + coef_ref[:, D_MODEL:] * yb_ref[...]
    x1 = x_ref[...] + jnp.dot(merged.astype(BF16), wo_ref[...], preferred_element_type=F32)
    xn = _rmsnorm(x1, g2_ref[...]).astype(BF16)
    gate = jnp.dot(xn, wi_ref[:, :D_FF], preferred_element_type=F32)
    up = jnp.dot(xn, wi_ref[:, D_FF:], preferred_element_type=F32)
    act = ((gate * jnp.tanh(gate) + gate) * up).astype(BF16)
    x2 = x1 + jnp.dot(act, wd_ref[...], preferred_element_type=F32)
    o_ref[...] = _rmsnorm(x2, g3_ref[...])


def _ffn(x2, hf, hb, yb, coef, wo, g2, wi, wd, g3):
    t = x2.shape[0]
    tm = TM_FFN
    row = lambda w_: pl.BlockSpec((tm, w_), lambda i: (i, 0))
    return pl.pallas_call(
        _ffn_kernel,
        grid=(t // tm,),
        in_specs=[row(D_MODEL), row(D_MODEL), row(D_MODEL), row(D_MODEL), row(2 * D_MODEL),
                  _resident((D_MODEL, D_MODEL)), _resident((1, D_MODEL)),
                  _resident((D_MODEL, 2 * D_FF)), _resident((D_FF, D_MODEL)), _resident((1, D_MODEL))],
        out_specs=row(D_MODEL),
        out_shape=jax.ShapeDtypeStruct((t, D_MODEL), F32),
        compiler_params=pltpu.CompilerParams(
            dimension_semantics=("arbitrary",), vmem_limit_bytes=VMEM_LIMIT_BYTES),
        name="merge_ffn",
    )(x2, hf, hb, yb, coef, wo, g2, wi, wd, g3)


def _gate_weights(wa, wx):
    hpg = LRU_GROUP // LRU_BLOCK
    w = jnp.stack([wa, wx], axis=1).astype(BF16)
    w = w.reshape(w.shape[0], 2, N_GROUPS, hpg, LRU_BLOCK, LRU_BLOCK)
    eye = jnp.eye(hpg, dtype=BF16)
    bd = jnp.einsum("dqghij,hk->dghiqkj", w, eye)
    return bd.reshape(w.shape[0], N_GROUPS, LRU_GROUP, 2 * LRU_GROUP)


def kernel(x, norm_mix_g, w_in, b_gate, conv_w, conv_b, lru_lambda, lru_wa, lru_ba, lru_wx, lru_bx,
           attn_sink, w_out, norm_ffn_g, w_ffn_in, w_ffn_out, norm_final_g):
    bsz, s, d = x.shape
    depth = w_in.shape[0]
    assert depth == 1, "the merge/ffn kernel applies the final RMSNorm, so it must be the last layer"
    t = bsz * s
    row = lambda a: a.reshape(1, -1)
    time_major = lambda a: a.reshape(a.shape[:-1] + (LANE_TILES, LANES))
    x2 = x.reshape(t, d)
    for l in range(depth):
        huc, q, k, v, merge_coef = _in_proj(
            x2, s, row(norm_mix_g[l]), w_in[l].astype(BF16), row(b_gate[l]),
            time_major(0.5 * conv_w[l]), time_major(0.5 * conv_b[l]))
        seq = lambda a: a.reshape(bsz, s, a.shape[-1])
        wg = _gate_weights(lru_wa[l], lru_wx[l])
        lru = lambda dr: (row(lru_lambda[l, dr]), wg[dr], row(lru_ba[l, dr]), row(lru_bx[l, dr]))
        h_fwd = _lru(False, seq(huc), *lru(0))
        h_bwd = _lru(True, seq(huc), *lru(1))
        y_b, wo_b, wi_b, wd_b = _attention(seq(q), seq(k), seq(v), attn_sink[l],
                                           (w_out[l], w_ffn_in[l], w_ffn_out[l]))
        x2 = _ffn(x2, h_fwd.reshape(t, d), h_bwd.reshape(t, d), y_b.reshape(t, d), merge_coef,
                  wo_b, row(norm_ffn_g[l]), wi_b, wd_b, row(norm_final_g))
    return x2.reshape(bsz, s, d)
```

```python
import functools
import math

import numpy as np
import jax
import jax.numpy as jnp
from jax import lax
from jax.experimental import pallas as pl
from jax.experimental.pallas import tpu as pltpu

F32 = jnp.float32
BF16 = jnp.bfloat16

D_MODEL = 1024
LRU_HEADS = 16
LRU_BLOCK = D_MODEL // LRU_HEADS
CONV_WIDTH = 4
CONV_LEFT = 2
RGLRU_C = 8.0
N_HEADS = 16
N_KV_HEADS = 4
HEAD_DIM = 64
GROUP = N_HEADS // N_KV_HEADS
WINDOW = 128
BLOCK = 128
D_FF = 2816
Q_W = N_HEADS * HEAD_DIM
KV_W = N_KV_HEADS * HEAD_DIM
IN_W = 2 * D_MODEL + Q_W + 2 * KV_W + 2 * D_MODEL
EPS = 1e-6
NEG_INF = -1e30

LANES = 128
SUBLANES = 8
VMEM_LIMIT_BYTES = 56 * 1024 * 1024

TM_PROJ = 512
TS_LRU = 1024
TQ_ATTN = 1024
TM_FFN = 512
LRU_GROUP = 256
N_GROUPS = D_MODEL // LRU_GROUP
LANE_TILES = D_MODEL // LANES
assert LANE_TILES == SUBLANES
SLAB = SUBLANES * LANE_TILES
TILES_PER_STEP = 16
N_SLABS = 3 * TILES_PER_STEP

_ALIBI_SLOPES = [float(v) for v in np.exp2(
    -8.0 * (np.arange(N_HEADS, dtype=np.float32) + 1.0) / N_HEADS).astype(np.float32)]
LOG2_E = math.log2(math.e)
Q_SCALE = HEAD_DIM ** -0.5 * LOG2_E


def _sigmoid(x):
    return 0.5 * jnp.tanh(0.5 * x) + 0.5


def _rmsnorm(x, g):
    return x * lax.rsqrt(jnp.mean(x * x, axis=-1, keepdims=True) + EPS) * g


def _resident(shape):
    nd = len(shape)
    return pl.BlockSpec(shape, lambda *_: (0,) * nd, pipeline_mode=pl.Buffered(1))


def _in_proj_kernel(nblk, x_ref, xp_ref, xnx_ref, g_ref, w_ref, bg_ref, cw_ref, cb_ref,
                    uc_ref, q_ref, k_ref, v_ref, coef_ref, uext_ref, slab_ref):
    tm = x_ref.shape[0]
    nt = tm // SUBLANES
    jb = pl.program_id(0) % nblk
    gn = g_ref[...]
    xn = _rmsnorm(x_ref[...], gn).astype(BF16)
    halo = jnp.concatenate([xp_ref[...], xnx_ref[...]], axis=0)
    xh = _rmsnorm(halo, gn).astype(BF16)

    def proj(lo, hi):
        return jnp.dot(xn, w_ref[:, lo:hi], preferred_element_type=F32)

    ue = jnp.dot(jnp.concatenate([xn, xh], axis=0), w_ref[:, :D_MODEL], preferred_element_type=F32)
    uext_ref[SUBLANES:SUBLANES + tm, :] = ue[:tm]
    uext_ref[0:SUBLANES, :] = jnp.where(jb > 0, ue[tm:tm + SUBLANES], 0.0)
    uext_ref[SUBLANES + tm:, :] = jnp.where(jb < nblk - 1, ue[tm + SUBLANES:], 0.0)

    o_g = D_MODEL
    o_q = o_g + D_MODEL
    o_k = o_q + Q_W
    o_v = o_k + KV_W
    o_z = o_v + KV_W
    gate_a = _sigmoid(proj(o_z, o_z + D_MODEL) + bg_ref[:, :D_MODEL])
    coef_ref[:, :D_MODEL] = gate_a * _gelu_tanh(proj(o_g, o_g + D_MODEL))
    coef_ref[:, D_MODEL:] = _sigmoid(proj(o_z + D_MODEL, o_z + 2 * D_MODEL) + bg_ref[:, D_MODEL:])
    q_ref[...] = (proj(o_q, o_q + Q_W) * Q_SCALE).astype(BF16)
    k_ref[...] = proj(o_k, o_k + KV_W).astype(BF16)
    v_ref[...] = proj(o_v, o_v + KV_W).astype(BF16)

    w = [cw_ref[k] for k in range(CONV_WIDTH)]
    cb = cb_ref[...]
    ring = slab_ref.shape[0] // 2

    def ext_steps(e, s):
        return _split_time(uext_ref[e * SUBLANES:(e + 1) * SUBLANES, :], slab_ref, s)

    tail = ext_steps(0, 0)
    prev2, prev1 = tail[SUBLANES - 2], tail[SUBLANES - 1]
    cur = ext_steps(1, 1)
    for t in range(nt):
        nxt = ext_steps(t + 2, 2 * (t % ring))
        x = [prev2, prev1] + cur + [nxt[0]]
        out = [(x[r] * w[0] + x[r + 1] * w[1]) + (x[r + 2] * w[2] + x[r + 3] * w[3]) + cb
               for r in range(SUBLANES)]
        uc_ref[t * SUBLANES:(t + 1) * SUBLANES, :] = _join_time(out, slab_ref, 2 * (t % ring) + 1)
        prev2, prev1, cur = cur[SUBLANES - 2], cur[SUBLANES - 1], nxt


def _in_proj(x2, seq, g, w, bg, cw, cb):
    t = x2.shape[0]
    tm = TM_PROJ
    nblk = seq // tm
    hb = tm // SUBLANES
    nh = t // SUBLANES
    row = lambda w_: pl.BlockSpec((tm, w_), lambda i: (i, 0))
    prev = pl.BlockSpec((SUBLANES, D_MODEL), lambda i: (jnp.maximum(i * hb - 1, 0), 0))
    nxt = pl.BlockSpec((SUBLANES, D_MODEL), lambda i: (jnp.minimum((i + 1) * hb, nh - 1), 0))
    return pl.pallas_call(
        functools.partial(_in_proj_kernel, nblk),
        grid=(t // tm,),
        in_specs=[row(D_MODEL), prev, nxt, _resident((1, D_MODEL)), _resident((D_MODEL, IN_W)),
                  _resident((1, 2 * D_MODEL)), _resident(cw.shape), _resident(cb.shape)],
        out_specs=[row(D_MODEL), row(Q_W), row(KV_W), row(KV_W), row(2 * D_MODEL)],
        out_shape=[
            jax.ShapeDtypeStruct((t, D_MODEL), F32),
            jax.ShapeDtypeStruct((t, Q_W), BF16),
            jax.ShapeDtypeStruct((t, KV_W), BF16),
            jax.ShapeDtypeStruct((t, KV_W), BF16),
            jax.ShapeDtypeStruct((t, 2 * D_MODEL), F32),
        ],
        scratch_shapes=[
            pltpu.VMEM((tm + 2 * SUBLANES, D_MODEL), F32),
            pltpu.VMEM((16, SLAB, LANES), F32),
        ],
        compiler_params=pltpu.CompilerParams(
            dimension_semantics=("arbitrary",), vmem_limit_bytes=VMEM_LIMIT_BYTES),
        name="in_proj",
    )(x2, x2, x2, g, w, bg, cw, cb)


def _split_time(tile, slab_ref, s):
    for j in range(LANE_TILES):
        slab_ref[s, j * SUBLANES:(j + 1) * SUBLANES, :] = tile[:, j * LANES:(j + 1) * LANES]
    return [slab_ref[s, pl.ds(r, LANE_TILES, stride=SUBLANES), :] for r in range(SUBLANES)]


def _join_time(steps, slab_ref, s):
    for r in range(SUBLANES):
        slab_ref[s, pl.ds(r, LANE_TILES, stride=SUBLANES), :] = steps[r]
    return jnp.concatenate(
        [slab_ref[s, j * SUBLANES:(j + 1) * SUBLANES, :] for j in range(LANE_TILES)], axis=1)


def _half_decay_log2(lam):
    return (-0.5 * RGLRU_C * math.log2(math.e)) * (
        jnp.maximum(-lam, 0.0) + jnp.log1p(jnp.exp(-jnp.abs(lam))))


def _gate_terms(pre_r, pre_i, huc, hd2, hba, hbx):
    a = jnp.exp2(hd2 * jnp.tanh(pre_r + hba) + hd2)
    y = jnp.maximum(1.0 - a * a, 0.0)
    beta = jnp.where(y > 0.0, y * lax.rsqrt(y), 0.0)
    return a, beta * (jnp.tanh(pre_i + hbx) * huc + huc)


def _gelu_tanh(g):
    c = math.sqrt(2.0 / math.pi)
    half = 0.5 * g
    return half * jnp.tanh(g * ((c * 0.044715) * (g * g) + c)) + half


def _gate_maps(uc_ref, wg_ref, ra_ref, ix_ref, rows):
    for gi in range(N_GROUPS):
        cols = slice(gi * LRU_GROUP, (gi + 1) * LRU_GROUP)
        pre = jnp.dot(uc_ref[rows, cols].astype(BF16), wg_ref[gi], preferred_element_type=F32)
        ra_ref[rows, cols] = pre[:, :LRU_GROUP]
        ix_ref[rows, cols] = pre[:, LRU_GROUP:]


def _recurrence(reverse, uc_ref, wg_ref, ra_ref, ix_ref, lam_ref, ba_ref, bx_ref, carry_ref, slab_ref,
                emit):
    ts = uc_ref.shape[0]
    nt = ts // SUBLANES
    nsteps = nt // TILES_PER_STEP
    hd2 = _half_decay_log2(lam_ref[...])
    hba = 0.5 * ba_ref[...]
    hbx = 0.5 * bx_ref[...]

    def tile(k, s, h):
        kk = (nt - 1 - k) if reverse else k
        rows = pl.ds(pl.multiple_of(kk * SUBLANES, SUBLANES), SUBLANES)
        a, b = _gate_terms(ra_ref[rows, :], ix_ref[rows, :], uc_ref[rows, :], hd2, hba, hbx)
        a_t = _split_time(a, slab_ref, 3 * s)
        b_t = _split_time(b, slab_ref, 3 * s + 1)
        hs = [None] * SUBLANES
        for r in (range(SUBLANES - 1, -1, -1) if reverse else range(SUBLANES)):
            h = a_t[r] * h + b_t[r]
            hs[r] = h
        emit(rows, _join_time(hs, slab_ref, 3 * s + 2))
        return h

    def group(i, h):
        for s in range(TILES_PER_STEP):
            h = tile(TILES_PER_STEP * i + s, s, h)
        return h

    _gate_maps(uc_ref, wg_ref, ra_ref, ix_ref, slice(0, ts))
    carry_ref[...] = lax.fori_loop(0, nsteps, group, carry_ref[...])


def _lru_kernel(reverse, uc_ref, lam_ref, wg_ref, ba_ref, bx_ref, h_ref, ra_ref, ix_ref, carry_ref,
                slab_ref):
    @pl.when(pl.program_id(1) == 0)
    def _():
        carry_ref[...] = jnp.zeros_like(carry_ref)

    def emit(rows, h):
        h_ref[rows, :] = h

    _recurrence(reverse, uc_ref, wg_ref, ra_ref, ix_ref, lam_ref, ba_ref, bx_ref, carry_ref, slab_ref,
                emit)


def _lru(reverse, uc3, lam, wg, ba, bx):
    bsz, s, _ = uc3.shape
    ts = TS_LRU
    nblk = s // ts
    blk = pl.BlockSpec((None, ts, D_MODEL), lambda b, j: (b, (nblk - 1 - j) if reverse else j, 0))
    return pl.pallas_call(
        functools.partial(_lru_kernel, reverse),
        grid=(bsz, nblk),
        in_specs=[blk, _resident((1, D_MODEL)), _resident(wg.shape),
                  _resident((1, D_MODEL)), _resident((1, D_MODEL))],
        out_specs=blk,
        out_shape=jax.ShapeDtypeStruct((bsz, s, D_MODEL), F32),
        scratch_shapes=[
            pltpu.VMEM((ts, D_MODEL), F32),
            pltpu.VMEM((ts, D_MODEL), F32),
            pltpu.VMEM((LANE_TILES, LANES), F32),
            pltpu.VMEM((N_SLABS, SLAB, LANES), F32),
        ],
        compiler_params=pltpu.CompilerParams(
            dimension_semantics=("arbitrary", "arbitrary"), vmem_limit_bytes=VMEM_LIMIT_BYTES),
        name="lru_bwd" if reverse else "lru_fwd",
    )(uc3, lam, wg, ba, bx)


_N_BIAS = 3 * N_HEADS + 1
_MASKED_TILE = 3 * N_HEADS


def _attn_kernel(nb, sink_ref, q_ref, kp_ref, km_ref, kn_ref, vp_ref, vm_ref, vn_ref,
                 wo_ref, wi_ref, wd_ref, o_ref, wob_ref, wib_ref, wdb_ref,
                 kz_ref, vab_ref, bias_ref, s_ref, m_ref, p_ref):
    tq = q_ref.shape[0]
    tk = tq + 2 * BLOCK
    nsub = tq // BLOCK
    j = pl.program_id(1)

    wob_ref[...] = wo_ref[...].astype(BF16)
    wdb_ref[...] = wd_ref[...].astype(BF16)
    gate_col = lax.broadcasted_iota(jnp.int32, (1, 2 * D_FF), 1) < D_FF
    wib_ref[...] = (wi_ref[...] * jnp.where(gate_col, 0.5, 1.0)).astype(BF16)

    @pl.when((pl.program_id(0) == 0) & (j == 0))
    def _():
        row = lax.broadcasted_iota(jnp.int32, (BLOCK, BLOCK), 0)
        col = lax.broadcasted_iota(jnp.int32, (BLOCK, BLOCK), 1)
        bias_ref[_MASKED_TILE] = jnp.ones((BLOCK, BLOCK), F32)
        for c in range(3):
            dist = jnp.abs(row + (1 - c) * BLOCK - col)
            absd = dist.astype(F32)
            for h in range(N_HEADS):
                bias_ref[3 * h + c] = jnp.where(dist <= WINDOW, (-_ALIBI_SLOPES[h] * absd) * LOG2_E, 1.0)

    low = lax.broadcasted_iota(jnp.int32, (tk, LANES), 1) < HEAD_DIM
    low_q = lax.broadcasted_iota(jnp.int32, (BLOCK, LANES), 1) < HEAD_DIM
    ones_q = [jnp.where(low_q, 1.0, 0.0).astype(BF16), jnp.where(low_q, 0.0, 1.0).astype(BF16)]
    for t in range(KV_W // LANES):
        cols = slice(t * LANES, (t + 1) * LANES)
        full = jnp.concatenate([r[:, cols] for r in (kp_ref, km_ref, kn_ref)], axis=0).astype(F32)
        swapped = pltpu.roll(full, HEAD_DIM, axis=1)
        kz_ref[4 * t + 0] = jnp.where(low, full, 0.0).astype(BF16)
        kz_ref[4 * t + 1] = jnp.where(low, 0.0, swapped).astype(BF16)
        kz_ref[4 * t + 2] = jnp.where(low, swapped, 0.0).astype(BF16)
        kz_ref[4 * t + 3] = jnp.where(low, 0.0, full).astype(BF16)
        nkb = tk // BLOCK
        for b in range(nkb):
            if b == 0:
                src = vp_ref[:, cols]
            elif b == nkb - 1:
                src = vn_ref[:, cols]
            else:
                src = vm_ref[(b - 1) * BLOCK:b * BLOCK, cols]
            full = src.astype(F32)
            swapped = pltpu.roll(full, HEAD_DIM, axis=1)
            placed = [jnp.where(low_q, full, 0.0), jnp.where(low_q, 0.0, swapped),
                      jnp.where(low_q, swapped, 0.0), jnp.where(low_q, 0.0, full)]
            for i in range(4):
                g, half = 2 * t + i // 2, i % 2
                dst = slice((2 * b + half) * BLOCK, (2 * b + half + 1) * BLOCK)
                vab_ref[g, dst, :LANES] = placed[i].astype(BF16)
                vab_ref[g, dst, LANES:] = ones_q[half]

    def scores(qi, slot):
        q0 = pl.multiple_of(qi * BLOCK, BLOCK)
        blk = j * nsub + qi
        win = pl.ds(q0, 3 * BLOCK)
        sc2 = [None, None]
        for h in range(N_HEADS):
            g, pp, half = h // GROUP, (h % GROUP) // 2, h % 2
            if pp == 0:
                q2 = jnp.concatenate(
                    [q_ref[pl.ds(q0, BLOCK), (2 * g + i) * LANES:(2 * g + i + 1) * LANES] for i in range(2)],
                    axis=0)
                sc2[half] = lax.dot_general(q2, kz_ref[2 * g + half, win, :], (((1,), (1,)), ((), ())),
                                            preferred_element_type=F32)
            sc = sc2[half][pp * BLOCK:(pp + 1) * BLOCK]
            tiles = []
            for c in range(3):
                t = sc[:, c * BLOCK:(c + 1) * BLOCK]
                if c == 1:
                    t = t + bias_ref[3 * h + 1]
                else:
                    edge = (blk == 0) if c == 0 else (blk == nb - 1)
                    b = bias_ref[jnp.where(edge, _MASKED_TILE, 3 * h + c)]
                    t = jnp.where(b > 0.0, NEG_INF, t + b)
                s_ref[slot, h, :, c * BLOCK:(c + 1) * BLOCK] = t
                tiles.append(t)
            mx = jnp.max(jnp.maximum(jnp.maximum(tiles[0], tiles[1]), tiles[2]), axis=-1, keepdims=True)
            m_ref[slot, h] = jnp.broadcast_to(jnp.maximum(mx, sink_ref[h] * LOG2_E), (BLOCK, LANES))

    def finish(qi, slot):
        q0 = pl.multiple_of(qi * BLOCK, BLOCK)
        win2 = pl.ds(pl.multiple_of(qi * (2 * BLOCK), 2 * BLOCK), 6 * BLOCK)
        for h in range(N_HEADS):
            m = m_ref[slot, h]
            for c in range(3):
                cs = slice(c * BLOCK, (c + 1) * BLOCK)
                dst = slice((2 * c + h % 2) * BLOCK, (2 * c + h % 2 + 1) * BLOCK)
                prow = slice(((h // 2) % 2) * BLOCK, ((h // 2) % 2 + 1) * BLOCK)
                p_ref[slot, h // GROUP, prow, dst] = jnp.exp2(s_ref[slot, h, :, cs] - m).astype(BF16)
        for g in range(N_KV_HEADS):
            o2 = jnp.dot(p_ref[slot, g], vab_ref[g, win2, :], preferred_element_type=F32)
            for pp in range(2):
                pr = 2 * g + pp
                h0, h1 = 2 * pr, 2 * pr + 1
                o = o2[pp * BLOCK:(pp + 1) * BLOCK]
                e = jnp.where(low_q, jnp.exp2(sink_ref[h0] * LOG2_E - m_ref[slot, h0]),
                              jnp.exp2(sink_ref[h1] * LOG2_E - m_ref[slot, h1]))
                o_ref[pl.ds(q0, BLOCK), pr * LANES:(pr + 1) * LANES] = (
                    o[:, :LANES] * (1.0 / (o[:, LANES:] + e)))

    scores(0, 0)

    def two_sub_blocks(i, _):
        qa = 2 * i
        scores(qa + 1, 1)
        finish(qa, 0)
        scores(jnp.minimum(qa + 2, nsub - 1), 0)
        finish(qa + 1, 1)
        return 0

    lax.fori_loop(0, nsub // 2, two_sub_blocks, 0)


def _attention(q3, k3, v3, sink, ffn_weights):
    bsz, s, _ = q3.shape
    tq = TQ_ATTN
    nq = s // tq
    hb = tq // BLOCK
    nb = s // BLOCK
    tk = tq + 2 * BLOCK
    kv_main = pl.BlockSpec((None, tq, KV_W), lambda b, j: (b, j, 0))
    kv_prev = pl.BlockSpec((None, BLOCK, KV_W), lambda b, j: (b, jnp.maximum(j * hb - 1, 0), 0))
    kv_next = pl.BlockSpec((None, BLOCK, KV_W), lambda b, j: (b, jnp.minimum((j + 1) * hb, nb - 1), 0))
    qo = pl.BlockSpec((None, tq, Q_W), lambda b, j: (b, j, 0))
    nsteps = bsz * nq
    w_slice = lambda w: pl.BlockSpec((w.shape[0] // nsteps, w.shape[1]), lambda b, j: (b * nq + j, 0))
    w_specs = [w_slice(w) for w in ffn_weights]
    return pl.pallas_call(
        functools.partial(_attn_kernel, nb),
        grid=(bsz, nq),
        in_specs=[pl.BlockSpec(memory_space=pltpu.SMEM), qo,
                  kv_prev, kv_main, kv_next, kv_prev, kv_main, kv_next] + w_specs,
        out_specs=[qo] + w_specs,
        out_shape=[jax.ShapeDtypeStruct((bsz, s, Q_W), F32)]
        + [jax.ShapeDtypeStruct(w.shape, BF16) for w in ffn_weights],
        scratch_shapes=[
            pltpu.VMEM((2 * N_KV_HEADS, tk, LANES), BF16),
            pltpu.VMEM((N_KV_HEADS, 2 * tk, 2 * LANES), BF16),
            pltpu.VMEM((_N_BIAS, BLOCK, BLOCK), F32),
            pltpu.VMEM((2, N_HEADS, BLOCK, 3 * BLOCK), F32),
            pltpu.VMEM((2, N_HEADS, BLOCK, LANES), F32),
            pltpu.VMEM((2, N_KV_HEADS, 2 * BLOCK, 6 * BLOCK), BF16),
        ],
        compiler_params=pltpu.CompilerParams(
            dimension_semantics=("arbitrary", "arbitrary"), vmem_limit_bytes=VMEM_LIMIT_BYTES),
        name="attention",
    )(sink, q3, k3, k3, k3, v3, v3, v3, *ffn_weights)


def _ffn_kernel(x_ref, hf_ref, hb_ref, yb_ref, coef_ref, wo_ref, g2_ref, wi_ref, wd_ref, g3_ref, o_ref):
    merged = coef_ref[:, :D_MODEL] * (hf_ref[...] + hb_ref[...]) + coef_ref[:, D_MODEL:] * yb_ref[...]
    x1 = x_ref[...] + jnp.dot(merged.astype(BF16), wo_ref[...], preferred_element_type=F32)
    xn = _rmsnorm(x1, g2_ref[...]).astype(BF16)
    gate = jnp.dot(xn, wi_ref[:, :D_FF], preferred_element_type=F32)
    up = jnp.dot(xn, wi_ref[:, D_FF:], preferred_element_type=F32)
    act = ((gate * jnp.tanh(gate) + gate) * up).astype(BF16)
    x2 = x1 + jnp.dot(act, wd_ref[...], preferred_element_type=F32)
    o_ref[...] = _rmsnorm(x2, g3_ref[...])


def _ffn(x2, hf, hb, yb, coef, wo, g2, wi, wd, g3):
    t = x2.shape[0]
    tm = TM_FFN
    row = lambda w_: pl.BlockSpec((tm, w_), lambda i: (i, 0))
    return pl.pallas_call(
        _ffn_kernel,
        grid=(t // tm,),
        in_specs=[row(D_MODEL), row(D_MODEL), row(D_MODEL), row(D_MODEL), row(2 * D_MODEL),
                  _resident((D_MODEL, D_MODEL)), _resident((1, D_MODEL)),
                  _resident((D_MODEL, 2 * D_FF)), _resident((D_FF, D_MODEL)), _resident((1, D_MODEL))],
        out_specs=row(D_MODEL),
        out_shape=jax.ShapeDtypeStruct((t, D_MODEL), F32),
        compiler_params=pltpu.CompilerParams(
            dimension_semantics=("arbitrary",), vmem_limit_bytes=VMEM_LIMIT_BYTES),
        name="merge_ffn",
    )(x2, hf, hb, yb, coef, wo, g2, wi, wd, g3)


def _gate_weights(wa, wx):
    def blockdiag(w):
        w = w.reshape(N_GROUPS, LRU_GROUP // LRU_BLOCK, LRU_BLOCK, LRU_BLOCK)
        eye = jnp.eye(LRU_GROUP // LRU_BLOCK, dtype=w.dtype)
        return jnp.einsum("ghij,hk->ghikj", w, eye).reshape(N_GROUPS, LRU_GROUP, LRU_GROUP)
    return jnp.concatenate([blockdiag(wa), blockdiag(wx)], axis=-1).astype(BF16)


def kernel(x, norm_mix_g, w_in, b_gate, conv_w, conv_b, lru_lambda, lru_wa, lru_ba, lru_wx, lru_bx,
           attn_sink, w_out, norm_ffn_g, w_ffn_in, w_ffn_out, norm_final_g):
    bsz, s, d = x.shape
    depth = w_in.shape[0]
    assert depth == 1, "the merge/ffn kernel applies the final RMSNorm, so it must be the last layer"
    t = bsz * s
    row = lambda a: a.reshape(1, -1)
    time_major = lambda a: a.reshape(a.shape[:-1] + (LANE_TILES, LANES))
    x2 = x.reshape(t, d)
    for l in range(depth):
        huc, q, k, v, merge_coef = _in_proj(
            x2, s, row(norm_mix_g[l]), w_in[l].astype(BF16), row(b_gate[l]),
            time_major(0.5 * conv_w[l]), time_major(0.5 * conv_b[l]))
        seq = lambda a: a.reshape(bsz, s, a.shape[-1])
        lru = lambda dr: (row(lru_lambda[l, dr]), _gate_weights(lru_wa[l, dr], lru_wx[l, dr]),
                          row(lru_ba[l, dr]), row(lru_bx[l, dr]))
        h_fwd = _lru(False, seq(huc), *lru(0))
        h_bwd = _lru(True, seq(huc), *lru(1))
        y_b, wo_b, wi_b, wd_b = _attention(seq(q), seq(k), seq(v), attn_sink[l],
                                           (w_out[l], w_ffn_in[l], w_ffn_out[l]))
        x2 = _ffn(x2, h_fwd.reshape(t, d), h_bwd.reshape(t, d), y_b.reshape(t, d), merge_coef,
                  wo_b, row(norm_ffn_g[l]), wi_b, wd_b, row(norm_final_g))
    return x2.reshape(bsz, s, d)
```

```python
import functools
import math

import numpy as np
import jax
import jax.numpy as jnp
from jax import lax
from jax.experimental import pallas as pl
from jax.experimental.pallas import tpu as pltpu

F32 = jnp.float32
BF16 = jnp.bfloat16

D_MODEL = 1024
LRU_HEADS = 16
LRU_BLOCK = D_MODEL // LRU_HEADS
CONV_WIDTH = 4
CONV_LEFT = 2
RGLRU_C = 8.0
N_HEADS = 16
N_KV_HEADS = 4
HEAD_DIM = 64
GROUP = N_HEADS // N_KV_HEADS
WINDOW = 128
BLOCK = 128
D_FF = 2816
Q_W = N_HEADS * HEAD_DIM
KV_W = N_KV_HEADS * HEAD_DIM
IN_W = 2 * D_MODEL + Q_W + 2 * KV_W + 2 * D_MODEL
EPS = 1e-6
NEG_INF = -1e30

LANES = 128
SUBLANES = 8
VMEM_LIMIT_BYTES = 56 * 1024 * 1024

TM_PROJ = 512
TS_LRU = 1024
TQ_ATTN = 1024
TM_FFN = 512
LRU_GROUP = 256
N_GROUPS = D_MODEL // LRU_GROUP
LANE_TILES = D_MODEL // LANES
assert LANE_TILES == SUBLANES
SLAB = SUBLANES * LANE_TILES
TILES_PER_STEP = 16
N_SLABS = 3 * TILES_PER_STEP

_ALIBI_SLOPES = [float(v) for v in np.exp2(
    -8.0 * (np.arange(N_HEADS, dtype=np.float32) + 1.0) / N_HEADS).astype(np.float32)]
LOG2_E = math.log2(math.e)
Q_SCALE = HEAD_DIM ** -0.5 * LOG2_E


def _sigmoid(x):
    return 0.5 * jnp.tanh(0.5 * x) + 0.5


def _rmsnorm(x, g):
    return x * lax.rsqrt(jnp.mean(x * x, axis=-1, keepdims=True) + EPS) * g


def _resident(shape):
    nd = len(shape)
    return pl.BlockSpec(shape, lambda *_: (0,) * nd, pipeline_mode=pl.Buffered(1))


def _in_proj_kernel(nblk, x_ref, xp_ref, xnx_ref, g_ref, w_ref, bg_ref, cw_ref, cb_ref,
                    uc_ref, q_ref, k_ref, v_ref, coef_ref, uext_ref, slab_ref):
    tm = x_ref.shape[0]
    nt = tm // SUBLANES
    jb = pl.program_id(0) % nblk
    gn = g_ref[...]
    xn = _rmsnorm(x_ref[...], gn).astype(BF16)
    halo = jnp.concatenate([xp_ref[...], xnx_ref[...]], axis=0)
    xh = _rmsnorm(halo, gn).astype(BF16)

    def proj(lo, hi):
        return jnp.dot(xn, w_ref[:, lo:hi], preferred_element_type=F32)

    ue = jnp.dot(jnp.concatenate([xn, xh], axis=0), w_ref[:, :D_MODEL], preferred_element_type=F32)
    uext_ref[SUBLANES:SUBLANES + tm, :] = ue[:tm]
    uext_ref[0:SUBLANES, :] = jnp.where(jb > 0, ue[tm:tm + SUBLANES], 0.0)
    uext_ref[SUBLANES + tm:, :] = jnp.where(jb < nblk - 1, ue[tm + SUBLANES:], 0.0)

    o_g = D_MODEL
    o_q = o_g + D_MODEL
    o_k = o_q + Q_W
    o_v = o_k + KV_W
    o_z = o_v + KV_W
    pw = 2 * LANES
    todo = []

    def coef_a(c):
        gate = _sigmoid(proj(o_z + c, o_z + c + pw) + bg_ref[:, c:c + pw])
        coef_ref[:, c:c + pw] = gate * _gelu_tanh(proj(o_g + c, o_g + c + pw))

    def coef_b(c):
        cz = D_MODEL + c
        coef_ref[:, cz:cz + pw] = _sigmoid(proj(o_z + cz, o_z + cz + pw) + bg_ref[:, cz:cz + pw])

    def plain(dst, lo, post):
        def run(c):
            dst[:, c:c + pw] = post(proj(lo + c, lo + c + pw))
        return run

    for fn, width in ((coef_a, D_MODEL), (coef_b, D_MODEL),
                      (plain(q_ref, o_q, lambda v: (v * Q_SCALE).astype(BF16)), Q_W),
                      (plain(k_ref, o_k, lambda v: v.astype(BF16)), KV_W),
                      (plain(v_ref, o_v, lambda v: v.astype(BF16)), KV_W)):
        todo += [(fn, c) for c in range(0, width, pw)]
    every = -(-nt // len(todo))

    def project_block():
        fn, c = todo.pop(0)
        fn(c)

    w = [cw_ref[k] for k in range(CONV_WIDTH)]
    cb = cb_ref[...]
    ring = slab_ref.shape[0] // 2

    def ext_steps(e, s):
        return _split_time(uext_ref[e * SUBLANES:(e + 1) * SUBLANES, :], slab_ref, s)

    tail = ext_steps(0, 0)
    prev2, prev1 = tail[SUBLANES - 2], tail[SUBLANES - 1]
    cur = ext_steps(1, 1)
    for t in range(nt):
        if t % every == 0 and todo:
            project_block()
        nxt = ext_steps(t + 2, 2 * (t % ring))
        x = [prev2, prev1] + cur + [nxt[0]]
        out = [(x[r] * w[0] + x[r + 1] * w[1]) + (x[r + 2] * w[2] + x[r + 3] * w[3]) + cb
               for r in range(SUBLANES)]
        uc_ref[t * SUBLANES:(t + 1) * SUBLANES, :] = _join_time(out, slab_ref, 2 * (t % ring) + 1)
        prev2, prev1, cur = cur[SUBLANES - 2], cur[SUBLANES - 1], nxt
    while todo:
        project_block()


def _in_proj(x2, seq, g, w, bg, cw, cb):
    t = x2.shape[0]
    tm = TM_PROJ
    nblk = seq // tm
    hb = tm // SUBLANES
    nh = t // SUBLANES
    row = lambda w_: pl.BlockSpec((tm, w_), lambda i: (i, 0))
    prev = pl.BlockSpec((SUBLANES, D_MODEL), lambda i: (jnp.maximum(i * hb - 1, 0), 0))
    nxt = pl.BlockSpec((SUBLANES, D_MODEL), lambda i: (jnp.minimum((i + 1) * hb, nh - 1), 0))
    return pl.pallas_call(
        functools.partial(_in_proj_kernel, nblk),
        grid=(t // tm,),
        in_specs=[row(D_MODEL), prev, nxt, _resident((1, D_MODEL)), _resident((D_MODEL, IN_W)),
                  _resident((1, 2 * D_MODEL)), _resident(cw.shape), _resident(cb.shape)],
        out_specs=[row(D_MODEL), row(Q_W), row(KV_W), row(KV_W), row(2 * D_MODEL)],
        out_shape=[
            jax.ShapeDtypeStruct((t, D_MODEL), F32),
            jax.ShapeDtypeStruct((t, Q_W), BF16),
            jax.ShapeDtypeStruct((t, KV_W), BF16),
            jax.ShapeDtypeStruct((t, KV_W), BF16),
            jax.ShapeDtypeStruct((t, 2 * D_MODEL), F32),
        ],
        scratch_shapes=[
            pltpu.VMEM((tm + 2 * SUBLANES, D_MODEL), F32),
            pltpu.VMEM((16, SLAB, LANES), F32),
        ],
        compiler_params=pltpu.CompilerParams(
            dimension_semantics=("arbitrary",), vmem_limit_bytes=VMEM_LIMIT_BYTES),
        name="in_proj",
    )(x2, x2, x2, g, w, bg, cw, cb)


def _split_time(tile, slab_ref, s):
    for j in range(LANE_TILES):
        slab_ref[s, j * SUBLANES:(j + 1) * SUBLANES, :] = tile[:, j * LANES:(j + 1) * LANES]
    return [slab_ref[s, pl.ds(r, LANE_TILES, stride=SUBLANES), :] for r in range(SUBLANES)]


def _join_time(steps, slab_ref, s):
    for r in range(SUBLANES):
        slab_ref[s, pl.ds(r, LANE_TILES, stride=SUBLANES), :] = steps[r]
    return jnp.concatenate(
        [slab_ref[s, j * SUBLANES:(j + 1) * SUBLANES, :] for j in range(LANE_TILES)], axis=1)


def _half_decay_log2(lam):
    return (-0.5 * RGLRU_C * math.log2(math.e)) * (
        jnp.maximum(-lam, 0.0) + jnp.log1p(jnp.exp(-jnp.abs(lam))))


def _gate_terms(pre_r, pre_i, huc, hd2, hba, hbx):
    a = jnp.exp2(hd2 * jnp.tanh(pre_r + hba) + hd2)
    y = jnp.maximum(1.0 - a * a, 0.0)
    beta = jnp.where(y > 0.0, y * lax.rsqrt(y), 0.0)
    return a, beta * (jnp.tanh(pre_i + hbx) * huc + huc)


def _gelu_tanh(g):
    c = math.sqrt(2.0 / math.pi)
    half = 0.5 * g
    return half * jnp.tanh(g * ((c * 0.044715) * (g * g) + c)) + half


def _gate_maps(uc_ref, wg_ref, ra_ref, ix_ref, rows):
    for gi in range(N_GROUPS):
        cols = slice(gi * LRU_GROUP, (gi + 1) * LRU_GROUP)
        pre = jnp.dot(uc_ref[rows, cols].astype(BF16), wg_ref[gi], preferred_element_type=F32)
        ra_ref[rows, cols] = pre[:, :LRU_GROUP]
        ix_ref[rows, cols] = pre[:, LRU_GROUP:]


def _recurrence(reverse, uc_ref, wg_ref, ra_ref, ix_ref, lam_ref, ba_ref, bx_ref, carry_ref, slab_ref,
                emit):
    ts = uc_ref.shape[0]
    nt = ts // SUBLANES
    nsteps = nt // TILES_PER_STEP
    hd2 = _half_decay_log2(lam_ref[...])
    hba = 0.5 * ba_ref[...]
    hbx = 0.5 * bx_ref[...]

    def tile(k, s, h):
        kk = (nt - 1 - k) if reverse else k
        rows = pl.ds(pl.multiple_of(kk * SUBLANES, SUBLANES), SUBLANES)
        a, b = _gate_terms(ra_ref[rows, :], ix_ref[rows, :], uc_ref[rows, :], hd2, hba, hbx)
        a_t = _split_time(a, slab_ref, 3 * s)
        b_t = _split_time(b, slab_ref, 3 * s + 1)
        hs = [None] * SUBLANES
        for r in (range(SUBLANES - 1, -1, -1) if reverse else range(SUBLANES)):
            h = a_t[r] * h + b_t[r]
            hs[r] = h
        emit(rows, _join_time(hs, slab_ref, 3 * s + 2))
        return h

    def group(i, h):
        for s in range(TILES_PER_STEP):
            h = tile(TILES_PER_STEP * i + s, s, h)
        return h

    _gate_maps(uc_ref, wg_ref, ra_ref, ix_ref, slice(0, ts))
    carry_ref[...] = lax.fori_loop(0, nsteps, group, carry_ref[...])


def _lru_kernel(reverse, uc_ref, lam_ref, wg_ref, ba_ref, bx_ref, h_ref, ra_ref, ix_ref, carry_ref,
                slab_ref):
    @pl.when(pl.program_id(1) == 0)
    def _():
        carry_ref[...] = jnp.zeros_like(carry_ref)

    def emit(rows, h):
        h_ref[rows, :] = h

    _recurrence(reverse, uc_ref, wg_ref, ra_ref, ix_ref, lam_ref, ba_ref, bx_ref, carry_ref, slab_ref,
                emit)


def _lru(reverse, uc3, lam, wg, ba, bx):
    bsz, s, _ = uc3.shape
    ts = TS_LRU
    nblk = s // ts
    blk = pl.BlockSpec((None, ts, D_MODEL), lambda b, j: (b, (nblk - 1 - j) if reverse else j, 0))
    return pl.pallas_call(
        functools.partial(_lru_kernel, reverse),
        grid=(bsz, nblk),
        in_specs=[blk, _resident((1, D_MODEL)), _resident(wg.shape),
                  _resident((1, D_MODEL)), _resident((1, D_MODEL))],
        out_specs=blk,
        out_shape=jax.ShapeDtypeStruct((bsz, s, D_MODEL), F32),
        scratch_shapes=[
            pltpu.VMEM((ts, D_MODEL), F32),
            pltpu.VMEM((ts, D_MODEL), F32),
            pltpu.VMEM((LANE_TILES, LANES), F32),
            pltpu.VMEM((N_SLABS, SLAB, LANES), F32),
        ],
        compiler_params=pltpu.CompilerParams(
            dimension_semantics=("arbitrary", "arbitrary"), vmem_limit_bytes=VMEM_LIMIT_BYTES),
        name="lru_bwd" if reverse else "lru_fwd",
    )(uc3, lam, wg, ba, bx)


_N_BIAS = 3 * N_HEADS + 1
_MASKED_TILE = 3 * N_HEADS


def _swap_lane_halves(x):
    packed = pltpu.bitcast(x, jnp.uint32)
    return pltpu.bitcast(pltpu.roll(packed, HEAD_DIM, axis=1), BF16)


def _attn_kernel(nb, sink_ref, q_ref, kp_ref, km_ref, kn_ref, vp_ref, vm_ref, vn_ref,
                 wo_ref, wi_ref, wd_ref, o_ref, wob_ref, wib_ref, wdb_ref,
                 kz_ref, vab_ref, bias_ref, s_ref, m_ref, p_ref):
    tq = q_ref.shape[0]
    tk = tq + 2 * BLOCK
    nsub = tq // BLOCK
    j = pl.program_id(1)

    wob_ref[...] = wo_ref[...].astype(BF16)
    wdb_ref[...] = wd_ref[...].astype(BF16)
    gate_col = lax.broadcasted_iota(jnp.int32, (1, 2 * D_FF), 1) < D_FF
    wib_ref[...] = (wi_ref[...] * jnp.where(gate_col, 0.5, 1.0)).astype(BF16)

    @pl.when((pl.program_id(0) == 0) & (j == 0))
    def _():
        row = lax.broadcasted_iota(jnp.int32, (BLOCK, BLOCK), 0)
        col = lax.broadcasted_iota(jnp.int32, (BLOCK, BLOCK), 1)
        bias_ref[_MASKED_TILE] = jnp.ones((BLOCK, BLOCK), F32)
        for c in range(3):
            dist = jnp.abs(row + (1 - c) * BLOCK - col)
            absd = dist.astype(F32)
            for h in range(N_HEADS):
                bias_ref[3 * h + c] = jnp.where(dist <= WINDOW, (-_ALIBI_SLOPES[h] * absd) * LOG2_E, 1.0)

    low = lax.broadcasted_iota(jnp.int32, (tk, LANES), 1) < HEAD_DIM
    low_q = lax.broadcasted_iota(jnp.int32, (BLOCK, LANES), 1) < HEAD_DIM
    ones_q = [jnp.where(low_q, 1.0, 0.0).astype(BF16), jnp.where(low_q, 0.0, 1.0).astype(BF16)]
    for t in range(KV_W // LANES):
        cols = slice(t * LANES, (t + 1) * LANES)
        full = jnp.concatenate([r[:, cols] for r in (kp_ref, km_ref, kn_ref)], axis=0)
        swapped = _swap_lane_halves(full)
        zero = jnp.zeros_like(full)
        kz_ref[4 * t + 0] = jnp.where(low, full, zero)
        kz_ref[4 * t + 1] = jnp.where(low, zero, swapped)
        kz_ref[4 * t + 2] = jnp.where(low, swapped, zero)
        kz_ref[4 * t + 3] = jnp.where(low, zero, full)
        nkb = tk // BLOCK
        for b in range(nkb):
            if b == 0:
                src = vp_ref[:, cols]
            elif b == nkb - 1:
                src = vn_ref[:, cols]
            else:
                src = vm_ref[(b - 1) * BLOCK:b * BLOCK, cols]
            swapped = _swap_lane_halves(src)
            zero = jnp.zeros_like(src)
            placed = [jnp.where(low_q, src, zero), jnp.where(low_q, zero, swapped),
                      jnp.where(low_q, swapped, zero), jnp.where(low_q, zero, src)]
            for i in range(4):
                g, half = 2 * t + i // 2, i % 2
                dst = slice((2 * b + half) * BLOCK, (2 * b + half + 1) * BLOCK)
                vab_ref[g, dst, :LANES] = placed[i]
                vab_ref[g, dst, LANES:] = ones_q[half]

    def scores(qi, slot):
        q0 = pl.multiple_of(qi * BLOCK, BLOCK)
        blk = j * nsub + qi
        win = pl.ds(q0, 3 * BLOCK)
        sc2 = [None, None]
        for h in range(N_HEADS):
            g, pp, half = h // GROUP, (h % GROUP) // 2, h % 2
            if pp == 0:
                q2 = jnp.concatenate(
                    [q_ref[pl.ds(q0, BLOCK), (2 * g + i) * LANES:(2 * g + i + 1) * LANES] for i in range(2)],
                    axis=0)
                sc2[half] = lax.dot_general(q2, kz_ref[2 * g + half, win, :], (((1,), (1,)), ((), ())),
                                            preferred_element_type=F32)
            sc = sc2[half][pp * BLOCK:(pp + 1) * BLOCK]
            tiles = []
            for c in range(3):
                t = sc[:, c * BLOCK:(c + 1) * BLOCK]
                if c == 1:
                    t = t + bias_ref[3 * h + 1]
                else:
                    edge = (blk == 0) if c == 0 else (blk == nb - 1)
                    b = bias_ref[jnp.where(edge, _MASKED_TILE, 3 * h + c)]
                    t = jnp.where(b > 0.0, NEG_INF, t + b)
                s_ref[slot, h, :, c * BLOCK:(c + 1) * BLOCK] = t
                tiles.append(t)
            mx = jnp.max(jnp.maximum(jnp.maximum(tiles[0], tiles[1]), tiles[2]), axis=-1, keepdims=True)
            m_ref[slot, h] = jnp.broadcast_to(jnp.maximum(mx, sink_ref[h] * LOG2_E), (BLOCK, LANES))

    def finish(qi, slot):
        q0 = pl.multiple_of(qi * BLOCK, BLOCK)
        win2 = pl.ds(pl.multiple_of(qi * (2 * BLOCK), 2 * BLOCK), 6 * BLOCK)
        for h in range(N_HEADS):
            m = m_ref[slot, h]
            for c in range(3):
                cs = slice(c * BLOCK, (c + 1) * BLOCK)
                dst = slice((2 * c + h % 2) * BLOCK, (2 * c + h % 2 + 1) * BLOCK)
                prow = slice(((h // 2) % 2) * BLOCK, ((h // 2) % 2 + 1) * BLOCK)
                p_ref[slot, h // GROUP, prow, dst] = jnp.exp2(s_ref[slot, h, :, cs] - m).astype(BF16)
        for g in range(N_KV_HEADS):
            o2 = jnp.dot(p_ref[slot, g], vab_ref[g, win2, :], preferred_element_type=F32)
            for pp in range(2):
                pr = 2 * g + pp
                h0, h1 = 2 * pr, 2 * pr + 1
                o = o2[pp * BLOCK:(pp + 1) * BLOCK]
                e = jnp.where(low_q, jnp.exp2(sink_ref[h0] * LOG2_E - m_ref[slot, h0]),
                              jnp.exp2(sink_ref[h1] * LOG2_E - m_ref[slot, h1]))
                o_ref[pl.ds(q0, BLOCK), pr * LANES:(pr + 1) * LANES] = (
                    o[:, :LANES] * (1.0 / (o[:, LANES:] + e)))

    scores(0, 0)

    def two_sub_blocks(i, _):
        qa = 2 * i
        scores(qa + 1, 1)
        finish(qa, 0)
        scores(jnp.minimum(qa + 2, nsub - 1), 0)
        finish(qa + 1, 1)
        return 0

    lax.fori_loop(0, nsub // 2, two_sub_blocks, 0)


def _attention(q3, k3, v3, sink, ffn_weights):
    bsz, s, _ = q3.shape
    tq = TQ_ATTN
    nq = s // tq
    hb = tq // BLOCK
    nb = s // BLOCK
    tk = tq + 2 * BLOCK
    kv_main = pl.BlockSpec((None, tq, KV_W), lambda b, j: (b, j, 0))
    kv_prev = pl.BlockSpec((None, BLOCK, KV_W), lambda b, j: (b, jnp.maximum(j * hb - 1, 0), 0))
    kv_next = pl.BlockSpec((None, BLOCK, KV_W), lambda b, j: (b, jnp.minimum((j + 1) * hb, nb - 1), 0))
    qo = pl.BlockSpec((None, tq, Q_W), lambda b, j: (b, j, 0))
    nsteps = bsz * nq
    w_slice = lambda w: pl.BlockSpec((w.shape[0] // nsteps, w.shape[1]), lambda b, j: (b * nq + j, 0))
    w_specs = [w_slice(w) for w in ffn_weights]
    return pl.pallas_call(
        functools.partial(_attn_kernel, nb),
        grid=(bsz, nq),
        in_specs=[pl.BlockSpec(memory_space=pltpu.SMEM), qo,
                  kv_prev, kv_main, kv_next, kv_prev, kv_main, kv_next] + w_specs,
        out_specs=[qo] + w_specs,
        out_shape=[jax.ShapeDtypeStruct((bsz, s, Q_W), F32)]
        + [jax.ShapeDtypeStruct(w.shape, BF16) for w in ffn_weights],
        scratch_shapes=[
            pltpu.VMEM((2 * N_KV_HEADS, tk, LANES), BF16),
            pltpu.VMEM((N_KV_HEADS, 2 * tk, 2 * LANES), BF16),
            pltpu.VMEM((_N_BIAS, BLOCK, BLOCK), F32),
            pltpu.VMEM((2, N_HEADS, BLOCK, 3 * BLOCK), F32),
            pltpu.VMEM((2, N_HEADS, BLOCK, LANES), F32),
            pltpu.VMEM((2, N_KV_HEADS, 2 * BLOCK, 6 * BLOCK), BF16),
        ],
        compiler_params=pltpu.CompilerParams(
            dimension_semantics=("arbitrary", "arbitrary"), vmem_limit_bytes=VMEM_LIMIT_BYTES),
        name="attention",
    )(sink, q3, k3, k3, k3, v3, v3, v3, *ffn_weights)


def _ffn_kernel(x_ref, hf_ref, hb_ref, yb_ref, coef_ref, wo_ref, g2_ref, wi_ref, wd_ref, g3_ref, o_ref):
    merged = coef_ref[:, :D_MODEL] * (hf_ref[...] + hb_ref[...]) + coef_ref[:, D_MODEL:] * yb_ref[...]
    x1 = x_ref[...] + jnp.dot(merged.astype(BF16), wo_ref[...], preferred_element_type=F32)
    xn = _rmsnorm(x1, g2_ref[...]).astype(BF16)
    gate = jnp.dot(xn, wi_ref[:, :D_FF], preferred_element_type=F32)
    up = jnp.dot(xn, wi_ref[:, D_FF:], preferred_element_type=F32)
    act = ((gate * jnp.tanh(gate) + gate) * up).astype(BF16)
    x2 = x1 + jnp.dot(act, wd_ref[...], preferred_element_type=F32)
    o_ref[...] = _rmsnorm(x2, g3_ref[...])


def _ffn(x2, hf, hb, yb, coef, wo, g2, wi, wd, g3):
    t = x2.shape[0]
    tm = TM_FFN
    row = lambda w_: pl.BlockSpec((tm, w_), lambda i: (i, 0))
    return pl.pallas_call(
        _ffn_kernel,
        grid=(t // tm,),
        in_specs=[row(D_MODEL), row(D_MODEL), row(D_MODEL), row(D_MODEL), row(2 * D_MODEL),
                  _resident((D_MODEL, D_MODEL)), _resident((1, D_MODEL)),
                  _resident((D_MODEL, 2 * D_FF)), _resident((D_FF, D_MODEL)), _resident((1, D_MODEL))],
        out_specs=row(D_MODEL),
        out_shape=jax.ShapeDtypeStruct((t, D_MODEL), F32),
        compiler_params=pltpu.CompilerParams(
            dimension_semantics=("arbitrary",), vmem_limit_bytes=VMEM_LIMIT_BYTES),
        name="merge_ffn",
    )(x2, hf, hb, yb, coef, wo, g2, wi, wd, g3)


def _gate_weights(wa, wx):
    def blockdiag(w):
        w = w.reshape(N_GROUPS, LRU_GROUP // LRU_BLOCK, LRU_BLOCK, LRU_BLOCK)
        eye = jnp.eye(LRU_GROUP // LRU_BLOCK, dtype=w.dtype)
        return jnp.einsum("ghij,hk->ghikj", w, eye).reshape(N_GROUPS, LRU_GROUP, LRU_GROUP)
    return jnp.concatenate([blockdiag(wa), blockdiag(wx)], axis=-1).astype(BF16)


def kernel(x, norm_mix_g, w_in, b_gate, conv_w, conv_b, lru_lambda, lru_wa, lru_ba, lru_wx, lru_bx,
           attn_sink, w_out, norm_ffn_g, w_ffn_in, w_ffn_out, norm_final_g):
    bsz, s, d = x.shape
    depth = w_in.shape[0]
    assert depth == 1, "the merge/ffn kernel applies the final RMSNorm, so it must be the last layer"
    t = bsz * s
    row = lambda a: a.reshape(1, -1)
    time_major = lambda a: a.reshape(a.shape[:-1] + (LANE_TILES, LANES))
    x2 = x.reshape(t, d)
    for l in range(depth):
        huc, q, k, v, merge_coef = _in_proj(
            x2, s, row(norm_mix_g[l]), w_in[l].astype(BF16), row(b_gate[l]),
            time_major(0.5 * conv_w[l]), time_major(0.5 * conv_b[l]))
        seq = lambda a: a.reshape(bsz, s, a.shape[-1])
        lru = lambda dr: (row(lru_lambda[l, dr]), _gate_weights(lru_wa[l, dr], lru_wx[l, dr]),
                          row(lru_ba[l, dr]), row(lru_bx[l, dr]))
        h_fwd = _lru(False, seq(huc), *lru(0))
        h_bwd = _lru(True, seq(huc), *lru(1))
        y_b, wo_b, wi_b, wd_b = _attention(seq(q), seq(k), seq(v), attn_sink[l],
                                           (w_out[l], w_ffn_in[l], w_ffn_out[l]))
        x2 = _ffn(x2, h_fwd.reshape(t, d), h_bwd.reshape(t, d), y_b.reshape(t, d), merge_coef,
                  wo_b, row(norm_ffn_g[l]), wi_b, wd_b, row(norm_final_g))
    return x2.reshape(bsz, s, d)
```

```python
import functools
import math

import numpy as np
import jax
import jax.numpy as jnp
from jax import lax
from jax.experimental import pallas as pl
from jax.experimental.pallas import tpu as pltpu

F32 = jnp.float32
BF16 = jnp.bfloat16

D_MODEL = 1024
LRU_HEADS = 16
LRU_BLOCK = D_MODEL // LRU_HEADS
CONV_WIDTH = 4
CONV_LEFT = 2
RGLRU_C = 8.0
N_HEADS = 16
N_KV_HEADS = 4
HEAD_DIM = 64
GROUP = N_HEADS // N_KV_HEADS
WINDOW = 128
BLOCK = 128
D_FF = 2816
Q_W = N_HEADS * HEAD_DIM
KV_W = N_KV_HEADS * HEAD_DIM
IN_W = 2 * D_MODEL + Q_W + 2 * KV_W + 2 * D_MODEL
EPS = 1e-6
NEG_INF = -1e30

LANES = 128
SUBLANES = 8
VMEM_LIMIT_BYTES = 56 * 1024 * 1024

TM_PROJ = 512
TS_LRU = 1024
TQ_ATTN = 1024
TM_FFN = 512
LRU_GROUP = 256
N_GROUPS = D_MODEL // LRU_GROUP
LANE_TILES = D_MODEL // LANES
assert LANE_TILES == SUBLANES
SLAB = SUBLANES * LANE_TILES
TILES_PER_STEP = 32
N_SLABS = 3 * TILES_PER_STEP
CONV_RING = 8

_ALIBI_SLOPES = [float(v) for v in np.exp2(
    -8.0 * (np.arange(N_HEADS, dtype=np.float32) + 1.0) / N_HEADS).astype(np.float32)]
LOG2_E = math.log2(math.e)
Q_SCALE = HEAD_DIM ** -0.5 * LOG2_E


def _sigmoid(x):
    return 0.5 * jnp.tanh(0.5 * x) + 0.5


def _rmsnorm(x, g):
    return x * lax.rsqrt(jnp.mean(x * x, axis=-1, keepdims=True) + EPS) * g


def _resident(shape):
    nd = len(shape)
    return pl.BlockSpec(shape, lambda *_: (0,) * nd, pipeline_mode=pl.Buffered(1))


def _in_proj_kernel(nblk, x_ref, xp_ref, xnx_ref, g_ref, w_ref, bg_ref, cw_ref, cb_ref,
                    uc_ref, q_ref, k_ref, v_ref, coef_ref, uext_ref, slab_ref):
    tm = x_ref.shape[0]
    nt = tm // SUBLANES
    jb = pl.program_id(0) % nblk
    gn = g_ref[...]
    xn = _rmsnorm(x_ref[...], gn).astype(BF16)
    halo = jnp.concatenate([xp_ref[...], xnx_ref[...]], axis=0)
    xh = _rmsnorm(halo, gn).astype(BF16)

    def proj(lo, hi):
        return jnp.dot(xn, w_ref[:, lo:hi], preferred_element_type=F32)

    ue = jnp.dot(jnp.concatenate([xn, xh], axis=0), w_ref[:, :D_MODEL], preferred_element_type=F32)
    uext_ref[SUBLANES:SUBLANES + tm, :] = ue[:tm]
    uext_ref[0:SUBLANES, :] = jnp.where(jb > 0, ue[tm:tm + SUBLANES], 0.0)
    uext_ref[SUBLANES + tm:, :] = jnp.where(jb < nblk - 1, ue[tm + SUBLANES:], 0.0)

    o_g = D_MODEL
    o_q = o_g + D_MODEL
    o_k = o_q + Q_W
    o_v = o_k + KV_W
    o_z = o_v + KV_W
    pw = 2 * LANES
    todo = []

    def coef_a(c):
        gate = _sigmoid(proj(o_z + c, o_z + c + pw) + bg_ref[:, c:c + pw])
        coef_ref[:, c:c + pw] = gate * _gelu_tanh(proj(o_g + c, o_g + c + pw))

    def coef_b(c):
        cz = D_MODEL + c
        coef_ref[:, cz:cz + pw] = _sigmoid(proj(o_z + cz, o_z + cz + pw) + bg_ref[:, cz:cz + pw])

    def plain(dst, lo, post):
        def run(c):
            dst[:, c:c + pw] = post(proj(lo + c, lo + c + pw))
        return run

    for fn, width in ((coef_a, D_MODEL), (coef_b, D_MODEL),
                      (plain(q_ref, o_q, lambda v: (v * Q_SCALE).astype(BF16)), Q_W),
                      (plain(k_ref, o_k, lambda v: v.astype(BF16)), KV_W),
                      (plain(v_ref, o_v, lambda v: v.astype(BF16)), KV_W)):
        todo += [(fn, c) for c in range(0, width, pw)]
    every = -(-nt // len(todo))

    def project_block():
        fn, c = todo.pop(0)
        fn(c)

    w = [cw_ref[k] for k in range(CONV_WIDTH)]
    cb = cb_ref[...]
    ring = CONV_RING

    def ext_steps(e, s):
        return _split_time(uext_ref[e * SUBLANES:(e + 1) * SUBLANES, :], slab_ref, s)

    tail = ext_steps(0, 0)
    prev2, prev1 = tail[SUBLANES - 2], tail[SUBLANES - 1]
    cur = ext_steps(1, 1)
    for t in range(nt):
        if t % every == 0 and todo:
            project_block()
        nxt = ext_steps(t + 2, 2 * (t % ring))
        x = [prev2, prev1] + cur + [nxt[0]]
        out = [(x[r] * w[0] + x[r + 1] * w[1]) + (x[r + 2] * w[2] + x[r + 3] * w[3]) + cb
               for r in range(SUBLANES)]
        uc_ref[t * SUBLANES:(t + 1) * SUBLANES, :] = _join_time(out, slab_ref, 2 * (t % ring) + 1)
        prev2, prev1, cur = cur[SUBLANES - 2], cur[SUBLANES - 1], nxt
    while todo:
        project_block()


def _in_proj(x2, seq, g, w, bg, cw, cb):
    t = x2.shape[0]
    tm = TM_PROJ
    nblk = seq // tm
    hb = tm // SUBLANES
    nh = t // SUBLANES
    row = lambda w_: pl.BlockSpec((tm, w_), lambda i: (i, 0))
    prev = pl.BlockSpec((SUBLANES, D_MODEL), lambda i: (jnp.maximum(i * hb - 1, 0), 0))
    nxt = pl.BlockSpec((SUBLANES, D_MODEL), lambda i: (jnp.minimum((i + 1) * hb, nh - 1), 0))
    return pl.pallas_call(
        functools.partial(_in_proj_kernel, nblk),
        grid=(t // tm,),
        in_specs=[row(D_MODEL), prev, nxt, _resident((1, D_MODEL)), _resident((D_MODEL, IN_W)),
                  _resident((1, 2 * D_MODEL)), _resident(cw.shape), _resident(cb.shape)],
        out_specs=[row(D_MODEL), row(Q_W), row(KV_W), row(KV_W), row(2 * D_MODEL)],
        out_shape=[
            jax.ShapeDtypeStruct((t, D_MODEL), F32),
            jax.ShapeDtypeStruct((t, Q_W), BF16),
            jax.ShapeDtypeStruct((t, KV_W), BF16),
            jax.ShapeDtypeStruct((t, KV_W), BF16),
            jax.ShapeDtypeStruct((t, 2 * D_MODEL), F32),
        ],
        scratch_shapes=[
            pltpu.VMEM((tm + 2 * SUBLANES, D_MODEL), F32),
            pltpu.VMEM((2 * CONV_RING, SLAB, LANES), F32),
        ],
        compiler_params=pltpu.CompilerParams(
            dimension_semantics=("arbitrary",), vmem_limit_bytes=VMEM_LIMIT_BYTES),
        name="in_proj",
    )(x2, x2, x2, g, w, bg, cw, cb)


def _split_time(tile, slab_ref, s):
    for j in range(LANE_TILES):
        slab_ref[s, j * SUBLANES:(j + 1) * SUBLANES, :] = tile[:, j * LANES:(j + 1) * LANES]
    return [slab_ref[s, pl.ds(r, LANE_TILES, stride=SUBLANES), :] for r in range(SUBLANES)]


def _join_time(steps, slab_ref, s):
    for r in range(SUBLANES):
        slab_ref[s, pl.ds(r, LANE_TILES, stride=SUBLANES), :] = steps[r]
    return jnp.concatenate(
        [slab_ref[s, j * SUBLANES:(j + 1) * SUBLANES, :] for j in range(LANE_TILES)], axis=1)


def _half_decay_log2(lam):
    return (-0.5 * RGLRU_C * math.log2(math.e)) * (
        jnp.maximum(-lam, 0.0) + jnp.log1p(jnp.exp(-jnp.abs(lam))))


def _gate_terms(pre_r, pre_i, huc, hd2, hba, hbx):
    a = jnp.exp2(hd2 * jnp.tanh(pre_r + hba) + hd2)
    y = jnp.maximum(1.0 - a * a, 0.0)
    beta = jnp.where(y > 0.0, y * lax.rsqrt(y), 0.0)
    return a, beta * (jnp.tanh(pre_i + hbx) * huc + huc)


def _gelu_tanh(g):
    c = math.sqrt(2.0 / math.pi)
    half = 0.5 * g
    return half * jnp.tanh(g * ((c * 0.044715) * (g * g) + c)) + half


def _gate_maps(uc_ref, wg_ref, ra_ref, ix_ref, rows):
    for gi in range(N_GROUPS):
        cols = slice(gi * LRU_GROUP, (gi + 1) * LRU_GROUP)
        pre = jnp.dot(uc_ref[rows, cols].astype(BF16), wg_ref[gi], preferred_element_type=F32)
        ra_ref[rows, cols] = pre[:, :LRU_GROUP]
        ix_ref[rows, cols] = pre[:, LRU_GROUP:]


def _recurrence(reverse, uc_ref, wg_ref, ra_ref, ix_ref, lam_ref, ba_ref, bx_ref, carry_ref, slab_ref,
                emit):
    ts = uc_ref.shape[0]
    nt = ts // SUBLANES
    nsteps = nt // TILES_PER_STEP
    hd2 = _half_decay_log2(lam_ref[...])
    hba = 0.5 * ba_ref[...]
    hbx = 0.5 * bx_ref[...]

    def tile(k, s, h):
        kk = (nt - 1 - k) if reverse else k
        rows = pl.ds(pl.multiple_of(kk * SUBLANES, SUBLANES), SUBLANES)
        a, b = _gate_terms(ra_ref[rows, :], ix_ref[rows, :], uc_ref[rows, :], hd2, hba, hbx)
        a_t = _split_time(a, slab_ref, 3 * s)
        b_t = _split_time(b, slab_ref, 3 * s + 1)
        hs = [None] * SUBLANES
        for r in (range(SUBLANES - 1, -1, -1) if reverse else range(SUBLANES)):
            h = a_t[r] * h + b_t[r]
            hs[r] = h
        emit(rows, _join_time(hs, slab_ref, 3 * s + 2))
        return h

    def group(i, h):
        for s in range(TILES_PER_STEP):
            h = tile(TILES_PER_STEP * i + s, s, h)
        return h

    _gate_maps(uc_ref, wg_ref, ra_ref, ix_ref, slice(0, ts))
    carry_ref[...] = lax.fori_loop(0, nsteps, group, carry_ref[...])


def _lru_kernel(reverse, uc_ref, lam_ref, wg_ref, ba_ref, bx_ref, h_ref, ra_ref, ix_ref, carry_ref,
                slab_ref):
    @pl.when(pl.program_id(1) == 0)
    def _():
        carry_ref[...] = jnp.zeros_like(carry_ref)

    def emit(rows, h):
        h_ref[rows, :] = h

    _recurrence(reverse, uc_ref, wg_ref, ra_ref, ix_ref, lam_ref, ba_ref, bx_ref, carry_ref, slab_ref,
                emit)


def _lru(reverse, uc3, lam, wg, ba, bx):
    bsz, s, _ = uc3.shape
    ts = TS_LRU
    nblk = s // ts
    blk = pl.BlockSpec((None, ts, D_MODEL), lambda b, j: (b, (nblk - 1 - j) if reverse else j, 0))
    return pl.pallas_call(
        functools.partial(_lru_kernel, reverse),
        grid=(bsz, nblk),
        in_specs=[blk, _resident((1, D_MODEL)), _resident(wg.shape),
                  _resident((1, D_MODEL)), _resident((1, D_MODEL))],
        out_specs=blk,
        out_shape=jax.ShapeDtypeStruct((bsz, s, D_MODEL), F32),
        scratch_shapes=[
            pltpu.VMEM((ts, D_MODEL), F32),
            pltpu.VMEM((ts, D_MODEL), F32),
            pltpu.VMEM((LANE_TILES, LANES), F32),
            pltpu.VMEM((N_SLABS, SLAB, LANES), F32),
        ],
        compiler_params=pltpu.CompilerParams(
            dimension_semantics=("arbitrary", "arbitrary"), vmem_limit_bytes=VMEM_LIMIT_BYTES),
        name="lru_bwd" if reverse else "lru_fwd",
    )(uc3, lam, wg, ba, bx)


_N_BIAS = 3 * N_HEADS + 1
_MASKED_TILE = 3 * N_HEADS


def _attn_kernel(nb, sink_ref, q_ref, kp_ref, km_ref, kn_ref, vp_ref, vm_ref, vn_ref,
                 wo_ref, wi_ref, wd_ref, o_ref, wob_ref, wib_ref, wdb_ref,
                 kz_ref, vab_ref, bias_ref, s_ref, m_ref, p_ref):
    tq = q_ref.shape[0]
    tk = tq + 2 * BLOCK
    nsub = tq // BLOCK
    j = pl.program_id(1)

    wob_ref[...] = wo_ref[...].astype(BF16)
    wdb_ref[...] = wd_ref[...].astype(BF16)
    gate_col = lax.broadcasted_iota(jnp.int32, (1, 2 * D_FF), 1) < D_FF
    wib_ref[...] = (wi_ref[...] * jnp.where(gate_col, 0.5, 1.0)).astype(BF16)

    @pl.when((pl.program_id(0) == 0) & (j == 0))
    def _():
        row = lax.broadcasted_iota(jnp.int32, (BLOCK, BLOCK), 0)
        col = lax.broadcasted_iota(jnp.int32, (BLOCK, BLOCK), 1)
        bias_ref[_MASKED_TILE] = jnp.ones((BLOCK, BLOCK), F32)
        for c in range(3):
            dist = jnp.abs(row + (1 - c) * BLOCK - col)
            absd = dist.astype(F32)
            for h in range(N_HEADS):
                bias_ref[3 * h + c] = jnp.where(dist <= WINDOW, (-_ALIBI_SLOPES[h] * absd) * LOG2_E, 1.0)

    low = lax.broadcasted_iota(jnp.int32, (tk, LANES), 1) < HEAD_DIM
    low_q = lax.broadcasted_iota(jnp.int32, (BLOCK, LANES), 1) < HEAD_DIM
    ones_q = [jnp.where(low_q, 1.0, 0.0).astype(BF16), jnp.where(low_q, 0.0, 1.0).astype(BF16)]
    for t in range(KV_W // LANES):
        cols = slice(t * LANES, (t + 1) * LANES)
        full = jnp.concatenate([r[:, cols] for r in (kp_ref, km_ref, kn_ref)], axis=0).astype(F32)
        swapped = pltpu.roll(full, HEAD_DIM, axis=1)
        kz_ref[4 * t + 0] = jnp.where(low, full, 0.0).astype(BF16)
        kz_ref[4 * t + 1] = jnp.where(low, 0.0, swapped).astype(BF16)
        kz_ref[4 * t + 2] = jnp.where(low, swapped, 0.0).astype(BF16)
        kz_ref[4 * t + 3] = jnp.where(low, 0.0, full).astype(BF16)
        nkb = tk // BLOCK
        for b in range(nkb):
            if b == 0:
                src = vp_ref[:, cols]
            elif b == nkb - 1:
                src = vn_ref[:, cols]
            else:
                src = vm_ref[(b - 1) * BLOCK:b * BLOCK, cols]
            full = src.astype(F32)
            swapped = pltpu.roll(full, HEAD_DIM, axis=1)
            placed = [jnp.where(low_q, full, 0.0), jnp.where(low_q, 0.0, swapped),
                      jnp.where(low_q, swapped, 0.0), jnp.where(low_q, 0.0, full)]
            for i in range(4):
                g, half = 2 * t + i // 2, i % 2
                dst = slice((2 * b + half) * BLOCK, (2 * b + half + 1) * BLOCK)
                vab_ref[g, dst, :LANES] = placed[i].astype(BF16)
                vab_ref[g, dst, LANES:] = ones_q[half]

    def scores(qi, slot):
        q0 = pl.multiple_of(qi * BLOCK, BLOCK)
        blk = j * nsub + qi
        win = pl.ds(q0, 3 * BLOCK)
        sc2 = [None, None]
        for h in range(N_HEADS):
            g, pp, half = h // GROUP, (h % GROUP) // 2, h % 2
            if pp == 0:
                q2 = jnp.concatenate(
                    [q_ref[pl.ds(q0, BLOCK), (2 * g + i) * LANES:(2 * g + i + 1) * LANES] for i in range(2)],
                    axis=0)
                sc2[half] = lax.dot_general(q2, kz_ref[2 * g + half, win, :], (((1,), (1,)), ((), ())),
                                            preferred_element_type=F32)
            sc = sc2[half][pp * BLOCK:(pp + 1) * BLOCK]
            tiles = []
            for c in range(3):
                t = sc[:, c * BLOCK:(c + 1) * BLOCK]
                if c == 1:
                    t = t + bias_ref[3 * h + 1]
                else:
                    edge = (blk == 0) if c == 0 else (blk == nb - 1)
                    b = bias_ref[jnp.where(edge, _MASKED_TILE, 3 * h + c)]
                    t = jnp.where(b > 0.0, NEG_INF, t + b)
                s_ref[slot, h, :, c * BLOCK:(c + 1) * BLOCK] = t
                tiles.append(t)
            mx = jnp.max(jnp.maximum(jnp.maximum(tiles[0], tiles[1]), tiles[2]), axis=-1, keepdims=True)
            m_ref[slot, h] = jnp.broadcast_to(jnp.maximum(mx, sink_ref[h] * LOG2_E), (BLOCK, LANES))

    def finish(qi, slot):
        q0 = pl.multiple_of(qi * BLOCK, BLOCK)
        win2 = pl.ds(pl.multiple_of(qi * (2 * BLOCK), 2 * BLOCK), 6 * BLOCK)
        for h in range(N_HEADS):
            m = m_ref[slot, h]
            for c in range(3):
                cs = slice(c * BLOCK, (c + 1) * BLOCK)
                dst = slice((2 * c + h % 2) * BLOCK, (2 * c + h % 2 + 1) * BLOCK)
                prow = slice(((h // 2) % 2) * BLOCK, ((h // 2) % 2 + 1) * BLOCK)
                p_ref[slot, h // GROUP, prow, dst] = jnp.exp2(s_ref[slot, h, :, cs] - m).astype(BF16)
        for g in range(N_KV_HEADS):
            o2 = jnp.dot(p_ref[slot, g], vab_ref[g, win2, :], preferred_element_type=F32)
            for pp in range(2):
                pr = 2 * g + pp
                h0, h1 = 2 * pr, 2 * pr + 1
                o = o2[pp * BLOCK:(pp + 1) * BLOCK]
                e = jnp.where(low_q, jnp.exp2(sink_ref[h0] * LOG2_E - m_ref[slot, h0]),
                              jnp.exp2(sink_ref[h1] * LOG2_E - m_ref[slot, h1]))
                o_ref[pl.ds(q0, BLOCK), pr * LANES:(pr + 1) * LANES] = (
                    o[:, :LANES] * (1.0 / (o[:, LANES:] + e)))

    scores(0, 0)

    def two_sub_blocks(i, _):
        qa = 2 * i
        scores(qa + 1, 1)
        finish(qa, 0)
        scores(jnp.minimum(qa + 2, nsub - 1), 0)
        finish(qa + 1, 1)
        return 0

    lax.fori_loop(0, nsub // 2, two_sub_blocks, 0)


def _attention(q3, k3, v3, sink, ffn_weights):
    bsz, s, _ = q3.shape
    tq = TQ_ATTN
    nq = s // tq
    hb = tq // BLOCK
    nb = s // BLOCK
    tk = tq + 2 * BLOCK
    kv_main = pl.BlockSpec((None, tq, KV_W), lambda b, j: (b, j, 0))
    kv_prev = pl.BlockSpec((None, BLOCK, KV_W), lambda b, j: (b, jnp.maximum(j * hb - 1, 0), 0))
    kv_next = pl.BlockSpec((None, BLOCK, KV_W), lambda b, j: (b, jnp.minimum((j + 1) * hb, nb - 1), 0))
    qo = pl.BlockSpec((None, tq, Q_W), lambda b, j: (b, j, 0))
    nsteps = bsz * nq
    w_slice = lambda w: pl.BlockSpec((w.shape[0] // nsteps, w.shape[1]), lambda b, j: (b * nq + j, 0))
    w_specs = [w_slice(w) for w in ffn_weights]
    return pl.pallas_call(
        functools.partial(_attn_kernel, nb),
        grid=(bsz, nq),
        in_specs=[pl.BlockSpec(memory_space=pltpu.SMEM), qo,
                  kv_prev, kv_main, kv_next, kv_prev, kv_main, kv_next] + w_specs,
        out_specs=[qo] + w_specs,
        out_shape=[jax.ShapeDtypeStruct((bsz, s, Q_W), F32)]
        + [jax.ShapeDtypeStruct(w.shape, BF16) for w in ffn_weights],
        scratch_shapes=[
            pltpu.VMEM((2 * N_KV_HEADS, tk, LANES), BF16),
            pltpu.VMEM((N_KV_HEADS, 2 * tk, 2 * LANES), BF16),
            pltpu.VMEM((_N_BIAS, BLOCK, BLOCK), F32),
            pltpu.VMEM((2, N_HEADS, BLOCK, 3 * BLOCK), F32),
            pltpu.VMEM((2, N_HEADS, BLOCK, LANES), F32),
            pltpu.VMEM((2, N_KV_HEADS, 2 * BLOCK, 6 * BLOCK), BF16),
        ],
        compiler_params=pltpu.CompilerParams(
            dimension_semantics=("arbitrary", "arbitrary"), vmem_limit_bytes=VMEM_LIMIT_BYTES),
        name="attention",
    )(sink, q3, k3, k3, k3, v3, v3, v3, *ffn_weights)


def _ffn_kernel(x_ref, hf_ref, hb_ref, yb_ref, coef_ref, wo_ref, g2_ref, wi_ref, wd_ref, g3_ref, o_ref):
    merged = coef_ref[:, :D_MODEL] * (hf_ref[...] + hb_ref[...]) + coef_ref[:, D_MODEL:] * yb_ref[...]
    x1 = x_ref[...] + jnp.dot(merged.astype(BF16), wo_ref[...], preferred_element_type=F32)
    xn = _rmsnorm(x1, g2_ref[...]).astype(BF16)
    gate = jnp.dot(xn, wi_ref[:, :D_FF], preferred_element_type=F32)
    up = jnp.dot(xn, wi_ref[:, D_FF:], preferred_element_type=F32)
    act = ((gate * jnp.tanh(gate) + gate) * up).astype(BF16)
    x2 = x1 + jnp.dot(act, wd_ref[...], preferred_element_type=F32)
    o_ref[...] = _rmsnorm(x2, g3_ref[...])


def _ffn(x2, hf, hb, yb, coef, wo, g2, wi, wd, g3):
    t = x2.shape[0]
    tm = TM_FFN
    row = lambda w_: pl.BlockSpec((tm, w_), lambda i: (i, 0))
    return pl.pallas_call(
        _ffn_kernel,
        grid=(t // tm,),
        in_specs=[row(D_MODEL), row(D_MODEL), row(D_MODEL), row(D_MODEL), row(2 * D_MODEL),
                  _resident((D_MODEL, D_MODEL)), _resident((1, D_MODEL)),
                  _resident((D_MODEL, 2 * D_FF)), _resident((D_FF, D_MODEL)), _resident((1, D_MODEL))],
        out_specs=row(D_MODEL),
        out_shape=jax.ShapeDtypeStruct((t, D_MODEL), F32),
        compiler_params=pltpu.CompilerParams(
            dimension_semantics=("arbitrary",), vmem_limit_bytes=VMEM_LIMIT_BYTES),
        name="merge_ffn",
    )(x2, hf, hb, yb, coef, wo, g2, wi, wd, g3)


def _gate_weights(wa, wx):
    def blockdiag(w):
        w = w.reshape(N_GROUPS, LRU_GROUP // LRU_BLOCK, LRU_BLOCK, LRU_BLOCK)
        eye = jnp.eye(LRU_GROUP // LRU_BLOCK, dtype=w.dtype)
        return jnp.einsum("ghij,hk->ghikj", w, eye).reshape(N_GROUPS, LRU_GROUP, LRU_GROUP)
    return jnp.concatenate([blockdiag(wa), blockdiag(wx)], axis=-1).astype(BF16)


def kernel(x, norm_mix_g, w_in, b_gate, conv_w, conv_b, lru_lambda, lru_wa, lru_ba, lru_wx, lru_bx,
           attn_sink, w_out, norm_ffn_g, w_ffn_in, w_ffn_out, norm_final_g):
    bsz, s, d = x.shape
    depth = w_in.shape[0]
    assert depth == 1, "the merge/ffn kernel applies the final RMSNorm, so it must be the last layer"
    t = bsz * s
    row = lambda a: a.reshape(1, -1)
    time_major = lambda a: a.reshape(a.shape[:-1] + (LANE_TILES, LANES))
    x2 = x.reshape(t, d)
    for l in range(depth):
        huc, q, k, v, merge_coef = _in_proj(
            x2, s, row(norm_mix_g[l]), w_in[l].astype(BF16), row(b_gate[l]),
            time_major(0.5 * conv_w[l]), time_major(0.5 * conv_b[l]))
        seq = lambda a: a.reshape(bsz, s, a.shape[-1])
        lru = lambda dr: (row(lru_lambda[l, dr]), _gate_weights(lru_wa[l, dr], lru_wx[l, dr]),
                          row(lru_ba[l, dr]), row(lru_bx[l, dr]))
        h_fwd = _lru(False, seq(huc), *lru(0))
        h_bwd = _lru(True, seq(huc), *lru(1))
        y_b, wo_b, wi_b, wd_b = _attention(seq(q), seq(k), seq(v), attn_sink[l],
                                           (w_out[l], w_ffn_in[l], w_ffn_out[l]))
        x2 = _ffn(x2, h_fwd.reshape(t, d), h_bwd.reshape(t, d), y_b.reshape(t, d), merge_coef,
                  wo_b, row(norm_ffn_g[l]), wi_b, wd_b, row(norm_final_g))
    return x2.reshape(bsz, s, d)
```

```python
import functools
import math

import numpy as np
import jax
import jax.numpy as jnp
from jax import lax
from jax.experimental import pallas as pl
from jax.experimental.pallas import tpu as pltpu

F32 = jnp.float32
BF16 = jnp.bfloat16

D_MODEL = 1024
LRU_HEADS = 16
LRU_BLOCK = D_MODEL // LRU_HEADS
CONV_WIDTH = 4
CONV_LEFT = 2
RGLRU_C = 8.0
N_HEADS = 16
N_KV_HEADS = 4
HEAD_DIM = 64
GROUP = N_HEADS // N_KV_HEADS
WINDOW = 128
BLOCK = 128
D_FF = 2816
Q_W = N_HEADS * HEAD_DIM
KV_W = N_KV_HEADS * HEAD_DIM
IN_W = 2 * D_MODEL + Q_W + 2 * KV_W + 2 * D_MODEL
EPS = 1e-6
NEG_INF = -1e30

LANES = 128
SUBLANES = 8
VMEM_LIMIT_BYTES = 56 * 1024 * 1024

TM_PROJ = 1024
TS_LRU = 1024
TQ_ATTN = 1024
TM_FFN = 512
LRU_GROUP = 256
N_GROUPS = D_MODEL // LRU_GROUP
LANE_TILES = D_MODEL // LANES
assert LANE_TILES == SUBLANES
SLAB = SUBLANES * LANE_TILES
TILES_PER_STEP = 32
N_SLABS = 3 * TILES_PER_STEP
CONV_RING = 8

_ALIBI_SLOPES = [float(v) for v in np.exp2(
    -8.0 * (np.arange(N_HEADS, dtype=np.float32) + 1.0) / N_HEADS).astype(np.float32)]
LOG2_E = math.log2(math.e)
Q_SCALE = HEAD_DIM ** -0.5 * LOG2_E


def _sigmoid(x):
    return 0.5 * jnp.tanh(0.5 * x) + 0.5


def _rmsnorm(x, g):
    return x * lax.rsqrt(jnp.mean(x * x, axis=-1, keepdims=True) + EPS) * g


def _resident(shape):
    nd = len(shape)
    return pl.BlockSpec(shape, lambda *_: (0,) * nd, pipeline_mode=pl.Buffered(1))


def _in_proj_kernel(nblk, x_ref, xp_ref, xnx_ref, g_ref, w_ref, bg_ref, cw_ref, cb_ref,
                    uc_ref, q_ref, k_ref, v_ref, coef_ref, uext_ref, slab_ref):
    tm = x_ref.shape[0]
    nt = tm // SUBLANES
    jb = pl.program_id(0) % nblk
    gn = g_ref[...]
    xn = _rmsnorm(x_ref[...], gn).astype(BF16)
    halo = jnp.concatenate([xp_ref[...], xnx_ref[...]], axis=0)
    xh = _rmsnorm(halo, gn).astype(BF16)

    def proj(lo, hi):
        return jnp.dot(xn, w_ref[:, lo:hi], preferred_element_type=F32)

    ue = jnp.dot(jnp.concatenate([xn, xh], axis=0), w_ref[:, :D_MODEL], preferred_element_type=F32)
    uext_ref[SUBLANES:SUBLANES + tm, :] = ue[:tm]
    uext_ref[0:SUBLANES, :] = jnp.where(jb > 0, ue[tm:tm + SUBLANES], 0.0)
    uext_ref[SUBLANES + tm:, :] = jnp.where(jb < nblk - 1, ue[tm + SUBLANES:], 0.0)

    o_g = D_MODEL
    o_q = o_g + D_MODEL
    o_k = o_q + Q_W
    o_v = o_k + KV_W
    o_z = o_v + KV_W
    pw = 2 * LANES
    todo = []

    def coef_a(c):
        gate = _sigmoid(proj(o_z + c, o_z + c + pw) + bg_ref[:, c:c + pw])
        coef_ref[:, c:c + pw] = gate * _gelu_tanh(proj(o_g + c, o_g + c + pw))

    def coef_b(c):
        cz = D_MODEL + c
        coef_ref[:, cz:cz + pw] = _sigmoid(proj(o_z + cz, o_z + cz + pw) + bg_ref[:, cz:cz + pw])

    def plain(dst, lo, post):
        def run(c):
            dst[:, c:c + pw] = post(proj(lo + c, lo + c + pw))
        return run

    for fn, width in ((coef_a, D_MODEL), (coef_b, D_MODEL),
                      (plain(q_ref, o_q, lambda v: (v * Q_SCALE).astype(BF16)), Q_W),
                      (plain(k_ref, o_k, lambda v: v.astype(BF16)), KV_W),
                      (plain(v_ref, o_v, lambda v: v.astype(BF16)), KV_W)):
        todo += [(fn, c) for c in range(0, width, pw)]
    every = -(-nt // len(todo))

    def project_block():
        fn, c = todo.pop(0)
        fn(c)

    w = [cw_ref[k] for k in range(CONV_WIDTH)]
    cb = cb_ref[...]
    ring = CONV_RING

    def ext_steps(e, s):
        return _split_time(uext_ref[e * SUBLANES:(e + 1) * SUBLANES, :], slab_ref, s)

    tail = ext_steps(0, 0)
    prev2, prev1 = tail[SUBLANES - 2], tail[SUBLANES - 1]
    cur = ext_steps(1, 1)
    for t in range(nt):
        if t % every == 0 and todo:
            project_block()
        nxt = ext_steps(t + 2, 2 * (t % ring))
        x = [prev2, prev1] + cur + [nxt[0]]
        out = [(x[r] * w[0] + x[r + 1] * w[1]) + (x[r + 2] * w[2] + x[r + 3] * w[3]) + cb
               for r in range(SUBLANES)]
        uc_ref[t * SUBLANES:(t + 1) * SUBLANES, :] = _join_time(out, slab_ref, 2 * (t % ring) + 1)
        prev2, prev1, cur = cur[SUBLANES - 2], cur[SUBLANES - 1], nxt
    while todo:
        project_block()


def _in_proj(x2, seq, g, w, bg, cw, cb):
    t = x2.shape[0]
    tm = TM_PROJ
    nblk = seq // tm
    hb = tm // SUBLANES
    nh = t // SUBLANES
    row = lambda w_: pl.BlockSpec((tm, w_), lambda i: (i, 0))
    prev = pl.BlockSpec((SUBLANES, D_MODEL), lambda i: (jnp.maximum(i * hb - 1, 0), 0))
    nxt = pl.BlockSpec((SUBLANES, D_MODEL), lambda i: (jnp.minimum((i + 1) * hb, nh - 1), 0))
    return pl.pallas_call(
        functools.partial(_in_proj_kernel, nblk),
        grid=(t // tm,),
        in_specs=[row(D_MODEL), prev, nxt, _resident((1, D_MODEL)), _resident((D_MODEL, IN_W)),
                  _resident((1, 2 * D_MODEL)), _resident(cw.shape), _resident(cb.shape)],
        out_specs=[row(D_MODEL), row(Q_W), row(KV_W), row(KV_W), row(2 * D_MODEL)],
        out_shape=[
            jax.ShapeDtypeStruct((t, D_MODEL), F32),
            jax.ShapeDtypeStruct((t, Q_W), BF16),
            jax.ShapeDtypeStruct((t, KV_W), BF16),
            jax.ShapeDtypeStruct((t, KV_W), BF16),
            jax.ShapeDtypeStruct((t, 2 * D_MODEL), F32),
        ],
        scratch_shapes=[
            pltpu.VMEM((tm + 2 * SUBLANES, D_MODEL), F32),
            pltpu.VMEM((2 * CONV_RING, SLAB, LANES), F32),
        ],
        compiler_params=pltpu.CompilerParams(
            dimension_semantics=("arbitrary",), vmem_limit_bytes=VMEM_LIMIT_BYTES),
        name="in_proj",
    )(x2, x2, x2, g, w, bg, cw, cb)


def _split_time(tile, slab_ref, s):
    for j in range(LANE_TILES):
        slab_ref[s, j * SUBLANES:(j + 1) * SUBLANES, :] = tile[:, j * LANES:(j + 1) * LANES]
    return [slab_ref[s, pl.ds(r, LANE_TILES, stride=SUBLANES), :] for r in range(SUBLANES)]


def _join_time(steps, slab_ref, s):
    for r in range(SUBLANES):
        slab_ref[s, pl.ds(r, LANE_TILES, stride=SUBLANES), :] = steps[r]
    return jnp.concatenate(
        [slab_ref[s, j * SUBLANES:(j + 1) * SUBLANES, :] for j in range(LANE_TILES)], axis=1)


def _half_decay_log2(lam):
    return (-0.5 * RGLRU_C * math.log2(math.e)) * (
        jnp.maximum(-lam, 0.0) + jnp.log1p(jnp.exp(-jnp.abs(lam))))


def _gate_terms(pre_r, pre_i, huc, hd2, hba, hbx):
    a = jnp.exp2(hd2 * jnp.tanh(pre_r + hba) + hd2)
    y = jnp.maximum(1.0 - a * a, 0.0)
    beta = jnp.where(y > 0.0, y * lax.rsqrt(y), 0.0)
    return a, beta * (jnp.tanh(pre_i + hbx) * huc + huc)


def _gelu_tanh(g):
    c = math.sqrt(2.0 / math.pi)
    half = 0.5 * g
    return half * jnp.tanh(g * ((c * 0.044715) * (g * g) + c)) + half


def _gate_maps(uc_ref, wg_ref, ra_ref, ix_ref, rows):
    for gi in range(N_GROUPS):
        cols = slice(gi * LRU_GROUP, (gi + 1) * LRU_GROUP)
        pre = jnp.dot(uc_ref[rows, cols].astype(BF16), wg_ref[gi], preferred_element_type=F32)
        ra_ref[rows, cols] = pre[:, :LRU_GROUP]
        ix_ref[rows, cols] = pre[:, LRU_GROUP:]


def _recurrence(reverse, uc_ref, wg_ref, ra_ref, ix_ref, lam_ref, ba_ref, bx_ref, carry_ref, slab_ref,
                emit):
    ts = uc_ref.shape[0]
    nt = ts // SUBLANES
    nsteps = nt // TILES_PER_STEP
    hd2 = _half_decay_log2(lam_ref[...])
    hba = 0.5 * ba_ref[...]
    hbx = 0.5 * bx_ref[...]

    def tile(k, s, h):
        kk = (nt - 1 - k) if reverse else k
        rows = pl.ds(pl.multiple_of(kk * SUBLANES, SUBLANES), SUBLANES)
        a, b = _gate_terms(ra_ref[rows, :], ix_ref[rows, :], uc_ref[rows, :], hd2, hba, hbx)
        a_t = _split_time(a, slab_ref, 3 * s)
        b_t = _split_time(b, slab_ref, 3 * s + 1)
        hs = [None] * SUBLANES
        for r in (range(SUBLANES - 1, -1, -1) if reverse else range(SUBLANES)):
            h = a_t[r] * h + b_t[r]
            hs[r] = h
        emit(rows, _join_time(hs, slab_ref, 3 * s + 2))
        return h

    def group(i, h):
        for s in range(TILES_PER_STEP):
            h = tile(TILES_PER_STEP * i + s, s, h)
        return h

    _gate_maps(uc_ref, wg_ref, ra_ref, ix_ref, slice(0, ts))
    carry_ref[...] = lax.fori_loop(0, nsteps, group, carry_ref[...])


def _lru_kernel(reverse, uc_ref, lam_ref, wg_ref, ba_ref, bx_ref, h_ref, ra_ref, ix_ref, carry_ref,
                slab_ref):
    @pl.when(pl.program_id(1) == 0)
    def _():
        carry_ref[...] = jnp.zeros_like(carry_ref)

    def emit(rows, h):
        h_ref[rows, :] = h

    _recurrence(reverse, uc_ref, wg_ref, ra_ref, ix_ref, lam_ref, ba_ref, bx_ref, carry_ref, slab_ref,
                emit)


def _lru(reverse, uc3, lam, wg, ba, bx):
    bsz, s, _ = uc3.shape
    ts = TS_LRU
    nblk = s // ts
    blk = pl.BlockSpec((None, ts, D_MODEL), lambda b, j: (b, (nblk - 1 - j) if reverse else j, 0))
    return pl.pallas_call(
        functools.partial(_lru_kernel, reverse),
        grid=(bsz, nblk),
        in_specs=[blk, _resident((1, D_MODEL)), _resident(wg.shape),
                  _resident((1, D_MODEL)), _resident((1, D_MODEL))],
        out_specs=blk,
        out_shape=jax.ShapeDtypeStruct((bsz, s, D_MODEL), F32),
        scratch_shapes=[
            pltpu.VMEM((ts, D_MODEL), F32),
            pltpu.VMEM((ts, D_MODEL), F32),
            pltpu.VMEM((LANE_TILES, LANES), F32),
            pltpu.VMEM((N_SLABS, SLAB, LANES), F32),
        ],
        compiler_params=pltpu.CompilerParams(
            dimension_semantics=("arbitrary", "arbitrary"), vmem_limit_bytes=VMEM_LIMIT_BYTES),
        name="lru_bwd" if reverse else "lru_fwd",
    )(uc3, lam, wg, ba, bx)


_N_BIAS = 3 * N_HEADS + 1
_MASKED_TILE = 3 * N_HEADS


def _attn_kernel(nb, sink_ref, q_ref, kp_ref, km_ref, kn_ref, vp_ref, vm_ref, vn_ref,
                 wo_ref, wi_ref, wd_ref, o_ref, wob_ref, wib_ref, wdb_ref,
                 kz_ref, vab_ref, bias_ref, s_ref, m_ref, p_ref):
    tq = q_ref.shape[0]
    tk = tq + 2 * BLOCK
    nsub = tq // BLOCK
    j = pl.program_id(1)

    wob_ref[...] = wo_ref[...].astype(BF16)
    wdb_ref[...] = wd_ref[...].astype(BF16)
    gate_col = lax.broadcasted_iota(jnp.int32, (1, 2 * D_FF), 1) < D_FF
    wib_ref[...] = (wi_ref[...] * jnp.where(gate_col, 0.5, 1.0)).astype(BF16)

    @pl.when((pl.program_id(0) == 0) & (j == 0))
    def _():
        row = lax.broadcasted_iota(jnp.int32, (BLOCK, BLOCK), 0)
        col = lax.broadcasted_iota(jnp.int32, (BLOCK, BLOCK), 1)
        bias_ref[_MASKED_TILE] = jnp.ones((BLOCK, BLOCK), F32)
        for c in range(3):
            dist = jnp.abs(row + (1 - c) * BLOCK - col)
            absd = dist.astype(F32)
            for h in range(N_HEADS):
                bias_ref[3 * h + c] = jnp.where(dist <= WINDOW, (-_ALIBI_SLOPES[h] * absd) * LOG2_E, 1.0)

    low = lax.broadcasted_iota(jnp.int32, (tk, LANES), 1) < HEAD_DIM
    low_q = lax.broadcasted_iota(jnp.int32, (BLOCK, LANES), 1) < HEAD_DIM
    ones_q = [jnp.where(low_q, 1.0, 0.0).astype(BF16), jnp.where(low_q, 0.0, 1.0).astype(BF16)]
    for t in range(KV_W // LANES):
        cols = slice(t * LANES, (t + 1) * LANES)
        full = jnp.concatenate([r[:, cols] for r in (kp_ref, km_ref, kn_ref)], axis=0).astype(F32)
        swapped = pltpu.roll(full, HEAD_DIM, axis=1)
        kz_ref[4 * t + 0] = jnp.where(low, full, 0.0).astype(BF16)
        kz_ref[4 * t + 1] = jnp.where(low, 0.0, swapped).astype(BF16)
        kz_ref[4 * t + 2] = jnp.where(low, swapped, 0.0).astype(BF16)
        kz_ref[4 * t + 3] = jnp.where(low, 0.0, full).astype(BF16)
        nkb = tk // BLOCK
        for b in range(nkb):
            if b == 0:
                src = vp_ref[:, cols]
            elif b == nkb - 1:
                src = vn_ref[:, cols]
            else:
                src = vm_ref[(b - 1) * BLOCK:b * BLOCK, cols]
            full = src.astype(F32)
            swapped = pltpu.roll(full, HEAD_DIM, axis=1)
            placed = [jnp.where(low_q, full, 0.0), jnp.where(low_q, 0.0, swapped),
                      jnp.where(low_q, swapped, 0.0), jnp.where(low_q, 0.0, full)]
            for i in range(4):
                g, half = 2 * t + i // 2, i % 2
                dst = slice((2 * b + half) * BLOCK, (2 * b + half + 1) * BLOCK)
                vab_ref[g, dst, :LANES] = placed[i].astype(BF16)
                vab_ref[g, dst, LANES:] = ones_q[half]

    def scores(qi, slot):
        q0 = pl.multiple_of(qi * BLOCK, BLOCK)
        blk = j * nsub + qi
        win = pl.ds(q0, 3 * BLOCK)
        sc2 = [None, None]
        for h in range(N_HEADS):
            g, pp, half = h // GROUP, (h % GROUP) // 2, h % 2
            if pp == 0:
                q2 = jnp.concatenate(
                    [q_ref[pl.ds(q0, BLOCK), (2 * g + i) * LANES:(2 * g + i + 1) * LANES] for i in range(2)],
                    axis=0)
                sc2[half] = lax.dot_general(q2, kz_ref[2 * g + half, win, :], (((1,), (1,)), ((), ())),
                                            preferred_element_type=F32)
            sc = sc2[half][pp * BLOCK:(pp + 1) * BLOCK]
            tiles = []
            for c in range(3):
                t = sc[:, c * BLOCK:(c + 1) * BLOCK]
                if c == 1:
                    t = t + bias_ref[3 * h + 1]
                else:
                    edge = (blk == 0) if c == 0 else (blk == nb - 1)
                    b = bias_ref[jnp.where(edge, _MASKED_TILE, 3 * h + c)]
                    t = jnp.where(b > 0.0, NEG_INF, t + b)
                s_ref[slot, h, :, c * BLOCK:(c + 1) * BLOCK] = t
                tiles.append(t)
            mx = jnp.max(jnp.maximum(jnp.maximum(tiles[0], tiles[1]), tiles[2]), axis=-1, keepdims=True)
            m_ref[slot, h] = jnp.broadcast_to(jnp.maximum(mx, sink_ref[h] * LOG2_E), (BLOCK, LANES))

    def finish(qi, slot):
        q0 = pl.multiple_of(qi * BLOCK, BLOCK)
        win2 = pl.ds(pl.multiple_of(qi * (2 * BLOCK), 2 * BLOCK), 6 * BLOCK)
        for h in range(N_HEADS):
            m = m_ref[slot, h]
            for c in range(3):
                cs = slice(c * BLOCK, (c + 1) * BLOCK)
                dst = slice((2 * c + h % 2) * BLOCK, (2 * c + h % 2 + 1) * BLOCK)
                prow = slice(((h // 2) % 2) * BLOCK, ((h // 2) % 2 + 1) * BLOCK)
                p_ref[slot, h // GROUP, prow, dst] = jnp.exp2(s_ref[slot, h, :, cs] - m).astype(BF16)
        for g in range(N_KV_HEADS):
            o2 = jnp.dot(p_ref[slot, g], vab_ref[g, win2, :], preferred_element_type=F32)
            for pp in range(2):
                pr = 2 * g + pp
                h0, h1 = 2 * pr, 2 * pr + 1
                o = o2[pp * BLOCK:(pp + 1) * BLOCK]
                e = jnp.where(low_q, jnp.exp2(sink_ref[h0] * LOG2_E - m_ref[slot, h0]),
                              jnp.exp2(sink_ref[h1] * LOG2_E - m_ref[slot, h1]))
                o_ref[pl.ds(q0, BLOCK), pr * LANES:(pr + 1) * LANES] = (
                    o[:, :LANES] * (1.0 / (o[:, LANES:] + e)))

    scores(0, 0)

    def two_sub_blocks(i, _):
        qa = 2 * i
        scores(qa + 1, 1)
        finish(qa, 0)
        scores(jnp.minimum(qa + 2, nsub - 1), 0)
        finish(qa + 1, 1)
        return 0

    lax.fori_loop(0, nsub // 2, two_sub_blocks, 0)


def _attention(q3, k3, v3, sink, ffn_weights):
    bsz, s, _ = q3.shape
    tq = TQ_ATTN
    nq = s // tq
    hb = tq // BLOCK
    nb = s // BLOCK
    tk = tq + 2 * BLOCK
    kv_main = pl.BlockSpec((None, tq, KV_W), lambda b, j: (b, j, 0))
    kv_prev = pl.BlockSpec((None, BLOCK, KV_W), lambda b, j: (b, jnp.maximum(j * hb - 1, 0), 0))
    kv_next = pl.BlockSpec((None, BLOCK, KV_W), lambda b, j: (b, jnp.minimum((j + 1) * hb, nb - 1), 0))
    qo = pl.BlockSpec((None, tq, Q_W), lambda b, j: (b, j, 0))
    nsteps = bsz * nq
    w_slice = lambda w: pl.BlockSpec((w.shape[0] // nsteps, w.shape[1]), lambda b, j: (b * nq + j, 0))
    w_specs = [w_slice(w) for w in ffn_weights]
    return pl.pallas_call(
        functools.partial(_attn_kernel, nb),
        grid=(bsz, nq),
        in_specs=[pl.BlockSpec(memory_space=pltpu.SMEM), qo,
                  kv_prev, kv_main, kv_next, kv_prev, kv_main, kv_next] + w_specs,
        out_specs=[qo] + w_specs,
        out_shape=[jax.ShapeDtypeStruct((bsz, s, Q_W), F32)]
        + [jax.ShapeDtypeStruct(w.shape, BF16) for w in ffn_weights],
        scratch_shapes=[
            pltpu.VMEM((2 * N_KV_HEADS, tk, LANES), BF16),
            pltpu.VMEM((N_KV_HEADS, 2 * tk, 2 * LANES), BF16),
            pltpu.VMEM((_N_BIAS, BLOCK, BLOCK), F32),
            pltpu.VMEM((2, N_HEADS, BLOCK, 3 * BLOCK), F32),
            pltpu.VMEM((2, N_HEADS, BLOCK, LANES), F32),
            pltpu.VMEM((2, N_KV_HEADS, 2 * BLOCK, 6 * BLOCK), BF16),
        ],
        compiler_params=pltpu.CompilerParams(
            dimension_semantics=("arbitrary", "arbitrary"), vmem_limit_bytes=VMEM_LIMIT_BYTES),
        name="attention",
    )(sink, q3, k3, k3, k3, v3, v3, v3, *ffn_weights)


def _ffn_kernel(x_ref, hf_ref, hb_ref, yb_ref, coef_ref, wo_ref, g2_ref, wi_ref, wd_ref, g3_ref, o_ref):
    merged = coef_ref[:, :D_MODEL] * (hf_ref[...] + hb_ref[...]) + coef_ref[:, D_MODEL:] * yb_ref[...]
    x1 = x_ref[...] + jnp.dot(merged.astype(BF16), wo_ref[...], preferred_element_type=F32)
    xn = _rmsnorm(x1, g2_ref[...]).astype(BF16)
    gate = jnp.dot(xn, wi_ref[:, :D_FF], preferred_element_type=F32)
    up = jnp.dot(xn, wi_ref[:, D_FF:], preferred_element_type=F32)
    act = ((gate * jnp.tanh(gate) + gate) * up).astype(BF16)
    x2 = x1 + jnp.dot(act, wd_ref[...], preferred_element_type=F32)
    o_ref[...] = _rmsnorm(x2, g3_ref[...])


def _ffn(x2, hf, hb, yb, coef, wo, g2, wi, wd, g3):
    t = x2.shape[0]
    tm = TM_FFN
    row = lambda w_: pl.BlockSpec((tm, w_), lambda i: (i, 0))
    return pl.pallas_call(
        _ffn_kernel,
        grid=(t // tm,),
        in_specs=[row(D_MODEL), row(D_MODEL), row(D_MODEL), row(D_MODEL), row(2 * D_MODEL),
                  _resident((D_MODEL, D_MODEL)), _resident((1, D_MODEL)),
                  _resident((D_MODEL, 2 * D_FF)), _resident((D_FF, D_MODEL)), _resident((1, D_MODEL))],
        out_specs=row(D_MODEL),
        out_shape=jax.ShapeDtypeStruct((t, D_MODEL), F32),
        compiler_params=pltpu.CompilerParams(
            dimension_semantics=("arbitrary",), vmem_limit_bytes=VMEM_LIMIT_BYTES),
        name="merge_ffn",
    )(x2, hf, hb, yb, coef, wo, g2, wi, wd, g3)


def _gate_weights(wa, wx):
    def blockdiag(w):
        w = w.reshape(N_GROUPS, LRU_GROUP // LRU_BLOCK, LRU_BLOCK, LRU_BLOCK)
        eye = jnp.eye(LRU_GROUP // LRU_BLOCK, dtype=w.dtype)
        return jnp.einsum("ghij,hk->ghikj", w, eye).reshape(N_GROUPS, LRU_GROUP, LRU_GROUP)
    return jnp.concatenate([blockdiag(wa), blockdiag(wx)], axis=-1).astype(BF16)


def kernel(x, norm_mix_g, w_in, b_gate, conv_w, conv_b, lru_lambda, lru_wa, lru_ba, lru_wx, lru_bx,
           attn_sink, w_out, norm_ffn_g, w_ffn_in, w_ffn_out, norm_final_g):
    bsz, s, d = x.shape
    depth = w_in.shape[0]
    assert depth == 1, "the merge/ffn kernel applies the final RMSNorm, so it must be the last layer"
    t = bsz * s
    row = lambda a: a.reshape(1, -1)
    time_major = lambda a: a.reshape(a.shape[:-1] + (LANE_TILES, LANES))
    x2 = x.reshape(t, d)
    for l in range(depth):
        huc, q, k, v, merge_coef = _in_proj(
            x2, s, row(norm_mix_g[l]), w_in[l].astype(BF16), row(b_gate[l]),
            time_major(0.5 * conv_w[l]), time_major(0.5 * conv_b[l]))
        seq = lambda a: a.reshape(bsz, s, a.shape[-1])
        lru = lambda dr: (row(lru_lambda[l, dr]), _gate_weights(lru_wa[l, dr], lru_wx[l, dr]),
                          row(lru_ba[l, dr]), row(lru_bx[l, dr]))
        h_fwd = _lru(False, seq(huc), *lru(0))
        h_bwd = _lru(True, seq(huc), *lru(1))
        y_b, wo_b, wi_b, wd_b = _attention(seq(q), seq(k), seq(v), attn_sink[l],
                                           (w_out[l], w_ffn_in[l], w_ffn_out[l]))
        x2 = _ffn(x2, h_fwd.reshape(t, d), h_bwd.reshape(t, d), y_b.reshape(t, d), merge_coef,
                  wo_b, row(norm_ffn_g[l]), wi_b, wd_b, row(norm_final_g))
    return x2.reshape(bsz, s, d)
```

```python
import functools
import math

import numpy as np
import jax
import jax.numpy as jnp
from jax import lax
from jax.experimental import pallas as pl
from jax.experimental.pallas import tpu as pltpu

F32 = jnp.float32
BF16 = jnp.bfloat16

D_MODEL = 1024
LRU_HEADS = 16
LRU_BLOCK = D_MODEL // LRU_HEADS
CONV_WIDTH = 4
CONV_LEFT = 2
RGLRU_C = 8.0
N_HEADS = 16
N_KV_HEADS = 4
HEAD_DIM = 64
GROUP = N_HEADS // N_KV_HEADS
WINDOW = 128
BLOCK = 128
D_FF = 2816
Q_W = N_HEADS * HEAD_DIM
KV_W = N_KV_HEADS * HEAD_DIM
IN_W = 2 * D_MODEL + Q_W + 2 * KV_W + 2 * D_MODEL
EPS = 1e-6
NEG_INF = -1e30

LANES = 128
SUBLANES = 8
VMEM_LIMIT_BYTES = 56 * 1024 * 1024

TM_PROJ = 1024
TS_LRU = 2048
TQ_ATTN = 1024
TM_FFN = 512
LRU_GROUP = 256
N_GROUPS = D_MODEL // LRU_GROUP
LANE_TILES = D_MODEL // LANES
assert LANE_TILES == SUBLANES
SLAB = SUBLANES * LANE_TILES
TILES_PER_STEP = 32
N_SLABS = 3 * TILES_PER_STEP
CONV_RING = 8

_ALIBI_SLOPES = [float(v) for v in np.exp2(
    -8.0 * (np.arange(N_HEADS, dtype=np.float32) + 1.0) / N_HEADS).astype(np.float32)]
LOG2_E = math.log2(math.e)
Q_SCALE = HEAD_DIM ** -0.5 * LOG2_E


def _sigmoid(x):
    return 0.5 * jnp.tanh(0.5 * x) + 0.5


def _rmsnorm(x, g):
    return x * lax.rsqrt(jnp.mean(x * x, axis=-1, keepdims=True) + EPS) * g


def _resident(shape):
    nd = len(shape)
    return pl.BlockSpec(shape, lambda *_: (0,) * nd, pipeline_mode=pl.Buffered(1))


def _in_proj_kernel(nblk, x_ref, xp_ref, xnx_ref, g_ref, w_ref, bg_ref, cw_ref, cb_ref,
                    uc_ref, q_ref, k_ref, v_ref, coef_ref, uext_ref, slab_ref):
    tm = x_ref.shape[0]
    nt = tm // SUBLANES
    jb = pl.program_id(0) % nblk
    gn = g_ref[...]
    xn = _rmsnorm(x_ref[...], gn).astype(BF16)
    halo = jnp.concatenate([xp_ref[...], xnx_ref[...]], axis=0)
    xh = _rmsnorm(halo, gn).astype(BF16)

    def proj(lo, hi):
        return jnp.dot(xn, w_ref[:, lo:hi], preferred_element_type=F32)

    ue = jnp.dot(jnp.concatenate([xn, xh], axis=0), w_ref[:, :D_MODEL], preferred_element_type=F32)
    uext_ref[SUBLANES:SUBLANES + tm, :] = ue[:tm]
    uext_ref[0:SUBLANES, :] = jnp.where(jb > 0, ue[tm:tm + SUBLANES], 0.0)
    uext_ref[SUBLANES + tm:, :] = jnp.where(jb < nblk - 1, ue[tm + SUBLANES:], 0.0)

    o_g = D_MODEL
    o_q = o_g + D_MODEL
    o_k = o_q + Q_W
    o_v = o_k + KV_W
    o_z = o_v + KV_W
    pw = 2 * LANES
    todo = []

    def coef_a(c):
        gate = _sigmoid(proj(o_z + c, o_z + c + pw) + bg_ref[:, c:c + pw])
        coef_ref[:, c:c + pw] = gate * _gelu_tanh(proj(o_g + c, o_g + c + pw))

    def coef_b(c):
        cz = D_MODEL + c
        coef_ref[:, cz:cz + pw] = _sigmoid(proj(o_z + cz, o_z + cz + pw) + bg_ref[:, cz:cz + pw])

    def plain(dst, lo, post):
        def run(c):
            dst[:, c:c + pw] = post(proj(lo + c, lo + c + pw))
        return run

    for fn, width in ((coef_a, D_MODEL), (coef_b, D_MODEL),
                      (plain(q_ref, o_q, lambda v: (v * Q_SCALE).astype(BF16)), Q_W),
                      (plain(k_ref, o_k, lambda v: v.astype(BF16)), KV_W),
                      (plain(v_ref, o_v, lambda v: v.astype(BF16)), KV_W)):
        todo += [(fn, c) for c in range(0, width, pw)]
    every = -(-nt // len(todo))

    def project_block():
        fn, c = todo.pop(0)
        fn(c)

    w = [cw_ref[k] for k in range(CONV_WIDTH)]
    cb = cb_ref[...]
    ring = CONV_RING

    def ext_steps(e, s):
        return _split_time(uext_ref[e * SUBLANES:(e + 1) * SUBLANES, :], slab_ref, s)

    tail = ext_steps(0, 0)
    prev2, prev1 = tail[SUBLANES - 2], tail[SUBLANES - 1]
    cur = ext_steps(1, 1)
    for t in range(nt):
        if t % every == 0 and todo:
            project_block()
        nxt = ext_steps(t + 2, 2 * (t % ring))
        x = [prev2, prev1] + cur + [nxt[0]]
        out = [(x[r] * w[0] + x[r + 1] * w[1]) + (x[r + 2] * w[2] + x[r + 3] * w[3]) + cb
               for r in range(SUBLANES)]
        uc_ref[t * SUBLANES:(t + 1) * SUBLANES, :] = _join_time(out, slab_ref, 2 * (t % ring) + 1)
        prev2, prev1, cur = cur[SUBLANES - 2], cur[SUBLANES - 1], nxt
    while todo:
        project_block()


def _in_proj(x2, seq, g, w, bg, cw, cb):
    t = x2.shape[0]
    tm = TM_PROJ
    nblk = seq // tm
    hb = tm // SUBLANES
    nh = t // SUBLANES
    row = lambda w_: pl.BlockSpec((tm, w_), lambda i: (i, 0))
    prev = pl.BlockSpec((SUBLANES, D_MODEL), lambda i: (jnp.maximum(i * hb - 1, 0), 0))
    nxt = pl.BlockSpec((SUBLANES, D_MODEL), lambda i: (jnp.minimum((i + 1) * hb, nh - 1), 0))
    return pl.pallas_call(
        functools.partial(_in_proj_kernel, nblk),
        grid=(t // tm,),
        in_specs=[row(D_MODEL), prev, nxt, _resident((1, D_MODEL)), _resident((D_MODEL, IN_W)),
                  _resident((1, 2 * D_MODEL)), _resident(cw.shape), _resident(cb.shape)],
        out_specs=[row(D_MODEL), row(Q_W), row(KV_W), row(KV_W), row(2 * D_MODEL)],
        out_shape=[
            jax.ShapeDtypeStruct((t, D_MODEL), F32),
            jax.ShapeDtypeStruct((t, Q_W), BF16),
            jax.ShapeDtypeStruct((t, KV_W), BF16),
            jax.ShapeDtypeStruct((t, KV_W), BF16),
            jax.ShapeDtypeStruct((t, 2 * D_MODEL), F32),
        ],
        scratch_shapes=[
            pltpu.VMEM((tm + 2 * SUBLANES, D_MODEL), F32),
            pltpu.VMEM((2 * CONV_RING, SLAB, LANES), F32),
        ],
        compiler_params=pltpu.CompilerParams(
            dimension_semantics=("arbitrary",), vmem_limit_bytes=VMEM_LIMIT_BYTES),
        name="in_proj",
    )(x2, x2, x2, g, w, bg, cw, cb)


def _split_time(tile, slab_ref, s):
    for j in range(LANE_TILES):
        slab_ref[s, j * SUBLANES:(j + 1) * SUBLANES, :] = tile[:, j * LANES:(j + 1) * LANES]
    return [slab_ref[s, pl.ds(r, LANE_TILES, stride=SUBLANES), :] for r in range(SUBLANES)]


def _join_time(steps, slab_ref, s):
    for r in range(SUBLANES):
        slab_ref[s, pl.ds(r, LANE_TILES, stride=SUBLANES), :] = steps[r]
    return jnp.concatenate(
        [slab_ref[s, j * SUBLANES:(j + 1) * SUBLANES, :] for j in range(LANE_TILES)], axis=1)


def _half_decay_log2(lam):
    return (-0.5 * RGLRU_C * math.log2(math.e)) * (
        jnp.maximum(-lam, 0.0) + jnp.log1p(jnp.exp(-jnp.abs(lam))))


def _gate_terms(pre_r, pre_i, huc, hd2, hba, hbx):
    a = jnp.exp2(hd2 * jnp.tanh(pre_r + hba) + hd2)
    y = jnp.maximum(1.0 - a * a, 0.0)
    beta = jnp.where(y > 0.0, y * lax.rsqrt(y), 0.0)
    return a, beta * (jnp.tanh(pre_i + hbx) * huc + huc)


def _gelu_tanh(g):
    c = math.sqrt(2.0 / math.pi)
    half = 0.5 * g
    return half * jnp.tanh(g * ((c * 0.044715) * (g * g) + c)) + half


def _gate_maps(uc_ref, wg_ref, ra_ref, ix_ref, rows):
    for gi in range(N_GROUPS):
        cols = slice(gi * LRU_GROUP, (gi + 1) * LRU_GROUP)
        pre = jnp.dot(uc_ref[rows, cols].astype(BF16), wg_ref[gi], preferred_element_type=F32)
        ra_ref[rows, cols] = pre[:, :LRU_GROUP]
        ix_ref[rows, cols] = pre[:, LRU_GROUP:]


def _recurrence(reverse, uc_ref, wg_ref, ra_ref, ix_ref, lam_ref, ba_ref, bx_ref, carry_ref, slab_ref,
                emit):
    ts = uc_ref.shape[0]
    nt = ts // SUBLANES
    nsteps = nt // TILES_PER_STEP
    hd2 = _half_decay_log2(lam_ref[...])
    hba = 0.5 * ba_ref[...]
    hbx = 0.5 * bx_ref[...]

    def tile(k, s, h):
        kk = (nt - 1 - k) if reverse else k
        rows = pl.ds(pl.multiple_of(kk * SUBLANES, SUBLANES), SUBLANES)
        a, b = _gate_terms(ra_ref[rows, :], ix_ref[rows, :], uc_ref[rows, :], hd2, hba, hbx)
        a_t = _split_time(a, slab_ref, 3 * s)
        b_t = _split_time(b, slab_ref, 3 * s + 1)
        hs = [None] * SUBLANES
        for r in (range(SUBLANES - 1, -1, -1) if reverse else range(SUBLANES)):
            h = a_t[r] * h + b_t[r]
            hs[r] = h
        emit(rows, _join_time(hs, slab_ref, 3 * s + 2))
        return h

    def group(i, h):
        for s in range(TILES_PER_STEP):
            h = tile(TILES_PER_STEP * i + s, s, h)
        return h

    _gate_maps(uc_ref, wg_ref, ra_ref, ix_ref, slice(0, ts))
    carry_ref[...] = lax.fori_loop(0, nsteps, group, carry_ref[...])


def _lru_kernel(reverse, uc_ref, lam_ref, wg_ref, ba_ref, bx_ref, h_ref, ra_ref, ix_ref, carry_ref,
                slab_ref):
    @pl.when(pl.program_id(1) == 0)
    def _():
        carry_ref[...] = jnp.zeros_like(carry_ref)

    def emit(rows, h):
        h_ref[rows, :] = h

    _recurrence(reverse, uc_ref, wg_ref, ra_ref, ix_ref, lam_ref, ba_ref, bx_ref, carry_ref, slab_ref,
                emit)


def _lru(reverse, uc3, lam, wg, ba, bx):
    bsz, s, _ = uc3.shape
    ts = TS_LRU
    nblk = s // ts
    blk = pl.BlockSpec((None, ts, D_MODEL), lambda b, j: (b, (nblk - 1 - j) if reverse else j, 0))
    return pl.pallas_call(
        functools.partial(_lru_kernel, reverse),
        grid=(bsz, nblk),
        in_specs=[blk, _resident((1, D_MODEL)), _resident(wg.shape),
                  _resident((1, D_MODEL)), _resident((1, D_MODEL))],
        out_specs=blk,
        out_shape=jax.ShapeDtypeStruct((bsz, s, D_MODEL), F32),
        scratch_shapes=[
            pltpu.VMEM((ts, D_MODEL), F32),
            pltpu.VMEM((ts, D_MODEL), F32),
            pltpu.VMEM((LANE_TILES, LANES), F32),
            pltpu.VMEM((N_SLABS, SLAB, LANES), F32),
        ],
        compiler_params=pltpu.CompilerParams(
            dimension_semantics=("arbitrary", "arbitrary"), vmem_limit_bytes=VMEM_LIMIT_BYTES),
        name="lru_bwd" if reverse else "lru_fwd",
    )(uc3, lam, wg, ba, bx)


_N_BIAS = 3 * N_HEADS + 1
_MASKED_TILE = 3 * N_HEADS


def _attn_kernel(nb, sink_ref, q_ref, kp_ref, km_ref, kn_ref, vp_ref, vm_ref, vn_ref,
                 wo_ref, wi_ref, wd_ref, o_ref, wob_ref, wib_ref, wdb_ref,
                 kz_ref, vab_ref, bias_ref, s_ref, m_ref, p_ref):
    tq = q_ref.shape[0]
    tk = tq + 2 * BLOCK
    nsub = tq // BLOCK
    j = pl.program_id(1)

    wob_ref[...] = wo_ref[...].astype(BF16)
    wdb_ref[...] = wd_ref[...].astype(BF16)
    gate_col = lax.broadcasted_iota(jnp.int32, (1, 2 * D_FF), 1) < D_FF
    wib_ref[...] = (wi_ref[...] * jnp.where(gate_col, 0.5, 1.0)).astype(BF16)

    @pl.when((pl.program_id(0) == 0) & (j == 0))
    def _():
        row = lax.broadcasted_iota(jnp.int32, (BLOCK, BLOCK), 0)
        col = lax.broadcasted_iota(jnp.int32, (BLOCK, BLOCK), 1)
        bias_ref[_MASKED_TILE] = jnp.ones((BLOCK, BLOCK), F32)
        for c in range(3):
            dist = jnp.abs(row + (1 - c) * BLOCK - col)
            absd = dist.astype(F32)
            for h in range(N_HEADS):
                bias_ref[3 * h + c] = jnp.where(dist <= WINDOW, (-_ALIBI_SLOPES[h] * absd) * LOG2_E, 1.0)

    low = lax.broadcasted_iota(jnp.int32, (tk, LANES), 1) < HEAD_DIM
    low_q = lax.broadcasted_iota(jnp.int32, (BLOCK, LANES), 1) < HEAD_DIM
    ones_q = [jnp.where(low_q, 1.0, 0.0).astype(BF16), jnp.where(low_q, 0.0, 1.0).astype(BF16)]
    for t in range(KV_W // LANES):
        cols = slice(t * LANES, (t + 1) * LANES)
        full = jnp.concatenate([r[:, cols] for r in (kp_ref, km_ref, kn_ref)], axis=0).astype(F32)
        swapped = pltpu.roll(full, HEAD_DIM, axis=1)
        kz_ref[4 * t + 0] = jnp.where(low, full, 0.0).astype(BF16)
        kz_ref[4 * t + 1] = jnp.where(low, 0.0, swapped).astype(BF16)
        kz_ref[4 * t + 2] = jnp.where(low, swapped, 0.0).astype(BF16)
        kz_ref[4 * t + 3] = jnp.where(low, 0.0, full).astype(BF16)
        nkb = tk // BLOCK
        for b in range(nkb):
            if b == 0:
                src = vp_ref[:, cols]
            elif b == nkb - 1:
                src = vn_ref[:, cols]
            else:
                src = vm_ref[(b - 1) * BLOCK:b * BLOCK, cols]
            full = src.astype(F32)
            swapped = pltpu.roll(full, HEAD_DIM, axis=1)
            placed = [jnp.where(low_q, full, 0.0), jnp.where(low_q, 0.0, swapped),
                      jnp.where(low_q, swapped, 0.0), jnp.where(low_q, 0.0, full)]
            for i in range(4):
                g, half = 2 * t + i // 2, i % 2
                dst = slice((2 * b + half) * BLOCK, (2 * b + half + 1) * BLOCK)
                vab_ref[g, dst, :LANES] = placed[i].astype(BF16)
                vab_ref[g, dst, LANES:] = ones_q[half]

    def scores(qi, slot):
        q0 = pl.multiple_of(qi * BLOCK, BLOCK)
        blk = j * nsub + qi
        win = pl.ds(q0, 3 * BLOCK)
        sc2 = [None, None]
        for h in range(N_HEADS):
            g, pp, half = h // GROUP, (h % GROUP) // 2, h % 2
            if pp == 0:
                q2 = jnp.concatenate(
                    [q_ref[pl.ds(q0, BLOCK), (2 * g + i) * LANES:(2 * g + i + 1) * LANES] for i in range(2)],
                    axis=0)
                sc2[half] = lax.dot_general(q2, kz_ref[2 * g + half, win, :], (((1,), (1,)), ((), ())),
                                            preferred_element_type=F32)
            sc = sc2[half][pp * BLOCK:(pp + 1) * BLOCK]
            tiles = []
            for c in range(3):
                t = sc[:, c * BLOCK:(c + 1) * BLOCK]
                if c == 1:
                    t = t + bias_ref[3 * h + 1]
                else:
                    edge = (blk == 0) if c == 0 else (blk == nb - 1)
                    b = bias_ref[jnp.where(edge, _MASKED_TILE, 3 * h + c)]
                    t = jnp.where(b > 0.0, NEG_INF, t + b)
                s_ref[slot, h, :, c * BLOCK:(c + 1) * BLOCK] = t
                tiles.append(t)
            mx = jnp.max(jnp.maximum(jnp.maximum(tiles[0], tiles[1]), tiles[2]), axis=-1, keepdims=True)
            m_ref[slot, h] = jnp.broadcast_to(jnp.maximum(mx, sink_ref[h] * LOG2_E), (BLOCK, LANES))

    def finish(qi, slot):
        q0 = pl.multiple_of(qi * BLOCK, BLOCK)
        win2 = pl.ds(pl.multiple_of(qi * (2 * BLOCK), 2 * BLOCK), 6 * BLOCK)
        for h in range(N_HEADS):
            m = m_ref[slot, h]
            for c in range(3):
                cs = slice(c * BLOCK, (c + 1) * BLOCK)
                dst = slice((2 * c + h % 2) * BLOCK, (2 * c + h % 2 + 1) * BLOCK)
                prow = slice(((h // 2) % 2) * BLOCK, ((h // 2) % 2 + 1) * BLOCK)
                p_ref[slot, h // GROUP, prow, dst] = jnp.exp2(s_ref[slot, h, :, cs] - m).astype(BF16)
        for g in range(N_KV_HEADS):
            o2 = jnp.dot(p_ref[slot, g], vab_ref[g, win2, :], preferred_element_type=F32)
            for pp in range(2):
                pr = 2 * g + pp
                h0, h1 = 2 * pr, 2 * pr + 1
                o = o2[pp * BLOCK:(pp + 1) * BLOCK]
                e = jnp.where(low_q, jnp.exp2(sink_ref[h0] * LOG2_E - m_ref[slot, h0]),
                              jnp.exp2(sink_ref[h1] * LOG2_E - m_ref[slot, h1]))
                o_ref[pl.ds(q0, BLOCK), pr * LANES:(pr + 1) * LANES] = (
                    o[:, :LANES] * (1.0 / (o[:, LANES:] + e)))

    scores(0, 0)

    def two_sub_blocks(i, _):
        qa = 2 * i
        scores(qa + 1, 1)
        finish(qa, 0)
        scores(jnp.minimum(qa + 2, nsub - 1), 0)
        finish(qa + 1, 1)
        return 0

    lax.fori_loop(0, nsub // 2, two_sub_blocks, 0)


def _attention(q3, k3, v3, sink, ffn_weights):
    bsz, s, _ = q3.shape
    tq = TQ_ATTN
    nq = s // tq
    hb = tq // BLOCK
    nb = s // BLOCK
    tk = tq + 2 * BLOCK
    kv_main = pl.BlockSpec((None, tq, KV_W), lambda b, j: (b, j, 0))
    kv_prev = pl.BlockSpec((None, BLOCK, KV_W), lambda b, j: (b, jnp.maximum(j * hb - 1, 0), 0))
    kv_next = pl.BlockSpec((None, BLOCK, KV_W), lambda b, j: (b, jnp.minimum((j + 1) * hb, nb - 1), 0))
    qo = pl.BlockSpec((None, tq, Q_W), lambda b, j: (b, j, 0))
    nsteps = bsz * nq
    w_slice = lambda w: pl.BlockSpec((w.shape[0] // nsteps, w.shape[1]), lambda b, j: (b * nq + j, 0))
    w_specs = [w_slice(w) for w in ffn_weights]
    return pl.pallas_call(
        functools.partial(_attn_kernel, nb),
        grid=(bsz, nq),
        in_specs=[pl.BlockSpec(memory_space=pltpu.SMEM), qo,
                  kv_prev, kv_main, kv_next, kv_prev, kv_main, kv_next] + w_specs,
        out_specs=[qo] + w_specs,
        out_shape=[jax.ShapeDtypeStruct((bsz, s, Q_W), F32)]
        + [jax.ShapeDtypeStruct(w.shape, BF16) for w in ffn_weights],
        scratch_shapes=[
            pltpu.VMEM((2 * N_KV_HEADS, tk, LANES), BF16),
            pltpu.VMEM((N_KV_HEADS, 2 * tk, 2 * LANES), BF16),
            pltpu.VMEM((_N_BIAS, BLOCK, BLOCK), F32),
            pltpu.VMEM((2, N_HEADS, BLOCK, 3 * BLOCK), F32),
            pltpu.VMEM((2, N_HEADS, BLOCK, LANES), F32),
            pltpu.VMEM((2, N_KV_HEADS, 2 * BLOCK, 6 * BLOCK), BF16),
        ],
        compiler_params=pltpu.CompilerParams(
            dimension_semantics=("arbitrary", "arbitrary"), vmem_limit_bytes=VMEM_LIMIT_BYTES),
        name="attention",
    )(sink, q3, k3, k3, k3, v3, v3, v3, *ffn_weights)


def _ffn_kernel(x_ref, hf_ref, hb_ref, yb_ref, coef_ref, wo_ref, g2_ref, wi_ref, wd_ref, g3_ref, o_ref):
    merged = coef_ref[:, :D_MODEL] * (hf_ref[...] + hb_ref[...]) + coef_ref[:, D_MODEL:] * yb_ref[...]
    x1 = x_ref[...] + jnp.dot(merged.astype(BF16), wo_ref[...], preferred_element_type=F32)
    xn = _rmsnorm(x1, g2_ref[...]).astype(BF16)
    gate = jnp.dot(xn, wi_ref[:, :D_FF], preferred_element_type=F32)
    up = jnp.dot(xn, wi_ref[:, D_FF:], preferred_element_type=F32)
    act = ((gate * jnp.tanh(gate) + gate) * up).astype(BF16)
    x2 = x1 + jnp.dot(act, wd_ref[...], preferred_element_type=F32)
    o_ref[...] = _rmsnorm(x2, g3_ref[...])


def _ffn(x2, hf, hb, yb, coef, wo, g2, wi, wd, g3):
    t = x2.shape[0]
    tm = TM_FFN
    row = lambda w_: pl.BlockSpec((tm, w_), lambda i: (i, 0))
    return pl.pallas_call(
        _ffn_kernel,
        grid=(t // tm,),
        in_specs=[row(D_MODEL), row(D_MODEL), row(D_MODEL), row(D_MODEL), row(2 * D_MODEL),
                  _resident((D_MODEL, D_MODEL)), _resident((1, D_MODEL)),
                  _resident((D_MODEL, 2 * D_FF)), _resident((D_FF, D_MODEL)), _resident((1, D_MODEL))],
        out_specs=row(D_MODEL),
        out_shape=jax.ShapeDtypeStruct((t, D_MODEL), F32),
        compiler_params=pltpu.CompilerParams(
            dimension_semantics=("arbitrary",), vmem_limit_bytes=VMEM_LIMIT_BYTES),
        name="merge_ffn",
    )(x2, hf, hb, yb, coef, wo, g2, wi, wd, g3)


def _gate_weights(wa, wx):
    def blockdiag(w):
        w = w.reshape(N_GROUPS, LRU_GROUP // LRU_BLOCK, LRU_BLOCK, LRU_BLOCK)
        eye = jnp.eye(LRU_GROUP // LRU_BLOCK, dtype=w.dtype)
        return jnp.einsum("ghij,hk->ghikj", w, eye).reshape(N_GROUPS, LRU_GROUP, LRU_GROUP)
    return jnp.concatenate([blockdiag(wa), blockdiag(wx)], axis=-1).astype(BF16)


def kernel(x, norm_mix_g, w_in, b_gate, conv_w, conv_b, lru_lambda, lru_wa, lru_ba, lru_wx, lru_bx,
           attn_sink, w_out, norm_ffn_g, w_ffn_in, w_ffn_out, norm_final_g):
    bsz, s, d = x.shape
    depth = w_in.shape[0]
    assert depth == 1, "the merge/ffn kernel applies the final RMSNorm, so it must be the last layer"
    t = bsz * s
    row = lambda a: a.reshape(1, -1)
    time_major = lambda a: a.reshape(a.shape[:-1] + (LANE_TILES, LANES))
    x2 = x.reshape(t, d)
    for l in range(depth):
        huc, q, k, v, merge_coef = _in_proj(
            x2, s, row(norm_mix_g[l]), w_in[l].astype(BF16), row(b_gate[l]),
            time_major(0.5 * conv_w[l]), time_major(0.5 * conv_b[l]))
        seq = lambda a: a.reshape(bsz, s, a.shape[-1])
        lru = lambda dr: (row(lru_lambda[l, dr]), _gate_weights(lru_wa[l, dr], lru_wx[l, dr]),
                          row(lru_ba[l, dr]), row(lru_bx[l, dr]))
        h_fwd = _lru(False, seq(huc), *lru(0))
        h_bwd = _lru(True, seq(huc), *lru(1))
        y_b, wo_b, wi_b, wd_b = _attention(seq(q), seq(k), seq(v), attn_sink[l],
                                           (w_out[l], w_ffn_in[l], w_ffn_out[l]))
        x2 = _ffn(x2, h_fwd.reshape(t, d), h_bwd.reshape(t, d), y_b.reshape(t, d), merge_coef,
                  wo_b, row(norm_ffn_g[l]), wi_b, wd_b, row(norm_final_g))
    return x2.reshape(bsz, s, d)
```

```python
import functools
import math

import numpy as np
import jax
import jax.numpy as jnp
from jax import lax
from jax.experimental import pallas as pl
from jax.experimental.pallas import tpu as pltpu

F32 = jnp.float32
BF16 = jnp.bfloat16

D_MODEL = 1024
LRU_HEADS = 16
LRU_BLOCK = D_MODEL // LRU_HEADS
CONV_WIDTH = 4
CONV_LEFT = 2
RGLRU_C = 8.0
N_HEADS = 16
N_KV_HEADS = 4
HEAD_DIM = 64
GROUP = N_HEADS // N_KV_HEADS
WINDOW = 128
BLOCK = 128
D_FF = 2816
Q_W = N_HEADS * HEAD_DIM
KV_W = N_KV_HEADS * HEAD_DIM
IN_W = 2 * D_MODEL + Q_W + 2 * KV_W + 2 * D_MODEL
EPS = 1e-6
NEG_INF = -1e30

LANES = 128
SUBLANES = 8
VMEM_LIMIT_BYTES = 56 * 1024 * 1024

TM_PROJ = 1024
TS_LRU = 1024
TQ_ATTN = 1024
TM_FFN = 512
LRU_GROUP = 256
N_GROUPS = D_MODEL // LRU_GROUP
LANE_TILES = D_MODEL // LANES
assert LANE_TILES == SUBLANES
SLAB = SUBLANES * LANE_TILES
TILES_PER_STEP = 32
N_SLABS = 3 * TILES_PER_STEP
CONV_RING = 8

_ALIBI_SLOPES = [float(v) for v in np.exp2(
    -8.0 * (np.arange(N_HEADS, dtype=np.float32) + 1.0) / N_HEADS).astype(np.float32)]
LOG2_E = math.log2(math.e)
Q_SCALE = HEAD_DIM ** -0.5 * LOG2_E


def _sigmoid(x):
    return 0.5 * jnp.tanh(0.5 * x) + 0.5


def _rmsnorm(x, g):
    return x * lax.rsqrt(jnp.mean(x * x, axis=-1, keepdims=True) + EPS) * g


def _resident(shape):
    nd = len(shape)
    return pl.BlockSpec(shape, lambda *_: (0,) * nd, pipeline_mode=pl.Buffered(1))


def _in_proj_kernel(nblk, x_ref, xp_ref, xnx_ref, g_ref, w_ref, bg_ref, cw_ref, cb_ref,
                    uc_ref, q_ref, k_ref, v_ref, coef_ref, uext_ref, slab_ref):
    tm = x_ref.shape[0]
    nt = tm // SUBLANES
    jb = pl.program_id(0) % nblk
    gn = g_ref[...]
    xn = _rmsnorm(x_ref[...], gn).astype(BF16)
    halo = jnp.concatenate([xp_ref[...], xnx_ref[...]], axis=0)
    xh = _rmsnorm(halo, gn).astype(BF16)

    def proj(lo, hi):
        return jnp.dot(xn, w_ref[:, lo:hi], preferred_element_type=F32)

    ue = jnp.dot(jnp.concatenate([xn, xh], axis=0), w_ref[:, :D_MODEL], preferred_element_type=F32)
    uext_ref[SUBLANES:SUBLANES + tm, :] = ue[:tm]
    uext_ref[0:SUBLANES, :] = jnp.where(jb > 0, ue[tm:tm + SUBLANES], 0.0)
    uext_ref[SUBLANES + tm:, :] = jnp.where(jb < nblk - 1, ue[tm + SUBLANES:], 0.0)

    o_g = D_MODEL
    o_q = o_g + D_MODEL
    o_k = o_q + Q_W
    o_v = o_k + KV_W
    o_z = o_v + KV_W
    pw = 2 * LANES
    todo = []

    def coef_a(c):
        gate = _sigmoid(proj(o_z + c, o_z + c + pw) + bg_ref[:, c:c + pw])
        coef_ref[:, c:c + pw] = gate * _gelu_tanh(proj(o_g + c, o_g + c + pw))

    def coef_b(c):
        cz = D_MODEL + c
        coef_ref[:, cz:cz + pw] = _sigmoid(proj(o_z + cz, o_z + cz + pw) + bg_ref[:, cz:cz + pw])

    def plain(dst, lo, post):
        def run(c):
            dst[:, c:c + pw] = post(proj(lo + c, lo + c + pw))
        return run

    for fn, width in ((coef_a, D_MODEL), (coef_b, D_MODEL),
                      (plain(q_ref, o_q, lambda v: (v * Q_SCALE).astype(BF16)), Q_W),
                      (plain(k_ref, o_k, lambda v: v.astype(BF16)), KV_W),
                      (plain(v_ref, o_v, lambda v: v.astype(BF16)), KV_W)):
        todo += [(fn, c) for c in range(0, width, pw)]
    every = -(-nt // len(todo))

    def project_block():
        fn, c = todo.pop(0)
        fn(c)

    w = [cw_ref[k] for k in range(CONV_WIDTH)]
    cb = cb_ref[...]
    ring = CONV_RING

    def ext_steps(e, s):
        return _split_time(uext_ref[e * SUBLANES:(e + 1) * SUBLANES, :], slab_ref, s)

    tail = ext_steps(0, 0)
    prev2, prev1 = tail[SUBLANES - 2], tail[SUBLANES - 1]
    cur = ext_steps(1, 1)
    for t in range(nt):
        if t % every == 0 and todo:
            project_block()
        nxt = ext_steps(t + 2, 2 * (t % ring))
        x = [prev2, prev1] + cur + [nxt[0]]
        out = [(x[r] * w[0] + x[r + 1] * w[1]) + (x[r + 2] * w[2] + x[r + 3] * w[3]) + cb
               for r in range(SUBLANES)]
        uc_ref[t * SUBLANES:(t + 1) * SUBLANES, :] = _join_time(out, slab_ref, 2 * (t % ring) + 1)
        prev2, prev1, cur = cur[SUBLANES - 2], cur[SUBLANES - 1], nxt
    while todo:
        project_block()


def _in_proj(x2, seq, g, w, bg, cw, cb):
    t = x2.shape[0]
    tm = TM_PROJ
    nblk = seq // tm
    hb = tm // SUBLANES
    nh = t // SUBLANES
    row = lambda w_: pl.BlockSpec((tm, w_), lambda i: (i, 0))
    prev = pl.BlockSpec((SUBLANES, D_MODEL), lambda i: (jnp.maximum(i * hb - 1, 0), 0))
    nxt = pl.BlockSpec((SUBLANES, D_MODEL), lambda i: (jnp.minimum((i + 1) * hb, nh - 1), 0))
    return pl.pallas_call(
        functools.partial(_in_proj_kernel, nblk),
        grid=(t // tm,),
        in_specs=[row(D_MODEL), prev, nxt, _resident((1, D_MODEL)), _resident((D_MODEL, IN_W)),
                  _resident((1, 2 * D_MODEL)), _resident(cw.shape), _resident(cb.shape)],
        out_specs=[row(D_MODEL), row(Q_W), row(KV_W), row(KV_W), row(2 * D_MODEL)],
        out_shape=[
            jax.ShapeDtypeStruct((t, D_MODEL), F32),
            jax.ShapeDtypeStruct((t, Q_W), BF16),
            jax.ShapeDtypeStruct((t, KV_W), BF16),
            jax.ShapeDtypeStruct((t, KV_W), BF16),
            jax.ShapeDtypeStruct((t, 2 * D_MODEL), F32),
        ],
        scratch_shapes=[
            pltpu.VMEM((tm + 2 * SUBLANES, D_MODEL), F32),
            pltpu.VMEM((2 * CONV_RING, SLAB, LANES), F32),
        ],
        compiler_params=pltpu.CompilerParams(
            dimension_semantics=("arbitrary",), vmem_limit_bytes=VMEM_LIMIT_BYTES),
        name="in_proj",
    )(x2, x2, x2, g, w, bg, cw, cb)


def _split_time(tile, slab_ref, s):
    for j in range(LANE_TILES):
        slab_ref[s, j * SUBLANES:(j + 1) * SUBLANES, :] = tile[:, j * LANES:(j + 1) * LANES]
    return [slab_ref[s, pl.ds(r, LANE_TILES, stride=SUBLANES), :] for r in range(SUBLANES)]


def _join_time(steps, slab_ref, s):
    for r in range(SUBLANES):
        slab_ref[s, pl.ds(r, LANE_TILES, stride=SUBLANES), :] = steps[r]
    return jnp.concatenate(
        [slab_ref[s, j * SUBLANES:(j + 1) * SUBLANES, :] for j in range(LANE_TILES)], axis=1)


def _half_decay_log2(lam):
    return (-0.5 * RGLRU_C * math.log2(math.e)) * (
        jnp.maximum(-lam, 0.0) + jnp.log1p(jnp.exp(-jnp.abs(lam))))


def _gate_terms(pre_r, pre_i, huc, hd2, hba, hbx):
    a = jnp.exp2(hd2 * jnp.tanh(pre_r + hba) + hd2)
    y = jnp.maximum(1.0 - a * a, 0.0)
    beta = jnp.where(y > 0.0, y * lax.rsqrt(y), 0.0)
    return a, beta * (jnp.tanh(pre_i + hbx) * huc + huc)


def _gelu_tanh(g):
    c = math.sqrt(2.0 / math.pi)
    half = 0.5 * g
    return half * jnp.tanh(g * ((c * 0.044715) * (g * g) + c)) + half


def _gate_maps(uc_ref, wg_ref, ra_ref, ix_ref, rows):
    for gi in range(N_GROUPS):
        cols = slice(gi * LRU_GROUP, (gi + 1) * LRU_GROUP)
        pre = jnp.dot(uc_ref[rows, cols].astype(BF16), wg_ref[gi], preferred_element_type=F32)
        ra_ref[rows, cols] = pre[:, :LRU_GROUP]
        ix_ref[rows, cols] = pre[:, LRU_GROUP:]


def _recurrence(reverse, uc_ref, wg_ref, ra_ref, ix_ref, lam_ref, ba_ref, bx_ref, carry_ref, slab_ref,
                emit):
    ts = uc_ref.shape[0]
    nt = ts // SUBLANES
    nsteps = nt // TILES_PER_STEP
    hd2 = _half_decay_log2(lam_ref[...])
    hba = 0.5 * ba_ref[...]
    hbx = 0.5 * bx_ref[...]

    def tile(k, s, h):
        kk = (nt - 1 - k) if reverse else k
        rows = pl.ds(pl.multiple_of(kk * SUBLANES, SUBLANES), SUBLANES)
        a, b = _gate_terms(ra_ref[rows, :], ix_ref[rows, :], uc_ref[rows, :], hd2, hba, hbx)
        a_t = _split_time(a, slab_ref, 3 * s)
        b_t = _split_time(b, slab_ref, 3 * s + 1)
        hs = [None] * SUBLANES
        for r in (range(SUBLANES - 1, -1, -1) if reverse else range(SUBLANES)):
            h = a_t[r] * h + b_t[r]
            hs[r] = h
        emit(rows, _join_time(hs, slab_ref, 3 * s + 2))
        return h

    def group(i, h):
        for s in range(TILES_PER_STEP):
            h = tile(TILES_PER_STEP * i + s, s, h)
        return h

    _gate_maps(uc_ref, wg_ref, ra_ref, ix_ref, slice(0, ts))
    carry_ref[...] = lax.fori_loop(0, nsteps, group, carry_ref[...])


def _lru_kernel(reverse, uc_ref, lam_ref, wg_ref, ba_ref, bx_ref, h_ref, ra_ref, ix_ref, carry_ref,
                slab_ref):
    @pl.when(pl.program_id(1) == 0)
    def _():
        carry_ref[...] = jnp.zeros_like(carry_ref)

    def emit(rows, h):
        h_ref[rows, :] = h

    _recurrence(reverse, uc_ref, wg_ref, ra_ref, ix_ref, lam_ref, ba_ref, bx_ref, carry_ref, slab_ref,
                emit)


def _lru(reverse, uc3, lam, wg, ba, bx):
    bsz, s, _ = uc3.shape
    ts = TS_LRU
    nblk = s // ts
    blk = pl.BlockSpec((None, ts, D_MODEL), lambda b, j: (b, (nblk - 1 - j) if reverse else j, 0))
    return pl.pallas_call(
        functools.partial(_lru_kernel, reverse),
        grid=(bsz, nblk),
        in_specs=[blk, _resident((1, D_MODEL)), _resident(wg.shape),
                  _resident((1, D_MODEL)), _resident((1, D_MODEL))],
        out_specs=blk,
        out_shape=jax.ShapeDtypeStruct((bsz, s, D_MODEL), F32),
        scratch_shapes=[
            pltpu.VMEM((ts, D_MODEL), F32),
            pltpu.VMEM((ts, D_MODEL), F32),
            pltpu.VMEM((LANE_TILES, LANES), F32),
            pltpu.VMEM((N_SLABS, SLAB, LANES), F32),
        ],
        compiler_params=pltpu.CompilerParams(
            dimension_semantics=("arbitrary", "arbitrary"), vmem_limit_bytes=VMEM_LIMIT_BYTES),
        name="lru_bwd" if reverse else "lru_fwd",
    )(uc3, lam, wg, ba, bx)


_N_BIAS = 3 * N_HEADS + 1
_MASKED_TILE = 3 * N_HEADS


def _attn_kernel(nb, sink_ref, q_ref, kp_ref, km_ref, kn_ref, vp_ref, vm_ref, vn_ref,
                 wo_ref, wi_ref, wd_ref, o_ref, wob_ref, wib_ref, wdb_ref,
                 kz_ref, vab_ref, bias_ref, s_ref, m_ref, p_ref):
    tq = q_ref.shape[0]
    tk = tq + 2 * BLOCK
    nsub = tq // BLOCK
    j = pl.program_id(1)

    wob_ref[...] = wo_ref[...].astype(BF16)
    wdb_ref[...] = wd_ref[...].astype(BF16)
    gate_col = lax.broadcasted_iota(jnp.int32, (1, 2 * D_FF), 1) < D_FF
    wib_ref[...] = (wi_ref[...] * jnp.where(gate_col, 0.5, 1.0)).astype(BF16)

    @pl.when((pl.program_id(0) == 0) & (j == 0))
    def _():
        row = lax.broadcasted_iota(jnp.int32, (BLOCK, BLOCK), 0)
        col = lax.broadcasted_iota(jnp.int32, (BLOCK, BLOCK), 1)
        bias_ref[_MASKED_TILE] = jnp.ones((BLOCK, BLOCK), F32)
        for c in range(3):
            dist = jnp.abs(row + (1 - c) * BLOCK - col)
            absd = dist.astype(F32)
            for h in range(N_HEADS):
                bias_ref[3 * h + c] = jnp.where(dist <= WINDOW, (-_ALIBI_SLOPES[h] * absd) * LOG2_E, 1.0)

    low = lax.broadcasted_iota(jnp.int32, (tk, LANES), 1) < HEAD_DIM
    low_q = lax.broadcasted_iota(jnp.int32, (BLOCK, LANES), 1) < HEAD_DIM
    ones_q = [jnp.where(low_q, 1.0, 0.0).astype(BF16), jnp.where(low_q, 0.0, 1.0).astype(BF16)]
    for t in range(KV_W // LANES):
        cols = slice(t * LANES, (t + 1) * LANES)
        full = jnp.concatenate([r[:, cols] for r in (kp_ref, km_ref, kn_ref)], axis=0).astype(F32)
        swapped = pltpu.roll(full, HEAD_DIM, axis=1)
        kz_ref[4 * t + 0] = jnp.where(low, full, 0.0).astype(BF16)
        kz_ref[4 * t + 1] = jnp.where(low, 0.0, swapped).astype(BF16)
        kz_ref[4 * t + 2] = jnp.where(low, swapped, 0.0).astype(BF16)
        kz_ref[4 * t + 3] = jnp.where(low, 0.0, full).astype(BF16)
        nkb = tk // BLOCK
        for b in range(nkb):
            if b == 0:
                src = vp_ref[:, cols]
            elif b == nkb - 1:
                src = vn_ref[:, cols]
            else:
                src = vm_ref[(b - 1) * BLOCK:b * BLOCK, cols]
            full = src.astype(F32)
            swapped = pltpu.roll(full, HEAD_DIM, axis=1)
            placed = [jnp.where(low_q, full, 0.0), jnp.where(low_q, 0.0, swapped),
                      jnp.where(low_q, swapped, 0.0), jnp.where(low_q, 0.0, full)]
            for i in range(4):
                g, half = 2 * t + i // 2, i % 2
                dst = slice((2 * b + half) * BLOCK, (2 * b + half + 1) * BLOCK)
                vab_ref[g, dst, :LANES] = placed[i].astype(BF16)
                vab_ref[g, dst, LANES:] = ones_q[half]

    def scores(qi, slot):
        q0 = pl.multiple_of(qi * BLOCK, BLOCK)
        blk = j * nsub + qi
        win = pl.ds(q0, 3 * BLOCK)
        sc2 = [None, None]
        for h in range(N_HEADS):
            g, pp, half = h // GROUP, (h % GROUP) // 2, h % 2
            if pp == 0:
                q2 = jnp.concatenate(
                    [q_ref[pl.ds(q0, BLOCK), (2 * g + i) * LANES:(2 * g + i + 1) * LANES] for i in range(2)],
                    axis=0)
                sc2[half] = lax.dot_general(q2, kz_ref[2 * g + half, win, :], (((1,), (1,)), ((), ())),
                                            preferred_element_type=F32)
            sc = sc2[half][pp * BLOCK:(pp + 1) * BLOCK]
            tiles = []
            for c in range(3):
                t = sc[:, c * BLOCK:(c + 1) * BLOCK]
                if c == 1:
                    t = t + bias_ref[3 * h + 1]
                else:
                    edge = (blk == 0) if c == 0 else (blk == nb - 1)
                    b = bias_ref[jnp.where(edge, _MASKED_TILE, 3 * h + c)]
                    t = jnp.where(b > 0.0, NEG_INF, t + b)
                s_ref[slot, h, :, c * BLOCK:(c + 1) * BLOCK] = t
                tiles.append(t)
            mx = jnp.max(jnp.maximum(jnp.maximum(tiles[0], tiles[1]), tiles[2]), axis=-1, keepdims=True)
            m_ref[slot, h] = jnp.broadcast_to(jnp.maximum(mx, sink_ref[h] * LOG2_E), (BLOCK, LANES))

    def finish(qi, slot):
        q0 = pl.multiple_of(qi * BLOCK, BLOCK)
        win2 = pl.ds(pl.multiple_of(qi * (2 * BLOCK), 2 * BLOCK), 6 * BLOCK)
        for h in range(N_HEADS):
            m = m_ref[slot, h]
            for c in range(3):
                cs = slice(c * BLOCK, (c + 1) * BLOCK)
                dst = slice((2 * c + h % 2) * BLOCK, (2 * c + h % 2 + 1) * BLOCK)
                prow = slice(((h // 2) % 2) * BLOCK, ((h // 2) % 2 + 1) * BLOCK)
                p_ref[slot, h // GROUP, prow, dst] = jnp.exp2(s_ref[slot, h, :, cs] - m).astype(BF16)
        for g in range(N_KV_HEADS):
            o2 = jnp.dot(p_ref[slot, g], vab_ref[g, win2, :], preferred_element_type=F32)
            for pp in range(2):
                pr = 2 * g + pp
                h0, h1 = 2 * pr, 2 * pr + 1
                o = o2[pp * BLOCK:(pp + 1) * BLOCK]
                sink2 = jnp.where(low_q[0:1, :], sink_ref[h0] * LOG2_E, sink_ref[h1] * LOG2_E)
                e = jnp.exp2(sink2 - jnp.where(low_q, m_ref[slot, h0], m_ref[slot, h1]))
                o_ref[pl.ds(q0, BLOCK), pr * LANES:(pr + 1) * LANES] = (
                    o[:, :LANES] * (1.0 / (o[:, LANES:] + e)))

    scores(0, 0)

    def two_sub_blocks(i, _):
        qa = 2 * i
        scores(qa + 1, 1)
        finish(qa, 0)
        scores(jnp.minimum(qa + 2, nsub - 1), 0)
        finish(qa + 1, 1)
        return 0

    lax.fori_loop(0, nsub // 2, two_sub_blocks, 0)


def _attention(q3, k3, v3, sink, ffn_weights):
    bsz, s, _ = q3.shape
    tq = TQ_ATTN
    nq = s // tq
    hb = tq // BLOCK
    nb = s // BLOCK
    tk = tq + 2 * BLOCK
    kv_main = pl.BlockSpec((None, tq, KV_W), lambda b, j: (b, j, 0))
    kv_prev = pl.BlockSpec((None, BLOCK, KV_W), lambda b, j: (b, jnp.maximum(j * hb - 1, 0), 0))
    kv_next = pl.BlockSpec((None, BLOCK, KV_W), lambda b, j: (b, jnp.minimum((j + 1) * hb, nb - 1), 0))
    qo = pl.BlockSpec((None, tq, Q_W), lambda b, j: (b, j, 0))
    nsteps = bsz * nq
    w_slice = lambda w: pl.BlockSpec((w.shape[0] // nsteps, w.shape[1]), lambda b, j: (b * nq + j, 0))
    w_specs = [w_slice(w) for w in ffn_weights]
    return pl.pallas_call(
        functools.partial(_attn_kernel, nb),
        grid=(bsz, nq),
        in_specs=[pl.BlockSpec(memory_space=pltpu.SMEM), qo,
                  kv_prev, kv_main, kv_next, kv_prev, kv_main, kv_next] + w_specs,
        out_specs=[qo] + w_specs,
        out_shape=[jax.ShapeDtypeStruct((bsz, s, Q_W), F32)]
        + [jax.ShapeDtypeStruct(w.shape, BF16) for w in ffn_weights],
        scratch_shapes=[
            pltpu.VMEM((2 * N_KV_HEADS, tk, LANES), BF16),
            pltpu.VMEM((N_KV_HEADS, 2 * tk, 2 * LANES), BF16),
            pltpu.VMEM((_N_BIAS, BLOCK, BLOCK), F32),
            pltpu.VMEM((2, N_HEADS, BLOCK, 3 * BLOCK), F32),
            pltpu.VMEM((2, N_HEADS, BLOCK, LANES), F32),
            pltpu.VMEM((2, N_KV_HEADS, 2 * BLOCK, 6 * BLOCK), BF16),
        ],
        compiler_params=pltpu.CompilerParams(
            dimension_semantics=("arbitrary", "arbitrary"), vmem_limit_bytes=VMEM_LIMIT_BYTES),
        name="attention",
    )(sink, q3, k3, k3, k3, v3, v3, v3, *ffn_weights)


def _ffn_kernel(x_ref, hf_ref, hb_ref, yb_ref, coef_ref, wo_ref, g2_ref, wi_ref, wd_ref, g3_ref, o_ref):
    merged = coef_ref[:, :D_MODEL] * (hf_ref[...] + hb_ref[...]) + coef_ref[:, D_MODEL:] * yb_ref[...]
    x1 = x_ref[...] + jnp.dot(merged.astype(BF16), wo_ref[...], preferred_element_type=F32)
    xn = _rmsnorm(x1, g2_ref[...]).astype(BF16)
    gate = jnp.dot(xn, wi_ref[:, :D_FF], preferred_element_type=F32)
    up = jnp.dot(xn, wi_ref[:, D_FF:], preferred_element_type=F32)
    act = ((gate * jnp.tanh(gate) + gate) * up).astype(BF16)
    x2 = x1 + jnp.dot(act, wd_ref[...], preferred_element_type=F32)
    o_ref[...] = _rmsnorm(x2, g3_ref[...])


def _ffn(x2, hf, hb, yb, coef, wo, g2, wi, wd, g3):
    t = x2.shape[0]
    tm = TM_FFN
    row = lambda w_: pl.BlockSpec((tm, w_), lambda i: (i, 0))
    return pl.pallas_call(
        _ffn_kernel,
        grid=(t // tm,),
        in_specs=[row(D_MODEL), row(D_MODEL), row(D_MODEL), row(D_MODEL), row(2 * D_MODEL),
                  _resident((D_MODEL, D_MODEL)), _resident((1, D_MODEL)),
                  _resident((D_MODEL, 2 * D_FF)), _resident((D_FF, D_MODEL)), _resident((1, D_MODEL))],
        out_specs=row(D_MODEL),
        out_shape=jax.ShapeDtypeStruct((t, D_MODEL), F32),
        compiler_params=pltpu.CompilerParams(
            dimension_semantics=("arbitrary",), vmem_limit_bytes=VMEM_LIMIT_BYTES),
        name="merge_ffn",
    )(x2, hf, hb, yb, coef, wo, g2, wi, wd, g3)


def _gate_weights(wa, wx):
    def blockdiag(w):
        w = w.reshape(N_GROUPS, LRU_GROUP // LRU_BLOCK, LRU_BLOCK, LRU_BLOCK)
        eye = jnp.eye(LRU_GROUP // LRU_BLOCK, dtype=w.dtype)
        return jnp.einsum("ghij,hk->ghikj", w, eye).reshape(N_GROUPS, LRU_GROUP, LRU_GROUP)
    return jnp.concatenate([blockdiag(wa), blockdiag(wx)], axis=-1).astype(BF16)


def kernel(x, norm_mix_g, w_in, b_gate, conv_w, conv_b, lru_lambda, lru_wa, lru_ba, lru_wx, lru_bx,
           attn_sink, w_out, norm_ffn_g, w_ffn_in, w_ffn_out, norm_final_g):
    bsz, s, d = x.shape
    depth = w_in.shape[0]
    assert depth == 1, "the merge/ffn kernel applies the final RMSNorm, so it must be the last layer"
    t = bsz * s
    row = lambda a: a.reshape(1, -1)
    time_major = lambda a: a.reshape(a.shape[:-1] + (LANE_TILES, LANES))
    x2 = x.reshape(t, d)
    for l in range(depth):
        huc, q, k, v, merge_coef = _in_proj(
            x2, s, row(norm_mix_g[l]), w_in[l].astype(BF16), row(b_gate[l]),
            time_major(0.5 * conv_w[l]), time_major(0.5 * conv_b[l]))
        seq = lambda a: a.reshape(bsz, s, a.shape[-1])
        lru = lambda dr: (row(lru_lambda[l, dr]), _gate_weights(lru_wa[l, dr], lru_wx[l, dr]),
                          row(lru_ba[l, dr]), row(lru_bx[l, dr]))
        h_fwd = _lru(False, seq(huc), *lru(0))
        h_bwd = _lru(True, seq(huc), *lru(1))
        y_b, wo_b, wi_b, wd_b = _attention(seq(q), seq(k), seq(v), attn_sink[l],
                                           (w_out[l], w_ffn_in[l], w_ffn_out[l]))
        x2 = _ffn(x2, h_fwd.reshape(t, d), h_bwd.reshape(t, d), y_b.reshape(t, d), merge_coef,
                  wo_b, row(norm_ffn_g[l]), wi_b, wd_b, row(norm_final_g))
    return x2.reshape(bsz, s, d)
```

```python
import functools
import math

import numpy as np
import jax
import jax.numpy as jnp
from jax import lax
from jax.experimental import pallas as pl
from jax.experimental.pallas import tpu as pltpu

F32 = jnp.float32
BF16 = jnp.bfloat16

D_MODEL = 1024
LRU_HEADS = 16
LRU_BLOCK = D_MODEL // LRU_HEADS
CONV_WIDTH = 4
CONV_LEFT = 2
RGLRU_C = 8.0
N_HEADS = 16
N_KV_HEADS = 4
HEAD_DIM = 64
GROUP = N_HEADS // N_KV_HEADS
WINDOW = 128
BLOCK = 128
D_FF = 2816
Q_W = N_HEADS * HEAD_DIM
KV_W = N_KV_HEADS * HEAD_DIM
IN_W = 2 * D_MODEL + Q_W + 2 * KV_W + 2 * D_MODEL
EPS = 1e-6
NEG_INF = -1e30
F32_TINY = float(np.finfo(np.float32).tiny)

LANES = 128
SUBLANES = 8
VMEM_LIMIT_BYTES = 56 * 1024 * 1024

TM_PROJ = 1024
TS_LRU = 1024
TQ_ATTN = 1024
TM_FFN = 512
LRU_GROUP = 256
N_GROUPS = D_MODEL // LRU_GROUP
LANE_TILES = D_MODEL // LANES
assert LANE_TILES == SUBLANES
SLAB = SUBLANES * LANE_TILES
TILES_PER_STEP = 32
N_SLABS = 3 * TILES_PER_STEP
CONV_RING = 8

_ALIBI_SLOPES = [float(v) for v in np.exp2(
    -8.0 * (np.arange(N_HEADS, dtype=np.float32) + 1.0) / N_HEADS).astype(np.float32)]
LOG2_E = math.log2(math.e)
Q_SCALE = HEAD_DIM ** -0.5 * LOG2_E


def _sigmoid(x):
    return 0.5 * jnp.tanh(0.5 * x) + 0.5


def _rmsnorm(x, g):
    return x * lax.rsqrt(jnp.mean(x * x, axis=-1, keepdims=True) + EPS) * g


def _resident(shape):
    nd = len(shape)
    return pl.BlockSpec(shape, lambda *_: (0,) * nd, pipeline_mode=pl.Buffered(1))


def _in_proj_kernel(nblk, x_ref, xp_ref, xnx_ref, g_ref, w_ref, bg_ref, cw_ref, cb_ref,
                    uc_ref, q_ref, k_ref, v_ref, coef_ref, uext_ref, slab_ref):
    tm = x_ref.shape[0]
    nt = tm // SUBLANES
    jb = pl.program_id(0) % nblk
    gn = g_ref[...]
    xn = _rmsnorm(x_ref[...], gn).astype(BF16)
    halo = jnp.concatenate([xp_ref[...], xnx_ref[...]], axis=0)
    xh = _rmsnorm(halo, gn).astype(BF16)

    def proj(lo, hi):
        return jnp.dot(xn, w_ref[:, lo:hi], preferred_element_type=F32)

    ue = jnp.dot(jnp.concatenate([xn, xh], axis=0), w_ref[:, :D_MODEL], preferred_element_type=F32)
    uext_ref[SUBLANES:SUBLANES + tm, :] = ue[:tm]
    uext_ref[0:SUBLANES, :] = jnp.where(jb > 0, ue[tm:tm + SUBLANES], 0.0)
    uext_ref[SUBLANES + tm:, :] = jnp.where(jb < nblk - 1, ue[tm + SUBLANES:], 0.0)

    o_g = D_MODEL
    o_q = o_g + D_MODEL
    o_k = o_q + Q_W
    o_v = o_k + KV_W
    o_z = o_v + KV_W
    pw = 2 * LANES
    todo = []

    def coef_a(c):
        gate = _sigmoid(proj(o_z + c, o_z + c + pw) + bg_ref[:, c:c + pw])
        coef_ref[:, c:c + pw] = gate * _gelu_tanh(proj(o_g + c, o_g + c + pw))

    def coef_b(c):
        cz = D_MODEL + c
        coef_ref[:, cz:cz + pw] = _sigmoid(proj(o_z + cz, o_z + cz + pw) + bg_ref[:, cz:cz + pw])

    def plain(dst, lo, post):
        def run(c):
            dst[:, c:c + pw] = post(proj(lo + c, lo + c + pw))
        return run

    for fn, width in ((coef_a, D_MODEL), (coef_b, D_MODEL),
                      (plain(q_ref, o_q, lambda v: (v * Q_SCALE).astype(BF16)), Q_W),
                      (plain(k_ref, o_k, lambda v: v.astype(BF16)), KV_W),
                      (plain(v_ref, o_v, lambda v: v.astype(BF16)), KV_W)):
        todo += [(fn, c) for c in range(0, width, pw)]
    every = -(-nt // len(todo))

    def project_block():
        fn, c = todo.pop(0)
        fn(c)

    w = [cw_ref[k] for k in range(CONV_WIDTH)]
    cb = cb_ref[...]
    ring = CONV_RING

    def ext_steps(e, s):
        return _split_time(uext_ref[e * SUBLANES:(e + 1) * SUBLANES, :], slab_ref, s)

    tail = ext_steps(0, 0)
    prev2, prev1 = tail[SUBLANES - 2], tail[SUBLANES - 1]
    cur = ext_steps(1, 1)
    for t in range(nt):
        if t % every == 0 and todo:
            project_block()
        nxt = ext_steps(t + 2, 2 * (t % ring))
        x = [prev2, prev1] + cur + [nxt[0]]
        out = [(x[r] * w[0] + x[r + 1] * w[1]) + (x[r + 2] * w[2] + x[r + 3] * w[3]) + cb
               for r in range(SUBLANES)]
        uc_ref[t * SUBLANES:(t + 1) * SUBLANES, :] = _join_time(out, slab_ref, 2 * (t % ring) + 1)
        prev2, prev1, cur = cur[SUBLANES - 2], cur[SUBLANES - 1], nxt
    while todo:
        project_block()


def _in_proj(x2, seq, g, w, bg, cw, cb):
    t = x2.shape[0]
    tm = TM_PROJ
    nblk = seq // tm
    hb = tm // SUBLANES
    nh = t // SUBLANES
    row = lambda w_: pl.BlockSpec((tm, w_), lambda i: (i, 0))
    prev = pl.BlockSpec((SUBLANES, D_MODEL), lambda i: (jnp.maximum(i * hb - 1, 0), 0))
    nxt = pl.BlockSpec((SUBLANES, D_MODEL), lambda i: (jnp.minimum((i + 1) * hb, nh - 1), 0))
    return pl.pallas_call(
        functools.partial(_in_proj_kernel, nblk),
        grid=(t // tm,),
        in_specs=[row(D_MODEL), prev, nxt, _resident((1, D_MODEL)), _resident((D_MODEL, IN_W)),
                  _resident((1, 2 * D_MODEL)), _resident(cw.shape), _resident(cb.shape)],
        out_specs=[row(D_MODEL), row(Q_W), row(KV_W), row(KV_W), row(2 * D_MODEL)],
        out_shape=[
            jax.ShapeDtypeStruct((t, D_MODEL), F32),
            jax.ShapeDtypeStruct((t, Q_W), BF16),
            jax.ShapeDtypeStruct((t, KV_W), BF16),
            jax.ShapeDtypeStruct((t, KV_W), BF16),
            jax.ShapeDtypeStruct((t, 2 * D_MODEL), F32),
        ],
        scratch_shapes=[
            pltpu.VMEM((tm + 2 * SUBLANES, D_MODEL), F32),
            pltpu.VMEM((2 * CONV_RING, SLAB, LANES), F32),
        ],
        compiler_params=pltpu.CompilerParams(
            dimension_semantics=("arbitrary",), vmem_limit_bytes=VMEM_LIMIT_BYTES),
        name="in_proj",
    )(x2, x2, x2, g, w, bg, cw, cb)


def _split_time(tile, slab_ref, s):
    for j in range(LANE_TILES):
        slab_ref[s, j * SUBLANES:(j + 1) * SUBLANES, :] = tile[:, j * LANES:(j + 1) * LANES]
    return [slab_ref[s, pl.ds(r, LANE_TILES, stride=SUBLANES), :] for r in range(SUBLANES)]


def _join_time(steps, slab_ref, s):
    for r in range(SUBLANES):
        slab_ref[s, pl.ds(r, LANE_TILES, stride=SUBLANES), :] = steps[r]
    return jnp.concatenate(
        [slab_ref[s, j * SUBLANES:(j + 1) * SUBLANES, :] for j in range(LANE_TILES)], axis=1)


def _half_decay_log2(lam):
    return (-0.5 * RGLRU_C * math.log2(math.e)) * (
        jnp.maximum(-lam, 0.0) + jnp.log1p(jnp.exp(-jnp.abs(lam))))


def _gate_terms(pre_r, pre_i, huc, hd2, hba, hbx):
    a = jnp.exp2(hd2 * jnp.tanh(pre_r + hba) + hd2)
    y = jnp.maximum(1.0 - a * a, 0.0)
    beta = y * lax.rsqrt(jnp.maximum(y, F32_TINY))
    return a, beta * (jnp.tanh(pre_i + hbx) * huc + huc)


def _gelu_tanh(g):
    c = math.sqrt(2.0 / math.pi)
    half = 0.5 * g
    return half * jnp.tanh(g * ((c * 0.044715) * (g * g) + c)) + half


def _gate_maps(uc_ref, wg_ref, ra_ref, ix_ref, rows):
    for gi in range(N_GROUPS):
        cols = slice(gi * LRU_GROUP, (gi + 1) * LRU_GROUP)
        pre = jnp.dot(uc_ref[rows, cols].astype(BF16), wg_ref[gi], preferred_element_type=F32)
        ra_ref[rows, cols] = pre[:, :LRU_GROUP]
        ix_ref[rows, cols] = pre[:, LRU_GROUP:]


def _recurrence(reverse, uc_ref, wg_ref, ra_ref, ix_ref, lam_ref, ba_ref, bx_ref, carry_ref, slab_ref,
                emit):
    ts = uc_ref.shape[0]
    nt = ts // SUBLANES
    nsteps = nt // TILES_PER_STEP
    hd2 = _half_decay_log2(lam_ref[...])
    hba = 0.5 * ba_ref[...]
    hbx = 0.5 * bx_ref[...]

    def tile(k, s, h):
        kk = (nt - 1 - k) if reverse else k
        rows = pl.ds(pl.multiple_of(kk * SUBLANES, SUBLANES), SUBLANES)
        a, b = _gate_terms(ra_ref[rows, :], ix_ref[rows, :], uc_ref[rows, :], hd2, hba, hbx)
        a_t = _split_time(a, slab_ref, 3 * s)
        b_t = _split_time(b, slab_ref, 3 * s + 1)
        hs = [None] * SUBLANES
        for r in (range(SUBLANES - 1, -1, -1) if reverse else range(SUBLANES)):
            h = a_t[r] * h + b_t[r]
            hs[r] = h
        emit(rows, _join_time(hs, slab_ref, 3 * s + 2))
        return h

    def group(i, h):
        for s in range(TILES_PER_STEP):
            h = tile(TILES_PER_STEP * i + s, s, h)
        return h

    _gate_maps(uc_ref, wg_ref, ra_ref, ix_ref, slice(0, ts))
    carry_ref[...] = lax.fori_loop(0, nsteps, group, carry_ref[...])


def _lru_kernel(reverse, uc_ref, lam_ref, wg_ref, ba_ref, bx_ref, h_ref, ra_ref, ix_ref, carry_ref,
                slab_ref):
    @pl.when(pl.program_id(1) == 0)
    def _():
        carry_ref[...] = jnp.zeros_like(carry_ref)

    def emit(rows, h):
        h_ref[rows, :] = h

    _recurrence(reverse, uc_ref, wg_ref, ra_ref, ix_ref, lam_ref, ba_ref, bx_ref, carry_ref, slab_ref,
                emit)


def _lru(reverse, uc3, lam, wg, ba, bx):
    bsz, s, _ = uc3.shape
    ts = TS_LRU
    nblk = s // ts
    blk = pl.BlockSpec((None, ts, D_MODEL), lambda b, j: (b, (nblk - 1 - j) if reverse else j, 0))
    return pl.pallas_call(
        functools.partial(_lru_kernel, reverse),
        grid=(bsz, nblk),
        in_specs=[blk, _resident((1, D_MODEL)), _resident(wg.shape),
                  _resident((1, D_MODEL)), _resident((1, D_MODEL))],
        out_specs=blk,
        out_shape=jax.ShapeDtypeStruct((bsz, s, D_MODEL), F32),
        scratch_shapes=[
            pltpu.VMEM((ts, D_MODEL), F32),
            pltpu.VMEM((ts, D_MODEL), F32),
            pltpu.VMEM((LANE_TILES, LANES), F32),
            pltpu.VMEM((N_SLABS, SLAB, LANES), F32),
        ],
        compiler_params=pltpu.CompilerParams(
            dimension_semantics=("arbitrary", "arbitrary"), vmem_limit_bytes=VMEM_LIMIT_BYTES),
        name="lru_bwd" if reverse else "lru_fwd",
    )(uc3, lam, wg, ba, bx)


_N_BIAS = 3 * N_HEADS + 1
_MASKED_TILE = 3 * N_HEADS


def _attn_kernel(nb, sink_ref, q_ref, kp_ref, km_ref, kn_ref, vp_ref, vm_ref, vn_ref,
                 wo_ref, wi_ref, wd_ref, o_ref, wob_ref, wib_ref, wdb_ref,
                 kz_ref, vab_ref, bias_ref, s_ref, m_ref, p_ref):
    tq = q_ref.shape[0]
    tk = tq + 2 * BLOCK
    nsub = tq // BLOCK
    j = pl.program_id(1)

    wob_ref[...] = wo_ref[...].astype(BF16)
    wdb_ref[...] = wd_ref[...].astype(BF16)
    gate_col = lax.broadcasted_iota(jnp.int32, (1, 2 * D_FF), 1) < D_FF
    wib_ref[...] = (wi_ref[...] * jnp.where(gate_col, 0.5, 1.0)).astype(BF16)

    @pl.when((pl.program_id(0) == 0) & (j == 0))
    def _():
        row = lax.broadcasted_iota(jnp.int32, (BLOCK, BLOCK), 0)
        col = lax.broadcasted_iota(jnp.int32, (BLOCK, BLOCK), 1)
        bias_ref[_MASKED_TILE] = jnp.ones((BLOCK, BLOCK), F32)
        for c in range(3):
            dist = jnp.abs(row + (1 - c) * BLOCK - col)
            absd = dist.astype(F32)
            for h in range(N_HEADS):
                bias_ref[3 * h + c] = jnp.where(dist <= WINDOW, (-_ALIBI_SLOPES[h] * absd) * LOG2_E, 1.0)

    low = lax.broadcasted_iota(jnp.int32, (tk, LANES), 1) < HEAD_DIM
    low_q = lax.broadcasted_iota(jnp.int32, (BLOCK, LANES), 1) < HEAD_DIM
    ones_q = [jnp.where(low_q, 1.0, 0.0).astype(BF16), jnp.where(low_q, 0.0, 1.0).astype(BF16)]
    for t in range(KV_W // LANES):
        cols = slice(t * LANES, (t + 1) * LANES)
        full = jnp.concatenate([r[:, cols] for r in (kp_ref, km_ref, kn_ref)], axis=0).astype(F32)
        swapped = pltpu.roll(full, HEAD_DIM, axis=1)
        kz_ref[4 * t + 0] = jnp.where(low, full, 0.0).astype(BF16)
        kz_ref[4 * t + 1] = jnp.where(low, 0.0, swapped).astype(BF16)
        kz_ref[4 * t + 2] = jnp.where(low, swapped, 0.0).astype(BF16)
        kz_ref[4 * t + 3] = jnp.where(low, 0.0, full).astype(BF16)
        nkb = tk // BLOCK
        for b in range(nkb):
            if b == 0:
                src = vp_ref[:, cols]
            elif b == nkb - 1:
                src = vn_ref[:, cols]
            else:
                src = vm_ref[(b - 1) * BLOCK:b * BLOCK, cols]
            full = src.astype(F32)
            swapped = pltpu.roll(full, HEAD_DIM, axis=1)
            placed = [jnp.where(low_q, full, 0.0), jnp.where(low_q, 0.0, swapped),
                      jnp.where(low_q, swapped, 0.0), jnp.where(low_q, 0.0, full)]
            for i in range(4):
                g, half = 2 * t + i // 2, i % 2
                dst = slice((2 * b + half) * BLOCK, (2 * b + half + 1) * BLOCK)
                vab_ref[g, dst, :LANES] = placed[i].astype(BF16)
                vab_ref[g, dst, LANES:] = ones_q[half]

    def scores(qi, slot):
        q0 = pl.multiple_of(qi * BLOCK, BLOCK)
        blk = j * nsub + qi
        win = pl.ds(q0, 3 * BLOCK)
        sc2 = [None, None]
        for h in range(N_HEADS):
            g, pp, half = h // GROUP, (h % GROUP) // 2, h % 2
            if pp == 0:
                q2 = jnp.concatenate(
                    [q_ref[pl.ds(q0, BLOCK), (2 * g + i) * LANES:(2 * g + i + 1) * LANES] for i in range(2)],
                    axis=0)
                sc2[half] = lax.dot_general(q2, kz_ref[2 * g + half, win, :], (((1,), (1,)), ((), ())),
                                            preferred_element_type=F32)
            sc = sc2[half][pp * BLOCK:(pp + 1) * BLOCK]
            tiles = []
            for c in range(3):
                t = sc[:, c * BLOCK:(c + 1) * BLOCK]
                if c == 1:
                    t = t + bias_ref[3 * h + 1]
                else:
                    edge = (blk == 0) if c == 0 else (blk == nb - 1)
                    b = bias_ref[jnp.where(edge, _MASKED_TILE, 3 * h + c)]
                    t = jnp.where(b > 0.0, NEG_INF, t + b)
                s_ref[slot, h, :, c * BLOCK:(c + 1) * BLOCK] = t
                tiles.append(t)
            mx = jnp.max(jnp.maximum(jnp.maximum(tiles[0], tiles[1]), tiles[2]), axis=-1, keepdims=True)
            m_ref[slot, h] = jnp.broadcast_to(jnp.maximum(mx, sink_ref[h] * LOG2_E), (BLOCK, LANES))

    def finish(qi, slot):
        q0 = pl.multiple_of(qi * BLOCK, BLOCK)
        win2 = pl.ds(pl.multiple_of(qi * (2 * BLOCK), 2 * BLOCK), 6 * BLOCK)
        for h in range(N_HEADS):
            m = m_ref[slot, h]
            for c in range(3):
                cs = slice(c * BLOCK, (c + 1) * BLOCK)
                dst = slice((2 * c + h % 2) * BLOCK, (2 * c + h % 2 + 1) * BLOCK)
                prow = slice(((h // 2) % 2) * BLOCK, ((h // 2) % 2 + 1) * BLOCK)
                p_ref[slot, h // GROUP, prow, dst] = jnp.exp2(s_ref[slot, h, :, cs] - m).astype(BF16)
        for g in range(N_KV_HEADS):
            o2 = jnp.dot(p_ref[slot, g], vab_ref[g, win2, :], preferred_element_type=F32)
            for pp in range(2):
                pr = 2 * g + pp
                h0, h1 = 2 * pr, 2 * pr + 1
                o = o2[pp * BLOCK:(pp + 1) * BLOCK]
                sink2 = jnp.where(low_q[0:1, :], sink_ref[h0] * LOG2_E, sink_ref[h1] * LOG2_E)
                e = jnp.exp2(sink2 - jnp.where(low_q, m_ref[slot, h0], m_ref[slot, h1]))
                o_ref[pl.ds(q0, BLOCK), pr * LANES:(pr + 1) * LANES] = (
                    o[:, :LANES] * (1.0 / (o[:, LANES:] + e)))

    scores(0, 0)

    def two_sub_blocks(i, _):
        qa = 2 * i
        scores(qa + 1, 1)
        finish(qa, 0)
        scores(jnp.minimum(qa + 2, nsub - 1), 0)
        finish(qa + 1, 1)
        return 0

    lax.fori_loop(0, nsub // 2, two_sub_blocks, 0)


def _attention(q3, k3, v3, sink, ffn_weights):
    bsz, s, _ = q3.shape
    tq = TQ_ATTN
    nq = s // tq
    hb = tq // BLOCK
    nb = s // BLOCK
    tk = tq + 2 * BLOCK
    kv_main = pl.BlockSpec((None, tq, KV_W), lambda b, j: (b, j, 0))
    kv_prev = pl.BlockSpec((None, BLOCK, KV_W), lambda b, j: (b, jnp.maximum(j * hb - 1, 0), 0))
    kv_next = pl.BlockSpec((None, BLOCK, KV_W), lambda b, j: (b, jnp.minimum((j + 1) * hb, nb - 1), 0))
    qo = pl.BlockSpec((None, tq, Q_W), lambda b, j: (b, j, 0))
    nsteps = bsz * nq
    w_slice = lambda w: pl.BlockSpec((w.shape[0] // nsteps, w.shape[1]), lambda b, j: (b * nq + j, 0))
    w_specs = [w_slice(w) for w in ffn_weights]
    return pl.pallas_call(
        functools.partial(_attn_kernel, nb),
        grid=(bsz, nq),
        in_specs=[pl.BlockSpec(memory_space=pltpu.SMEM), qo,
                  kv_prev, kv_main, kv_next, kv_prev, kv_main, kv_next] + w_specs,
        out_specs=[qo] + w_specs,
        out_shape=[jax.ShapeDtypeStruct((bsz, s, Q_W), F32)]
        + [jax.ShapeDtypeStruct(w.shape, BF16) for w in ffn_weights],
        scratch_shapes=[
            pltpu.VMEM((2 * N_KV_HEADS, tk, LANES), BF16),
            pltpu.VMEM((N_KV_HEADS, 2 * tk, 2 * LANES), BF16),
            pltpu.VMEM((_N_BIAS, BLOCK, BLOCK), F32),
            pltpu.VMEM((2, N_HEADS, BLOCK, 3 * BLOCK), F32),
            pltpu.VMEM((2, N_HEADS, BLOCK, LANES), F32),
            pltpu.VMEM((2, N_KV_HEADS, 2 * BLOCK, 6 * BLOCK), BF16),
        ],
        compiler_params=pltpu.CompilerParams(
            dimension_semantics=("arbitrary", "arbitrary"), vmem_limit_bytes=VMEM_LIMIT_BYTES),
        name="attention",
    )(sink, q3, k3, k3, k3, v3, v3, v3, *ffn_weights)


def _ffn_kernel(x_ref, hf_ref, hb_ref, yb_ref, coef_ref, wo_ref, g2_ref, wi_ref, wd_ref, g3_ref, o_ref):
    merged = coef_ref[:, :D_MODEL] * (hf_ref[...] + hb_ref[...]) + coef_ref[:, D_MODEL:] * yb_ref[...]
    x1 = x_ref[...] + jnp.dot(merged.astype(BF16), wo_ref[...], preferred_element_type=F32)
    xn = _rmsnorm(x1, g2_ref[...]).astype(BF16)
    gate = jnp.dot(xn, wi_ref[:, :D_FF], preferred_element_type=F32)
    up = jnp.dot(xn, wi_ref[:, D_FF:], preferred_element_type=F32)
    act = ((gate * jnp.tanh(gate) + gate) * up).astype(BF16)
    x2 = x1 + jnp.dot(act, wd_ref[...], preferred_element_type=F32)
    o_ref[...] = _rmsnorm(x2, g3_ref[...])


def _ffn(x2, hf, hb, yb, coef, wo, g2, wi, wd, g3):
    t = x2.shape[0]
    tm = TM_FFN
    row = lambda w_: pl.BlockSpec((tm, w_), lambda i: (i, 0))
    return pl.pallas_call(
        _ffn_kernel,
        grid=(t // tm,),
        in_specs=[row(D_MODEL), row(D_MODEL), row(D_MODEL), row(D_MODEL), row(2 * D_MODEL),
                  _resident((D_MODEL, D_MODEL)), _resident((1, D_MODEL)),
                  _resident((D_MODEL, 2 * D_FF)), _resident((D_FF, D_MODEL)), _resident((1, D_MODEL))],
        out_specs=row(D_MODEL),
        out_shape=jax.ShapeDtypeStruct((t, D_MODEL), F32),
        compiler_params=pltpu.CompilerParams(
            dimension_semantics=("arbitrary",), vmem_limit_bytes=VMEM_LIMIT_BYTES),
        name="merge_ffn",
    )(x2, hf, hb, yb, coef, wo, g2, wi, wd, g3)


def _gate_weights(wa, wx):
    def blockdiag(w):
        w = w.reshape(N_GROUPS, LRU_GROUP // LRU_BLOCK, LRU_BLOCK, LRU_BLOCK)
        eye = jnp.eye(LRU_GROUP // LRU_BLOCK, dtype=w.dtype)
        return jnp.einsum("ghij,hk->ghikj", w, eye).reshape(N_GROUPS, LRU_GROUP, LRU_GROUP)
    return jnp.concatenate([blockdiag(wa), blockdiag(wx)], axis=-1).astype(BF16)


def kernel(x, norm_mix_g, w_in, b_gate, conv_w, conv_b, lru_lambda, lru_wa, lru_ba, lru_wx, lru_bx,
           attn_sink, w_out, norm_ffn_g, w_ffn_in, w_ffn_out, norm_final_g):
    bsz, s, d = x.shape
    depth = w_in.shape[0]
    assert depth == 1, "the merge/ffn kernel applies the final RMSNorm, so it must be the last layer"
    t = bsz * s
    row = lambda a: a.reshape(1, -1)
    time_major = lambda a: a.reshape(a.shape[:-1] + (LANE_TILES, LANES))
    x2 = x.reshape(t, d)
    for l in range(depth):
        huc, q, k, v, merge_coef = _in_proj(
            x2, s, row(norm_mix_g[l]), w_in[l].astype(BF16), row(b_gate[l]),
            time_major(0.5 * conv_w[l]), time_major(0.5 * conv_b[l]))
        seq = lambda a: a.reshape(bsz, s, a.shape[-1])
        lru = lambda dr: (row(lru_lambda[l, dr]), _gate_weights(lru_wa[l, dr], lru_wx[l, dr]),
                          row(lru_ba[l, dr]), row(lru_bx[l, dr]))
        h_fwd = _lru(False, seq(huc), *lru(0))
        h_bwd = _lru(True, seq(huc), *lru(1))
        y_b, wo_b, wi_b, wd_b = _attention(seq(q), seq(k), seq(v), attn_sink[l],
                                           (w_out[l], w_ffn_in[l], w_ffn_out[l]))
        x2 = _ffn(x2, h_fwd.reshape(t, d), h_bwd.reshape(t, d), y_b.reshape(t, d), merge_coef,
                  wo_b, row(norm_ffn_g[l]), wi_b, wd_b, row(norm_final_g))
    return x2.reshape(bsz, s, d)
```

```python
import functools
import math

import numpy as np
import jax
import jax.numpy as jnp
from jax import lax
from jax.experimental import pallas as pl
from jax.experimental.pallas import tpu as pltpu

F32 = jnp.float32
BF16 = jnp.bfloat16

D_MODEL = 1024
LRU_HEADS = 16
LRU_BLOCK = D_MODEL // LRU_HEADS
CONV_WIDTH = 4
CONV_LEFT = 2
RGLRU_C = 8.0
N_HEADS = 16
N_KV_HEADS = 4
HEAD_DIM = 64
GROUP = N_HEADS // N_KV_HEADS
WINDOW = 128
BLOCK = 128
D_FF = 2816
Q_W = N_HEADS * HEAD_DIM
KV_W = N_KV_HEADS * HEAD_DIM
IN_W = 2 * D_MODEL + Q_W + 2 * KV_W + 2 * D_MODEL
EPS = 1e-6
NEG_INF = -1e30
F32_TINY = float(np.finfo(np.float32).tiny)

LANES = 128
SUBLANES = 8
VMEM_LIMIT_BYTES = 56 * 1024 * 1024

TM_PROJ = 1024
TS_LRU = 1024
TQ_ATTN = 1024
TM_FFN = 512
LRU_GROUP = 256
N_GROUPS = D_MODEL // LRU_GROUP
LANE_TILES = D_MODEL // LANES
assert LANE_TILES == SUBLANES
SLAB = SUBLANES * LANE_TILES
TILES_PER_STEP = 32
N_SLABS = 3 * TILES_PER_STEP
CONV_RING = 8

_ALIBI_SLOPES = [float(v) for v in np.exp2(
    -8.0 * (np.arange(N_HEADS, dtype=np.float32) + 1.0) / N_HEADS).astype(np.float32)]
LOG2_E = math.log2(math.e)
Q_SCALE = HEAD_DIM ** -0.5 * LOG2_E


def _sigmoid(x):
    return 0.5 * jnp.tanh(0.5 * x) + 0.5


def _rmsnorm(x, g):
    return x * lax.rsqrt(jnp.mean(x * x, axis=-1, keepdims=True) + EPS) * g


def _resident(shape):
    nd = len(shape)
    return pl.BlockSpec(shape, lambda *_: (0,) * nd, pipeline_mode=pl.Buffered(1))


def _in_proj_kernel(nblk, x_ref, xp_ref, xnx_ref, g_ref, w_ref, bg_ref, cw_ref, cb_ref,
                    uc_ref, q_ref, k_ref, v_ref, coef_ref, uext_ref, slab_ref):
    tm = x_ref.shape[0]
    nt = tm // SUBLANES
    jb = pl.program_id(0) % nblk
    gn = g_ref[...]
    xn = _rmsnorm(x_ref[...], gn).astype(BF16)
    halo = jnp.concatenate([xp_ref[...], xnx_ref[...]], axis=0)
    xh = _rmsnorm(halo, gn).astype(BF16)

    def proj(lo, hi):
        return jnp.dot(xn, w_ref[:, lo:hi], preferred_element_type=F32)

    ue = jnp.dot(jnp.concatenate([xn, xh], axis=0), w_ref[:, :D_MODEL], preferred_element_type=F32)
    uext_ref[SUBLANES:SUBLANES + tm, :] = ue[:tm]
    uext_ref[0:SUBLANES, :] = jnp.where(jb > 0, ue[tm:tm + SUBLANES], 0.0)
    uext_ref[SUBLANES + tm:, :] = jnp.where(jb < nblk - 1, ue[tm + SUBLANES:], 0.0)

    o_g = D_MODEL
    o_q = o_g + D_MODEL
    o_k = o_q + Q_W
    o_v = o_k + KV_W
    o_z = o_v + KV_W
    pw = 2 * LANES
    todo = []

    def coef_a(c):
        gate = _sigmoid(proj(o_z + c, o_z + c + pw) + bg_ref[:, c:c + pw])
        coef_ref[:, c:c + pw] = gate * _gelu_tanh(proj(o_g + c, o_g + c + pw))

    def coef_b(c):
        cz = D_MODEL + c
        coef_ref[:, cz:cz + pw] = _sigmoid(proj(o_z + cz, o_z + cz + pw) + bg_ref[:, cz:cz + pw])

    def plain(dst, lo, post):
        def run(c):
            dst[:, c:c + pw] = post(proj(lo + c, lo + c + pw))
        return run

    for fn, width in ((coef_a, D_MODEL), (coef_b, D_MODEL),
                      (plain(q_ref, o_q, lambda v: (v * Q_SCALE).astype(BF16)), Q_W),
                      (plain(k_ref, o_k, lambda v: v.astype(BF16)), KV_W),
                      (plain(v_ref, o_v, lambda v: v.astype(BF16)), KV_W)):
        todo += [(fn, c) for c in range(0, width, pw)]
    every = -(-nt // len(todo))

    def project_block():
        fn, c = todo.pop(0)
        fn(c)

    w = [0.5 * cw_ref[k] for k in range(CONV_WIDTH)]
    cb = 0.5 * cb_ref[...]
    ring = CONV_RING

    def ext_steps(e, s):
        return _split_time(uext_ref[e * SUBLANES:(e + 1) * SUBLANES, :], slab_ref, s)

    tail = ext_steps(0, 0)
    prev2, prev1 = tail[SUBLANES - 2], tail[SUBLANES - 1]
    cur = ext_steps(1, 1)
    for t in range(nt):
        if t % every == 0 and todo:
            project_block()
        nxt = ext_steps(t + 2, 2 * (t % ring))
        x = [prev2, prev1] + cur + [nxt[0]]
        out = [(x[r] * w[0] + x[r + 1] * w[1]) + (x[r + 2] * w[2] + x[r + 3] * w[3]) + cb
               for r in range(SUBLANES)]
        uc_ref[t * SUBLANES:(t + 1) * SUBLANES, :] = _join_time(out, slab_ref, 2 * (t % ring) + 1)
        prev2, prev1, cur = cur[SUBLANES - 2], cur[SUBLANES - 1], nxt
    while todo:
        project_block()


def _in_proj(x2, seq, g, w, bg, cw, cb):
    t = x2.shape[0]
    tm = TM_PROJ
    nblk = seq // tm
    hb = tm // SUBLANES
    nh = t // SUBLANES
    row = lambda w_: pl.BlockSpec((tm, w_), lambda i: (i, 0))
    prev = pl.BlockSpec((SUBLANES, D_MODEL), lambda i: (jnp.maximum(i * hb - 1, 0), 0))
    nxt = pl.BlockSpec((SUBLANES, D_MODEL), lambda i: (jnp.minimum((i + 1) * hb, nh - 1), 0))
    return pl.pallas_call(
        functools.partial(_in_proj_kernel, nblk),
        grid=(t // tm,),
        in_specs=[row(D_MODEL), prev, nxt, _resident((1, D_MODEL)), _resident((D_MODEL, IN_W)),
                  _resident((1, 2 * D_MODEL)), _resident(cw.shape), _resident(cb.shape)],
        out_specs=[row(D_MODEL), row(Q_W), row(KV_W), row(KV_W), row(2 * D_MODEL)],
        out_shape=[
            jax.ShapeDtypeStruct((t, D_MODEL), F32),
            jax.ShapeDtypeStruct((t, Q_W), BF16),
            jax.ShapeDtypeStruct((t, KV_W), BF16),
            jax.ShapeDtypeStruct((t, KV_W), BF16),
            jax.ShapeDtypeStruct((t, 2 * D_MODEL), F32),
        ],
        scratch_shapes=[
            pltpu.VMEM((tm + 2 * SUBLANES, D_MODEL), F32),
            pltpu.VMEM((2 * CONV_RING, SLAB, LANES), F32),
        ],
        compiler_params=pltpu.CompilerParams(
            dimension_semantics=("arbitrary",), vmem_limit_bytes=VMEM_LIMIT_BYTES),
        name="in_proj",
    )(x2, x2, x2, g, w, bg, cw, cb)


def _split_time(tile, slab_ref, s):
    for j in range(LANE_TILES):
        slab_ref[s, j * SUBLANES:(j + 1) * SUBLANES, :] = tile[:, j * LANES:(j + 1) * LANES]
    return [slab_ref[s, pl.ds(r, LANE_TILES, stride=SUBLANES), :] for r in range(SUBLANES)]


def _join_time(steps, slab_ref, s):
    for r in range(SUBLANES):
        slab_ref[s, pl.ds(r, LANE_TILES, stride=SUBLANES), :] = steps[r]
    return jnp.concatenate(
        [slab_ref[s, j * SUBLANES:(j + 1) * SUBLANES, :] for j in range(LANE_TILES)], axis=1)


def _half_decay_log2(lam):
    return (-0.5 * RGLRU_C * math.log2(math.e)) * (
        jnp.maximum(-lam, 0.0) + jnp.log1p(jnp.exp(-jnp.abs(lam))))


def _gate_terms(pre_r, pre_i, huc, hd2, hba, hbx):
    a = jnp.exp2(hd2 * jnp.tanh(pre_r + hba) + hd2)
    y = jnp.maximum(1.0 - a * a, 0.0)
    beta = y * lax.rsqrt(jnp.maximum(y, F32_TINY))
    return a, beta * (jnp.tanh(pre_i + hbx) * huc + huc)


def _gelu_tanh(g):
    c = math.sqrt(2.0 / math.pi)
    half = 0.5 * g
    return half * jnp.tanh(g * ((c * 0.044715) * (g * g) + c)) + half


def _gate_maps(uc_ref, wg_ref, ra_ref, ix_ref, rows):
    for gi in range(N_GROUPS):
        cols = slice(gi * LRU_GROUP, (gi + 1) * LRU_GROUP)
        pre = jnp.dot(uc_ref[rows, cols].astype(BF16), wg_ref[gi], preferred_element_type=F32)
        ra_ref[rows, cols] = pre[:, :LRU_GROUP]
        ix_ref[rows, cols] = pre[:, LRU_GROUP:]


def _recurrence(reverse, uc_ref, wg_ref, ra_ref, ix_ref, lam_ref, ba_ref, bx_ref, carry_ref, slab_ref,
                emit):
    ts = uc_ref.shape[0]
    nt = ts // SUBLANES
    nsteps = nt // TILES_PER_STEP
    d = slice(int(reverse), int(reverse) + 1)
    hd2 = _half_decay_log2(lam_ref[d, :])
    hba = 0.5 * ba_ref[d, :]
    hbx = 0.5 * bx_ref[d, :]

    def tile(k, s, h):
        kk = (nt - 1 - k) if reverse else k
        rows = pl.ds(pl.multiple_of(kk * SUBLANES, SUBLANES), SUBLANES)
        a, b = _gate_terms(ra_ref[rows, :], ix_ref[rows, :], uc_ref[rows, :], hd2, hba, hbx)
        a_t = _split_time(a, slab_ref, 3 * s)
        b_t = _split_time(b, slab_ref, 3 * s + 1)
        hs = [None] * SUBLANES
        for r in (range(SUBLANES - 1, -1, -1) if reverse else range(SUBLANES)):
            h = a_t[r] * h + b_t[r]
            hs[r] = h
        emit(rows, _join_time(hs, slab_ref, 3 * s + 2))
        return h

    def group(i, h):
        for s in range(TILES_PER_STEP):
            h = tile(TILES_PER_STEP * i + s, s, h)
        return h

    _gate_maps(uc_ref, wg_ref, ra_ref, ix_ref, slice(0, ts))
    carry_ref[...] = lax.fori_loop(0, nsteps, group, carry_ref[...])


def _lru_kernel(reverse, uc_ref, lam_ref, wg_ref, ba_ref, bx_ref, h_ref, ra_ref, ix_ref, carry_ref,
                slab_ref):
    @pl.when(pl.program_id(1) == 0)
    def _():
        carry_ref[...] = jnp.zeros_like(carry_ref)

    def emit(rows, h):
        h_ref[rows, :] = h

    _recurrence(reverse, uc_ref, wg_ref, ra_ref, ix_ref, lam_ref, ba_ref, bx_ref, carry_ref, slab_ref,
                emit)


def _lru(reverse, uc3, lam, wg, ba, bx):
    bsz, s, _ = uc3.shape
    ts = TS_LRU
    nblk = s // ts
    blk = pl.BlockSpec((None, ts, D_MODEL), lambda b, j: (b, (nblk - 1 - j) if reverse else j, 0))
    vec = _resident(lam.shape)
    return pl.pallas_call(
        functools.partial(_lru_kernel, reverse),
        grid=(bsz, nblk),
        in_specs=[blk, vec, _resident(wg.shape), vec, vec],
        out_specs=blk,
        out_shape=jax.ShapeDtypeStruct((bsz, s, D_MODEL), F32),
        scratch_shapes=[
            pltpu.VMEM((ts, D_MODEL), F32),
            pltpu.VMEM((ts, D_MODEL), F32),
            pltpu.VMEM((LANE_TILES, LANES), F32),
            pltpu.VMEM((N_SLABS, SLAB, LANES), F32),
        ],
        compiler_params=pltpu.CompilerParams(
            dimension_semantics=("arbitrary", "arbitrary"), vmem_limit_bytes=VMEM_LIMIT_BYTES),
        name="lru_bwd" if reverse else "lru_fwd",
    )(uc3, lam, wg, ba, bx)


_N_BIAS = 3 * N_HEADS + 1
_MASKED_TILE = 3 * N_HEADS


def _attn_kernel(nb, sink_ref, q_ref, kp_ref, km_ref, kn_ref, vp_ref, vm_ref, vn_ref,
                 wo_ref, wi_ref, wd_ref, o_ref, wob_ref, wib_ref, wdb_ref,
                 kz_ref, vab_ref, bias_ref, s_ref, m_ref, p_ref):
    tq = q_ref.shape[0]
    tk = tq + 2 * BLOCK
    nsub = tq // BLOCK
    j = pl.program_id(1)

    wob_ref[...] = wo_ref[...].astype(BF16)
    wdb_ref[...] = wd_ref[...].astype(BF16)
    gate_col = lax.broadcasted_iota(jnp.int32, (1, 2 * D_FF), 1) < D_FF
    wib_ref[...] = (wi_ref[...] * jnp.where(gate_col, 0.5, 1.0)).astype(BF16)

    @pl.when((pl.program_id(0) == 0) & (j == 0))
    def _():
        row = lax.broadcasted_iota(jnp.int32, (BLOCK, BLOCK), 0)
        col = lax.broadcasted_iota(jnp.int32, (BLOCK, BLOCK), 1)
        bias_ref[_MASKED_TILE] = jnp.ones((BLOCK, BLOCK), F32)
        for c in range(3):
            dist = jnp.abs(row + (1 - c) * BLOCK - col)
            absd = dist.astype(F32)
            for h in range(N_HEADS):
                bias_ref[3 * h + c] = jnp.where(dist <= WINDOW, (-_ALIBI_SLOPES[h] * absd) * LOG2_E, 1.0)

    low = lax.broadcasted_iota(jnp.int32, (tk, LANES), 1) < HEAD_DIM
    low_q = lax.broadcasted_iota(jnp.int32, (BLOCK, LANES), 1) < HEAD_DIM
    ones_q = [jnp.where(low_q, 1.0, 0.0).astype(BF16), jnp.where(low_q, 0.0, 1.0).astype(BF16)]
    for t in range(KV_W // LANES):
        cols = slice(t * LANES, (t + 1) * LANES)
        full = jnp.concatenate([r[:, cols] for r in (kp_ref, km_ref, kn_ref)], axis=0).astype(F32)
        swapped = pltpu.roll(full, HEAD_DIM, axis=1)
        kz_ref[4 * t + 0] = jnp.where(low, full, 0.0).astype(BF16)
        kz_ref[4 * t + 1] = jnp.where(low, 0.0, swapped).astype(BF16)
        kz_ref[4 * t + 2] = jnp.where(low, swapped, 0.0).astype(BF16)
        kz_ref[4 * t + 3] = jnp.where(low, 0.0, full).astype(BF16)
        nkb = tk // BLOCK
        for b in range(nkb):
            if b == 0:
                src = vp_ref[:, cols]
            elif b == nkb - 1:
                src = vn_ref[:, cols]
            else:
                src = vm_ref[(b - 1) * BLOCK:b * BLOCK, cols]
            full = src.astype(F32)
            swapped = pltpu.roll(full, HEAD_DIM, axis=1)
            placed = [jnp.where(low_q, full, 0.0), jnp.where(low_q, 0.0, swapped),
                      jnp.where(low_q, swapped, 0.0), jnp.where(low_q, 0.0, full)]
            for i in range(4):
                g, half = 2 * t + i // 2, i % 2
                dst = slice((2 * b + half) * BLOCK, (2 * b + half + 1) * BLOCK)
                vab_ref[g, dst, :LANES] = placed[i].astype(BF16)
                vab_ref[g, dst, LANES:] = ones_q[half]

    def scores(qi, slot):
        q0 = pl.multiple_of(qi * BLOCK, BLOCK)
        blk = j * nsub + qi
        win = pl.ds(q0, 3 * BLOCK)
        sc2 = [None, None]
        for h in range(N_HEADS):
            g, pp, half = h // GROUP, (h % GROUP) // 2, h % 2
            if pp == 0:
                q2 = jnp.concatenate(
                    [q_ref[pl.ds(q0, BLOCK), (2 * g + i) * LANES:(2 * g + i + 1) * LANES] for i in range(2)],
                    axis=0)
                sc2[half] = lax.dot_general(q2, kz_ref[2 * g + half, win, :], (((1,), (1,)), ((), ())),
                                            preferred_element_type=F32)
            sc = sc2[half][pp * BLOCK:(pp + 1) * BLOCK]
            tiles = []
            for c in range(3):
                t = sc[:, c * BLOCK:(c + 1) * BLOCK]
                if c == 1:
                    t = t + bias_ref[3 * h + 1]
                else:
                    edge = (blk == 0) if c == 0 else (blk == nb - 1)
                    b = bias_ref[jnp.where(edge, _MASKED_TILE, 3 * h + c)]
                    t = jnp.where(b > 0.0, NEG_INF, t + b)
                s_ref[slot, h, :, c * BLOCK:(c + 1) * BLOCK] = t
                tiles.append(t)
            mx = jnp.max(jnp.maximum(jnp.maximum(tiles[0], tiles[1]), tiles[2]), axis=-1, keepdims=True)
            m_ref[slot, h] = jnp.broadcast_to(jnp.maximum(mx, sink_ref[h] * LOG2_E), (BLOCK, LANES))

    def finish(qi, slot):
        q0 = pl.multiple_of(qi * BLOCK, BLOCK)
        win2 = pl.ds(pl.multiple_of(qi * (2 * BLOCK), 2 * BLOCK), 6 * BLOCK)
        for h in range(N_HEADS):
            m = m_ref[slot, h]
            for c in range(3):
                cs = slice(c * BLOCK, (c + 1) * BLOCK)
                dst = slice((2 * c + h % 2) * BLOCK, (2 * c + h % 2 + 1) * BLOCK)
                prow = slice(((h // 2) % 2) * BLOCK, ((h // 2) % 2 + 1) * BLOCK)
                p_ref[slot, h // GROUP, prow, dst] = jnp.exp2(s_ref[slot, h, :, cs] - m).astype(BF16)
        for g in range(N_KV_HEADS):
            o2 = jnp.dot(p_ref[slot, g], vab_ref[g, win2, :], preferred_element_type=F32)
            for pp in range(2):
                pr = 2 * g + pp
                h0, h1 = 2 * pr, 2 * pr + 1
                o = o2[pp * BLOCK:(pp + 1) * BLOCK]
                sink2 = jnp.where(low_q[0:1, :], sink_ref[h0] * LOG2_E, sink_ref[h1] * LOG2_E)
                e = jnp.exp2(sink2 - jnp.where(low_q, m_ref[slot, h0], m_ref[slot, h1]))
                o_ref[pl.ds(q0, BLOCK), pr * LANES:(pr + 1) * LANES] = (
                    o[:, :LANES] * (1.0 / (o[:, LANES:] + e)))

    scores(0, 0)

    def two_sub_blocks(i, _):
        qa = 2 * i
        scores(qa + 1, 1)
        finish(qa, 0)
        scores(jnp.minimum(qa + 2, nsub - 1), 0)
        finish(qa + 1, 1)
        return 0

    lax.fori_loop(0, nsub // 2, two_sub_blocks, 0)


def _attention(q3, k3, v3, sink, ffn_weights):
    bsz, s, _ = q3.shape
    tq = TQ_ATTN
    nq = s // tq
    hb = tq // BLOCK
    nb = s // BLOCK
    tk = tq + 2 * BLOCK
    kv_main = pl.BlockSpec((None, tq, KV_W), lambda b, j: (b, j, 0))
    kv_prev = pl.BlockSpec((None, BLOCK, KV_W), lambda b, j: (b, jnp.maximum(j * hb - 1, 0), 0))
    kv_next = pl.BlockSpec((None, BLOCK, KV_W), lambda b, j: (b, jnp.minimum((j + 1) * hb, nb - 1), 0))
    qo = pl.BlockSpec((None, tq, Q_W), lambda b, j: (b, j, 0))
    nsteps = bsz * nq
    w_slice = lambda w: pl.BlockSpec((w.shape[0] // nsteps, w.shape[1]), lambda b, j: (b * nq + j, 0))
    w_specs = [w_slice(w) for w in ffn_weights]
    return pl.pallas_call(
        functools.partial(_attn_kernel, nb),
        grid=(bsz, nq),
        in_specs=[pl.BlockSpec(memory_space=pltpu.SMEM), qo,
                  kv_prev, kv_main, kv_next, kv_prev, kv_main, kv_next] + w_specs,
        out_specs=[qo] + w_specs,
        out_shape=[jax.ShapeDtypeStruct((bsz, s, Q_W), F32)]
        + [jax.ShapeDtypeStruct(w.shape, BF16) for w in ffn_weights],
        scratch_shapes=[
            pltpu.VMEM((2 * N_KV_HEADS, tk, LANES), BF16),
            pltpu.VMEM((N_KV_HEADS, 2 * tk, 2 * LANES), BF16),
            pltpu.VMEM((_N_BIAS, BLOCK, BLOCK), F32),
            pltpu.VMEM((2, N_HEADS, BLOCK, 3 * BLOCK), F32),
            pltpu.VMEM((2, N_HEADS, BLOCK, LANES), F32),
            pltpu.VMEM((2, N_KV_HEADS, 2 * BLOCK, 6 * BLOCK), BF16),
        ],
        compiler_params=pltpu.CompilerParams(
            dimension_semantics=("arbitrary", "arbitrary"), vmem_limit_bytes=VMEM_LIMIT_BYTES),
        name="attention",
    )(sink, q3, k3, k3, k3, v3, v3, v3, *ffn_weights)


def _ffn_kernel(x_ref, hf_ref, hb_ref, yb_ref, coef_ref, wo_ref, g2_ref, wi_ref, wd_ref, g3_ref, o_ref):
    merged = coef_ref[:, :D_MODEL] * (hf_ref[...] + hb_ref[...]) + coef_ref[:, D_MODEL:] * yb_ref[...]
    x1 = x_ref[...] + jnp.dot(merged.astype(BF16), wo_ref[...], preferred_element_type=F32)
    xn = _rmsnorm(x1, g2_ref[...]).astype(BF16)
    gate = jnp.dot(xn, wi_ref[:, :D_FF], preferred_element_type=F32)
    up = jnp.dot(xn, wi_ref[:, D_FF:], preferred_element_type=F32)
    act = ((gate * jnp.tanh(gate) + gate) * up).astype(BF16)
    x2 = x1 + jnp.dot(act, wd_ref[...], preferred_element_type=F32)
    o_ref[...] = _rmsnorm(x2, g3_ref[...])


def _ffn(x2, hf, hb, yb, coef, wo, g2, wi, wd, g3):
    t = x2.shape[0]
    tm = TM_FFN
    row = lambda w_: pl.BlockSpec((tm, w_), lambda i: (i, 0))
    return pl.pallas_call(
        _ffn_kernel,
        grid=(t // tm,),
        in_specs=[row(D_MODEL), row(D_MODEL), row(D_MODEL), row(D_MODEL), row(2 * D_MODEL),
                  _resident((D_MODEL, D_MODEL)), _resident((1, D_MODEL)),
                  _resident((D_MODEL, 2 * D_FF)), _resident((D_FF, D_MODEL)), _resident((1, D_MODEL))],
        out_specs=row(D_MODEL),
        out_shape=jax.ShapeDtypeStruct((t, D_MODEL), F32),
        compiler_params=pltpu.CompilerParams(
            dimension_semantics=("arbitrary",), vmem_limit_bytes=VMEM_LIMIT_BYTES),
        name="merge_ffn",
    )(x2, hf, hb, yb, coef, wo, g2, wi, wd, g3)


def _gate_weights(wa, wx):
    def blockdiag(w):
        w = w.reshape(N_GROUPS, LRU_GROUP // LRU_BLOCK, LRU_BLOCK, LRU_BLOCK)
        eye = jnp.eye(LRU_GROUP // LRU_BLOCK, dtype=w.dtype)
        return jnp.einsum("ghij,hk->ghikj", w, eye).reshape(N_GROUPS, LRU_GROUP, LRU_GROUP)
    return jnp.concatenate([blockdiag(wa), blockdiag(wx)], axis=-1).astype(BF16)


def kernel(x, norm_mix_g, w_in, b_gate, conv_w, conv_b, lru_lambda, lru_wa, lru_ba, lru_wx, lru_bx,
           attn_sink, w_out, norm_ffn_g, w_ffn_in, w_ffn_out, norm_final_g):
    bsz, s, d = x.shape
    depth = w_in.shape[0]
    assert depth == 1, "the merge/ffn kernel applies the final RMSNorm, so it must be the last layer"
    t = bsz * s
    row = lambda a: a.reshape(1, -1)
    time_major = lambda a: a.reshape(a.shape[:-1] + (LANE_TILES, LANES))
    x2 = x.reshape(t, d)
    for l in range(depth):
        huc, q, k, v, merge_coef = _in_proj(
            x2, s, row(norm_mix_g[l]), w_in[l].astype(BF16), row(b_gate[l]),
            time_major(conv_w[l]), time_major(conv_b[l]))
        seq = lambda a: a.reshape(bsz, s, a.shape[-1])
        lru = lambda dr: (lru_lambda[l], _gate_weights(lru_wa[l, dr], lru_wx[l, dr]), lru_ba[l], lru_bx[l])
        h_fwd = _lru(False, seq(huc), *lru(0))
        h_bwd = _lru(True, seq(huc), *lru(1))
        y_b, wo_b, wi_b, wd_b = _attention(seq(q), seq(k), seq(v), attn_sink[l],
                                           (w_out[l], w_ffn_in[l], w_ffn_out[l]))
        x2 = _ffn(x2, h_fwd.reshape(t, d), h_bwd.reshape(t, d), y_b.reshape(t, d), merge_coef,
                  wo_b, row(norm_ffn_g[l]), wi_b, wd_b, row(norm_final_g))
    return x2.reshape(bsz, s, d)
```

```python
import functools
import math

import numpy as np
import jax
import jax.numpy as jnp
from jax import lax
from jax.experimental import pallas as pl
from jax.experimental.pallas import tpu as pltpu

F32 = jnp.float32
BF16 = jnp.bfloat16

D_MODEL = 1024
LRU_HEADS = 16
LRU_BLOCK = D_MODEL // LRU_HEADS
CONV_WIDTH = 4
CONV_LEFT = 2
RGLRU_C = 8.0
N_HEADS = 16
N_KV_HEADS = 4
HEAD_DIM = 64
GROUP = N_HEADS // N_KV_HEADS
WINDOW = 128
BLOCK = 128
D_FF = 2816
Q_W = N_HEADS * HEAD_DIM
KV_W = N_KV_HEADS * HEAD_DIM
IN_W = 2 * D_MODEL + Q_W + 2 * KV_W + 2 * D_MODEL
EPS = 1e-6
NEG_INF = -1e30
F32_TINY = float(np.finfo(np.float32).tiny)

LANES = 128
SUBLANES = 8
VMEM_LIMIT_BYTES = 56 * 1024 * 1024

TM_PROJ = 1024
TS_LRU = 1024
TQ_ATTN = 1024
TM_FFN = 512
LRU_GROUP = 256
N_GROUPS = D_MODEL // LRU_GROUP
LANE_TILES = D_MODEL // LANES
assert LANE_TILES == SUBLANES
SLAB = SUBLANES * LANE_TILES
TILES_PER_STEP = 32
N_SLABS = 3 * TILES_PER_STEP
CONV_RING = 8

_ALIBI_SLOPES = [float(v) for v in np.exp2(
    -8.0 * (np.arange(N_HEADS, dtype=np.float32) + 1.0) / N_HEADS).astype(np.float32)]
LOG2_E = math.log2(math.e)
Q_SCALE = HEAD_DIM ** -0.5 * LOG2_E


def _sigmoid(x):
    return 0.5 * jnp.tanh(0.5 * x) + 0.5


def _rmsnorm(x, g):
    return x * lax.rsqrt(jnp.mean(x * x, axis=-1, keepdims=True) + EPS) * g


def _resident(shape):
    nd = len(shape)
    return pl.BlockSpec(shape, lambda *_: (0,) * nd, pipeline_mode=pl.Buffered(1))


def _in_proj_kernel(nblk, x_ref, xp_ref, xnx_ref, g_ref, w_ref, bg_ref, cw_ref, cb_ref,
                    uc_ref, q_ref, k_ref, v_ref, coef_ref, uext_ref, slab_ref):
    tm = x_ref.shape[0]
    nt = tm // SUBLANES
    jb = pl.program_id(0) % nblk
    gn = g_ref[...]
    xn = _rmsnorm(x_ref[...], gn).astype(BF16)
    halo = jnp.concatenate([xp_ref[...], xnx_ref[...]], axis=0)
    xh = _rmsnorm(halo, gn).astype(BF16)

    def proj(lo, hi):
        return jnp.dot(xn, w_ref[:, lo:hi], preferred_element_type=F32)

    ue = jnp.dot(jnp.concatenate([xn, xh], axis=0), w_ref[:, :D_MODEL], preferred_element_type=F32)
    uext_ref[SUBLANES:SUBLANES + tm, :] = ue[:tm]
    uext_ref[0:SUBLANES, :] = jnp.where(jb > 0, ue[tm:tm + SUBLANES], 0.0)
    uext_ref[SUBLANES + tm:, :] = jnp.where(jb < nblk - 1, ue[tm + SUBLANES:], 0.0)

    o_g = D_MODEL
    o_q = o_g + D_MODEL
    o_k = o_q + Q_W
    o_v = o_k + KV_W
    o_z = o_v + KV_W
    pw = 2 * LANES
    todo = []

    def coef_a(c):
        gate = _sigmoid(proj(o_z + c, o_z + c + pw) + bg_ref[:, c:c + pw])
        coef_ref[:, c:c + pw] = gate * _gelu_tanh(proj(o_g + c, o_g + c + pw))

    def coef_b(c):
        cz = D_MODEL + c
        coef_ref[:, cz:cz + pw] = _sigmoid(proj(o_z + cz, o_z + cz + pw) + bg_ref[:, cz:cz + pw])

    def plain(dst, lo, post):
        def run(c):
            dst[:, c:c + pw] = post(proj(lo + c, lo + c + pw))
        return run

    for fn, width in ((coef_a, D_MODEL), (coef_b, D_MODEL),
                      (plain(q_ref, o_q, lambda v: (v * Q_SCALE).astype(BF16)), Q_W),
                      (plain(k_ref, o_k, lambda v: v.astype(BF16)), KV_W),
                      (plain(v_ref, o_v, lambda v: v.astype(BF16)), KV_W)):
        todo += [(fn, c) for c in range(0, width, pw)]
    every = -(-nt // len(todo))

    def project_block():
        fn, c = todo.pop(0)
        fn(c)

    w = [0.5 * cw_ref[k] for k in range(CONV_WIDTH)]
    cb = 0.5 * cb_ref[...]
    ring = CONV_RING

    def ext_steps(e, s):
        return _split_time(uext_ref[e * SUBLANES:(e + 1) * SUBLANES, :], slab_ref, s)

    tail = ext_steps(0, 0)
    prev2, prev1 = tail[SUBLANES - 2], tail[SUBLANES - 1]
    cur = ext_steps(1, 1)
    for t in range(nt):
        if t % every == 0 and todo:
            project_block()
        nxt = ext_steps(t + 2, 2 * (t % ring))
        x = [prev2, prev1] + cur + [nxt[0]]
        out = [(x[r] * w[0] + x[r + 1] * w[1]) + (x[r + 2] * w[2] + x[r + 3] * w[3]) + cb
               for r in range(SUBLANES)]
        uc_ref[t * SUBLANES:(t + 1) * SUBLANES, :] = _join_time(out, slab_ref, 2 * (t % ring) + 1)
        prev2, prev1, cur = cur[SUBLANES - 2], cur[SUBLANES - 1], nxt
    while todo:
        project_block()


def _in_proj(x2, seq, g, w, bg, cw, cb):
    t = x2.shape[0]
    tm = TM_PROJ
    nblk = seq // tm
    hb = tm // SUBLANES
    nh = t // SUBLANES
    row = lambda w_: pl.BlockSpec((tm, w_), lambda i: (i, 0))
    prev = pl.BlockSpec((SUBLANES, D_MODEL), lambda i: (jnp.maximum(i * hb - 1, 0), 0))
    nxt = pl.BlockSpec((SUBLANES, D_MODEL), lambda i: (jnp.minimum((i + 1) * hb, nh - 1), 0))
    return pl.pallas_call(
        functools.partial(_in_proj_kernel, nblk),
        grid=(t // tm,),
        in_specs=[row(D_MODEL), prev, nxt, _resident((1, D_MODEL)), _resident((D_MODEL, IN_W)),
                  _resident((1, 2 * D_MODEL)), _resident(cw.shape), _resident(cb.shape)],
        out_specs=[row(D_MODEL), row(Q_W), row(KV_W), row(KV_W), row(2 * D_MODEL)],
        out_shape=[
            jax.ShapeDtypeStruct((t, D_MODEL), F32),
            jax.ShapeDtypeStruct((t, Q_W), BF16),
            jax.ShapeDtypeStruct((t, KV_W), BF16),
            jax.ShapeDtypeStruct((t, KV_W), BF16),
            jax.ShapeDtypeStruct((t, 2 * D_MODEL), F32),
        ],
        scratch_shapes=[
            pltpu.VMEM((tm + 2 * SUBLANES, D_MODEL), F32),
            pltpu.VMEM((2 * CONV_RING, SLAB, LANES), F32),
        ],
        compiler_params=pltpu.CompilerParams(
            dimension_semantics=("arbitrary",), vmem_limit_bytes=VMEM_LIMIT_BYTES),
        name="in_proj",
    )(x2, x2, x2, g, w, bg, cw, cb)


def _split_time(tile, slab_ref, s):
    for j in range(LANE_TILES):
        slab_ref[s, j * SUBLANES:(j + 1) * SUBLANES, :] = tile[:, j * LANES:(j + 1) * LANES]
    return [slab_ref[s, pl.ds(r, LANE_TILES, stride=SUBLANES), :] for r in range(SUBLANES)]


def _join_time(steps, slab_ref, s):
    for r in range(SUBLANES):
        slab_ref[s, pl.ds(r, LANE_TILES, stride=SUBLANES), :] = steps[r]
    return jnp.concatenate(
        [slab_ref[s, j * SUBLANES:(j + 1) * SUBLANES, :] for j in range(LANE_TILES)], axis=1)


def _half_decay_log2(lam):
    return (-0.5 * RGLRU_C * math.log2(math.e)) * (
        jnp.maximum(-lam, 0.0) + jnp.log1p(jnp.exp(-jnp.abs(lam))))


def _gate_terms(pre_r, pre_i, huc, hd2, hba, hbx):
    a = jnp.exp2(hd2 * jnp.tanh(pre_r + hba) + hd2)
    y = jnp.maximum(1.0 - a * a, 0.0)
    beta = y * lax.rsqrt(jnp.maximum(y, F32_TINY))
    return a, beta * (jnp.tanh(pre_i + hbx) * huc + huc)


def _gelu_tanh(g):
    c = math.sqrt(2.0 / math.pi)
    half = 0.5 * g
    return half * jnp.tanh(g * ((c * 0.044715) * (g * g) + c)) + half


def _gate_maps(uc_ref, wg_ref, ra_ref, ix_ref, rows):
    for gi in range(N_GROUPS):
        cols = slice(gi * LRU_GROUP, (gi + 1) * LRU_GROUP)
        pre = jnp.dot(uc_ref[rows, cols].astype(BF16), wg_ref[gi], preferred_element_type=F32)
        ra_ref[rows, cols] = pre[:, :LRU_GROUP]
        ix_ref[rows, cols] = pre[:, LRU_GROUP:]


def _recurrence(reverse, uc_ref, wg_ref, ra_ref, ix_ref, lam_ref, ba_ref, bx_ref, carry_ref, slab_ref,
                emit):
    ts = uc_ref.shape[0]
    nt = ts // SUBLANES
    nsteps = nt // TILES_PER_STEP
    d = slice(int(reverse), int(reverse) + 1)
    hd2 = _half_decay_log2(lam_ref[d, :])
    hba = 0.5 * ba_ref[d, :]
    hbx = 0.5 * bx_ref[d, :]

    def tile(k, s, h):
        kk = (nt - 1 - k) if reverse else k
        rows = pl.ds(pl.multiple_of(kk * SUBLANES, SUBLANES), SUBLANES)
        a, b = _gate_terms(ra_ref[rows, :], ix_ref[rows, :], uc_ref[rows, :], hd2, hba, hbx)
        a_t = _split_time(a, slab_ref, 3 * s)
        b_t = _split_time(b, slab_ref, 3 * s + 1)
        hs = [None] * SUBLANES
        for r in (range(SUBLANES - 1, -1, -1) if reverse else range(SUBLANES)):
            h = a_t[r] * h + b_t[r]
            hs[r] = h
        emit(rows, _join_time(hs, slab_ref, 3 * s + 2))
        return h

    def group(i, h):
        for s in range(TILES_PER_STEP):
            h = tile(TILES_PER_STEP * i + s, s, h)
        return h

    _gate_maps(uc_ref, wg_ref, ra_ref, ix_ref, slice(0, ts))
    carry_ref[...] = lax.fori_loop(0, nsteps, group, carry_ref[...])


def _lru_kernel(reverse, uc_ref, lam_ref, wg_ref, ba_ref, bx_ref, h_ref, ra_ref, ix_ref, carry_ref,
                slab_ref):
    @pl.when(pl.program_id(1) == 0)
    def _():
        carry_ref[...] = jnp.zeros_like(carry_ref)

    def emit(rows, h):
        h_ref[rows, :] = h

    _recurrence(reverse, uc_ref, wg_ref, ra_ref, ix_ref, lam_ref, ba_ref, bx_ref, carry_ref, slab_ref,
                emit)


def _lru(reverse, uc3, lam, wg, ba, bx):
    bsz, s, _ = uc3.shape
    ts = TS_LRU
    nblk = s // ts
    blk = pl.BlockSpec((None, ts, D_MODEL), lambda b, j: (b, (nblk - 1 - j) if reverse else j, 0))
    vec = _resident(lam.shape)
    return pl.pallas_call(
        functools.partial(_lru_kernel, reverse),
        grid=(bsz, nblk),
        in_specs=[blk, vec, _resident(wg.shape), vec, vec],
        out_specs=blk,
        out_shape=jax.ShapeDtypeStruct((bsz, s, D_MODEL), F32),
        scratch_shapes=[
            pltpu.VMEM((ts, D_MODEL), F32),
            pltpu.VMEM((ts, D_MODEL), F32),
            pltpu.VMEM((LANE_TILES, LANES), F32),
            pltpu.VMEM((N_SLABS, SLAB, LANES), F32),
        ],
        compiler_params=pltpu.CompilerParams(
            dimension_semantics=("arbitrary", "arbitrary"), vmem_limit_bytes=VMEM_LIMIT_BYTES),
        name="lru_bwd" if reverse else "lru_fwd",
    )(uc3, lam, wg, ba, bx)


_N_BIAS = 3 * N_HEADS + 1
_MASKED_TILE = 3 * N_HEADS


def _attn_kernel(nb, sink_ref, q_ref, kp_ref, km_ref, kn_ref, vp_ref, vm_ref, vn_ref,
                 wo_ref, wi_ref, wd_ref, o_ref, wob_ref, wib_ref, wdb_ref,
                 kz_ref, vab_ref, bias_ref, s_ref, m_ref, p_ref):
    tq = q_ref.shape[0]
    tk = tq + 2 * BLOCK
    nsub = tq // BLOCK
    j = pl.program_id(1)

    wob_ref[...] = wo_ref[...].astype(BF16)
    wdb_ref[...] = wd_ref[...].astype(BF16)
    gate_col = lax.broadcasted_iota(jnp.int32, (1, 2 * D_FF), 1) < D_FF
    wib_ref[...] = (wi_ref[...] * jnp.where(gate_col, 0.5, 1.0)).astype(BF16)

    @pl.when((pl.program_id(0) == 0) & (j == 0))
    def _():
        row = lax.broadcasted_iota(jnp.int32, (BLOCK, BLOCK), 0)
        col = lax.broadcasted_iota(jnp.int32, (BLOCK, BLOCK), 1)
        bias_ref[_MASKED_TILE] = jnp.ones((BLOCK, BLOCK), F32)
        for c in range(3):
            dist = jnp.abs(row + (1 - c) * BLOCK - col)
            absd = dist.astype(F32)
            for h in range(N_HEADS):
                bias_ref[3 * h + c] = jnp.where(dist <= WINDOW, (-_ALIBI_SLOPES[h] * absd) * LOG2_E, 1.0)
        low_c = lax.broadcasted_iota(jnp.int32, (BLOCK, LANES), 1) < HEAD_DIM
        for half in range(2):
            ones = jnp.where(low_c if half == 0 else ~low_c, 1.0, 0.0).astype(BF16)
            for g in range(N_KV_HEADS):
                for b in range(tk // BLOCK):
                    vab_ref[g, (2 * b + half) * BLOCK:(2 * b + half + 1) * BLOCK, LANES:] = ones

    low = lax.broadcasted_iota(jnp.int32, (tk, LANES), 1) < HEAD_DIM
    low_q = lax.broadcasted_iota(jnp.int32, (BLOCK, LANES), 1) < HEAD_DIM
    for t in range(KV_W // LANES):
        cols = slice(t * LANES, (t + 1) * LANES)
        full = jnp.concatenate([r[:, cols] for r in (kp_ref, km_ref, kn_ref)], axis=0).astype(F32)
        swapped = pltpu.roll(full, HEAD_DIM, axis=1)
        kz_ref[4 * t + 0] = jnp.where(low, full, 0.0).astype(BF16)
        kz_ref[4 * t + 1] = jnp.where(low, 0.0, swapped).astype(BF16)
        kz_ref[4 * t + 2] = jnp.where(low, swapped, 0.0).astype(BF16)
        kz_ref[4 * t + 3] = jnp.where(low, 0.0, full).astype(BF16)
        nkb = tk // BLOCK
        for b in range(nkb):
            if b == 0:
                src = vp_ref[:, cols]
            elif b == nkb - 1:
                src = vn_ref[:, cols]
            else:
                src = vm_ref[(b - 1) * BLOCK:b * BLOCK, cols]
            full = src.astype(F32)
            swapped = pltpu.roll(full, HEAD_DIM, axis=1)
            placed = [jnp.where(low_q, full, 0.0), jnp.where(low_q, 0.0, swapped),
                      jnp.where(low_q, swapped, 0.0), jnp.where(low_q, 0.0, full)]
            for i in range(4):
                g, half = 2 * t + i // 2, i % 2
                dst = slice((2 * b + half) * BLOCK, (2 * b + half + 1) * BLOCK)
                vab_ref[g, dst, :LANES] = placed[i].astype(BF16)

    def scores(qi, slot):
        q0 = pl.multiple_of(qi * BLOCK, BLOCK)
        blk = j * nsub + qi
        win = pl.ds(q0, 3 * BLOCK)
        sc2 = [None, None]
        for h in range(N_HEADS):
            g, pp, half = h // GROUP, (h % GROUP) // 2, h % 2
            if pp == 0:
                q2 = jnp.concatenate(
                    [q_ref[pl.ds(q0, BLOCK), (2 * g + i) * LANES:(2 * g + i + 1) * LANES] for i in range(2)],
                    axis=0)
                sc2[half] = lax.dot_general(q2, kz_ref[2 * g + half, win, :], (((1,), (1,)), ((), ())),
                                            preferred_element_type=F32)
            sc = sc2[half][pp * BLOCK:(pp + 1) * BLOCK]
            tiles = []
            for c in range(3):
                t = sc[:, c * BLOCK:(c + 1) * BLOCK]
                if c == 1:
                    t = t + bias_ref[3 * h + 1]
                else:
                    edge = (blk == 0) if c == 0 else (blk == nb - 1)
                    b = bias_ref[jnp.where(edge, _MASKED_TILE, 3 * h + c)]
                    t = jnp.where(b > 0.0, NEG_INF, t + b)
                s_ref[slot, h, :, c * BLOCK:(c + 1) * BLOCK] = t
                tiles.append(t)
            mx = jnp.max(jnp.maximum(jnp.maximum(tiles[0], tiles[1]), tiles[2]), axis=-1, keepdims=True)
            m_ref[slot, h] = jnp.broadcast_to(jnp.maximum(mx, sink_ref[h] * LOG2_E), (BLOCK, LANES))

    def finish(qi, slot):
        q0 = pl.multiple_of(qi * BLOCK, BLOCK)
        win2 = pl.ds(pl.multiple_of(qi * (2 * BLOCK), 2 * BLOCK), 6 * BLOCK)
        for h in range(N_HEADS):
            m = m_ref[slot, h]
            for c in range(3):
                cs = slice(c * BLOCK, (c + 1) * BLOCK)
                dst = slice((2 * c + h % 2) * BLOCK, (2 * c + h % 2 + 1) * BLOCK)
                prow = slice(((h // 2) % 2) * BLOCK, ((h // 2) % 2 + 1) * BLOCK)
                p_ref[slot, h // GROUP, prow, dst] = jnp.exp2(s_ref[slot, h, :, cs] - m).astype(BF16)
        for g in range(N_KV_HEADS):
            o2 = jnp.dot(p_ref[slot, g], vab_ref[g, win2, :], preferred_element_type=F32)
            for pp in range(2):
                pr = 2 * g + pp
                h0, h1 = 2 * pr, 2 * pr + 1
                o = o2[pp * BLOCK:(pp + 1) * BLOCK]
                sink2 = jnp.where(low_q[0:1, :], sink_ref[h0] * LOG2_E, sink_ref[h1] * LOG2_E)
                e = jnp.exp2(sink2 - jnp.where(low_q, m_ref[slot, h0], m_ref[slot, h1]))
                o_ref[pl.ds(q0, BLOCK), pr * LANES:(pr + 1) * LANES] = (
                    o[:, :LANES] * (1.0 / (o[:, LANES:] + e)))

    scores(0, 0)

    def two_sub_blocks(i, _):
        qa = 2 * i
        scores(qa + 1, 1)
        finish(qa, 0)
        scores(jnp.minimum(qa + 2, nsub - 1), 0)
        finish(qa + 1, 1)
        return 0

    lax.fori_loop(0, nsub // 2, two_sub_blocks, 0)


def _attention(q3, k3, v3, sink, ffn_weights):
    bsz, s, _ = q3.shape
    tq = TQ_ATTN
    nq = s // tq
    hb = tq // BLOCK
    nb = s // BLOCK
    tk = tq + 2 * BLOCK
    kv_main = pl.BlockSpec((None, tq, KV_W), lambda b, j: (b, j, 0))
    kv_prev = pl.BlockSpec((None, BLOCK, KV_W), lambda b, j: (b, jnp.maximum(j * hb - 1, 0), 0))
    kv_next = pl.BlockSpec((None, BLOCK, KV_W), lambda b, j: (b, jnp.minimum((j + 1) * hb, nb - 1), 0))
    qo = pl.BlockSpec((None, tq, Q_W), lambda b, j: (b, j, 0))
    nsteps = bsz * nq
    w_slice = lambda w: pl.BlockSpec((w.shape[0] // nsteps, w.shape[1]), lambda b, j: (b * nq + j, 0))
    w_specs = [w_slice(w) for w in ffn_weights]
    return pl.pallas_call(
        functools.partial(_attn_kernel, nb),
        grid=(bsz, nq),
        in_specs=[pl.BlockSpec(memory_space=pltpu.SMEM), qo,
                  kv_prev, kv_main, kv_next, kv_prev, kv_main, kv_next] + w_specs,
        out_specs=[qo] + w_specs,
        out_shape=[jax.ShapeDtypeStruct((bsz, s, Q_W), F32)]
        + [jax.ShapeDtypeStruct(w.shape, BF16) for w in ffn_weights],
        scratch_shapes=[
            pltpu.VMEM((2 * N_KV_HEADS, tk, LANES), BF16),
            pltpu.VMEM((N_KV_HEADS, 2 * tk, 2 * LANES), BF16),
            pltpu.VMEM((_N_BIAS, BLOCK, BLOCK), F32),
            pltpu.VMEM((2, N_HEADS, BLOCK, 3 * BLOCK), F32),
            pltpu.VMEM((2, N_HEADS, BLOCK, LANES), F32),
            pltpu.VMEM((2, N_KV_HEADS, 2 * BLOCK, 6 * BLOCK), BF16),
        ],
        compiler_params=pltpu.CompilerParams(
            dimension_semantics=("arbitrary", "arbitrary"), vmem_limit_bytes=VMEM_LIMIT_BYTES),
        name="attention",
    )(sink, q3, k3, k3, k3, v3, v3, v3, *ffn_weights)


def _ffn_kernel(x_ref, hf_ref, hb_ref, yb_ref, coef_ref, wo_ref, g2_ref, wi_ref, wd_ref, g3_ref, o_ref):
    merged = coef_ref[:, :D_MODEL] * (hf_ref[...] + hb_ref[...]) + coef_ref[:, D_MODEL:] * yb_ref[...]
    x1 = x_ref[...] + jnp.dot(merged.astype(BF16), wo_ref[...], preferred_element_type=F32)
    xn = _rmsnorm(x1, g2_ref[...]).astype(BF16)
    gate = jnp.dot(xn, wi_ref[:, :D_FF], preferred_element_type=F32)
    up = jnp.dot(xn, wi_ref[:, D_FF:], preferred_element_type=F32)
    act = ((gate * jnp.tanh(gate) + gate) * up).astype(BF16)
    x2 = x1 + jnp.dot(act, wd_ref[...], preferred_element_type=F32)
    o_ref[...] = _rmsnorm(x2, g3_ref[...])


def _ffn(x2, hf, hb, yb, coef, wo, g2, wi, wd, g3):
    t = x2.shape[0]
    tm = TM_FFN
    row = lambda w_: pl.BlockSpec((tm, w_), lambda i: (i, 0))
    return pl.pallas_call(
        _ffn_kernel,
        grid=(t // tm,),
        in_specs=[row(D_MODEL), row(D_MODEL), row(D_MODEL), row(D_MODEL), row(2 * D_MODEL),
                  _resident((D_MODEL, D_MODEL)), _resident((1, D_MODEL)),
                  _resident((D_MODEL, 2 * D_FF)), _resident((D_FF, D_MODEL)), _resident((1, D_MODEL))],
        out_specs=row(D_MODEL),
        out_shape=jax.ShapeDtypeStruct((t, D_MODEL), F32),
        compiler_params=pltpu.CompilerParams(
            dimension_semantics=("arbitrary",), vmem_limit_bytes=VMEM_LIMIT_BYTES),
        name="merge_ffn",
    )(x2, hf, hb, yb, coef, wo, g2, wi, wd, g3)


def _gate_weights(wa, wx):
    def blockdiag(w):
        w = w.reshape(N_GROUPS, LRU_GROUP // LRU_BLOCK, LRU_BLOCK, LRU_BLOCK)
        eye = jnp.eye(LRU_GROUP // LRU_BLOCK, dtype=w.dtype)
        return jnp.einsum("ghij,hk->ghikj", w, eye).reshape(N_GROUPS, LRU_GROUP, LRU_GROUP)
    return jnp.concatenate([blockdiag(wa), blockdiag(wx)], axis=-1).astype(BF16)


def kernel(x, norm_mix_g, w_in, b_gate, conv_w, conv_b, lru_lambda, lru_wa, lru_ba, lru_wx, lru_bx,
           attn_sink, w_out, norm_ffn_g, w_ffn_in, w_ffn_out, norm_final_g):
    bsz, s, d = x.shape
    depth = w_in.shape[0]
    assert depth == 1, "the merge/ffn kernel applies the final RMSNorm, so it must be the last layer"
    t = bsz * s
    row = lambda a: a.reshape(1, -1)
    time_major = lambda a: a.reshape(a.shape[:-1] + (LANE_TILES, LANES))
    x2 = x.reshape(t, d)
    for l in range(depth):
        huc, q, k, v, merge_coef = _in_proj(
            x2, s, row(norm_mix_g[l]), w_in[l].astype(BF16), row(b_gate[l]),
            time_major(conv_w[l]), time_major(conv_b[l]))
        seq = lambda a: a.reshape(bsz, s, a.shape[-1])
        lru = lambda dr: (lru_lambda[l], _gate_weights(lru_wa[l, dr], lru_wx[l, dr]), lru_ba[l], lru_bx[l])
        h_fwd = _lru(False, seq(huc), *lru(0))
        h_bwd = _lru(True, seq(huc), *lru(1))
        y_b, wo_b, wi_b, wd_b = _attention(seq(q), seq(k), seq(v), attn_sink[l],
                                           (w_out[l], w_ffn_in[l], w_ffn_out[l]))
        x2 = _ffn(x2, h_fwd.reshape(t, d), h_bwd.reshape(t, d), y_b.reshape(t, d), merge_coef,
                  wo_b, row(norm_ffn_g[l]), wi_b, wd_b, row(norm_final_g))
    return x2.reshape(bsz, s, d)
```

```python
import functools
import math

import numpy as np
import jax
import jax.numpy as jnp
from jax import lax
from jax.experimental import pallas as pl
from jax.experimental.pallas import tpu as pltpu

F32 = jnp.float32
BF16 = jnp.bfloat16

D_MODEL = 1024
LRU_HEADS = 16
LRU_BLOCK = D_MODEL // LRU_HEADS
CONV_WIDTH = 4
CONV_LEFT = 2
RGLRU_C = 8.0
N_HEADS = 16
N_KV_HEADS = 4
HEAD_DIM = 64
GROUP = N_HEADS // N_KV_HEADS
WINDOW = 128
BLOCK = 128
D_FF = 2816
Q_W = N_HEADS * HEAD_DIM
KV_W = N_KV_HEADS * HEAD_DIM
IN_W = 2 * D_MODEL + Q_W + 2 * KV_W + 2 * D_MODEL
EPS = 1e-6
NEG_INF = -1e30
F32_TINY = float(np.finfo(np.float32).tiny)

LANES = 128
SUBLANES = 8
VMEM_LIMIT_BYTES = 56 * 1024 * 1024

TM_PROJ = 1024
TS_LRU = 1024
TQ_ATTN = 1024
TM_FFN = 512
LRU_GROUP = 256
N_GROUPS = D_MODEL // LRU_GROUP
LANE_TILES = D_MODEL // LANES
assert LANE_TILES == SUBLANES
SLAB = SUBLANES * LANE_TILES
TILES_PER_STEP = 32
N_SLABS = 3 * TILES_PER_STEP
CONV_RING = 8

_ALIBI_SLOPES = [float(v) for v in np.exp2(
    -8.0 * (np.arange(N_HEADS, dtype=np.float32) + 1.0) / N_HEADS).astype(np.float32)]
LOG2_E = math.log2(math.e)
Q_SCALE = HEAD_DIM ** -0.5 * LOG2_E


def _sigmoid(x):
    return 0.5 * jnp.tanh(0.5 * x) + 0.5


def _rmsnorm(x, g):
    return x * lax.rsqrt(jnp.mean(x * x, axis=-1, keepdims=True) + EPS) * g


def _resident(shape):
    nd = len(shape)
    return pl.BlockSpec(shape, lambda *_: (0,) * nd, pipeline_mode=pl.Buffered(1))


def _in_proj_kernel(nblk, x_ref, xp_ref, xnx_ref, g_ref, w_ref, bg_ref, cw_ref, cb_ref,
                    uc_ref, q_ref, k_ref, v_ref, coef_ref, uext_ref, slab_ref):
    tm = x_ref.shape[0]
    nt = tm // SUBLANES
    jb = pl.program_id(0) % nblk
    gn = g_ref[...]
    xn = _rmsnorm(x_ref[...], gn).astype(BF16)
    halo = jnp.concatenate([xp_ref[...], xnx_ref[...]], axis=0)
    xh = _rmsnorm(halo, gn).astype(BF16)

    def proj(lo, hi):
        return jnp.dot(xn, w_ref[:, lo:hi], preferred_element_type=F32)

    ue = jnp.dot(jnp.concatenate([xn, xh], axis=0), w_ref[:, :D_MODEL], preferred_element_type=F32)
    uext_ref[SUBLANES:SUBLANES + tm, :] = ue[:tm]
    uext_ref[0:SUBLANES, :] = jnp.where(jb > 0, ue[tm:tm + SUBLANES], 0.0)
    uext_ref[SUBLANES + tm:, :] = jnp.where(jb < nblk - 1, ue[tm + SUBLANES:], 0.0)

    o_g = D_MODEL
    o_q = o_g + D_MODEL
    o_k = o_q + Q_W
    o_v = o_k + KV_W
    o_z = o_v + KV_W
    pw = 2 * LANES
    todo = []

    def coef_a(c):
        gate = _sigmoid(proj(o_z + c, o_z + c + pw) + bg_ref[:, c:c + pw])
        coef_ref[:, c:c + pw] = gate * _gelu_tanh(proj(o_g + c, o_g + c + pw))

    def coef_b(c):
        cz = D_MODEL + c
        coef_ref[:, cz:cz + pw] = _sigmoid(proj(o_z + cz, o_z + cz + pw) + bg_ref[:, cz:cz + pw])

    def plain(dst, lo, post):
        def run(c):
            dst[:, c:c + pw] = post(proj(lo + c, lo + c + pw))
        return run

    for fn, width in ((coef_a, D_MODEL), (coef_b, D_MODEL),
                      (plain(q_ref, o_q, lambda v: (v * Q_SCALE).astype(BF16)), Q_W),
                      (plain(k_ref, o_k, lambda v: v.astype(BF16)), KV_W),
                      (plain(v_ref, o_v, lambda v: v.astype(BF16)), KV_W)):
        todo += [(fn, c) for c in range(0, width, pw)]
    every = -(-nt // len(todo))

    def project_block():
        fn, c = todo.pop(0)
        fn(c)

    w = [0.5 * cw_ref[k] for k in range(CONV_WIDTH)]
    cb = 0.5 * cb_ref[...]
    ring = CONV_RING

    def ext_steps(e, s):
        return _split_time(uext_ref[e * SUBLANES:(e + 1) * SUBLANES, :], slab_ref, s)

    tail = ext_steps(0, 0)
    prev2, prev1 = tail[SUBLANES - 2], tail[SUBLANES - 1]
    cur = ext_steps(1, 1)
    for t in range(nt):
        if t % every == 0 and todo:
            project_block()
        nxt = ext_steps(t + 2, 2 * (t % ring))
        x = [prev2, prev1] + cur + [nxt[0]]
        out = [(x[r] * w[0] + x[r + 1] * w[1]) + (x[r + 2] * w[2] + x[r + 3] * w[3]) + cb
               for r in range(SUBLANES)]
        uc_ref[t * SUBLANES:(t + 1) * SUBLANES, :] = _join_time(out, slab_ref, 2 * (t % ring) + 1)
        prev2, prev1, cur = cur[SUBLANES - 2], cur[SUBLANES - 1], nxt
    while todo:
        project_block()


def _in_proj(x2, seq, g, w, bg, cw, cb):
    t = x2.shape[0]
    tm = TM_PROJ
    nblk = seq // tm
    hb = tm // SUBLANES
    nh = t // SUBLANES
    row = lambda w_: pl.BlockSpec((tm, w_), lambda i: (i, 0))
    prev = pl.BlockSpec((SUBLANES, D_MODEL), lambda i: (jnp.maximum(i * hb - 1, 0), 0))
    nxt = pl.BlockSpec((SUBLANES, D_MODEL), lambda i: (jnp.minimum((i + 1) * hb, nh - 1), 0))
    return pl.pallas_call(
        functools.partial(_in_proj_kernel, nblk),
        grid=(t // tm,),
        in_specs=[row(D_MODEL), prev, nxt, _resident((1, D_MODEL)), _resident((D_MODEL, IN_W)),
                  _resident((1, 2 * D_MODEL)), _resident(cw.shape), _resident(cb.shape)],
        out_specs=[row(D_MODEL), row(Q_W), row(KV_W), row(KV_W), row(2 * D_MODEL)],
        out_shape=[
            jax.ShapeDtypeStruct((t, D_MODEL), F32),
            jax.ShapeDtypeStruct((t, Q_W), BF16),
            jax.ShapeDtypeStruct((t, KV_W), BF16),
            jax.ShapeDtypeStruct((t, KV_W), BF16),
            jax.ShapeDtypeStruct((t, 2 * D_MODEL), F32),
        ],
        scratch_shapes=[
            pltpu.VMEM((tm + 2 * SUBLANES, D_MODEL), F32),
            pltpu.VMEM((2 * CONV_RING, SLAB, LANES), F32),
        ],
        compiler_params=pltpu.CompilerParams(
            dimension_semantics=("arbitrary",), vmem_limit_bytes=VMEM_LIMIT_BYTES),
        name="in_proj",
    )(x2, x2, x2, g, w, bg, cw, cb)


def _split_time(tile, slab_ref, s):
    for j in range(LANE_TILES):
        slab_ref[s, j * SUBLANES:(j + 1) * SUBLANES, :] = tile[:, j * LANES:(j + 1) * LANES]
    return [slab_ref[s, pl.ds(r, LANE_TILES, stride=SUBLANES), :] for r in range(SUBLANES)]


def _join_time(steps, slab_ref, s):
    for r in range(SUBLANES):
        slab_ref[s, pl.ds(r, LANE_TILES, stride=SUBLANES), :] = steps[r]
    return jnp.concatenate(
        [slab_ref[s, j * SUBLANES:(j + 1) * SUBLANES, :] for j in range(LANE_TILES)], axis=1)


def _join_time_by_loads(steps, slab_ref, s):
    for r in range(SUBLANES):
        slab_ref[s, r * LANE_TILES:(r + 1) * LANE_TILES, :] = steps[r]
    return jnp.concatenate(
        [slab_ref[s, pl.ds(j, SUBLANES, stride=LANE_TILES), :] for j in range(LANE_TILES)], axis=1)


def _half_decay_log2(lam):
    return (-0.5 * RGLRU_C * math.log2(math.e)) * (
        jnp.maximum(-lam, 0.0) + jnp.log1p(jnp.exp(-jnp.abs(lam))))


def _gate_terms(pre_r, pre_i, huc, hd2, hba, hbx):
    a = jnp.exp2(hd2 * jnp.tanh(pre_r + hba) + hd2)
    y = jnp.maximum(1.0 - a * a, 0.0)
    beta = y * lax.rsqrt(jnp.maximum(y, F32_TINY))
    return a, beta * (jnp.tanh(pre_i + hbx) * huc + huc)


def _gelu_tanh(g):
    c = math.sqrt(2.0 / math.pi)
    half = 0.5 * g
    return half * jnp.tanh(g * ((c * 0.044715) * (g * g) + c)) + half


def _gate_maps(uc_ref, wg_ref, ra_ref, ix_ref, rows):
    for gi in range(N_GROUPS):
        cols = slice(gi * LRU_GROUP, (gi + 1) * LRU_GROUP)
        pre = jnp.dot(uc_ref[rows, cols].astype(BF16), wg_ref[gi], preferred_element_type=F32)
        ra_ref[rows, cols] = pre[:, :LRU_GROUP]
        ix_ref[rows, cols] = pre[:, LRU_GROUP:]


def _recurrence(reverse, uc_ref, wg_ref, ra_ref, ix_ref, lam_ref, ba_ref, bx_ref, carry_ref, slab_ref,
                emit):
    ts = uc_ref.shape[0]
    nt = ts // SUBLANES
    nsteps = nt // TILES_PER_STEP
    d = slice(int(reverse), int(reverse) + 1)
    hd2 = _half_decay_log2(lam_ref[d, :])
    hba = 0.5 * ba_ref[d, :]
    hbx = 0.5 * bx_ref[d, :]

    def tile(k, s, h):
        kk = (nt - 1 - k) if reverse else k
        rows = pl.ds(pl.multiple_of(kk * SUBLANES, SUBLANES), SUBLANES)
        a, b = _gate_terms(ra_ref[rows, :], ix_ref[rows, :], uc_ref[rows, :], hd2, hba, hbx)
        a_t = _split_time(a, slab_ref, 3 * s)
        b_t = _split_time(b, slab_ref, 3 * s + 1)
        hs = [None] * SUBLANES
        for r in (range(SUBLANES - 1, -1, -1) if reverse else range(SUBLANES)):
            h = a_t[r] * h + b_t[r]
            hs[r] = h
        emit(rows, _join_time_by_loads(hs, slab_ref, 3 * s + 2))
        return h

    def group(i, h):
        for s in range(TILES_PER_STEP):
            h = tile(TILES_PER_STEP * i + s, s, h)
        return h

    _gate_maps(uc_ref, wg_ref, ra_ref, ix_ref, slice(0, ts))
    carry_ref[...] = lax.fori_loop(0, nsteps, group, carry_ref[...])


def _lru_kernel(reverse, uc_ref, lam_ref, wg_ref, ba_ref, bx_ref, h_ref, ra_ref, ix_ref, carry_ref,
                slab_ref):
    @pl.when(pl.program_id(1) == 0)
    def _():
        carry_ref[...] = jnp.zeros_like(carry_ref)

    def emit(rows, h):
        h_ref[rows, :] = h

    _recurrence(reverse, uc_ref, wg_ref, ra_ref, ix_ref, lam_ref, ba_ref, bx_ref, carry_ref, slab_ref,
                emit)


def _lru(reverse, uc3, lam, wg, ba, bx):
    bsz, s, _ = uc3.shape
    ts = TS_LRU
    nblk = s // ts
    blk = pl.BlockSpec((None, ts, D_MODEL), lambda b, j: (b, (nblk - 1 - j) if reverse else j, 0))
    vec = _resident(lam.shape)
    return pl.pallas_call(
        functools.partial(_lru_kernel, reverse),
        grid=(bsz, nblk),
        in_specs=[blk, vec, _resident(wg.shape), vec, vec],
        out_specs=blk,
        out_shape=jax.ShapeDtypeStruct((bsz, s, D_MODEL), F32),
        scratch_shapes=[
            pltpu.VMEM((ts, D_MODEL), F32),
            pltpu.VMEM((ts, D_MODEL), F32),
            pltpu.VMEM((LANE_TILES, LANES), F32),
            pltpu.VMEM((N_SLABS, SLAB, LANES), F32),
        ],
        compiler_params=pltpu.CompilerParams(
            dimension_semantics=("arbitrary", "arbitrary"), vmem_limit_bytes=VMEM_LIMIT_BYTES),
        name="lru_bwd" if reverse else "lru_fwd",
    )(uc3, lam, wg, ba, bx)


_N_BIAS = 3 * N_HEADS + 1
_MASKED_TILE = 3 * N_HEADS


def _attn_kernel(nb, sink_ref, q_ref, kp_ref, km_ref, kn_ref, vp_ref, vm_ref, vn_ref,
                 wo_ref, wi_ref, wd_ref, o_ref, wob_ref, wib_ref, wdb_ref,
                 kz_ref, vab_ref, bias_ref, s_ref, m_ref, p_ref):
    tq = q_ref.shape[0]
    tk = tq + 2 * BLOCK
    nsub = tq // BLOCK
    j = pl.program_id(1)

    wob_ref[...] = wo_ref[...].astype(BF16)
    wdb_ref[...] = wd_ref[...].astype(BF16)
    gate_col = lax.broadcasted_iota(jnp.int32, (1, 2 * D_FF), 1) < D_FF
    wib_ref[...] = (wi_ref[...] * jnp.where(gate_col, 0.5, 1.0)).astype(BF16)

    @pl.when((pl.program_id(0) == 0) & (j == 0))
    def _():
        row = lax.broadcasted_iota(jnp.int32, (BLOCK, BLOCK), 0)
        col = lax.broadcasted_iota(jnp.int32, (BLOCK, BLOCK), 1)
        bias_ref[_MASKED_TILE] = jnp.ones((BLOCK, BLOCK), F32)
        for c in range(3):
            dist = jnp.abs(row + (1 - c) * BLOCK - col)
            absd = dist.astype(F32)
            for h in range(N_HEADS):
                bias_ref[3 * h + c] = jnp.where(dist <= WINDOW, (-_ALIBI_SLOPES[h] * absd) * LOG2_E, 1.0)
        low_c = lax.broadcasted_iota(jnp.int32, (BLOCK, LANES), 1) < HEAD_DIM
        for half in range(2):
            ones = jnp.where(low_c if half == 0 else ~low_c, 1.0, 0.0).astype(BF16)
            for g in range(N_KV_HEADS):
                for b in range(tk // BLOCK):
                    vab_ref[g, (2 * b + half) * BLOCK:(2 * b + half + 1) * BLOCK, LANES:] = ones

    low = lax.broadcasted_iota(jnp.int32, (tk, LANES), 1) < HEAD_DIM
    low_q = lax.broadcasted_iota(jnp.int32, (BLOCK, LANES), 1) < HEAD_DIM
    for t in range(KV_W // LANES):
        cols = slice(t * LANES, (t + 1) * LANES)
        full = jnp.concatenate([r[:, cols] for r in (kp_ref, km_ref, kn_ref)], axis=0).astype(F32)
        swapped = pltpu.roll(full, HEAD_DIM, axis=1)
        kz_ref[4 * t + 0] = jnp.where(low, full, 0.0).astype(BF16)
        kz_ref[4 * t + 1] = jnp.where(low, 0.0, swapped).astype(BF16)
        kz_ref[4 * t + 2] = jnp.where(low, swapped, 0.0).astype(BF16)
        kz_ref[4 * t + 3] = jnp.where(low, 0.0, full).astype(BF16)
        nkb = tk // BLOCK
        for b in range(nkb):
            if b == 0:
                src = vp_ref[:, cols]
            elif b == nkb - 1:
                src = vn_ref[:, cols]
            else:
                src = vm_ref[(b - 1) * BLOCK:b * BLOCK, cols]
            full = src.astype(F32)
            swapped = pltpu.roll(full, HEAD_DIM, axis=1)
            placed = [jnp.where(low_q, full, 0.0), jnp.where(low_q, 0.0, swapped),
                      jnp.where(low_q, swapped, 0.0), jnp.where(low_q, 0.0, full)]
            for i in range(4):
                g, half = 2 * t + i // 2, i % 2
                dst = slice((2 * b + half) * BLOCK, (2 * b + half + 1) * BLOCK)
                vab_ref[g, dst, :LANES] = placed[i].astype(BF16)

    def scores(qi, slot):
        q0 = pl.multiple_of(qi * BLOCK, BLOCK)
        blk = j * nsub + qi
        win = pl.ds(q0, 3 * BLOCK)
        sc2 = [None, None]
        for h in range(N_HEADS):
            g, pp, half = h // GROUP, (h % GROUP) // 2, h % 2
            if pp == 0:
                q2 = jnp.concatenate(
                    [q_ref[pl.ds(q0, BLOCK), (2 * g + i) * LANES:(2 * g + i + 1) * LANES] for i in range(2)],
                    axis=0)
                sc2[half] = lax.dot_general(q2, kz_ref[2 * g + half, win, :], (((1,), (1,)), ((), ())),
                                            preferred_element_type=F32)
            sc = sc2[half][pp * BLOCK:(pp + 1) * BLOCK]
            tiles = []
            for c in range(3):
                t = sc[:, c * BLOCK:(c + 1) * BLOCK]
                if c == 1:
                    t = t + bias_ref[3 * h + 1]
                else:
                    edge = (blk == 0) if c == 0 else (blk == nb - 1)
                    b = bias_ref[jnp.where(edge, _MASKED_TILE, 3 * h + c)]
                    t = jnp.where(b > 0.0, NEG_INF, t + b)
                s_ref[slot, h, :, c * BLOCK:(c + 1) * BLOCK] = t
                tiles.append(t)
            mx = jnp.max(jnp.maximum(jnp.maximum(tiles[0], tiles[1]), tiles[2]), axis=-1, keepdims=True)
            m_ref[slot, h] = jnp.broadcast_to(jnp.maximum(mx, sink_ref[h] * LOG2_E), (BLOCK, LANES))

    def finish(qi, slot):
        q0 = pl.multiple_of(qi * BLOCK, BLOCK)
        win2 = pl.ds(pl.multiple_of(qi * (2 * BLOCK), 2 * BLOCK), 6 * BLOCK)
        for h in range(N_HEADS):
            m = m_ref[slot, h]
            for c in range(3):
                cs = slice(c * BLOCK, (c + 1) * BLOCK)
                dst = slice((2 * c + h % 2) * BLOCK, (2 * c + h % 2 + 1) * BLOCK)
                prow = slice(((h // 2) % 2) * BLOCK, ((h // 2) % 2 + 1) * BLOCK)
                p_ref[slot, h // GROUP, prow, dst] = jnp.exp2(s_ref[slot, h, :, cs] - m).astype(BF16)
        for g in range(N_KV_HEADS):
            o2 = jnp.dot(p_ref[slot, g], vab_ref[g, win2, :], preferred_element_type=F32)
            for pp in range(2):
                pr = 2 * g + pp
                h0, h1 = 2 * pr, 2 * pr + 1
                o = o2[pp * BLOCK:(pp + 1) * BLOCK]
                sink2 = jnp.where(low_q[0:1, :], sink_ref[h0] * LOG2_E, sink_ref[h1] * LOG2_E)
                e = jnp.exp2(sink2 - jnp.where(low_q, m_ref[slot, h0], m_ref[slot, h1]))
                o_ref[pl.ds(q0, BLOCK), pr * LANES:(pr + 1) * LANES] = (
                    o[:, :LANES] * (1.0 / (o[:, LANES:] + e)))

    scores(0, 0)

    def two_sub_blocks(i, _):
        qa = 2 * i
        scores(qa + 1, 1)
        finish(qa, 0)
        scores(jnp.minimum(qa + 2, nsub - 1), 0)
        finish(qa + 1, 1)
        return 0

    lax.fori_loop(0, nsub // 2, two_sub_blocks, 0)


def _attention(q3, k3, v3, sink, ffn_weights):
    bsz, s, _ = q3.shape
    tq = TQ_ATTN
    nq = s // tq
    hb = tq // BLOCK
    nb = s // BLOCK
    tk = tq + 2 * BLOCK
    kv_main = pl.BlockSpec((None, tq, KV_W), lambda b, j: (b, j, 0))
    kv_prev = pl.BlockSpec((None, BLOCK, KV_W), lambda b, j: (b, jnp.maximum(j * hb - 1, 0), 0))
    kv_next = pl.BlockSpec((None, BLOCK, KV_W), lambda b, j: (b, jnp.minimum((j + 1) * hb, nb - 1), 0))
    qo = pl.BlockSpec((None, tq, Q_W), lambda b, j: (b, j, 0))
    nsteps = bsz * nq
    w_slice = lambda w: pl.BlockSpec((w.shape[0] // nsteps, w.shape[1]), lambda b, j: (b * nq + j, 0))
    w_specs = [w_slice(w) for w in ffn_weights]
    return pl.pallas_call(
        functools.partial(_attn_kernel, nb),
        grid=(bsz, nq),
        in_specs=[pl.BlockSpec(memory_space=pltpu.SMEM), qo,
                  kv_prev, kv_main, kv_next, kv_prev, kv_main, kv_next] + w_specs,
        out_specs=[qo] + w_specs,
        out_shape=[jax.ShapeDtypeStruct((bsz, s, Q_W), F32)]
        + [jax.ShapeDtypeStruct(w.shape, BF16) for w in ffn_weights],
        scratch_shapes=[
            pltpu.VMEM((2 * N_KV_HEADS, tk, LANES), BF16),
            pltpu.VMEM((N_KV_HEADS, 2 * tk, 2 * LANES), BF16),
            pltpu.VMEM((_N_BIAS, BLOCK, BLOCK), F32),
            pltpu.VMEM((2, N_HEADS, BLOCK, 3 * BLOCK), F32),
            pltpu.VMEM((2, N_HEADS, BLOCK, LANES), F32),
            pltpu.VMEM((2, N_KV_HEADS, 2 * BLOCK, 6 * BLOCK), BF16),
        ],
        compiler_params=pltpu.CompilerParams(
            dimension_semantics=("arbitrary", "arbitrary"), vmem_limit_bytes=VMEM_LIMIT_BYTES),
        name="attention",
    )(sink, q3, k3, k3, k3, v3, v3, v3, *ffn_weights)


def _ffn_kernel(x_ref, hf_ref, hb_ref, yb_ref, coef_ref, wo_ref, g2_ref, wi_ref, wd_ref, g3_ref, o_ref):
    merged = coef_ref[:, :D_MODEL] * (hf_ref[...] + hb_ref[...]) + coef_ref[:, D_MODEL:] * yb_ref[...]
    x1 = x_ref[...] + jnp.dot(merged.astype(BF16), wo_ref[...], preferred_element_type=F32)
    xn = _rmsnorm(x1, g2_ref[...]).astype(BF16)
    gate = jnp.dot(xn, wi_ref[:, :D_FF], preferred_element_type=F32)
    up = jnp.dot(xn, wi_ref[:, D_FF:], preferred_element_type=F32)
    act = ((gate * jnp.tanh(gate) + gate) * up).astype(BF16)
    x2 = x1 + jnp.dot(act, wd_ref[...], preferred_element_type=F32)
    o_ref[...] = _rmsnorm(x2, g3_ref[...])


def _ffn(x2, hf, hb, yb, coef, wo, g2, wi, wd, g3):
    t = x2.shape[0]
    tm = TM_FFN
    row = lambda w_: pl.BlockSpec((tm, w_), lambda i: (i, 0))
    return pl.pallas_call(
        _ffn_kernel,
        grid=(t // tm,),
        in_specs=[row(D_MODEL), row(D_MODEL), row(D_MODEL), row(D_MODEL), row(2 * D_MODEL),
                  _resident((D_MODEL, D_MODEL)), _resident((1, D_MODEL)),
                  _resident((D_MODEL, 2 * D_FF)), _resident((D_FF, D_MODEL)), _resident((1, D_MODEL))],
        out_specs=row(D_MODEL),
        out_shape=jax.ShapeDtypeStruct((t, D_MODEL), F32),
        compiler_params=pltpu.CompilerParams(
            dimension_semantics=("arbitrary",), vmem_limit_bytes=VMEM_LIMIT_BYTES),
        name="merge_ffn",
    )(x2, hf, hb, yb, coef, wo, g2, wi, wd, g3)


def _gate_weights(wa, wx):
    def blockdiag(w):
        w = w.reshape(N_GROUPS, LRU_GROUP // LRU_BLOCK, LRU_BLOCK, LRU_BLOCK)
        eye = jnp.eye(LRU_GROUP // LRU_BLOCK, dtype=w.dtype)
        return jnp.einsum("ghij,hk->ghikj", w, eye).reshape(N_GROUPS, LRU_GROUP, LRU_GROUP)
    return jnp.concatenate([blockdiag(wa), blockdiag(wx)], axis=-1).astype(BF16)


def kernel(x, norm_mix_g, w_in, b_gate, conv_w, conv_b, lru_lambda, lru_wa, lru_ba, lru_wx, lru_bx,
           attn_sink, w_out, norm_ffn_g, w_ffn_in, w_ffn_out, norm_final_g):
    bsz, s, d = x.shape
    depth = w_in.shape[0]
    assert depth == 1, "the merge/ffn kernel applies the final RMSNorm, so it must be the last layer"
    t = bsz * s
    row = lambda a: a.reshape(1, -1)
    time_major = lambda a: a.reshape(a.shape[:-1] + (LANE_TILES, LANES))
    x2 = x.reshape(t, d)
    for l in range(depth):
        huc, q, k, v, merge_coef = _in_proj(
            x2, s, row(norm_mix_g[l]), w_in[l].astype(BF16), row(b_gate[l]),
            time_major(conv_w[l]), time_major(conv_b[l]))
        seq = lambda a: a.reshape(bsz, s, a.shape[-1])
        lru = lambda dr: (lru_lambda[l], _gate_weights(lru_wa[l, dr], lru_wx[l, dr]), lru_ba[l], lru_bx[l])
        h_fwd = _lru(False, seq(huc), *lru(0))
        h_bwd = _lru(True, seq(huc), *lru(1))
        y_b, wo_b, wi_b, wd_b = _attention(seq(q), seq(k), seq(v), attn_sink[l],
                                           (w_out[l], w_ffn_in[l], w_ffn_out[l]))
        x2 = _ffn(x2, h_fwd.reshape(t, d), h_bwd.reshape(t, d), y_b.reshape(t, d), merge_coef,
                  wo_b, row(norm_ffn_g[l]), wi_b, wd_b, row(norm_final_g))
    return x2.reshape(bsz, s, d)
```

```python
import functools
import math

import numpy as np
import jax
import jax.numpy as jnp
from jax import lax
from jax.experimental import pallas as pl
from jax.experimental.pallas import tpu as pltpu

F32 = jnp.float32
BF16 = jnp.bfloat16

D_MODEL = 1024
LRU_HEADS = 16
LRU_BLOCK = D_MODEL // LRU_HEADS
CONV_WIDTH = 4
CONV_LEFT = 2
RGLRU_C = 8.0
N_HEADS = 16
N_KV_HEADS = 4
HEAD_DIM = 64
GROUP = N_HEADS // N_KV_HEADS
WINDOW = 128
BLOCK = 128
D_FF = 2816
Q_W = N_HEADS * HEAD_DIM
KV_W = N_KV_HEADS * HEAD_DIM
IN_W = 2 * D_MODEL + Q_W + 2 * KV_W + 2 * D_MODEL
EPS = 1e-6
NEG_INF = -1e30
F32_TINY = float(np.finfo(np.float32).tiny)

LANES = 128
SUBLANES = 8
VMEM_LIMIT_BYTES = 56 * 1024 * 1024

TM_PROJ = 1024
TS_LRU = 1024
TQ_ATTN = 1024
TM_FFN = 512
LRU_GROUP = 256
N_GROUPS = D_MODEL // LRU_GROUP
LANE_TILES = D_MODEL // LANES
assert LANE_TILES == SUBLANES
SLAB = SUBLANES * LANE_TILES
TILES_PER_STEP = 32
N_SLABS = 3 * TILES_PER_STEP
CONV_RING = 8

_ALIBI_SLOPES = [float(v) for v in np.exp2(
    -8.0 * (np.arange(N_HEADS, dtype=np.float32) + 1.0) / N_HEADS).astype(np.float32)]
LOG2_E = math.log2(math.e)
Q_SCALE = HEAD_DIM ** -0.5 * LOG2_E


def _sigmoid(x):
    return 0.5 * jnp.tanh(0.5 * x) + 0.5


def _rmsnorm(x, g):
    return x * lax.rsqrt(jnp.mean(x * x, axis=-1, keepdims=True) + EPS) * g


def _resident(shape):
    nd = len(shape)
    return pl.BlockSpec(shape, lambda *_: (0,) * nd, pipeline_mode=pl.Buffered(1))


def _in_proj_kernel(nblk, x_ref, xp_ref, xnx_ref, g_ref, w_ref, bg_ref, cw_ref, cb_ref,
                    uc_ref, q_ref, k_ref, v_ref, coef_ref, uext_ref, slab_ref):
    tm = x_ref.shape[0]
    nt = tm // SUBLANES
    jb = pl.program_id(0) % nblk
    gn = g_ref[...]
    xn = _rmsnorm(x_ref[...], gn).astype(BF16)
    halo = jnp.concatenate([xp_ref[...], xnx_ref[...]], axis=0)
    xh = _rmsnorm(halo, gn).astype(BF16)

    def proj(lo, hi):
        return jnp.dot(xn, w_ref[:, lo:hi], preferred_element_type=F32)

    ue = jnp.dot(jnp.concatenate([xn, xh], axis=0), w_ref[:, :D_MODEL], preferred_element_type=F32)
    uext_ref[SUBLANES:SUBLANES + tm, :] = ue[:tm]
    uext_ref[0:SUBLANES, :] = jnp.where(jb > 0, ue[tm:tm + SUBLANES], 0.0)
    uext_ref[SUBLANES + tm:, :] = jnp.where(jb < nblk - 1, ue[tm + SUBLANES:], 0.0)

    o_g = D_MODEL
    o_q = o_g + D_MODEL
    o_k = o_q + Q_W
    o_v = o_k + KV_W
    o_z = o_v + KV_W
    pw = 2 * LANES
    todo = []

    def coef_a(c):
        gate = _sigmoid(proj(o_z + c, o_z + c + pw) + bg_ref[:, c:c + pw])
        coef_ref[:, c:c + pw] = gate * _gelu_tanh(proj(o_g + c, o_g + c + pw))

    def coef_b(c):
        cz = D_MODEL + c
        coef_ref[:, cz:cz + pw] = _sigmoid(proj(o_z + cz, o_z + cz + pw) + bg_ref[:, cz:cz + pw])

    def plain(dst, lo, post):
        def run(c):
            dst[:, c:c + pw] = post(proj(lo + c, lo + c + pw))
        return run

    for fn, width in ((coef_a, D_MODEL), (coef_b, D_MODEL),
                      (plain(q_ref, o_q, lambda v: (v * Q_SCALE).astype(BF16)), Q_W),
                      (plain(k_ref, o_k, lambda v: v.astype(BF16)), KV_W),
                      (plain(v_ref, o_v, lambda v: v.astype(BF16)), KV_W)):
        todo += [(fn, c) for c in range(0, width, pw)]
    every = -(-nt // len(todo))

    def project_block():
        fn, c = todo.pop(0)
        fn(c)

    w = [0.5 * cw_ref[k] for k in range(CONV_WIDTH)]
    cb = 0.5 * cb_ref[...]
    ring = CONV_RING

    def ext_steps(e, s):
        return _split_time(uext_ref[e * SUBLANES:(e + 1) * SUBLANES, :], slab_ref, s)

    tail = ext_steps(0, 0)
    prev2, prev1 = tail[SUBLANES - 2], tail[SUBLANES - 1]
    cur = ext_steps(1, 1)
    for t in range(nt):
        if t % every == 0 and todo:
            project_block()
        nxt = ext_steps(t + 2, 2 * (t % ring))
        x = [prev2, prev1] + cur + [nxt[0]]
        out = [(x[r] * w[0] + x[r + 1] * w[1]) + (x[r + 2] * w[2] + x[r + 3] * w[3]) + cb
               for r in range(SUBLANES)]
        uc_ref[t * SUBLANES:(t + 1) * SUBLANES, :] = _join_time(out, slab_ref, 2 * (t % ring) + 1)
        prev2, prev1, cur = cur[SUBLANES - 2], cur[SUBLANES - 1], nxt
    while todo:
        project_block()


def _in_proj(x2, seq, g, w, bg, cw, cb):
    t = x2.shape[0]
    tm = TM_PROJ
    nblk = seq // tm
    hb = tm // SUBLANES
    nh = t // SUBLANES
    row = lambda w_: pl.BlockSpec((tm, w_), lambda i: (i, 0))
    prev = pl.BlockSpec((SUBLANES, D_MODEL), lambda i: (jnp.maximum(i * hb - 1, 0), 0))
    nxt = pl.BlockSpec((SUBLANES, D_MODEL), lambda i: (jnp.minimum((i + 1) * hb, nh - 1), 0))
    return pl.pallas_call(
        functools.partial(_in_proj_kernel, nblk),
        grid=(t // tm,),
        in_specs=[row(D_MODEL), prev, nxt, _resident((1, D_MODEL)), _resident((D_MODEL, IN_W)),
                  _resident((1, 2 * D_MODEL)), _resident(cw.shape), _resident(cb.shape)],
        out_specs=[row(D_MODEL), row(Q_W), row(KV_W), row(KV_W), row(2 * D_MODEL)],
        out_shape=[
            jax.ShapeDtypeStruct((t, D_MODEL), F32),
            jax.ShapeDtypeStruct((t, Q_W), BF16),
            jax.ShapeDtypeStruct((t, KV_W), BF16),
            jax.ShapeDtypeStruct((t, KV_W), BF16),
            jax.ShapeDtypeStruct((t, 2 * D_MODEL), F32),
        ],
        scratch_shapes=[
            pltpu.VMEM((tm + 2 * SUBLANES, D_MODEL), F32),
            pltpu.VMEM((2 * CONV_RING, SLAB, LANES), F32),
        ],
        compiler_params=pltpu.CompilerParams(
            dimension_semantics=("arbitrary",), vmem_limit_bytes=VMEM_LIMIT_BYTES),
        name="in_proj",
    )(x2, x2, x2, g, w, bg, cw, cb)


def _split_time(tile, slab_ref, s):
    for j in range(LANE_TILES):
        slab_ref[s, j * SUBLANES:(j + 1) * SUBLANES, :] = tile[:, j * LANES:(j + 1) * LANES]
    return [slab_ref[s, pl.ds(r, LANE_TILES, stride=SUBLANES), :] for r in range(SUBLANES)]


def _join_time(steps, slab_ref, s):
    for r in range(SUBLANES):
        slab_ref[s, pl.ds(r, LANE_TILES, stride=SUBLANES), :] = steps[r]
    return jnp.concatenate(
        [slab_ref[s, j * SUBLANES:(j + 1) * SUBLANES, :] for j in range(LANE_TILES)], axis=1)


def _half_decay_log2(lam):
    return (-0.5 * RGLRU_C * math.log2(math.e)) * (
        jnp.maximum(-lam, 0.0) + jnp.log1p(jnp.exp(-jnp.abs(lam))))


def _gate_terms(pre_r, pre_i, huc, hd2, hba, hbx):
    a = jnp.exp2(hd2 * jnp.tanh(pre_r + hba) + hd2)
    y = jnp.maximum(1.0 - a * a, 0.0)
    beta = y * lax.rsqrt(jnp.maximum(y, F32_TINY))
    return a, beta * (jnp.tanh(pre_i + hbx) * huc + huc)


def _gelu_tanh(g):
    c = math.sqrt(2.0 / math.pi)
    half = 0.5 * g
    return half * jnp.tanh(g * ((c * 0.044715) * (g * g) + c)) + half


def _gate_maps(uc_ref, wg_ref, ra_ref, ix_ref, rows):
    for gi in range(N_GROUPS):
        cols = slice(gi * LRU_GROUP, (gi + 1) * LRU_GROUP)
        pre = jnp.dot(uc_ref[rows, cols].astype(BF16), wg_ref[gi], preferred_element_type=F32)
        ra_ref[rows, cols] = pre[:, :LRU_GROUP]
        ix_ref[rows, cols] = pre[:, LRU_GROUP:]


def _recurrence(reverse, uc_ref, wg_ref, ra_ref, ix_ref, lam_ref, ba_ref, bx_ref, carry_ref, slab_ref,
                emit):
    ts = uc_ref.shape[0]
    nt = ts // SUBLANES
    nsteps = nt // TILES_PER_STEP
    d = slice(int(reverse), int(reverse) + 1)
    hd2 = _half_decay_log2(lam_ref[d, :])
    hba = 0.5 * ba_ref[d, :]
    hbx = 0.5 * bx_ref[d, :]

    def tile(k, s, h):
        kk = (nt - 1 - k) if reverse else k
        rows = pl.ds(pl.multiple_of(kk * SUBLANES, SUBLANES), SUBLANES)
        a, b = _gate_terms(ra_ref[rows, :], ix_ref[rows, :], uc_ref[rows, :], hd2, hba, hbx)
        a_t = _split_time(a, slab_ref, 3 * s)
        b_t = _split_time(b, slab_ref, 3 * s + 1)
        hs = [None] * SUBLANES
        for r in (range(SUBLANES - 1, -1, -1) if reverse else range(SUBLANES)):
            h = a_t[r] * h + b_t[r]
            hs[r] = h
        emit(rows, _join_time(hs, slab_ref, 3 * s + 2))
        return h

    def group(i, h):
        for s in range(TILES_PER_STEP):
            h = tile(TILES_PER_STEP * i + s, s, h)
        return h

    _gate_maps(uc_ref, wg_ref, ra_ref, ix_ref, slice(0, ts))
    carry_ref[...] = lax.fori_loop(0, nsteps, group, carry_ref[...])


def _lru_kernel(reverse, uc_ref, lam_ref, wg_ref, ba_ref, bx_ref, h_ref, ra_ref, ix_ref, carry_ref,
                slab_ref):
    @pl.when(pl.program_id(1) == 0)
    def _():
        carry_ref[...] = jnp.zeros_like(carry_ref)

    def emit(rows, h):
        h_ref[rows, :] = h

    _recurrence(reverse, uc_ref, wg_ref, ra_ref, ix_ref, lam_ref, ba_ref, bx_ref, carry_ref, slab_ref,
                emit)


def _lru(reverse, uc3, lam, wg, ba, bx):
    bsz, s, _ = uc3.shape
    ts = TS_LRU
    nblk = s // ts
    blk = pl.BlockSpec((None, ts, D_MODEL), lambda b, j: (b, (nblk - 1 - j) if reverse else j, 0))
    vec = _resident(lam.shape)
    return pl.pallas_call(
        functools.partial(_lru_kernel, reverse),
        grid=(bsz, nblk),
        in_specs=[blk, vec, _resident(wg.shape), vec, vec],
        out_specs=blk,
        out_shape=jax.ShapeDtypeStruct((bsz, s, D_MODEL), F32),
        scratch_shapes=[
            pltpu.VMEM((ts, D_MODEL), F32),
            pltpu.VMEM((ts, D_MODEL), F32),
            pltpu.VMEM((LANE_TILES, LANES), F32),
            pltpu.VMEM((N_SLABS, SLAB, LANES), F32),
        ],
        compiler_params=pltpu.CompilerParams(
            dimension_semantics=("arbitrary", "arbitrary"), vmem_limit_bytes=VMEM_LIMIT_BYTES),
        name="lru_bwd" if reverse else "lru_fwd",
    )(uc3, lam, wg, ba, bx)


_N_BIAS = 3 * N_HEADS + 1
_MASKED_TILE = 3 * N_HEADS


def _attn_kernel(nb, sink_ref, q_ref, kp_ref, km_ref, kn_ref, vp_ref, vm_ref, vn_ref,
                 wo_ref, wi_ref, wd_ref, o_ref, wob_ref, wib_ref, wdb_ref,
                 kz_ref, vab_ref, bias_ref, s_ref, m_ref, p_ref):
    tq = q_ref.shape[0]
    tk = tq + 2 * BLOCK
    nsub = tq // BLOCK
    j = pl.program_id(1)

    wob_ref[...] = wo_ref[...].astype(BF16)
    wdb_ref[...] = wd_ref[...].astype(BF16)
    gate_col = lax.broadcasted_iota(jnp.int32, (1, 2 * D_FF), 1) < D_FF
    wib_ref[...] = (wi_ref[...] * jnp.where(gate_col, 0.5, 1.0)).astype(BF16)

    @pl.when((pl.program_id(0) == 0) & (j == 0))
    def _():
        row = lax.broadcasted_iota(jnp.int32, (BLOCK, BLOCK), 0)
        col = lax.broadcasted_iota(jnp.int32, (BLOCK, BLOCK), 1)
        bias_ref[_MASKED_TILE] = jnp.ones((BLOCK, BLOCK), F32)
        for c in range(3):
            dist = jnp.abs(row + (1 - c) * BLOCK - col)
            absd = dist.astype(F32)
            for h in range(N_HEADS):
                bias_ref[3 * h + c] = jnp.where(dist <= WINDOW, (-_ALIBI_SLOPES[h] * absd) * LOG2_E, 1.0)
        low_c = lax.broadcasted_iota(jnp.int32, (BLOCK, LANES), 1) < HEAD_DIM
        for half in range(2):
            ones = jnp.where(low_c if half == 0 else ~low_c, 1.0, 0.0).astype(BF16)
            for g in range(N_KV_HEADS):
                for b in range(tk // BLOCK):
                    vab_ref[g, (2 * b + half) * BLOCK:(2 * b + half + 1) * BLOCK, LANES:] = ones

    low = lax.broadcasted_iota(jnp.int32, (tk, LANES), 1) < HEAD_DIM
    low_q = lax.broadcasted_iota(jnp.int32, (BLOCK, LANES), 1) < HEAD_DIM
    for t in range(KV_W // LANES):
        cols = slice(t * LANES, (t + 1) * LANES)
        full = jnp.concatenate([r[:, cols] for r in (kp_ref, km_ref, kn_ref)], axis=0).astype(F32)
        swapped = pltpu.roll(full, HEAD_DIM, axis=1)
        kz_ref[4 * t + 0] = jnp.where(low, full, 0.0).astype(BF16)
        kz_ref[4 * t + 1] = jnp.where(low, 0.0, swapped).astype(BF16)
        kz_ref[4 * t + 2] = jnp.where(low, swapped, 0.0).astype(BF16)
        kz_ref[4 * t + 3] = jnp.where(low, 0.0, full).astype(BF16)
        nkb = tk // BLOCK
        for b in range(nkb):
            if b == 0:
                src = vp_ref[:, cols]
            elif b == nkb - 1:
                src = vn_ref[:, cols]
            else:
                src = vm_ref[(b - 1) * BLOCK:b * BLOCK, cols]
            full = src.astype(F32)
            swapped = pltpu.roll(full, HEAD_DIM, axis=1)
            placed = [jnp.where(low_q, full, 0.0), jnp.where(low_q, 0.0, swapped),
                      jnp.where(low_q, swapped, 0.0), jnp.where(low_q, 0.0, full)]
            for i in range(4):
                g, half = 2 * t + i // 2, i % 2
                dst = slice((2 * b + half) * BLOCK, (2 * b + half + 1) * BLOCK)
                vab_ref[g, dst, :LANES] = placed[i].astype(BF16)

    def scores(qi, slot):
        q0 = pl.multiple_of(qi * BLOCK, BLOCK)
        blk = j * nsub + qi
        win = pl.ds(q0, 3 * BLOCK)
        sc2 = [None, None]
        for h in range(N_HEADS):
            g, pp, half = h // GROUP, (h % GROUP) // 2, h % 2
            if pp == 0:
                q2 = jnp.concatenate(
                    [q_ref[pl.ds(q0, BLOCK), (2 * g + i) * LANES:(2 * g + i + 1) * LANES] for i in range(2)],
                    axis=0)
                sc2[half] = lax.dot_general(q2, kz_ref[2 * g + half, win, :], (((1,), (1,)), ((), ())),
                                            preferred_element_type=F32)
            sc = sc2[half][pp * BLOCK:(pp + 1) * BLOCK]
            tiles = []
            for c in range(3):
                t = sc[:, c * BLOCK:(c + 1) * BLOCK]
                if c == 1:
                    t = t + bias_ref[3 * h + 1]
                else:
                    edge = (blk == 0) if c == 0 else (blk == nb - 1)
                    b = bias_ref[jnp.where(edge, _MASKED_TILE, 3 * h + c)]
                    t = jnp.where(b > 0.0, NEG_INF, t + b)
                s_ref[slot, h, :, c * BLOCK:(c + 1) * BLOCK] = t
                tiles.append(t)
            mx = jnp.max(jnp.maximum(jnp.maximum(tiles[0], tiles[1]), tiles[2]), axis=-1, keepdims=True)
            m_ref[slot, h] = jnp.broadcast_to(jnp.maximum(mx, sink_ref[h] * LOG2_E), (BLOCK, LANES))

    def finish(qi, slot):
        q0 = pl.multiple_of(qi * BLOCK, BLOCK)
        win2 = pl.ds(pl.multiple_of(qi * (2 * BLOCK), 2 * BLOCK), 6 * BLOCK)
        for h in range(N_HEADS):
            m = m_ref[slot, h]
            for c in range(3):
                cs = slice(c * BLOCK, (c + 1) * BLOCK)
                dst = slice((2 * c + h % 2) * BLOCK, (2 * c + h % 2 + 1) * BLOCK)
                prow = slice(((h // 2) % 2) * BLOCK, ((h // 2) % 2 + 1) * BLOCK)
                p_ref[slot, h // GROUP, prow, dst] = jnp.exp2(s_ref[slot, h, :, cs] - m).astype(BF16)
        for g in range(N_KV_HEADS):
            o2 = jnp.dot(p_ref[slot, g], vab_ref[g, win2, :], preferred_element_type=F32)
            for pp in range(2):
                pr = 2 * g + pp
                h0, h1 = 2 * pr, 2 * pr + 1
                o = o2[pp * BLOCK:(pp + 1) * BLOCK]
                sink2 = jnp.where(low_q[0:1, :], sink_ref[h0] * LOG2_E, sink_ref[h1] * LOG2_E)
                e = jnp.exp2(sink2 - jnp.where(low_q, m_ref[slot, h0], m_ref[slot, h1]))
                o_ref[pl.ds(q0, BLOCK), pr * LANES:(pr + 1) * LANES] = (
                    o[:, :LANES] * (1.0 / (o[:, LANES:] + e)))

    scores(0, 0)

    def two_sub_blocks(i, _):
        qa = 2 * i
        scores(qa + 1, 1)
        finish(qa, 0)
        scores(jnp.minimum(qa + 2, nsub - 1), 0)
        finish(qa + 1, 1)
        return 0

    lax.fori_loop(0, nsub // 2, two_sub_blocks, 0)


def _attention(q3, k3, v3, sink, ffn_weights):
    bsz, s, _ = q3.shape
    tq = TQ_ATTN
    nq = s // tq
    hb = tq // BLOCK
    nb = s // BLOCK
    tk = tq + 2 * BLOCK
    kv_main = pl.BlockSpec((None, tq, KV_W), lambda b, j: (b, j, 0))
    kv_prev = pl.BlockSpec((None, BLOCK, KV_W), lambda b, j: (b, jnp.maximum(j * hb - 1, 0), 0))
    kv_next = pl.BlockSpec((None, BLOCK, KV_W), lambda b, j: (b, jnp.minimum((j + 1) * hb, nb - 1), 0))
    qo = pl.BlockSpec((None, tq, Q_W), lambda b, j: (b, j, 0))
    nsteps = bsz * nq
    w_slice = lambda w: pl.BlockSpec((w.shape[0] // nsteps, w.shape[1]), lambda b, j: (b * nq + j, 0))
    w_specs = [w_slice(w) for w in ffn_weights]
    return pl.pallas_call(
        functools.partial(_attn_kernel, nb),
        grid=(bsz, nq),
        in_specs=[pl.BlockSpec(memory_space=pltpu.SMEM), qo,
                  kv_prev, kv_main, kv_next, kv_prev, kv_main, kv_next] + w_specs,
        out_specs=[qo] + w_specs,
        out_shape=[jax.ShapeDtypeStruct((bsz, s, Q_W), F32)]
        + [jax.ShapeDtypeStruct(w.shape, BF16) for w in ffn_weights],
        scratch_shapes=[
            pltpu.VMEM((2 * N_KV_HEADS, tk, LANES), BF16),
            pltpu.VMEM((N_KV_HEADS, 2 * tk, 2 * LANES), BF16),
            pltpu.VMEM((_N_BIAS, BLOCK, BLOCK), F32),
            pltpu.VMEM((2, N_HEADS, BLOCK, 3 * BLOCK), F32),
            pltpu.VMEM((2, N_HEADS, BLOCK, LANES), F32),
            pltpu.VMEM((2, N_KV_HEADS, 2 * BLOCK, 6 * BLOCK), BF16),
        ],
        compiler_params=pltpu.CompilerParams(
            dimension_semantics=("arbitrary", "arbitrary"), vmem_limit_bytes=VMEM_LIMIT_BYTES),
        name="attention",
    )(sink, q3, k3, k3, k3, v3, v3, v3, *ffn_weights)


def _ffn_kernel(x_ref, hf_ref, hb_ref, yb_ref, coef_ref, wo_ref, g2_ref, wi_ref, wd_ref, g3_ref, o_ref):
    merged = coef_ref[:, :D_MODEL] * (hf_ref[...] + hb_ref[...]) + coef_ref[:, D_MODEL:] * yb_ref[...]
    x1 = x_ref[...] + jnp.dot(merged.astype(BF16), wo_ref[...], preferred_element_type=F32)
    xn = _rmsnorm(x1, g2_ref[...]).astype(BF16)
    gate = jnp.dot(xn, wi_ref[:, :D_FF], preferred_element_type=F32)
    up = jnp.dot(xn, wi_ref[:, D_FF:], preferred_element_type=F32)
    act = ((gate * jnp.tanh(gate) + gate) * up).astype(BF16)
    x2 = x1 + jnp.dot(act, wd_ref[...], preferred_element_type=F32)
    o_ref[...] = _rmsnorm(x2, g3_ref[...])


def _ffn(x2, hf, hb, yb, coef, wo, g2, wi, wd, g3):
    t = x2.shape[0]
    tm = TM_FFN
    row = lambda w_: pl.BlockSpec((tm, w_), lambda i: (i, 0))
    stream = pl.BlockSpec(memory_space=pl.ANY)
    whole = pl.BlockSpec(memory_space=pltpu.VMEM)

    def outer(x_hbm, hf_hbm, hb_hbm, yb_hbm, coef_hbm, wo_ref, g2_ref, wi_ref, wd_ref, g3_ref, o_hbm):
        def step(x_ref, hf_ref, hb_ref, yb_ref, coef_ref, o_ref):
            _ffn_kernel(x_ref, hf_ref, hb_ref, yb_ref, coef_ref, wo_ref, g2_ref, wi_ref, wd_ref, g3_ref, o_ref)
        pltpu.emit_pipeline(
            step,
            grid=(t // tm,),
            in_specs=[row(D_MODEL), row(D_MODEL), row(D_MODEL), row(D_MODEL), row(2 * D_MODEL)],
            out_specs=[row(D_MODEL)],
        )(x_hbm, hf_hbm, hb_hbm, yb_hbm, coef_hbm, o_hbm)

    return pl.pallas_call(
        outer,
        in_specs=[stream] * 5 + [whole] * 5,
        out_specs=stream,
        out_shape=jax.ShapeDtypeStruct((t, D_MODEL), F32),
        compiler_params=pltpu.CompilerParams(vmem_limit_bytes=VMEM_LIMIT_BYTES),
        name="merge_ffn",
    )(x2, hf, hb, yb, coef, wo, g2, wi, wd, g3)


def _gate_weights(wa, wx):
    def blockdiag(w):
        w = w.reshape(N_GROUPS, LRU_GROUP // LRU_BLOCK, LRU_BLOCK, LRU_BLOCK)
        eye = jnp.eye(LRU_GROUP // LRU_BLOCK, dtype=w.dtype)
        return jnp.einsum("ghij,hk->ghikj", w, eye).reshape(N_GROUPS, LRU_GROUP, LRU_GROUP)
    return jnp.concatenate([blockdiag(wa), blockdiag(wx)], axis=-1).astype(BF16)


def kernel(x, norm_mix_g, w_in, b_gate, conv_w, conv_b, lru_lambda, lru_wa, lru_ba, lru_wx, lru_bx,
           attn_sink, w_out, norm_ffn_g, w_ffn_in, w_ffn_out, norm_final_g):
    bsz, s, d = x.shape
    depth = w_in.shape[0]
    assert depth == 1, "the merge/ffn kernel applies the final RMSNorm, so it must be the last layer"
    t = bsz * s
    row = lambda a: a.reshape(1, -1)
    time_major = lambda a: a.reshape(a.shape[:-1] + (LANE_TILES, LANES))
    x2 = x.reshape(t, d)
    for l in range(depth):
        huc, q, k, v, merge_coef = _in_proj(
            x2, s, row(norm_mix_g[l]), w_in[l].astype(BF16), row(b_gate[l]),
            time_major(conv_w[l]), time_major(conv_b[l]))
        seq = lambda a: a.reshape(bsz, s, a.shape[-1])
        lru = lambda dr: (lru_lambda[l], _gate_weights(lru_wa[l, dr], lru_wx[l, dr]), lru_ba[l], lru_bx[l])
        h_fwd = _lru(False, seq(huc), *lru(0))
        h_bwd = _lru(True, seq(huc), *lru(1))
        y_b, wo_b, wi_b, wd_b = _attention(seq(q), seq(k), seq(v), attn_sink[l],
                                           (w_out[l], w_ffn_in[l], w_ffn_out[l]))
        x2 = _ffn(x2, h_fwd.reshape(t, d), h_bwd.reshape(t, d), y_b.reshape(t, d), merge_coef,
                  wo_b, row(norm_ffn_g[l]), wi_b, wd_b, row(norm_final_g))
    return x2.reshape(bsz, s, d)
```
